```python
import math
import jax
import jax.numpy as jnp
from jax import lax
import numpy as np

D_MODEL = 2048
BATCH = 8
SEQ = 8192
DEPTH = 2

D_MIX = D_MODEL
N_MIXERS = 4
GROUP_WIDTH = D_MIX // N_MIXERS

S5_CH_PER_GROUP = 16
S5_GROUPS = GROUP_WIDTH // S5_CH_PER_GROUP
S5_STATE = 64
S5_STEP_MIN = 1e-3
S5_STEP_MAX = 1e-1

SGU_CHUNK = 128
SGU_HEADS = 8
SGU_HEAD_DIM = GROUP_WIDTH // SGU_HEADS

POOL_WINDOWS = (2, 4, 8, 16)
POOL_GROUPS = len(POOL_WINDOWS)
POOL_GROUP_DIM = GROUP_WIDTH // POOL_GROUPS

DN_HEAD_DIM = 128
DN_HEADS = GROUP_WIDTH // DN_HEAD_DIM
DN_CONV = 4
DN_CHUNK = 64
DN_DT_MIN = 1e-3
DN_DT_MAX = 1e-1

D_FF = 4 * D_MODEL

LN_EPS = 1e-5
RMS_EPS = 1e-6
L2_EPS = 1e-6
DEEPNORM_ALPHA = (2 * DEPTH) ** 0.25
DEEPNORM_BETA = (8 * DEPTH) ** -0.25

S5_OFF = 0
SGU_OFF = S5_OFF + GROUP_WIDTH
POOL_OFF = SGU_OFF + 2 * GROUP_WIDTH
DN_QKV_OFF = POOL_OFF + GROUP_WIDTH
DN_GATE_OFF = DN_QKV_OFF + 3 * GROUP_WIDTH
DN_A_OFF = DN_GATE_OFF + GROUP_WIDTH
DN_B_OFF = DN_A_OFF + DN_HEADS
IN_COLS = DN_B_OFF + DN_HEADS

kernel_name = 'hybrid_s5_sgu_pool_gdn_deepnorm'


def _layer_norm(x, g, b):
    xf = x.astype(jnp.float32)
    mu = jnp.mean(xf, axis=-1, keepdims=True)
    var = jnp.mean(jnp.square(xf - mu), axis=-1, keepdims=True)
    return (xf - mu) * lax.rsqrt(var + LN_EPS) * g.astype(jnp.float32) + b.astype(jnp.float32)


def _rms_norm(x, g):
    return x * lax.rsqrt(jnp.mean(jnp.square(x), axis=-1, keepdims=True) + RMS_EPS) * g.astype(jnp.float32)


def _l2_normalize(x):
    return x * lax.rsqrt(jnp.sum(jnp.square(x), axis=-1, keepdims=True) + L2_EPS)


def _linear_recurrence_combine(left, right):
    a_l, b_l = left
    a_r, b_r = right
    return a_l * a_r, a_r * b_l + b_r


def _s5_mixer(u, lam_re, lam_im, log_step, b_re, b_im, c_re, c_im, d, glu_w, glu_b):
    bsz, l, _ = u.shape
    uf = u.astype(jnp.float32).reshape(bsz, l, S5_GROUPS, S5_CH_PER_GROUP)
    lam = lax.complex(lam_re.astype(jnp.float32), lam_im.astype(jnp.float32))
    step = jnp.exp(log_step.astype(jnp.float32))[:, None]
    lam_bar = jnp.exp(lam * step)
    b_mat = lax.complex(b_re.astype(jnp.float32), b_im.astype(jnp.float32))
    b_bar = ((lam_bar - 1.0) / lam)[:, :, None] * b_mat
    bu = jnp.einsum('blgh,gph->blgp', uf.astype(jnp.complex64), b_bar)
    a = jnp.broadcast_to(lam_bar, bu.shape)
    _, states = lax.associative_scan(_linear_recurrence_combine, (a, bu), axis=1)
    c_mat = lax.complex(c_re.astype(jnp.float32), c_im.astype(jnp.float32))
    y = jnp.real(jnp.einsum('blgp,ghp->blgh', states, c_mat)) + d.astype(jnp.float32) * uf
    y = jax.nn.gelu(y.reshape(bsz, l, GROUP_WIDTH))
    return y * jax.nn.sigmoid(y @ glu_w.astype(jnp.float32) + glu_b.astype(jnp.float32))


def _sgu_mixer(z, norm_g, norm_b, w_s, b_s):
    z = jax.nn.gelu(z.astype(jnp.float32))
    u, v = jnp.split(z, 2, axis=-1)
    v = _layer_norm(v, norm_g, norm_b)
    bsz, l, _ = v.shape
    n = l // SGU_CHUNK
    v = v.reshape(bsz, n, SGU_CHUNK, SGU_HEADS, SGU_HEAD_DIM)
    causal = jnp.tril(jnp.ones((SGU_CHUNK, SGU_CHUNK), dtype=bool))
    w = jnp.where(causal, w_s.astype(jnp.float32), 0.0)
    bias = b_s.astype(jnp.float32).T[None, None, :, :, None]
    mixed = jnp.einsum('hts,bnshd->bnthd', w, v) + bias
    return u * mixed.reshape(bsz, l, GROUP_WIDTH)


def _pool_mixer(p, w_pool, scale):
    pf = p.astype(jnp.float32)
    bsz, l, _ = pf.shape
    groups = pf.reshape(bsz, l, POOL_GROUPS, POOL_GROUP_DIM)
    csum = jnp.cumsum(groups, axis=1)
    pos = jnp.arange(l)
    outs = []
    for gi, win in enumerate(POOL_WINDOWS):
        cs = csum[:, :, gi]
        prev = jnp.pad(cs, ((0, 0), (win, 0), (0, 0)))[:, :l]
        count = jnp.minimum(pos + 1, win).astype(jnp.float32)[None, :, None]
        outs.append((cs - prev) / count - groups[:, :, gi])
    pooled = jnp.stack(outs, axis=2)
    mixed = jnp.einsum('blgc,gcd->blgd', pooled, w_pool.astype(jnp.float32))
    return mixed.reshape(bsz, l, GROUP_WIDTH) * scale.astype(jnp.float32)


def _chunk_gated_delta_rule(q, k, v, g, beta):
    bsz, l, h, dk = q.shape
    dv = v.shape[-1]
    c = DN_CHUNK
    n = l // c

    def chunk4(t):
        return t.reshape(bsz, n, c, h, t.shape[-1]).transpose(0, 3, 1, 2, 4)

    def chunk3(t):
        return t.reshape(bsz, n, c, h).transpose(0, 3, 1, 2)

    q, k, v = chunk4(q), chunk4(k), chunk4(v)
    g, beta = chunk3(g), chunk3(beta)
    gc = jnp.cumsum(g, axis=-1)
    causal = jnp.tril(jnp.ones((c, c), dtype=bool))
    strict = jnp.tril(jnp.ones((c, c), dtype=bool), k=-1)
    decay = jnp.exp(jnp.where(causal, gc[..., :, None] - gc[..., None, :], -jnp.inf))
    k_beta = k * beta[..., None]
    v_beta = v * beta[..., None]
    eye = jnp.eye(c, dtype=jnp.float32)
    kk = jnp.einsum('bhncd,bhnsd->bhncs', k_beta, k) * decay
    a_mat = jnp.where(strict, kk, 0.0) + eye
    t_mat = lax.linalg.triangular_solve(a_mat, jnp.broadcast_to(eye, a_mat.shape),
                                        left_side=True, lower=True, unit_diagonal=True)
    u = jnp.einsum('bhncs,bhnsd->bhncd', t_mat, v_beta)
    w = jnp.einsum('bhncs,bhnsd->bhncd', t_mat, k_beta * jnp.exp(gc)[..., None])
    qk = jnp.einsum('bhncd,bhnsd->bhncs', q, k) * decay
    q_g = q * jnp.exp(gc)[..., None]
    k_tail = k * jnp.exp(gc[..., -1:] - gc)[..., None]
    g_last = jnp.exp(gc[..., -1])

    def step(state, inp):
        u_c, w_c, qg_c, qk_c, kt_c, gl_c = inp
        v_new = u_c - jnp.einsum('bhck,bhkv->bhcv', w_c, state)
        o_c = jnp.einsum('bhck,bhkv->bhcv', qg_c, state) + jnp.einsum('bhcs,bhsv->bhcv', qk_c, v_new)
        state = state * gl_c[..., None, None] + jnp.einsum('bhck,bhcv->bhkv', kt_c, v_new)
        return state, o_c

    xs = (jnp.moveaxis(u, 2, 0), jnp.moveaxis(w, 2, 0), jnp.moveaxis(q_g, 2, 0),
          jnp.moveaxis(qk, 2, 0), jnp.moveaxis(k_tail, 2, 0), jnp.moveaxis(g_last, 2, 0))
    state0 = jnp.zeros((bsz, h, dk, dv), jnp.float32)
    _, o = lax.scan(step, state0, xs)
    return o.transpose(1, 0, 3, 2, 4).reshape(bsz, l, h, dv)


def _deltanet_mixer(qkv, gate, a_logit, b_logit, conv_w, a_log, dt_bias, norm_g):
    bsz, l, _ = qkv.shape
    x3 = qkv.astype(jnp.float32)
    conv = lax.conv_general_dilated(x3, conv_w.astype(jnp.float32)[:, None, :], window_strides=(1,),
                                    padding=[(DN_CONV - 1, 0)], dimension_numbers=('NWC', 'WIO', 'NWC'),
                                    feature_group_count=3 * GROUP_WIDTH)
    x3 = jax.nn.silu(conv)
    q, k, v = jnp.split(x3, 3, axis=-1)
    q = _l2_normalize(q.reshape(bsz, l, DN_HEADS, DN_HEAD_DIM)) * (DN_HEAD_DIM ** -0.5)
    k = _l2_normalize(k.reshape(bsz, l, DN_HEADS, DN_HEAD_DIM))
    v = v.reshape(bsz, l, DN_HEADS, DN_HEAD_DIM)
    g = -jnp.exp(a_log.astype(jnp.float32)) * jax.nn.softplus(a_logit.astype(jnp.float32) + dt_bias.astype(jnp.float32))
    beta = jax.nn.sigmoid(b_logit.astype(jnp.float32))
    o = _chunk_gated_delta_rule(q, k, v, g, beta)
    gate = gate.astype(jnp.float32).reshape(bsz, l, DN_HEADS, DN_HEAD_DIM)
    o = _rms_norm(o, norm_g) * jax.nn.silu(gate)
    return o.reshape(bsz, l, GROUP_WIDTH)


def _fwd_setup_inputs(seed: int = 0) -> dict:
    key = jax.random.key(seed)
    ks = jax.random.split(key, 32)
    f32 = jnp.float32
    nl = DEPTH

    def nrm(k, shape, scale):
        return jax.random.normal(k, shape, f32) * scale

    x = nrm(ks[0], (BATCH, SEQ, D_MODEL), 1.0)
    w_in = nrm(ks[1], (nl, D_MODEL, IN_COLS), D_MODEL ** -0.5)
    n_idx = jnp.arange(S5_STATE, dtype=f32)
    s5_lambda_re = -0.5 + nrm(ks[2], (nl, S5_GROUPS, S5_STATE), 0.01)
    s5_lambda_im = math.pi * n_idx + nrm(ks[3], (nl, S5_GROUPS, S5_STATE), 0.01)
    s5_log_step = jax.random.uniform(ks[4], (nl, S5_GROUPS), f32, math.log(S5_STEP_MIN), math.log(S5_STEP_MAX))
    s5_b_re = nrm(ks[5], (nl, S5_GROUPS, S5_STATE, S5_CH_PER_GROUP), (2 * S5_CH_PER_GROUP) ** -0.5)
    s5_b_im = nrm(ks[6], (nl, S5_GROUPS, S5_STATE, S5_CH_PER_GROUP), (2 * S5_CH_PER_GROUP) ** -0.5)
    s5_c_re = nrm(ks[7], (nl, S5_GROUPS, S5_CH_PER_GROUP, S5_STATE), (2 * S5_STATE) ** -0.5)
    s5_c_im = nrm(ks[8], (nl, S5_GROUPS, S5_CH_PER_GROUP, S5_STATE), (2 * S5_STATE) ** -0.5)
    s5_d = nrm(ks[9], (nl, S5_GROUPS, S5_CH_PER_GROUP), 1.0)
    s5_glu_w = nrm(ks[10], (nl, GROUP_WIDTH, GROUP_WIDTH), GROUP_WIDTH ** -0.5)
    s5_glu_b = nrm(ks[11], (nl, GROUP_WIDTH), 0.01)
    sgu_norm_g = 1.0 + nrm(ks[12], (nl, GROUP_WIDTH), 0.02)
    sgu_norm_b = nrm(ks[13], (nl, GROUP_WIDTH), 0.02)
    sgu_w = nrm(ks[14], (nl, SGU_HEADS, SGU_CHUNK, SGU_CHUNK), SGU_CHUNK ** -0.5)
    sgu_b = 1.0 + nrm(ks[15], (nl, SGU_HEADS, SGU_CHUNK), 0.02)
    pool_w = nrm(ks[16], (nl, POOL_GROUPS, POOL_GROUP_DIM, POOL_GROUP_DIM), POOL_GROUP_DIM ** -0.5)
    pool_scale = 1.0 + nrm(ks[17], (nl, GROUP_WIDTH), 0.02)
    dn_conv_w = nrm(ks[18], (nl, DN_CONV, 3 * GROUP_WIDTH), DN_CONV ** -0.5)
    dn_a_log = jnp.log(jax.random.uniform(ks[19], (nl, DN_HEADS), f32, 1.0, 16.0))
    dt = jnp.exp(jax.random.uniform(ks[20], (nl, DN_HEADS), f32, math.log(DN_DT_MIN), math.log(DN_DT_MAX)))
    dn_dt_bias = dt + jnp.log(-jnp.expm1(-dt))
    dn_norm_g = 1.0 + nrm(ks[21], (nl, DN_HEAD_DIM), 0.02)
    w_out = nrm(ks[22], (nl, D_MIX, D_MODEL), D_MIX ** -0.5) * DEEPNORM_BETA
    ln1_g = 1.0 + nrm(ks[23], (nl, D_MODEL), 0.02)
    ln1_b = nrm(ks[24], (nl, D_MODEL), 0.02)
    w_up = nrm(ks[25], (nl, D_MODEL, D_FF), D_MODEL ** -0.5)
    w_down = nrm(ks[26], (nl, D_FF, D_MODEL), D_FF ** -0.5) * DEEPNORM_BETA
    ln2_g = 1.0 + nrm(ks[27], (nl, D_MODEL), 0.02)
    ln2_b = nrm(ks[28], (nl, D_MODEL), 0.02)
    return {'x': x, 'w_in': w_in, 's5_lambda_re': s5_lambda_re, 's5_lambda_im': s5_lambda_im,
            's5_log_step': s5_log_step, 's5_b_re': s5_b_re, 's5_b_im': s5_b_im, 's5_c_re': s5_c_re,
            's5_c_im': s5_c_im, 's5_d': s5_d, 's5_glu_w': s5_glu_w, 's5_glu_b': s5_glu_b,
            'sgu_norm_g': sgu_norm_g, 'sgu_norm_b': sgu_norm_b, 'sgu_w': sgu_w, 'sgu_b': sgu_b,
            'pool_w': pool_w, 'pool_scale': pool_scale, 'dn_conv_w': dn_conv_w, 'dn_a_log': dn_a_log,
            'dn_dt_bias': dn_dt_bias, 'dn_norm_g': dn_norm_g, 'w_out': w_out, 'ln1_g': ln1_g,
            'ln1_b': ln1_b, 'w_up': w_up, 'w_down': w_down, 'ln2_g': ln2_g, 'ln2_b': ln2_b}


def _fwd_reference(x, w_in, s5_lambda_re, s5_lambda_im, s5_log_step, s5_b_re, s5_b_im, s5_c_re, s5_c_im,
              s5_d, s5_glu_w, s5_glu_b, sgu_norm_g, sgu_norm_b, sgu_w, sgu_b, pool_w, pool_scale,
              dn_conv_w, dn_a_log, dn_dt_bias, dn_norm_g, w_out, ln1_g, ln1_b, w_up, w_down,
              ln2_g, ln2_b):
    for i in range(DEPTH):
        proj = x @ w_in[i]
        y_s5 = _s5_mixer(proj[..., S5_OFF:SGU_OFF], s5_lambda_re[i], s5_lambda_im[i], s5_log_step[i],
                         s5_b_re[i], s5_b_im[i], s5_c_re[i], s5_c_im[i], s5_d[i], s5_glu_w[i], s5_glu_b[i])
        y_sgu = _sgu_mixer(proj[..., SGU_OFF:POOL_OFF], sgu_norm_g[i], sgu_norm_b[i], sgu_w[i], sgu_b[i])
        y_pool = _pool_mixer(proj[..., POOL_OFF:DN_QKV_OFF], pool_w[i], pool_scale[i])
        y_dn = _deltanet_mixer(proj[..., DN_QKV_OFF:DN_GATE_OFF], proj[..., DN_GATE_OFF:DN_A_OFF],
                               proj[..., DN_A_OFF:DN_B_OFF], proj[..., DN_B_OFF:IN_COLS],
                               dn_conv_w[i], dn_a_log[i], dn_dt_bias[i], dn_norm_g[i])
        mixed = jnp.concatenate([y_s5, y_sgu, y_pool, y_dn], axis=-1).astype(x.dtype)
        x = _layer_norm(DEEPNORM_ALPHA * x + mixed @ w_out[i], ln1_g[i], ln1_b[i]).astype(x.dtype)
        hidden = jnp.square(jax.nn.relu(x @ w_up[i]))
        x = _layer_norm(DEEPNORM_ALPHA * x + hidden @ w_down[i], ln2_g[i], ln2_b[i]).astype(x.dtype)
    return x


import jax as _jax
import jax.numpy as _jnp

TWIN_FORMAT = 'train_step'
FWD_PARAMS = ['x', 'w_in', 's5_lambda_re', 's5_lambda_im', 's5_log_step', 's5_b_re', 's5_b_im', 's5_c_re', 's5_c_im', 's5_d', 's5_glu_w', 's5_glu_b', 'sgu_norm_g', 'sgu_norm_b', 'sgu_w', 'sgu_b', 'pool_w', 'pool_scale', 'dn_conv_w', 'dn_a_log', 'dn_dt_bias', 'dn_norm_g', 'w_out', 'ln1_g', 'ln1_b', 'w_up', 'w_down', 'ln2_g', 'ln2_b']
TWIN_WEIGHTS = ['w_in', 's5_lambda_re', 's5_lambda_im', 's5_log_step', 's5_b_re', 's5_b_im', 's5_c_re', 's5_c_im', 's5_d', 's5_glu_w', 's5_glu_b', 'sgu_norm_g', 'sgu_norm_b', 'sgu_w', 'sgu_b', 'pool_w', 'pool_scale', 'dn_conv_w', 'dn_a_log', 'dn_dt_bias', 'dn_norm_g', 'w_out', 'ln1_g', 'ln1_b', 'w_up', 'w_down', 'ln2_g', 'ln2_b']
TWIN_DIFF_INPUT = 'x'
TWIN_INPUTS = ['x', 'w_in', 's5_lambda_re', 's5_lambda_im', 's5_log_step', 's5_b_re', 's5_b_im', 's5_c_re', 's5_c_im', 's5_d', 's5_glu_w', 's5_glu_b', 'sgu_norm_g', 'sgu_norm_b', 'sgu_w', 'sgu_b', 'pool_w', 'pool_scale', 'dn_conv_w', 'dn_a_log', 'dn_dt_bias', 'dn_norm_g', 'w_out', 'ln1_g', 'ln1_b', 'w_up', 'w_down', 'ln2_g', 'ln2_b', 'loss_target', 'm_w_in', 'm_s5_lambda_re', 'm_s5_lambda_im', 'm_s5_log_step', 'm_s5_b_re', 'm_s5_b_im', 'm_s5_c_re', 'm_s5_c_im', 'm_s5_d', 'm_s5_glu_w', 'm_s5_glu_b', 'm_sgu_norm_g', 'm_sgu_norm_b', 'm_sgu_w', 'm_sgu_b', 'm_pool_w', 'm_pool_scale', 'm_dn_conv_w', 'm_dn_a_log', 'm_dn_dt_bias', 'm_dn_norm_g', 'm_w_out', 'm_ln1_g', 'm_ln1_b', 'm_w_up', 'm_w_down', 'm_ln2_g', 'm_ln2_b', 'v_w_in', 'v_s5_lambda_re', 'v_s5_lambda_im', 'v_s5_log_step', 'v_s5_b_re', 'v_s5_b_im', 'v_s5_c_re', 'v_s5_c_im', 'v_s5_d', 'v_s5_glu_w', 'v_s5_glu_b', 'v_sgu_norm_g', 'v_sgu_norm_b', 'v_sgu_w', 'v_sgu_b', 'v_pool_w', 'v_pool_scale', 'v_dn_conv_w', 'v_dn_a_log', 'v_dn_dt_bias', 'v_dn_norm_g', 'v_w_out', 'v_ln1_g', 'v_ln1_b', 'v_w_up', 'v_w_down', 'v_ln2_g', 'v_ln2_b']
TWIN_OUTPUTS = ['loss', 'grad_x', 'grad_w_in', 'grad_s5_lambda_re', 'grad_s5_lambda_im', 'grad_s5_log_step', 'grad_s5_b_re', 'grad_s5_b_im', 'grad_s5_c_re', 'grad_s5_c_im', 'grad_s5_d', 'grad_s5_glu_w', 'grad_s5_glu_b', 'grad_sgu_norm_g', 'grad_sgu_norm_b', 'grad_sgu_w', 'grad_sgu_b', 'grad_pool_w', 'grad_pool_scale', 'grad_dn_conv_w', 'grad_dn_a_log', 'grad_dn_dt_bias', 'grad_dn_norm_g', 'grad_w_out', 'grad_ln1_g', 'grad_ln1_b', 'grad_w_up', 'grad_w_down', 'grad_ln2_g', 'grad_ln2_b', 'delta_w_in', 'delta_s5_lambda_re', 'delta_s5_lambda_im', 'delta_s5_log_step', 'delta_s5_b_re', 'delta_s5_b_im', 'delta_s5_c_re', 'delta_s5_c_im', 'delta_s5_d', 'delta_s5_glu_w', 'delta_s5_glu_b', 'delta_sgu_norm_g', 'delta_sgu_norm_b', 'delta_sgu_w', 'delta_sgu_b', 'delta_pool_w', 'delta_pool_scale', 'delta_dn_conv_w', 'delta_dn_a_log', 'delta_dn_dt_bias', 'delta_dn_norm_g', 'delta_w_out', 'delta_ln1_g', 'delta_ln1_b', 'delta_w_up', 'delta_w_down', 'delta_ln2_g', 'delta_ln2_b', 'new_m_w_in', 'new_m_s5_lambda_re', 'new_m_s5_lambda_im', 'new_m_s5_log_step', 'new_m_s5_b_re', 'new_m_s5_b_im', 'new_m_s5_c_re', 'new_m_s5_c_im', 'new_m_s5_d', 'new_m_s5_glu_w', 'new_m_s5_glu_b', 'new_m_sgu_norm_g', 'new_m_sgu_norm_b', 'new_m_sgu_w', 'new_m_sgu_b', 'new_m_pool_w', 'new_m_pool_scale', 'new_m_dn_conv_w', 'new_m_dn_a_log', 'new_m_dn_dt_bias', 'new_m_dn_norm_g', 'new_m_w_out', 'new_m_ln1_g', 'new_m_ln1_b', 'new_m_w_up', 'new_m_w_down', 'new_m_ln2_g', 'new_m_ln2_b', 'new_v_w_in', 'new_v_s5_lambda_re', 'new_v_s5_lambda_im', 'new_v_s5_log_step', 'new_v_s5_b_re', 'new_v_s5_b_im', 'new_v_s5_c_re', 'new_v_s5_c_im', 'new_v_s5_d', 'new_v_s5_glu_w', 'new_v_s5_glu_b', 'new_v_sgu_norm_g', 'new_v_sgu_norm_b', 'new_v_sgu_w', 'new_v_sgu_b', 'new_v_pool_w', 'new_v_pool_scale', 'new_v_dn_conv_w', 'new_v_dn_a_log', 'new_v_dn_dt_bias', 'new_v_dn_norm_g', 'new_v_w_out', 'new_v_ln1_g', 'new_v_ln1_b', 'new_v_w_up', 'new_v_w_down', 'new_v_ln2_g', 'new_v_ln2_b']
TWIN_LEAF_KINDS = {'loss': 'loss', 'grad_x': 'grad_x', 'grad_w_in': 'grad_w', 'grad_s5_lambda_re': 'grad_w', 'grad_s5_lambda_im': 'grad_w', 'grad_s5_log_step': 'grad_w', 'grad_s5_b_re': 'grad_w', 'grad_s5_b_im': 'grad_w', 'grad_s5_c_re': 'grad_w', 'grad_s5_c_im': 'grad_w', 'grad_s5_d': 'grad_w', 'grad_s5_glu_w': 'grad_w', 'grad_s5_glu_b': 'grad_w', 'grad_sgu_norm_g': 'grad_w', 'grad_sgu_norm_b': 'grad_w', 'grad_sgu_w': 'grad_w', 'grad_sgu_b': 'grad_w', 'grad_pool_w': 'grad_w', 'grad_pool_scale': 'grad_w', 'grad_dn_conv_w': 'grad_w', 'grad_dn_a_log': 'grad_w', 'grad_dn_dt_bias': 'grad_w', 'grad_dn_norm_g': 'grad_w', 'grad_w_out': 'grad_w', 'grad_ln1_g': 'grad_w', 'grad_ln1_b': 'grad_w', 'grad_w_up': 'grad_w', 'grad_w_down': 'grad_w', 'grad_ln2_g': 'grad_w', 'grad_ln2_b': 'grad_w', 'delta_w_in': 'delta_w', 'delta_s5_lambda_re': 'delta_w', 'delta_s5_lambda_im': 'delta_w', 'delta_s5_log_step': 'delta_w', 'delta_s5_b_re': 'delta_w', 'delta_s5_b_im': 'delta_w', 'delta_s5_c_re': 'delta_w', 'delta_s5_c_im': 'delta_w', 'delta_s5_d': 'delta_w', 'delta_s5_glu_w': 'delta_w', 'delta_s5_glu_b': 'delta_w', 'delta_sgu_norm_g': 'delta_w', 'delta_sgu_norm_b': 'delta_w', 'delta_sgu_w': 'delta_w', 'delta_sgu_b': 'delta_w', 'delta_pool_w': 'delta_w', 'delta_pool_scale': 'delta_w', 'delta_dn_conv_w': 'delta_w', 'delta_dn_a_log': 'delta_w', 'delta_dn_dt_bias': 'delta_w', 'delta_dn_norm_g': 'delta_w', 'delta_w_out': 'delta_w', 'delta_ln1_g': 'delta_w', 'delta_ln1_b': 'delta_w', 'delta_w_up': 'delta_w', 'delta_w_down': 'delta_w', 'delta_ln2_g': 'delta_w', 'delta_ln2_b': 'delta_w', 'new_m_w_in': 'new_m', 'new_m_s5_lambda_re': 'new_m', 'new_m_s5_lambda_im': 'new_m', 'new_m_s5_log_step': 'new_m', 'new_m_s5_b_re': 'new_m', 'new_m_s5_b_im': 'new_m', 'new_m_s5_c_re': 'new_m', 'new_m_s5_c_im': 'new_m', 'new_m_s5_d': 'new_m', 'new_m_s5_glu_w': 'new_m', 'new_m_s5_glu_b': 'new_m', 'new_m_sgu_norm_g': 'new_m', 'new_m_sgu_norm_b': 'new_m', 'new_m_sgu_w': 'new_m', 'new_m_sgu_b': 'new_m', 'new_m_pool_w': 'new_m', 'new_m_pool_scale': 'new_m', 'new_m_dn_conv_w': 'new_m', 'new_m_dn_a_log': 'new_m', 'new_m_dn_dt_bias': 'new_m', 'new_m_dn_norm_g': 'new_m', 'new_m_w_out': 'new_m', 'new_m_ln1_g': 'new_m', 'new_m_ln1_b': 'new_m', 'new_m_w_up': 'new_m', 'new_m_w_down': 'new_m', 'new_m_ln2_g': 'new_m', 'new_m_ln2_b': 'new_m', 'new_v_w_in': 'new_v', 'new_v_s5_lambda_re': 'new_v', 'new_v_s5_lambda_im': 'new_v', 'new_v_s5_log_step': 'new_v', 'new_v_s5_b_re': 'new_v', 'new_v_s5_b_im': 'new_v', 'new_v_s5_c_re': 'new_v', 'new_v_s5_c_im': 'new_v', 'new_v_s5_d': 'new_v', 'new_v_s5_glu_w': 'new_v', 'new_v_s5_glu_b': 'new_v', 'new_v_sgu_norm_g': 'new_v', 'new_v_sgu_norm_b': 'new_v', 'new_v_sgu_w': 'new_v', 'new_v_sgu_b': 'new_v', 'new_v_pool_w': 'new_v', 'new_v_pool_scale': 'new_v', 'new_v_dn_conv_w': 'new_v', 'new_v_dn_a_log': 'new_v', 'new_v_dn_dt_bias': 'new_v', 'new_v_dn_norm_g': 'new_v', 'new_v_w_out': 'new_v', 'new_v_ln1_g': 'new_v', 'new_v_ln1_b': 'new_v', 'new_v_w_up': 'new_v', 'new_v_w_down': 'new_v', 'new_v_ln2_g': 'new_v', 'new_v_ln2_b': 'new_v'}


def _forward(args):
    return _fwd_reference(*[args[k] for k in FWD_PARAMS])


def _output_shape():
    def fwd():
        inp = _fwd_setup_inputs(0)
        return _fwd_reference(*[inp[k] for k in FWD_PARAMS])
    out = _jax.eval_shape(fwd)
    return out.shape, out.dtype

N_MICROBATCH = 1
ADAM_LR = 0.001
ADAM_B1 = 0.9
ADAM_B2 = 0.999
ADAM_EPS = 1e-08
ADAM_WD = 0.01
ADAM_STEP = 10
PER_EXAMPLE_BATCH_AXIS = {'x': 0, 'loss_target': 0}
SHARED_INPUTS = []
_WEIGHT_DTYPES = {'w_in': _jnp.float32, 's5_lambda_re': _jnp.float32, 's5_lambda_im': _jnp.float32, 's5_log_step': _jnp.float32, 's5_b_re': _jnp.float32, 's5_b_im': _jnp.float32, 's5_c_re': _jnp.float32, 's5_c_im': _jnp.float32, 's5_d': _jnp.float32, 's5_glu_w': _jnp.float32, 's5_glu_b': _jnp.float32, 'sgu_norm_g': _jnp.float32, 'sgu_norm_b': _jnp.float32, 'sgu_w': _jnp.float32, 'sgu_b': _jnp.float32, 'pool_w': _jnp.float32, 'pool_scale': _jnp.float32, 'dn_conv_w': _jnp.float32, 'dn_a_log': _jnp.float32, 'dn_dt_bias': _jnp.float32, 'dn_norm_g': _jnp.float32, 'w_out': _jnp.float32, 'ln1_g': _jnp.float32, 'ln1_b': _jnp.float32, 'w_up': _jnp.float32, 'w_down': _jnp.float32, 'ln2_g': _jnp.float32, 'ln2_b': _jnp.float32}
MOMENT_SCALE = {'w_in': 2.699847e-02, 's5_lambda_re': 1.203760e-03, 's5_lambda_im': 9.280270e-04, 's5_log_step': 5.844442e-01, 's5_b_re': 7.103059e-04, 's5_b_im': 5.869705e-04, 's5_c_re': 1.319728e-03, 's5_c_im': 1.330170e-03, 's5_d': 3.064262e-02, 's5_glu_w': 5.529463e-03, 's5_glu_b': 1.200738e-02, 'sgu_norm_g': 2.124209e-02, 'sgu_norm_b': 2.147994e-02, 'sgu_w': 1.478416e-02, 'sgu_b': 2.145724e-02, 'pool_w': 3.934434e-02, 'pool_scale': 3.870619e-02, 'dn_conv_w': 2.238091e-02, 'dn_a_log': 1.317832e-01, 'dn_dt_bias': 1.277604e-01, 'dn_norm_g': 5.664997e-02, 'w_out': 7.579972e-02, 'ln1_g': 8.131170e-01, 'ln1_b': 5.172824e-01, 'w_up': 3.010973e-02, 'w_down': 1.712417e-01, 'ln2_g': 2.273334e+01, 'ln2_b': 5.391098e+00}


def _to_microbatches(a, axis):
    t = _jnp.moveaxis(a, axis, 0)
    t = t.reshape((N_MICROBATCH, t.shape[0] // N_MICROBATCH) + t.shape[1:])
    return _jnp.moveaxis(t, 1, axis + 1)


def setup_inputs(seed: int = 0) -> dict:
    inp = _fwd_setup_inputs(seed)
    key = _jax.random.fold_in(_jax.random.key(seed), 7919)
    shape, _ = _output_shape()
    out = dict(inp)
    out["loss_target"] = _jax.random.normal(_jax.random.fold_in(key, 0), shape, _jnp.float32)
    for i, name in enumerate(TWIN_WEIGHTS):
        w = inp[name].astype(_jnp.float32)
        if MOMENT_SCALE is None:
            s = _jnp.sqrt(_jnp.mean(_jnp.square(w)) + 1e-30)
        else:
            s = MOMENT_SCALE[name]
        km, kv = _jax.random.split(_jax.random.fold_in(key, i + 1))
        out[name] = w
        out["m_" + name] = s * _jax.random.normal(km, w.shape, _jnp.float32)
        out["v_" + name] = (s * s) * _jax.random.uniform(kv, w.shape, _jnp.float32, 0.5, 1.5)
    if N_MICROBATCH > 1:
        for name, axis in PER_EXAMPLE_BATCH_AXIS.items():
            out[name] = _to_microbatches(out[name], axis)
    return {'x': out['x'], 'w_in': out['w_in'], 's5_lambda_re': out['s5_lambda_re'], 's5_lambda_im': out['s5_lambda_im'], 's5_log_step': out['s5_log_step'], 's5_b_re': out['s5_b_re'], 's5_b_im': out['s5_b_im'], 's5_c_re': out['s5_c_re'], 's5_c_im': out['s5_c_im'], 's5_d': out['s5_d'], 's5_glu_w': out['s5_glu_w'], 's5_glu_b': out['s5_glu_b'], 'sgu_norm_g': out['sgu_norm_g'], 'sgu_norm_b': out['sgu_norm_b'], 'sgu_w': out['sgu_w'], 'sgu_b': out['sgu_b'], 'pool_w': out['pool_w'], 'pool_scale': out['pool_scale'], 'dn_conv_w': out['dn_conv_w'], 'dn_a_log': out['dn_a_log'], 'dn_dt_bias': out['dn_dt_bias'], 'dn_norm_g': out['dn_norm_g'], 'w_out': out['w_out'], 'ln1_g': out['ln1_g'], 'ln1_b': out['ln1_b'], 'w_up': out['w_up'], 'w_down': out['w_down'], 'ln2_g': out['ln2_g'], 'ln2_b': out['ln2_b'], 'loss_target': out['loss_target'], 'm_w_in': out['m_w_in'], 'm_s5_lambda_re': out['m_s5_lambda_re'], 'm_s5_lambda_im': out['m_s5_lambda_im'], 'm_s5_log_step': out['m_s5_log_step'], 'm_s5_b_re': out['m_s5_b_re'], 'm_s5_b_im': out['m_s5_b_im'], 'm_s5_c_re': out['m_s5_c_re'], 'm_s5_c_im': out['m_s5_c_im'], 'm_s5_d': out['m_s5_d'], 'm_s5_glu_w': out['m_s5_glu_w'], 'm_s5_glu_b': out['m_s5_glu_b'], 'm_sgu_norm_g': out['m_sgu_norm_g'], 'm_sgu_norm_b': out['m_sgu_norm_b'], 'm_sgu_w': out['m_sgu_w'], 'm_sgu_b': out['m_sgu_b'], 'm_pool_w': out['m_pool_w'], 'm_pool_scale': out['m_pool_scale'], 'm_dn_conv_w': out['m_dn_conv_w'], 'm_dn_a_log': out['m_dn_a_log'], 'm_dn_dt_bias': out['m_dn_dt_bias'], 'm_dn_norm_g': out['m_dn_norm_g'], 'm_w_out': out['m_w_out'], 'm_ln1_g': out['m_ln1_g'], 'm_ln1_b': out['m_ln1_b'], 'm_w_up': out['m_w_up'], 'm_w_down': out['m_w_down'], 'm_ln2_g': out['m_ln2_g'], 'm_ln2_b': out['m_ln2_b'], 'v_w_in': out['v_w_in'], 'v_s5_lambda_re': out['v_s5_lambda_re'], 'v_s5_lambda_im': out['v_s5_lambda_im'], 'v_s5_log_step': out['v_s5_log_step'], 'v_s5_b_re': out['v_s5_b_re'], 'v_s5_b_im': out['v_s5_b_im'], 'v_s5_c_re': out['v_s5_c_re'], 'v_s5_c_im': out['v_s5_c_im'], 'v_s5_d': out['v_s5_d'], 'v_s5_glu_w': out['v_s5_glu_w'], 'v_s5_glu_b': out['v_s5_glu_b'], 'v_sgu_norm_g': out['v_sgu_norm_g'], 'v_sgu_norm_b': out['v_sgu_norm_b'], 'v_sgu_w': out['v_sgu_w'], 'v_sgu_b': out['v_sgu_b'], 'v_pool_w': out['v_pool_w'], 'v_pool_scale': out['v_pool_scale'], 'v_dn_conv_w': out['v_dn_conv_w'], 'v_dn_a_log': out['v_dn_a_log'], 'v_dn_dt_bias': out['v_dn_dt_bias'], 'v_dn_norm_g': out['v_dn_norm_g'], 'v_w_out': out['v_w_out'], 'v_ln1_g': out['v_ln1_g'], 'v_ln1_b': out['v_ln1_b'], 'v_w_up': out['v_w_up'], 'v_w_down': out['v_w_down'], 'v_ln2_g': out['v_ln2_g'], 'v_ln2_b': out['v_ln2_b']}


def _loss(weights, diff, rest, loss_target):
    with _jax.named_scope("forward"):
        args = {**rest, TWIN_DIFF_INPUT: diff, **{k: w.astype(_WEIGHT_DTYPES[k]) for k, w in weights.items()}}
        y = _forward(args)
    with _jax.named_scope("loss_head"):
        err = _jnp.square(y.astype(_jnp.float32) - loss_target)
        return 0.5 * _jnp.sum(_jnp.mean(err, axis=-1)) if err.ndim else 0.5 * err


def _adamw(w, g, m, v):
    m = ADAM_B1 * m + (1.0 - ADAM_B1) * g
    v = ADAM_B2 * v + (1.0 - ADAM_B2) * _jnp.square(g)
    m_hat = m / (1.0 - ADAM_B1 ** ADAM_STEP)
    v_hat = v / (1.0 - ADAM_B2 ** ADAM_STEP)
    delta = -ADAM_LR * (m_hat / (_jnp.sqrt(v_hat) + ADAM_EPS) + ADAM_WD * w)
    return delta, m, v


def reference(x, w_in, s5_lambda_re, s5_lambda_im, s5_log_step, s5_b_re, s5_b_im, s5_c_re, s5_c_im, s5_d, s5_glu_w, s5_glu_b, sgu_norm_g, sgu_norm_b, sgu_w, sgu_b, pool_w, pool_scale, dn_conv_w, dn_a_log, dn_dt_bias, dn_norm_g, w_out, ln1_g, ln1_b, w_up, w_down, ln2_g, ln2_b, loss_target, m_w_in, m_s5_lambda_re, m_s5_lambda_im, m_s5_log_step, m_s5_b_re, m_s5_b_im, m_s5_c_re, m_s5_c_im, m_s5_d, m_s5_glu_w, m_s5_glu_b, m_sgu_norm_g, m_sgu_norm_b, m_sgu_w, m_sgu_b, m_pool_w, m_pool_scale, m_dn_conv_w, m_dn_a_log, m_dn_dt_bias, m_dn_norm_g, m_w_out, m_ln1_g, m_ln1_b, m_w_up, m_w_down, m_ln2_g, m_ln2_b, v_w_in, v_s5_lambda_re, v_s5_lambda_im, v_s5_log_step, v_s5_b_re, v_s5_b_im, v_s5_c_re, v_s5_c_im, v_s5_d, v_s5_glu_w, v_s5_glu_b, v_sgu_norm_g, v_sgu_norm_b, v_sgu_w, v_sgu_b, v_pool_w, v_pool_scale, v_dn_conv_w, v_dn_a_log, v_dn_dt_bias, v_dn_norm_g, v_w_out, v_ln1_g, v_ln1_b, v_w_up, v_w_down, v_ln2_g, v_ln2_b):
    given = dict(x=x, w_in=w_in, s5_lambda_re=s5_lambda_re, s5_lambda_im=s5_lambda_im, s5_log_step=s5_log_step, s5_b_re=s5_b_re, s5_b_im=s5_b_im, s5_c_re=s5_c_re, s5_c_im=s5_c_im, s5_d=s5_d, s5_glu_w=s5_glu_w, s5_glu_b=s5_glu_b, sgu_norm_g=sgu_norm_g, sgu_norm_b=sgu_norm_b, sgu_w=sgu_w, sgu_b=sgu_b, pool_w=pool_w, pool_scale=pool_scale, dn_conv_w=dn_conv_w, dn_a_log=dn_a_log, dn_dt_bias=dn_dt_bias, dn_norm_g=dn_norm_g, w_out=w_out, ln1_g=ln1_g, ln1_b=ln1_b, w_up=w_up, w_down=w_down, ln2_g=ln2_g, ln2_b=ln2_b, loss_target=loss_target, m_w_in=m_w_in, m_s5_lambda_re=m_s5_lambda_re, m_s5_lambda_im=m_s5_lambda_im, m_s5_log_step=m_s5_log_step, m_s5_b_re=m_s5_b_re, m_s5_b_im=m_s5_b_im, m_s5_c_re=m_s5_c_re, m_s5_c_im=m_s5_c_im, m_s5_d=m_s5_d, m_s5_glu_w=m_s5_glu_w, m_s5_glu_b=m_s5_glu_b, m_sgu_norm_g=m_sgu_norm_g, m_sgu_norm_b=m_sgu_norm_b, m_sgu_w=m_sgu_w, m_sgu_b=m_sgu_b, m_pool_w=m_pool_w, m_pool_scale=m_pool_scale, m_dn_conv_w=m_dn_conv_w, m_dn_a_log=m_dn_a_log, m_dn_dt_bias=m_dn_dt_bias, m_dn_norm_g=m_dn_norm_g, m_w_out=m_w_out, m_ln1_g=m_ln1_g, m_ln1_b=m_ln1_b, m_w_up=m_w_up, m_w_down=m_w_down, m_ln2_g=m_ln2_g, m_ln2_b=m_ln2_b, v_w_in=v_w_in, v_s5_lambda_re=v_s5_lambda_re, v_s5_lambda_im=v_s5_lambda_im, v_s5_log_step=v_s5_log_step, v_s5_b_re=v_s5_b_re, v_s5_b_im=v_s5_b_im, v_s5_c_re=v_s5_c_re, v_s5_c_im=v_s5_c_im, v_s5_d=v_s5_d, v_s5_glu_w=v_s5_glu_w, v_s5_glu_b=v_s5_glu_b, v_sgu_norm_g=v_sgu_norm_g, v_sgu_norm_b=v_sgu_norm_b, v_sgu_w=v_sgu_w, v_sgu_b=v_sgu_b, v_pool_w=v_pool_w, v_pool_scale=v_pool_scale, v_dn_conv_w=v_dn_conv_w, v_dn_a_log=v_dn_a_log, v_dn_dt_bias=v_dn_dt_bias, v_dn_norm_g=v_dn_norm_g, v_w_out=v_w_out, v_ln1_g=v_ln1_g, v_ln1_b=v_ln1_b, v_w_up=v_w_up, v_w_down=v_w_down, v_ln2_g=v_ln2_g, v_ln2_b=v_ln2_b)
    weights = {n: given[n] for n in TWIN_WEIGHTS}
    shared = {n: given[n] for n in SHARED_INPUTS}
    per_example = {n: given[n] for n in ['x']}
    grad_fn = _jax.value_and_grad(_loss, argnums=(0, 1))

    def one_microbatch(ex, loss_target):
        ex = dict(ex)
        diff = ex.pop(TWIN_DIFF_INPUT)
        return grad_fn(weights, diff, {**shared, **ex}, loss_target)

    if N_MICROBATCH == 1:
        loss, (grad_w, grad_x) = one_microbatch(per_example, given["loss_target"])
    else:
        def body(carry, xs):
            loss_sum, grad_sum = carry
            l_k, (gw_k, gx_k) = one_microbatch(xs[0], xs[1])
            with _jax.named_scope("update"):
                return (loss_sum + l_k, _jax.tree.map(_jnp.add, grad_sum, gw_k)), gx_k

        init = (_jnp.zeros((), _jnp.float32), _jax.tree.map(_jnp.zeros_like, weights))
        (loss, grad_w), grad_x = _jax.lax.scan(body, init, (per_example, given["loss_target"]))
    with _jax.named_scope("update"):
        delta_w, new_m, new_v = {}, {}, {}
        for n in TWIN_WEIGHTS:
            delta_w[n], new_m[n], new_v[n] = _adamw(weights[n], grad_w[n], given["m_" + n], given["v_" + n])
    return (loss, grad_x, *[grad_w[n] for n in TWIN_WEIGHTS], *[delta_w[n] for n in TWIN_WEIGHTS],
            *[new_m[n] for n in TWIN_WEIGHTS], *[new_v[n] for n in TWIN_WEIGHTS])
```

```python
import functools
import math

import jax
import jax.numpy as jnp
from jax import lax
from jax.experimental import pallas as pl
from jax.experimental.pallas import tpu as pltpu

F32 = jnp.float32
MXU_DTYPE = jnp.bfloat16
HI = lax.Precision.HIGHEST

D_MODEL = 2048
DEPTH = 2
GW = 512
S5_H = 16
S5_G = GW // S5_H
S5_P = 64
S5_N = S5_G * S5_P
SGU_CHUNK = 128
SGU_HEADS = 8
SGU_HD = GW // SGU_HEADS
POOL_WINDOWS = (2, 4, 8, 16)
POOL_GD = 128
DN_HD = 128
DN_HEADS = 4
DN_CONV = 4
DN_CHUNK = 64
D_FF = 4 * D_MODEL
IN_COLS = 4104
IN_PAD = 4224
LN_EPS = 1e-5
RMS_EPS = 1e-6
L2_EPS = 1e-6
ALPHA = (2 * DEPTH) ** 0.25
ADAM_LR, ADAM_B1, ADAM_B2, ADAM_EPS, ADAM_WD, ADAM_STEP = 0.001, 0.9, 0.999, 1e-08, 0.01, 10

VMEM_LIMIT = 56 * 1024 * 1024
MESH = pl.DeviceIdType.MESH


def _params(sem=None, vmem=VMEM_LIMIT):
    return pltpu.CompilerParams(dimension_semantics=sem, vmem_limit_bytes=vmem)


def _full(shape):
    nd = len(shape)
    return pl.BlockSpec(shape, lambda *_: (0,) * nd)


def _dot(a, b, dims=(((1,), (0,)), ((), ())), exact=False):
    if exact:
        return lax.dot_general(a, b, dims, precision=HI, preferred_element_type=F32)
    return lax.dot_general(a.astype(MXU_DTYPE), b.astype(MXU_DTYPE), dims, preferred_element_type=F32)


NN = (((1,), (0,)), ((), ()))
NT = (((1,), (1,)), ((), ()))
TN = (((0,), (0,)), ((), ()))


def _gelu(x):
    c = math.sqrt(2.0 / math.pi)
    return 0.5 * x * (1.0 + jnp.tanh(c * (x + 0.044715 * x * x * x)))


def _gelu_grad(x):
    c = math.sqrt(2.0 / math.pi)
    t = jnp.tanh(c * (x + 0.044715 * x * x * x))
    return 0.5 * (1.0 + t) + 0.5 * x * (1.0 - t * t) * c * (1.0 + 3.0 * 0.044715 * x * x)


def _sigmoid(x):
    return 1.0 / (1.0 + jnp.exp(-x))


def _relu2(x):
    r = jnp.maximum(x, 0.0)
    return r * r


def _matmul(a, b, *, mode, name, a_fn=None, e=None, epi=None, out_dtype=F32, tm=1024, tn=1024, tk=512):
    if mode == "nn":
        (m, k), n = a.shape, b.shape[1]
    elif mode == "nt":
        (m, k), n = a.shape, b.shape[0]
    else:
        (k, m), n = a.shape, b.shape[1]
    tm, tn, tk = min(tm, m), min(tn, n), min(tk, k)
    assert m % tm == 0 and n % tn == 0 and k % tk == 0, (name, m, n, k, tm, tn, tk)
    nk = k // tk
    dims = {"nn": NN, "nt": NT, "tn": TN}[mode]
    a_spec = pl.BlockSpec((tk, tm), lambda i, j, l: (l, i)) if mode == "tn" else pl.BlockSpec((tm, tk), lambda i, j, l: (i, l))
    b_spec = pl.BlockSpec((tn, tk), lambda i, j, l: (j, l)) if mode == "nt" else pl.BlockSpec((tk, tn), lambda i, j, l: (l, j))
    o_spec = pl.BlockSpec((tm, tn), lambda i, j, l: (i, j))

    def body(*refs):
        if e is None:
            a_ref, b_ref, o_ref, acc = refs
            e_ref = None
        else:
            a_ref, b_ref, e_ref, o_ref, acc = refs
        l = pl.program_id(2)

        @pl.when(l == 0)
        def _():
            acc[...] = jnp.zeros_like(acc)

        at = a_ref[...]
        if a_fn is not None:
            at = a_fn(at.astype(F32))
        acc[...] += _dot(at, b_ref[...], dims)

        @pl.when(l == nk - 1)
        def _():
            r = acc[...]
            if epi is not None:
                r = epi(r, None if e_ref is None else e_ref[...])
            o_ref[...] = r.astype(out_dtype)

    ins, specs = [a, b], [a_spec, b_spec]
    if e is not None:
        ins.append(e)
        specs.append(o_spec)
    return pl.pallas_call(
        body, name=name, grid=(m // tm, n // tn, nk), in_specs=specs, out_specs=o_spec,
        out_shape=jax.ShapeDtypeStruct((m, n), out_dtype), scratch_shapes=[pltpu.VMEM((tm, tn), F32)],
        compiler_params=_params(("parallel", "parallel", "arbitrary")))(*ins)


def _ln_fwd(h, g, b, *, name, tr=256):
    t, d = h.shape

    def body(h_ref, g_ref, b_ref, o_ref):
        x = h_ref[...]
        mu = jnp.mean(x, axis=-1, keepdims=True)
        xc = x - mu
        var = jnp.mean(xc * xc, axis=-1, keepdims=True)
        o_ref[...] = xc * lax.rsqrt(var + LN_EPS) * g_ref[...] + b_ref[...]

    row = pl.BlockSpec((tr, d), lambda i: (i, 0))
    return pl.pallas_call(body, name=name, grid=(t // tr,), in_specs=[row, _full((1, d)), _full((1, d))], out_specs=row,
                          out_shape=jax.ShapeDtypeStruct((t, d), F32), compiler_params=_params(("parallel",)))(h, g, b)


def _ln_bwd(dy, h, g, *, name, tr=256):
    t, d = h.shape

    def body(dy_ref, h_ref, g_ref, dh_ref, dg_ref, db_ref):
        @pl.when(pl.program_id(0) == 0)
        def _():
            dg_ref[...] = jnp.zeros_like(dg_ref)
            db_ref[...] = jnp.zeros_like(db_ref)

        x, dyv = h_ref[...], dy_ref[...]
        mu = jnp.mean(x, axis=-1, keepdims=True)
        xc = x - mu
        rstd = lax.rsqrt(jnp.mean(xc * xc, axis=-1, keepdims=True) + LN_EPS)
        xh = xc * rstd
        w = dyv * g_ref[...]
        dh_ref[...] = rstd * (w - jnp.mean(w, axis=-1, keepdims=True) - xh * jnp.mean(w * xh, axis=-1, keepdims=True))
        dg_ref[...] += jnp.sum(dyv * xh, axis=0, keepdims=True)
        db_ref[...] += jnp.sum(dyv, axis=0, keepdims=True)

    row = pl.BlockSpec((tr, d), lambda i: (i, 0))
    vec = _full((1, d))
    return pl.pallas_call(
        body, name=name, grid=(t // tr,), in_specs=[row, row, vec], out_specs=[row, vec, vec],
        out_shape=[jax.ShapeDtypeStruct((t, d), F32), jax.ShapeDtypeStruct((1, d), F32), jax.ShapeDtypeStruct((1, d), F32)],
        compiler_params=_params(("arbitrary",)))(dy, h, g)


def _loss_head(y, target, *, tr=256):
    t, d = y.shape

    def body(y_ref, t_ref, dy_ref, s_ref):
        @pl.when(pl.program_id(0) == 0)
        def _():
            s_ref[...] = jnp.zeros_like(s_ref)

        err = y_ref[...] - t_ref[...]
        dy_ref[...] = err * (1.0 / d)
        s_ref[...] += jnp.sum(err * err, axis=0, keepdims=True)

    row = pl.BlockSpec((tr, d), lambda i: (i, 0))
    return pl.pallas_call(
        body, name="loss_head", grid=(t // tr,), in_specs=[row, row], out_specs=[row, _full((1, d))],
        out_shape=[jax.ShapeDtypeStruct((t, d), F32), jax.ShapeDtypeStruct((1, d), F32)],
        compiler_params=_params(("arbitrary",)))(y, target)


def _cols(tb, width, cb):
    return pl.BlockSpec((tb, width), lambda i: (i, cb))


def _cols_rev(tb, width, cb, nb):
    return pl.BlockSpec((tb, width), lambda i: (nb - 1 - i, cb))


POOL_HALO = 16


def _pool_fwd(proj, w, scale, *, cb, tb=512):
    t = proj.shape[0]
    tb = min(tb, t)

    def body(p_ref, w_ref, s_ref, o_ref, pooled_ref, ext):
        i = pl.program_id(0)

        @pl.when(i == 0)
        def _():
            ext[pl.ds(0, POOL_HALO), :] = jnp.zeros((POOL_HALO, GW), F32)

        p = p_ref[...]
        ext[pl.ds(POOL_HALO, tb), :] = p
        pos = (i * tb + lax.broadcasted_iota(jnp.int32, (tb, 1), 0) + 1).astype(F32)
        for gi, win in enumerate(POOL_WINDOWS):
            c0 = gi * POOL_GD
            s = p[:, c0:c0 + POOL_GD]
            for k in range(1, win):
                s = s + ext[pl.ds(POOL_HALO - k, tb), pl.ds(c0, POOL_GD)]
            pooled = s / jnp.minimum(pos, float(win)) - p[:, c0:c0 + POOL_GD]
            pooled_ref[:, pl.ds(c0, POOL_GD)] = pooled
            o_ref[:, pl.ds(c0, POOL_GD)] = _dot(pooled, w_ref[gi]) * s_ref[:, pl.ds(c0, POOL_GD)]
        ext[pl.ds(0, POOL_HALO), :] = p[tb - POOL_HALO:, :]

    row = pl.BlockSpec((tb, GW), lambda i: (i, 0))
    return pl.pallas_call(
        body, name="pool_fwd", grid=(t // tb,),
        in_specs=[_cols(tb, GW, cb), _full((4, POOL_GD, POOL_GD)), _full((1, GW))], out_specs=[row, row],
        out_shape=[jax.ShapeDtypeStruct((t, GW), F32)] * 2, scratch_shapes=[pltpu.VMEM((tb + POOL_HALO, GW), F32)],
        compiler_params=_params(("arbitrary",)))(proj, w, scale)


def _pool_bwd(dmixed, pooled, w, scale, *, cb, tb=512):
    t = pooled.shape[0]
    tb = min(tb, t)
    nb = t // tb

    def body(dy_ref, pooled_ref, w_ref, s_ref, dp_ref, dw_ref, ds_ref, ext):
        i = pl.program_id(0)

        @pl.when(i == 0)
        def _():
            ext[pl.ds(tb, POOL_HALO), :] = jnp.zeros((POOL_HALO, GW), F32)
            dw_ref[...] = jnp.zeros_like(dw_ref)
            ds_ref[...] = jnp.zeros_like(ds_ref)

        dy = dy_ref[...]
        pos = ((nb - 1 - i) * tb + lax.broadcasted_iota(jnp.int32, (tb, 1), 0) + 1).astype(F32)
        dpool_all = []
        for gi, win in enumerate(POOL_WINDOWS):
            c0 = gi * POOL_GD
            pg = pooled_ref[:, pl.ds(c0, POOL_GD)]
            dyg = dy[:, c0:c0 + POOL_GD]
            ds_ref[:, pl.ds(c0, POOL_GD)] += jnp.sum(dyg * _dot(pg, w_ref[gi]), axis=0, keepdims=True)
            dmp = dyg * s_ref[:, pl.ds(c0, POOL_GD)]
            dw_ref[gi] += _dot(pg, dmp, TN)
            dpool = _dot(dmp, w_ref[gi], NT)
            dpool_all.append(dpool)
            ext[pl.ds(0, tb), pl.ds(c0, POOL_GD)] = dpool / jnp.minimum(pos, float(win))
        for gi, win in enumerate(POOL_WINDOWS):
            c0 = gi * POOL_GD
            s = ext[pl.ds(0, tb), pl.ds(c0, POOL_GD)]
            for k in range(1, win):
                s = s + ext[pl.ds(k, tb), pl.ds(c0, POOL_GD)]
            dp_ref[:, pl.ds(c0, POOL_GD)] = s - dpool_all[gi]
        ext[pl.ds(tb, POOL_HALO), :] = ext[pl.ds(0, POOL_HALO), :]

    row = pl.BlockSpec((tb, GW), lambda i: (nb - 1 - i, 0))
    return pl.pallas_call(
        body, name="pool_bwd", grid=(nb,),
        in_specs=[_cols_rev(tb, GW, cb, nb), row, _full((4, POOL_GD, POOL_GD)), _full((1, GW))],
        out_specs=[row, _full((4, POOL_GD, POOL_GD)), _full((1, GW))],
        out_shape=[jax.ShapeDtypeStruct((t, GW), F32), jax.ShapeDtypeStruct((4, POOL_GD, POOL_GD), F32),
                   jax.ShapeDtypeStruct((1, GW), F32)],
        scratch_shapes=[pltpu.VMEM((tb + POOL_HALO, GW), F32)], compiler_params=_params(("arbitrary",)))(dmixed, pooled, w, scale)


def _sgu_core(zu, zv, ng, nb, w_ref, bias):
    tb = zu.shape[0]
    u = _gelu(zu)
    v0 = _gelu(zv)
    mu = jnp.mean(v0, axis=-1, keepdims=True)
    vc = v0 - mu
    rstd = lax.rsqrt(jnp.mean(vc * vc, axis=-1, keepdims=True) + LN_EPS)
    xh = vc * rstd
    vn = xh * ng + nb
    low = lax.broadcasted_iota(jnp.int32, (SGU_CHUNK, 2 * SGU_HD), 1) < SGU_HD
    rows = []
    for n in range(tb // SGU_CHUNK):
        pairs = []
        for j in range(SGU_HEADS // 2):
            vp = vn[n * SGU_CHUNK:(n + 1) * SGU_CHUNK, j * 128:(j + 1) * 128]
            pairs.append(jnp.where(low, _dot(w_ref[2 * j], vp), _dot(w_ref[2 * j + 1], vp)))
        rows.append(jnp.concatenate(pairs, axis=1) + bias)
    mixed = jnp.concatenate(rows, axis=0)
    return u, xh, rstd, vn, mixed


def _sgu_fwd(proj, ng, nb, w, bias, *, cbu, cbv, tb=512):
    t = proj.shape[0]
    tb = min(tb, t)

    def body(zu_ref, zv_ref, ng_ref, nb_ref, w_ref, bias_ref, o_ref):
        u, _, _, _, mixed = _sgu_core(zu_ref[...], zv_ref[...], ng_ref[...], nb_ref[...], w_ref, bias_ref[...])
        o_ref[...] = u * mixed

    vec = _full((1, GW))
    return pl.pallas_call(
        body, name="sgu_fwd", grid=(t // tb,),
        in_specs=[_cols(tb, GW, cbu), _cols(tb, GW, cbv), vec, vec, _full((8, 128, 128)), _full((128, GW))],
        out_specs=pl.BlockSpec((tb, GW), lambda i: (i, 0)), out_shape=jax.ShapeDtypeStruct((t, GW), F32),
        compiler_params=_params(("parallel",)))(proj, proj, ng, nb, w, bias)


def _sgu_bwd(dmixed, proj, ng, nb, w, wt, bias, *, cb, cbu, cbv, tb=512):
    t = proj.shape[0]
    tb = min(tb, t)

    def body(dy_ref, zu_ref, zv_ref, ng_ref, nb_ref, w_ref, wt_ref, bias_ref, dzu_ref, dzv_ref, dw_ref, dbias_ref, dng_ref, dnb_ref):
        @pl.when(pl.program_id(0) == 0)
        def _():
            dw_ref[...] = jnp.zeros_like(dw_ref)
            dbias_ref[...] = jnp.zeros_like(dbias_ref)
            dng_ref[...] = jnp.zeros_like(dng_ref)
            dnb_ref[...] = jnp.zeros_like(dnb_ref)

        zu, zv, dy = zu_ref[...], zv_ref[...], dy_ref[...]
        u, xh, rstd, vn, mixed = _sgu_core(zu, zv, ng_ref[...], nb_ref[...], w_ref, bias_ref[...])
        dzu_ref[...] = dy * mixed * _gelu_grad(zu)
        dmix = dy * u
        low = lax.broadcasted_iota(jnp.int32, (SGU_CHUNK, 2 * SGU_HD), 1) < SGU_HD
        dbias = jnp.zeros((SGU_CHUNK, GW), F32)
        rows = []
        for n in range(tb // SGU_CHUNK):
            dm = dmix[n * SGU_CHUNK:(n + 1) * SGU_CHUNK, :]
            dbias = dbias + dm
            pairs = []
            for j in range(SGU_HEADS // 2):
                dmp = dm[:, j * 128:(j + 1) * 128]
                vp = vn[n * SGU_CHUNK:(n + 1) * SGU_CHUNK, j * 128:(j + 1) * 128]
                dw_ref[2 * j] += _dot(jnp.where(low, dmp, 0.0), vp, NT)
                dw_ref[2 * j + 1] += _dot(jnp.where(low, 0.0, dmp), vp, NT)
                pairs.append(jnp.where(low, _dot(wt_ref[2 * j], dmp), _dot(wt_ref[2 * j + 1], dmp)))
            rows.append(jnp.concatenate(pairs, axis=1))
        dbias_ref[...] += dbias
        dvn = jnp.concatenate(rows, axis=0)
        dng_ref[...] += jnp.sum(dvn * xh, axis=0, keepdims=True)
        dnb_ref[...] += jnp.sum(dvn, axis=0, keepdims=True)
        wv = dvn * ng_ref[...]
        dv0 = rstd * (wv - jnp.mean(wv, axis=-1, keepdims=True) - xh * jnp.mean(wv * xh, axis=-1, keepdims=True))
        dzv_ref[...] = dv0 * _gelu_grad(zv)

    vec = _full((1, GW))
    row = pl.BlockSpec((tb, GW), lambda i: (i, 0))
    mat = _full((8, 128, 128))
    return pl.pallas_call(
        body, name="sgu_bwd", grid=(t // tb,),
        in_specs=[_cols(tb, GW, cb), _cols(tb, GW, cbu), _cols(tb, GW, cbv), vec, vec, mat, mat, _full((128, GW))],
        out_specs=[row, row, mat, _full((128, GW)), vec, vec],
        out_shape=[jax.ShapeDtypeStruct((t, GW), F32)] * 2 + [jax.ShapeDtypeStruct((8, 128, 128), F32),
                   jax.ShapeDtypeStruct((128, GW), F32), jax.ShapeDtypeStruct((1, GW), F32), jax.ShapeDtypeStruct((1, GW), F32)],
        compiler_params=_params(("arbitrary",)))(dmixed, proj, proj, ng, nb, w, wt, bias)


S5_KB = 4
SUB = 8


def _s5_discretize(lam_re, lam_im, log_step, b_re, b_im):
    step = jnp.exp(log_step)[:, None]
    mag = jnp.exp(lam_re * step)
    lr, li = mag * jnp.cos(lam_im * step), mag * jnp.sin(lam_im * step)
    den = lam_re * lam_re + lam_im * lam_im
    fr = ((lr - 1.0) * lam_re + li * lam_im) / den
    fi = (li * lam_re - (lr - 1.0) * lam_im) / den
    return lr, li, fr[:, :, None] * b_re - fi[:, :, None] * b_im, fr[:, :, None] * b_im + fi[:, :, None] * b_re


def _cpow(lr, li, n):
    rr, ri = lr, li
    for _ in range(n - 1):
        rr, ri = rr * lr - ri * li, rr * li + ri * lr
    return rr, ri


def _s5_scan_consts(lr, li, reverse):
    lr, li = lr.reshape(1, S5_N), (-li if reverse else li).reshape(1, S5_N)
    row = jnp.arange(SUB)[:, None]
    out = []
    for s in (1, 2, 4):
        pr, pi = _cpow(lr, li, s)
        keep = (row < SUB - s) if reverse else (row >= s)
        out += [jnp.where(keep, pr, 0.0), jnp.where(keep, pi, 0.0)]
    cr, ci = [], []
    for i in range(SUB):
        pr, pi = _cpow(lr, li, SUB - i if reverse else i + 1)
        cr.append(pr)
        ci.append(pi)
    out += [jnp.concatenate(cr, axis=0), jnp.concatenate(ci, axis=0)]
    return jnp.stack(out)


def _s5_blockdiag_in(b):
    bt = jnp.swapaxes(b, 1, 2).reshape(S5_KB, 8, S5_H, S5_P)
    eye = jnp.eye(8, dtype=b.dtype)
    return jnp.einsum("kghp,gj->kghjp", bt, eye).reshape(S5_KB, 128, 512)


def _s5_blockdiag_in_extract(bb):
    x = bb.reshape(S5_KB, 8, S5_H, 8, S5_P)
    d = jnp.einsum("kghgp->kghp", x).reshape(S5_G, S5_H, S5_P)
    return jnp.swapaxes(d, 1, 2)


def _s5_blockdiag_out(c):
    ct = jnp.swapaxes(c, 1, 2).reshape(S5_KB, 8, S5_P, S5_H)
    eye = jnp.eye(8, dtype=c.dtype)
    return jnp.einsum("kgph,gj->kgpjh", ct, eye).reshape(S5_KB, 512, 128)


def _s5_blockdiag_out_extract(cc):
    x = cc.reshape(S5_KB, 8, S5_P, 8, S5_H)
    d = jnp.einsum("kgpgh->kgph", x).reshape(S5_G, S5_P, S5_H)
    return jnp.swapaxes(d, 1, 2)


def _s5_tile_scan(a, b, c_ref, carry, reverse):
    for si, s in enumerate((1, 2, 4)):
        sh = SUB - s if reverse else s
        ar, br = pltpu.roll(a, sh, 0), pltpu.roll(b, sh, 0)
        mr, mi = c_ref[2 * si], c_ref[2 * si + 1]
        a, b = a + mr * ar - mi * br, b + mr * br + mi * ar
    pr, pi = c_ref[6], c_ref[7]
    cr, ci = carry
    return a + pr * cr - pi * ci, b + pr * ci + pi * cr


def _s5_readout(xre_ref, xim_ref, ccre_ref, ccim_ref):
    return jnp.concatenate(
        [_dot(xre_ref[:, pl.ds(512 * k, 512)], ccre_ref[k], exact=True) - _dot(xim_ref[:, pl.ds(512 * k, 512)], ccim_ref[k], exact=True)
         for k in range(S5_KB)], axis=1)


def _s5_fwd(proj, bbre, bbim, ccre, ccim, dvec, consts, glu_w, glu_b, *, cb, tb=256):
    t = proj.shape[0]
    tb = min(tb, t)
    nt = tb // SUB

    def body(u_ref, bbre_ref, bbim_ref, ccre_ref, ccim_ref, d_ref, c_ref, w_ref, b_ref, o_ref, xre_ref, xim_ref, car):
        @pl.when(pl.program_id(0) == 0)
        def _():
            car[...] = jnp.zeros_like(car)

        u = u_ref[...]
        for k in range(S5_KB):
            uk = u[:, 128 * k:128 * (k + 1)]
            xre_ref[:, pl.ds(512 * k, 512)] = _dot(uk, bbre_ref[k], exact=True)
            xim_ref[:, pl.ds(512 * k, 512)] = _dot(uk, bbim_ref[k], exact=True)

        def tile(r, carry):
            sl = pl.ds(pl.multiple_of(r * SUB, SUB), SUB)
            a, b = _s5_tile_scan(xre_ref[sl, :], xim_ref[sl, :], c_ref, carry, False)
            xre_ref[sl, :] = a
            xim_ref[sl, :] = b
            return a[SUB - 1:SUB, :], b[SUB - 1:SUB, :]

        cr, ci = lax.fori_loop(0, nt, tile, (car[0:1, :], car[1:2, :]))
        car[0:1, :] = cr
        car[1:2, :] = ci
        ys = _s5_readout(xre_ref, xim_ref, ccre_ref, ccim_ref) + d_ref[...] * u
        yg = _gelu(ys)
        o_ref[...] = yg * _sigmoid(_dot(yg, w_ref[...]) + b_ref[...])

    row = pl.BlockSpec((tb, GW), lambda i: (i, 0))
    xrow = pl.BlockSpec((tb, S5_N), lambda i: (i, 0))
    vec = _full((1, GW))
    return pl.pallas_call(
        body, name="s5_fwd", grid=(t // tb,),
        in_specs=[_cols(tb, GW, cb), _full((4, 128, 512)), _full((4, 128, 512)), _full((4, 512, 128)), _full((4, 512, 128)),
                  vec, _full((8, SUB, S5_N)), _full((GW, GW)), vec],
        out_specs=[row, xrow, xrow],
        out_shape=[jax.ShapeDtypeStruct((t, GW), F32), jax.ShapeDtypeStruct((t, S5_N), F32), jax.ShapeDtypeStruct((t, S5_N), F32)],
        scratch_shapes=[pltpu.VMEM((SUB, S5_N), F32)], compiler_params=_params(("arbitrary",)))(
            proj, bbre, bbim, ccre, ccim, dvec, consts, glu_w, glu_b)


def _s5_bwd(dmixed, proj, xre, xim, bbre, bbim, ccre, ccim, dvec, consts, glu_w, glu_b, *, cb_dy, cb, tb=256):
    t = proj.shape[0]
    tb = min(tb, t)
    nb = t // tb
    nt = tb // SUB

    def body(dy_ref, u_ref, xre_ref, xim_ref, bbre_ref, bbim_ref, ccre_ref, ccim_ref, d_ref, c_ref, w_ref, b_ref,
             du_ref, dw_ref, db_ref, dd_ref, dccre_ref, dccim_ref, dbbre_ref, dbbim_ref, sre_ref, sim_ref, are, aim, car):
        @pl.when(pl.program_id(0) == 0)
        def _():
            car[...] = jnp.zeros_like(car)
            for r in (dw_ref, db_ref, dd_ref, dccre_ref, dccim_ref, dbbre_ref, dbbim_ref, sre_ref, sim_ref):
                r[...] = jnp.zeros_like(r)

        u, dy = u_ref[...], dy_ref[...]
        ys = _s5_readout(xre_ref, xim_ref, ccre_ref, ccim_ref) + d_ref[...] * u
        yg = _gelu(ys)
        sg = _sigmoid(_dot(yg, w_ref[...]) + b_ref[...])
        dz = dy * yg * sg * (1.0 - sg)
        dyg = dy * sg + _dot(dz, w_ref[...], NT)
        dw_ref[...] += _dot(yg, dz, TN)
        db_ref[...] += jnp.sum(dz, axis=0, keepdims=True)
        dys = dyg * _gelu_grad(ys)
        dd_ref[...] += jnp.sum(dys * u, axis=0, keepdims=True)
        for k in range(S5_KB):
            dk = dys[:, 128 * k:128 * (k + 1)]
            lanes = pl.ds(512 * k, 512)
            are[:, lanes] = _dot(dk, ccre_ref[k], NT, exact=True)
            aim[:, lanes] = -_dot(dk, ccim_ref[k], NT, exact=True)
            dccre_ref[k] += _dot(xre_ref[:, lanes], dk, TN, exact=True)
            dccim_ref[k] -= _dot(xim_ref[:, lanes], dk, TN, exact=True)

        def tile(j, carry):
            sl = pl.ds(pl.multiple_of((nt - 1 - j) * SUB, SUB), SUB)
            gr, gi = are[sl, :], aim[sl, :]
            a, b = _s5_tile_scan(gr, gi, c_ref, carry, True)
            are[sl, :] = a
            aim[sl, :] = b
            er, ei = a - gr, b - gi
            xr, xi = xre_ref[sl, :], xim_ref[sl, :]
            sre_ref[...] += xr * er + xi * ei
            sim_ref[...] += xr * ei - xi * er
            return a[0:1, :], b[0:1, :]

        cr, ci = lax.fori_loop(0, nt, tile, (car[0:1, :], car[1:2, :]))
        car[0:1, :] = cr
        car[1:2, :] = ci
        dus = []
        for k in range(S5_KB):
            uk = u[:, 128 * k:128 * (k + 1)]
            lanes = pl.ds(512 * k, 512)
            dbbre_ref[k] += _dot(uk, are[:, lanes], TN, exact=True)
            dbbim_ref[k] += _dot(uk, aim[:, lanes], TN, exact=True)
            dus.append(_dot(are[:, lanes], bbre_ref[k], NT, exact=True) + _dot(aim[:, lanes], bbim_ref[k], NT, exact=True))
        du_ref[...] = d_ref[...] * dys + jnp.concatenate(dus, axis=1)

    row = pl.BlockSpec((tb, GW), lambda i: (nb - 1 - i, 0))
    xrow = pl.BlockSpec((tb, S5_N), lambda i: (nb - 1 - i, 0))
    vec = _full((1, GW))
    bbs, ccs = _full((4, 128, 512)), _full((4, 512, 128))
    sds = jax.ShapeDtypeStruct
    return pl.pallas_call(
        body, name="s5_bwd", grid=(nb,),
        in_specs=[_cols_rev(tb, GW, cb_dy, nb), _cols_rev(tb, GW, cb, nb), xrow, xrow, bbs, bbs, ccs, ccs, vec,
                  _full((8, SUB, S5_N)), _full((GW, GW)), vec],
        out_specs=[row, _full((GW, GW)), vec, vec, ccs, ccs, bbs, bbs, _full((SUB, S5_N)), _full((SUB, S5_N))],
        out_shape=[sds((t, GW), F32), sds((GW, GW), F32), sds((1, GW), F32), sds((1, GW), F32), sds((4, 512, 128), F32),
                   sds((4, 512, 128), F32), sds((4, 128, 512), F32), sds((4, 128, 512), F32), sds((SUB, S5_N), F32),
                   sds((SUB, S5_N), F32)],
        scratch_shapes=[pltpu.VMEM((tb, S5_N), F32), pltpu.VMEM((tb, S5_N), F32), pltpu.VMEM((SUB, S5_N), F32)],
        compiler_params=_params(("arbitrary",)))(dmixed, proj, xre, xim, bbre, bbim, ccre, ccim, dvec, consts, glu_w, glu_b)


def _s5_prepare(lam_re, lam_im, log_step, b_re, b_im, c_re, c_im):
    lr, li, bbr, bbi = _s5_discretize(lam_re, lam_im, log_step, b_re, b_im)
    return dict(bbre=_s5_blockdiag_in(bbr), bbim=_s5_blockdiag_in(bbi), ccre=_s5_blockdiag_out(c_re), ccim=_s5_blockdiag_out(c_im),
                cf=_s5_scan_consts(lr, li, False), cr=_s5_scan_consts(lr, li, True))


def _s5_param_grads(lam_re, lam_im, log_step, b_re, b_im, dbbre, dbbim, dccre, dccim, sre, sim):
    (lr, li, _, _), vjp = jax.vjp(_s5_discretize, lam_re, lam_im, log_step, b_re, b_im)
    sr, si = jnp.sum(sre, axis=0).reshape(S5_G, S5_P), jnp.sum(sim, axis=0).reshape(S5_G, S5_P)
    den = lr * lr + li * li
    glr, gli = (sr * lr - si * li) / den, (si * lr + sr * li) / den
    dlam_re, dlam_im, dlog_step, db_re, db_im = vjp((glr, gli, _s5_blockdiag_in_extract(dbbre), _s5_blockdiag_in_extract(dbbim)))
    return dlam_re, dlam_im, dlog_step, db_re, db_im, _s5_blockdiag_out_extract(dccre), _s5_blockdiag_out_extract(dccim)


HALO = 8
AB_CB = 4096 // 128
Q_SCALE = DN_HD ** -0.5


def _halo_prev(tb, width, cb):
    return pl.BlockSpec((HALO, width), lambda i: (jnp.maximum(i * (tb // HALO) - 1, 0), cb))


def _halo_next(tb, width, cb, nrows):
    last = nrows // HALO - 1
    return pl.BlockSpec((HALO, width), lambda i: (jnp.minimum((i + 1) * (tb // HALO), last), cb))


def _silu_parts(c):
    sg = _sigmoid(c)
    return c * sg, sg * (1.0 + c * (1.0 - sg))


def _softplus(x):
    return jnp.maximum(x, 0.0) + jnp.log(1.0 + jnp.exp(-jnp.abs(x)))


def _dn_conv(x_ref, halo_ref, w_ref, part, ext, first):
    tb = x_ref.shape[0]
    ext[pl.ds(0, HALO), :] = jnp.where(first, 0.0, halo_ref[...])
    ext[pl.ds(HALO, tb), :] = x_ref[...]
    c = None
    for j in range(DN_CONV):
        term = w_ref[pl.ds(j, 1), pl.ds(512 * part, 512)] * ext[pl.ds(HALO - (DN_CONV - 1) + j, tb), :]
        c = term if c is None else c + term
    return c


def _dn_gb(ab, alog, dtb):
    lane = lax.broadcasted_iota(jnp.int32, ab.shape, 1)
    pre = ab + dtb
    g = -jnp.exp(alog) * _softplus(pre)
    beta = _sigmoid(ab)
    return jnp.where(lane < DN_HEADS, g, jnp.where(lane < 2 * DN_HEADS, beta, 0.0)), pre, beta


def _dn_prep_fwd(proj, conv_w, alog, dtb, *, cbq, tb=512):
    t = proj.shape[0]
    tb = min(tb, t)

    def body(xq, xk, xv, hq, hk, hv, ab_ref, w_ref, alog_ref, dtb_ref, qn_ref, kn_ref, vs_ref, gb_ref, ext):
        first = pl.program_id(0) == 0
        for part, (x_ref, h_ref, o_ref) in enumerate(((xq, hq, qn_ref), (xk, hk, kn_ref), (xv, hv, vs_ref))):
            s, _ = _silu_parts(_dn_conv(x_ref, h_ref, w_ref, part, ext, first))
            if part < 2:
                scale = Q_SCALE if part == 0 else 1.0
                for h in range(DN_HEADS):
                    sh = s[:, DN_HD * h:DN_HD * (h + 1)]
                    rn = lax.rsqrt(jnp.sum(sh * sh, axis=-1, keepdims=True) + L2_EPS)
                    o_ref[:, pl.ds(DN_HD * h, DN_HD)] = sh * (rn * scale)
            else:
                o_ref[...] = s
        gb_ref[...] = _dn_gb(ab_ref[...], alog_ref[...], dtb_ref[...])[0]

    row = pl.BlockSpec((tb, GW), lambda i: (i, 0))
    small = pl.BlockSpec((tb, 128), lambda i: (i, 0))
    v128 = _full((1, 128))
    sds = jax.ShapeDtypeStruct
    return pl.pallas_call(
        body, name="dn_prep_fwd", grid=(t // tb,),
        in_specs=[_cols(tb, GW, cbq), _cols(tb, GW, cbq + 1), _cols(tb, GW, cbq + 2),
                  _halo_prev(tb, GW, cbq), _halo_prev(tb, GW, cbq + 1), _halo_prev(tb, GW, cbq + 2),
                  _cols(tb, 128, AB_CB), _full((DN_CONV, 3 * GW)), v128, v128],
        out_specs=[row, row, row, small],
        out_shape=[sds((t, GW), F32)] * 3 + [sds((t, 128), F32)],
        scratch_shapes=[pltpu.VMEM((tb + HALO, GW), F32)], compiler_params=_params(("parallel",)))(
            proj, proj, proj, proj, proj, proj, proj, conv_w, alog, dtb)


def _dn_prep_bwd_a(proj, conv_w, alog, dtb, dqn, dkn, dvs, dgb, *, cbq, tb=512):
    t = proj.shape[0]
    tb = min(tb, t)

    def body(xq, xk, xv, hq, hk, hv, ab_ref, w_ref, alog_ref, dtb_ref, dqn_ref, dkn_ref, dvs_ref, dgb_ref,
             dcq_ref, dck_ref, dcv_ref, dab_ref, dalog_ref, ddtb_ref, ext):
        first = pl.program_id(0) == 0

        @pl.when(first)
        def _():
            dalog_ref[...] = jnp.zeros_like(dalog_ref)
            ddtb_ref[...] = jnp.zeros_like(ddtb_ref)

        for part, (x_ref, h_ref, d_ref, o_ref) in enumerate(((xq, hq, dqn_ref, dcq_ref), (xk, hk, dkn_ref, dck_ref), (xv, hv, dvs_ref, dcv_ref))):
            s, ds_dc = _silu_parts(_dn_conv(x_ref, h_ref, w_ref, part, ext, first))
            d = d_ref[...]
            if part < 2:
                scale = Q_SCALE if part == 0 else 1.0
                for h in range(DN_HEADS):
                    lanes = slice(DN_HD * h, DN_HD * (h + 1))
                    sh, dh = s[:, lanes], d[:, lanes]
                    rn = lax.rsqrt(jnp.sum(sh * sh, axis=-1, keepdims=True) + L2_EPS)
                    dsh = scale * (rn * dh - sh * (rn * rn * rn) * jnp.sum(dh * sh, axis=-1, keepdims=True))
                    o_ref[:, pl.ds(DN_HD * h, DN_HD)] = dsh * ds_dc[:, lanes]
            else:
                o_ref[...] = d * ds_dc
        ab, dgb_v = ab_ref[...], dgb_ref[...]
        gb, pre, beta = _dn_gb(ab, alog_ref[...], dtb_ref[...])
        lane = lax.broadcasted_iota(jnp.int32, ab.shape, 1)
        is_g = lane < DN_HEADS
        da = jnp.where(is_g, dgb_v * (-jnp.exp(alog_ref[...])) * _sigmoid(pre), 0.0)
        db = jnp.where((lane >= DN_HEADS) & (lane < 2 * DN_HEADS), dgb_v * beta * (1.0 - beta), 0.0)
        dab_ref[...] = da + db
        ddtb_ref[...] += jnp.sum(da, axis=0, keepdims=True)
        dalog_ref[...] += jnp.sum(jnp.where(is_g, dgb_v * gb, 0.0), axis=0, keepdims=True)

    row = pl.BlockSpec((tb, GW), lambda i: (i, 0))
    small = pl.BlockSpec((tb, 128), lambda i: (i, 0))
    v128 = _full((1, 128))
    sds = jax.ShapeDtypeStruct
    return pl.pallas_call(
        body, name="dn_prep_bwd_a", grid=(t // tb,),
        in_specs=[_cols(tb, GW, cbq), _cols(tb, GW, cbq + 1), _cols(tb, GW, cbq + 2),
                  _halo_prev(tb, GW, cbq), _halo_prev(tb, GW, cbq + 1), _halo_prev(tb, GW, cbq + 2),
                  _cols(tb, 128, AB_CB), _full((DN_CONV, 3 * GW)), v128, v128, row, row, row, small],
        out_specs=[row, row, row, small, v128, v128],
        out_shape=[sds((t, GW), F32)] * 3 + [sds((t, 128), F32), sds((1, 128), F32), sds((1, 128), F32)],
        scratch_shapes=[pltpu.VMEM((tb + HALO, GW), F32)], compiler_params=_params(("arbitrary",)))(
            proj, proj, proj, proj, proj, proj, proj, conv_w, alog, dtb, dqn, dkn, dvs, dgb)


def _dn_prep_bwd_b(proj, conv_w, dcq, dck, dcv, *, cbq, tb=512):
    t = proj.shape[0]
    tb = min(tb, t)
    nb = t // tb

    def body(xq, xk, xv, hq, hk, hv, dq_in, dk_in, dv_in, nq, nk, nv, w_ref, dq_ref, dk_ref, dv_ref, dw_ref, ext):
        i = pl.program_id(0)
        first, last = i == 0, i == nb - 1

        @pl.when(first)
        def _():
            dw_ref[...] = jnp.zeros_like(dw_ref)

        for part, (x_ref, h_ref, d_ref, n_ref, o_ref) in enumerate(
                ((xq, hq, dq_in, nq, dq_ref), (xk, hk, dk_in, nk, dk_ref), (xv, hv, dv_in, nv, dv_ref))):
            lanes = pl.ds(512 * part, 512)
            d = d_ref[...]
            ext[pl.ds(0, HALO), :] = jnp.where(first, 0.0, h_ref[...])
            ext[pl.ds(HALO, tb), :] = x_ref[...]
            for j in range(DN_CONV):
                xs = ext[pl.ds(HALO - (DN_CONV - 1) + j, tb), :]
                dw_ref[pl.ds(j, 1), lanes] += jnp.sum(d * xs, axis=0, keepdims=True)
            ext[pl.ds(0, tb), :] = d
            ext[pl.ds(tb, HALO), :] = jnp.where(last, 0.0, n_ref[...])
            acc = None
            for j in range(DN_CONV):
                term = w_ref[pl.ds(j, 1), lanes] * ext[pl.ds(DN_CONV - 1 - j, tb), :]
                acc = term if acc is None else acc + term
            o_ref[...] = acc

    row = pl.BlockSpec((tb, GW), lambda i: (i, 0))
    nxt = _halo_next(tb, GW, 0, t)
    sds = jax.ShapeDtypeStruct
    return pl.pallas_call(
        body, name="dn_prep_bwd_b", grid=(nb,),
        in_specs=[_cols(tb, GW, cbq), _cols(tb, GW, cbq + 1), _cols(tb, GW, cbq + 2),
                  _halo_prev(tb, GW, cbq), _halo_prev(tb, GW, cbq + 1), _halo_prev(tb, GW, cbq + 2),
                  row, row, row, nxt, nxt, nxt, _full((DN_CONV, 3 * GW))],
        out_specs=[row, row, row, _full((DN_CONV, 3 * GW))],
        out_shape=[sds((t, GW), F32)] * 3 + [sds((DN_CONV, 3 * GW), F32)],
        scratch_shapes=[pltpu.VMEM((tb + HALO, GW), F32)], compiler_params=_params(("arbitrary",)))(
            proj, proj, proj, proj, proj, proj, dcq, dck, dcv, dcq, dck, dcv, conv_w)


def _tri_inv(n, eye):
    x = eye - n
    p = _dot(n, n, exact=True)
    for _ in range(4):
        x = x + _dot(x, p, exact=True)
        p = _dot(p, p, exact=True)
    return x + _dot(x, p, exact=True)


def _dn_masks():
    r = lax.broadcasted_iota(jnp.int32, (DN_CHUNK, DN_CHUNK), 0)
    c = lax.broadcasted_iota(jnp.int32, (DN_CHUNK, DN_CHUNK), 1)
    return r == c, r >= c, r > c, r <= c


def _to_row(col, eye):
    return jnp.sum(jnp.where(eye, col, 0.0), axis=0, keepdims=True)


def _dn_chunk_math(q, k, v, g, beta, s, masks):
    eye, tril, strict, triu = masks
    g_row = _to_row(g, eye)
    gc = jnp.sum(jnp.where(tril, g_row, 0.0), axis=1, keepdims=True)
    gc_row = jnp.sum(jnp.where(triu, g, 0.0), axis=0, keepdims=True)
    decay = jnp.where(tril, jnp.exp(jnp.minimum(gc - gc_row, 0.0)), 0.0)
    eg = jnp.exp(gc)
    gl = gc[DN_CHUNK - 1:DN_CHUNK, :]
    eg_last = jnp.exp(gl)
    etail = jnp.exp(gl - gc)
    kb, vb = k * beta, v * beta
    kk = _dot(kb, k, NT)
    tm = _tri_inv(jnp.where(strict, kk * decay, 0.0), eye.astype(F32))
    kbg = kb * eg
    u = _dot(tm, vb)
    wm = _dot(tm, kbg)
    qk = _dot(q, k, NT)
    qkd = qk * decay
    qg, ktail = q * eg, k * etail
    v_new = u - _dot(wm, s)
    return dict(decay=decay, eg=eg, eg_last=eg_last, etail=etail, kb=kb, vb=vb, kk=kk, tm=tm, kbg=kbg, wm=wm, qk=qk, qkd=qkd,
                qg=qg, ktail=ktail, v_new=v_new)


def _dn_chunk_fwd(qn, kn, vs, gb, proj, norm_g, *, cb_gate):
    t = qn.shape[0]
    nc = t // DN_CHUNK

    def body(q_ref, k_ref, v_ref, gb_ref, gate_ref, ng_ref, o_ref, raw_ref, st_ref, s_ref):
        @pl.when(pl.program_id(0) == 0)
        def _():
            s_ref[...] = jnp.zeros_like(s_ref)

        masks = _dn_masks()
        gbv = gb_ref[...]
        for h in range(DN_HEADS):
            lanes = pl.ds(DN_HD * h, DN_HD)
            s = s_ref[h]
            st_ref[0, h] = s
            m = _dn_chunk_math(q_ref[:, lanes], k_ref[:, lanes], v_ref[:, lanes], gbv[:, h:h + 1],
                               gbv[:, DN_HEADS + h:DN_HEADS + h + 1], s, masks)
            o = _dot(m["qg"], s) + _dot(m["qkd"], m["v_new"])
            s_ref[h] = s * m["eg_last"] + _dot(m["ktail"], m["v_new"], TN)
            raw_ref[:, lanes] = o
            r = lax.rsqrt(jnp.mean(o * o, axis=-1, keepdims=True) + RMS_EPS)
            gt = gate_ref[:, lanes]
            o_ref[:, lanes] = o * r * ng_ref[...] * (gt * _sigmoid(gt))

    row = pl.BlockSpec((DN_CHUNK, GW), lambda i: (i, 0))
    sds = jax.ShapeDtypeStruct
    return pl.pallas_call(
        body, name="dn_chunk_fwd", grid=(nc,),
        in_specs=[row, row, row, pl.BlockSpec((DN_CHUNK, 128), lambda i: (i, 0)), _cols(DN_CHUNK, GW, cb_gate), _full((1, DN_HD))],
        out_specs=[row, row, pl.BlockSpec((1, DN_HEADS, DN_HD, DN_HD), lambda i: (i, 0, 0, 0))],
        out_shape=[sds((t, GW), F32), sds((t, GW), F32), sds((nc, DN_HEADS, DN_HD, DN_HD), F32)],
        scratch_shapes=[pltpu.VMEM((DN_HEADS, DN_HD, DN_HD), F32)], compiler_params=_params(("arbitrary",)))(
            qn, kn, vs, gb, proj, norm_g)


def _dn_chunk_bwd(dmixed, qn, kn, vs, gb, proj, raw, states, norm_g, *, cb_dy, cb_gate):
    t = qn.shape[0]
    nc = t // DN_CHUNK

    def body(dy_ref, q_ref, k_ref, v_ref, gb_ref, gate_ref, raw_ref, st_ref, ng_ref,
             dq_ref, dk_ref, dv_ref, dgate_ref, dgb_ref, dng_ref, ds_ref):
        @pl.when(pl.program_id(0) == 0)
        def _():
            ds_ref[...] = jnp.zeros_like(ds_ref)
            dng_ref[...] = jnp.zeros_like(dng_ref)

        masks = _dn_masks()
        eye, tril, strict, triu = masks
        gbv = gb_ref[...]
        lane = lax.broadcasted_iota(jnp.int32, (DN_CHUNK, 128), 1)
        last_row = lax.broadcasted_iota(jnp.int32, (DN_CHUNK, 1), 0) == DN_CHUNK - 1
        dgb = jnp.zeros((DN_CHUNK, 128), F32)
        for h in range(DN_HEADS):
            lanes = pl.ds(DN_HD * h, DN_HD)
            q, k, v = q_ref[:, lanes], k_ref[:, lanes], v_ref[:, lanes]
            beta = gbv[:, DN_HEADS + h:DN_HEADS + h + 1]
            s, ds_out = st_ref[0, h], ds_ref[h]
            o, gt, dy, ng = raw_ref[:, lanes], gate_ref[:, lanes], dy_ref[:, lanes], ng_ref[...]
            r = lax.rsqrt(jnp.mean(o * o, axis=-1, keepdims=True) + RMS_EPS)
            sil, dsil = _silu_parts(gt)
            d_on = dy * sil
            dgate_ref[:, lanes] = dy * (o * r * ng) * dsil
            dng_ref[...] += jnp.sum(d_on * o * r, axis=0, keepdims=True)
            w = d_on * ng
            do = r * w - o * (r * r * r) * jnp.mean(w * o, axis=-1, keepdims=True)
            m = _dn_chunk_math(q, k, v, gbv[:, h:h + 1], beta, s, masks)
            decay, eg, tm = m["decay"], m["eg"], m["tm"]
            d_vnew = _dot(m["qkd"], do, TN) + _dot(m["ktail"], ds_out)
            d_qkd = jnp.where(tril, _dot(do, m["v_new"], NT), 0.0)
            d_qg = _dot(do, s, NT)
            ds_ref[h] = _dot(m["qg"], do, TN) + m["eg_last"] * ds_out - _dot(m["wm"], d_vnew, TN)
            d_eglast = jnp.sum(jnp.sum(s * ds_out, axis=1, keepdims=True), axis=0, keepdims=True)
            d_ktail = _dot(m["v_new"], ds_out, NT)
            d_wm = -_dot(d_vnew, s, NT)
            dq = d_qg * eg
            dk = d_ktail * m["etail"]
            tail_term = jnp.sum(d_ktail * m["ktail"], axis=1, keepdims=True)
            dgc = jnp.sum(d_qg * m["qg"], axis=1, keepdims=True) - tail_term
            dgl = jnp.sum(tail_term, axis=0, keepdims=True) + d_eglast * m["eg_last"]
            dqk_dec = d_qkd * decay
            dq = dq + _dot(dqk_dec, k)
            dk = dk + _dot(dqk_dec, q, TN)
            ddecay = d_qkd * m["qk"]
            d_tm = _dot(d_vnew, m["vb"], NT) + _dot(d_wm, m["kbg"], NT)
            d_vb = _dot(tm, d_vnew, TN)
            d_kbg = _dot(tm, d_wm, TN)
            d_kb = d_kbg * eg
            dgc = dgc + jnp.sum(d_kbg * m["kbg"], axis=1, keepdims=True)
            d_n = jnp.where(strict, -_dot(_dot(tm, d_tm, TN, exact=True), tm, NT, exact=True), 0.0)
            d_kk = d_n * decay
            d_kb = d_kb + _dot(d_kk, k)
            dk = dk + _dot(d_kk, m["kb"], TN)
            ddecay = ddecay + d_n * m["kk"]
            dk = dk + d_kb * beta
            dbeta = jnp.sum(d_kb * k, axis=1, keepdims=True) + jnp.sum(d_vb * v, axis=1, keepdims=True)
            dv_ref[:, lanes] = d_vb * beta
            dq_ref[:, lanes] = dq
            dk_ref[:, lanes] = dk
            dd = ddecay * decay
            dgc = dgc + jnp.sum(dd, axis=1, keepdims=True) + jnp.where(last_row, dgl, 0.0)
            dgc_row = _to_row(dgc, eye) - jnp.sum(dd, axis=0, keepdims=True)
            dg = jnp.sum(jnp.where(triu, dgc_row, 0.0), axis=1, keepdims=True)
            dgb = dgb + jnp.where(lane == h, dg, 0.0) + jnp.where(lane == DN_HEADS + h, dbeta, 0.0)
        dgb_ref[...] = dgb

    row = pl.BlockSpec((DN_CHUNK, GW), lambda i: (nc - 1 - i, 0))
    small = pl.BlockSpec((DN_CHUNK, 128), lambda i: (nc - 1 - i, 0))
    sds = jax.ShapeDtypeStruct
    return pl.pallas_call(
        body, name="dn_chunk_bwd", grid=(nc,),
        in_specs=[_cols_rev(DN_CHUNK, GW, cb_dy, nc), row, row, row, small, _cols_rev(DN_CHUNK, GW, cb_gate, nc), row,
                  pl.BlockSpec((1, DN_HEADS, DN_HD, DN_HD), lambda i: (nc - 1 - i, 0, 0, 0)), _full((1, DN_HD))],
        out_specs=[row, row, row, row, small, _full((1, DN_HD))],
        out_shape=[sds((t, GW), F32)] * 4 + [sds((t, 128), F32), sds((1, DN_HD), F32)],
        scratch_shapes=[pltpu.VMEM((DN_HEADS, DN_HD, DN_HD), F32)], compiler_params=_params(("arbitrary",)))(
            dmixed, qn, kn, vs, gb, proj, raw, states, norm_g)


ANY = pl.BlockSpec(memory_space=pl.ANY)


def _place():
    x, y, c = lax.axis_index("x"), lax.axis_index("y"), lax.axis_index("c")
    chips = [(1 - x, y), (x, 1 - y), (1 - x, 1 - y)]
    return x, y, c, chips


def _remote(src, dst, send_sem, recv_sem, to):
    return pltpu.make_async_remote_copy(src_ref=src, dst_ref=dst, send_sem=send_sem, recv_sem=recv_sem, device_id=to,
                                        device_id_type=MESH)


def _allgather_chips(arrs):
    n = len(arrs)

    def body(*refs):
        ins, outs = refs[:n], refs[n:2 * n]
        send_sems, recv_sems, local_sems = refs[2 * n:]
        x, y, c, chips = _place()
        me = 2 * x + y
        sibling = (x, y, 1 - c)
        local, sends, fwds = [], [], []
        for a in range(n):
            half = ins[a].shape[0] // 2
            mine = pl.ds(c * half, half)
            cp = pltpu.make_async_copy(ins[a], outs[a].at[me], local_sems.at[a])
            cp.start()
            local.append(cp)
            for k, chip in enumerate(chips):
                s = _remote(ins[a].at[mine], outs[a].at[me, mine], send_sems.at[a, k], recv_sems.at[a, k], (*chip, c))
                s.start()
                sends.append(s)
        for a in range(n):
            half = ins[a].shape[0] // 2
            mine, other = pl.ds(c * half, half), pl.ds((1 - c) * half, half)
            for k, (cx, cy) in enumerate(chips):
                j = 2 * cx + cy
                got = outs[a].at[j, mine]
                _remote(got, got, send_sems.at[a, k], recv_sems.at[a, k], (cx, cy, c)).wait_recv()
                f = _remote(got, got, send_sems.at[a, 3 + k], recv_sems.at[a, 3 + k], sibling)
                f.start()
                fwds.append(f)
        for a in range(n):
            half = ins[a].shape[0] // 2
            other = pl.ds((1 - c) * half, half)
            for k, (cx, cy) in enumerate(chips):
                got = outs[a].at[2 * cx + cy, other]
                _remote(got, got, send_sems.at[a, 3 + k], recv_sems.at[a, 3 + k], sibling).wait_recv()
        for s in sends + fwds:
            s.wait_send()
        for cp in local:
            cp.wait()

    return pl.pallas_call(
        body, name="allgather_weights", in_specs=[ANY] * n, out_specs=[ANY] * n,
        out_shape=[jax.ShapeDtypeStruct((4,) + a.shape, a.dtype) for a in arrs],
        scratch_shapes=[pltpu.SemaphoreType.DMA((n, 6)), pltpu.SemaphoreType.DMA((n, 6)), pltpu.SemaphoreType.DMA((n,))],
        )(*arrs)


def _pair_exchange(gbs):
    n = len(gbs)

    def body(*refs):
        ins, mine_refs, got_refs = refs[:n], refs[n:2 * n], refs[2 * n:3 * n]
        send_sems, recv_sems, local_sems = refs[3 * n:]
        x, y, c, _ = _place()
        sibling = (x, y, 1 - c)
        cps = []
        for a in range(n):
            half = ins[a].shape[1] // 2
            cp = pltpu.make_async_copy(ins[a].at[:, pl.ds(c * half, half)], mine_refs[a], local_sems.at[a])
            cp.start()
            s = _remote(ins[a].at[:, pl.ds((1 - c) * half, half)], got_refs[a], send_sems.at[a], recv_sems.at[a], sibling)
            s.start()
            cps.append((cp, s))
        for cp, s in cps:
            s.wait()
            cp.wait()

    shp = [jax.ShapeDtypeStruct((4, g.shape[1] // 2, g.shape[2]), g.dtype) for g in gbs]
    res = pl.pallas_call(
        body, name="grad_pair_exchange", in_specs=[ANY] * n, out_specs=[ANY] * (2 * n), out_shape=shp + shp,
        scratch_shapes=[pltpu.SemaphoreType.DMA((n,)), pltpu.SemaphoreType.DMA((n,)), pltpu.SemaphoreType.DMA((n,))],
        )(*gbs)
    return res[:n], res[n:]


def _chip_exchange(ps):
    n = len(ps)

    def body(*refs):
        ins, own_refs, got_refs = refs[:n], refs[n:2 * n], refs[2 * n:3 * n]
        send_sems, recv_sems, local_sems = refs[3 * n:]
        x, y, c, chips = _place()
        me = 2 * x + y
        work = []
        for a in range(n):
            cp = pltpu.make_async_copy(ins[a].at[me], own_refs[a], local_sems.at[a])
            cp.start()
            work.append(cp)
            for k, (cx, cy) in enumerate(chips):
                s = _remote(ins[a].at[2 * cx + cy], got_refs[a].at[k], send_sems.at[a, k], recv_sems.at[a, k], (cx, cy, c))
                s.start()
                work.append(s)
        for w in work:
            w.wait()

    res = pl.pallas_call(
        body, name="grad_chip_exchange", in_specs=[ANY] * n, out_specs=[ANY] * (2 * n),
        out_shape=[jax.ShapeDtypeStruct(p.shape[1:], p.dtype) for p in ps] + [jax.ShapeDtypeStruct((3,) + p.shape[1:], p.dtype) for p in ps],
        scratch_shapes=[pltpu.SemaphoreType.DMA((n, 3)), pltpu.SemaphoreType.DMA((n, 3)), pltpu.SemaphoreType.DMA((n,))],
        )(*ps)
    return res[:n], res[n:]


def _pair_join(qs):
    n = len(qs)

    def body(*refs):
        ins, outs = refs[:n], refs[n:2 * n]
        send_sems, recv_sems, local_sems = refs[2 * n:]
        x, y, c, _ = _place()
        work = []
        for a in range(n):
            cp = pltpu.make_async_copy(ins[a], outs[a].at[c], local_sems.at[a])
            cp.start()
            s = _remote(ins[a], outs[a].at[c], send_sems.at[a], recv_sems.at[a], (x, y, 1 - c))
            s.start()
            work += [cp, s]
        for w in work:
            w.wait()

    return pl.pallas_call(
        body, name="grad_pair_join", in_specs=[ANY] * n, out_specs=[ANY] * n,
        out_shape=[jax.ShapeDtypeStruct((2,) + q.shape, q.dtype) for q in qs],
        scratch_shapes=[pltpu.SemaphoreType.DMA((n,)), pltpu.SemaphoreType.DMA((n,)), pltpu.SemaphoreType.DMA((n,))],
        )(*qs)


def _allgather_all(v):
    def body(v_ref, out_ref, send_sems, recv_sems, local_sem):
        x, y, c, chips = _place()
        me, sibling = (x, y, c), (x, y, 1 - c)

        def rows(px, py, pc):
            return out_ref.at[4 * px + 2 * py + pc]

        def copy(k, block, to, src=None):
            return _remote(rows(*block) if src is None else src, rows(*block), send_sems.at[k], recv_sems.at[k], to)

        mine = pltpu.make_async_copy(v_ref, rows(*me), local_sem)
        mine.start()
        first = [copy(0, me, sibling, src=v_ref)] + [copy(1 + j, me, (*chip, c), src=v_ref) for j, chip in enumerate(chips)]
        for cp in first:
            cp.start()
        passed = [copy(4 + j, (*chip, c), sibling) for j, chip in enumerate(chips)]
        for j, chip in enumerate(chips):
            copy(1 + j, (*chip, c), me).wait_recv()
            passed[j].start()
        copy(0, sibling, me).wait_recv()
        for j, chip in enumerate(chips):
            copy(4 + j, (*chip, 1 - c), me).wait_recv()
        for cp in first + passed:
            cp.wait_send()
        mine.wait()

    return pl.pallas_call(
        body, name="allgather_small", in_specs=[ANY], out_specs=ANY, out_shape=jax.ShapeDtypeStruct((8,) + v.shape, v.dtype),
        scratch_shapes=[pltpu.SemaphoreType.DMA((7,)), pltpu.SemaphoreType.DMA((7,)), pltpu.SemaphoreType.DMA],
        )(v)


def _row_tile(rows, cols, limit_bytes):
    for d in range(1, rows + 1):
        if rows % d == 0 and (rows // d) % 8 == 0 and (rows // d) * cols * 4 <= limit_bytes:
            return rows // d
    return rows


def _sum_kernel(parts, *, name, block_bytes=1 << 20):
    n = len(parts)
    rows, cols = parts[0][0].shape[1:] if isinstance(parts[0], tuple) else parts[0].shape
    tr = _row_tile(rows, cols, block_bytes)
    ins, specs = [], []
    for part in parts:
        if isinstance(part, tuple):
            ins.append(part[0])
            specs.append(pl.BlockSpec((None, tr, cols), functools.partial(lambda i, s: (s, i, 0), s=part[1])))
        else:
            ins.append(part)
            specs.append(pl.BlockSpec((tr, cols), lambda i: (i, 0)))

    def body(*refs):
        acc = refs[0][...]
        for r in refs[1:n]:
            acc = acc + r[...]
        refs[n][...] = acc

    return pl.pallas_call(body, name=name, grid=(rows // tr,), in_specs=specs, out_specs=pl.BlockSpec((tr, cols), lambda i: (i, 0)),
                          out_shape=jax.ShapeDtypeStruct((rows, cols), F32), compiler_params=_params(("parallel",)))(*ins)


def _reduce_scatter(gbs, names):
    mine, got = _pair_exchange(gbs)
    ps = [_sum_kernel([m.reshape(-1, m.shape[-1]), g.reshape(-1, g.shape[-1])], name="pair_sum_" + nm).reshape(m.shape)
          for m, g, nm in zip(mine, got, names)]
    own, others = _chip_exchange(ps)
    qs = [_sum_kernel([o, (t, 0), (t, 1), (t, 2)], name="chip_sum_" + nm) for o, t, nm in zip(own, others, names)]
    return [j.reshape(-1, j.shape[-1]) for j in _pair_join(qs)]


def _adamw(w, g, m, v, *, name, block_bytes=1 << 20):
    rows, cols = w.shape
    tr = _row_tile(rows, cols, block_bytes)

    def body(w_ref, g_ref, m_ref, v_ref, d_ref, nm_ref, nv_ref):
        gv = g_ref[...]
        nm = ADAM_B1 * m_ref[...] + (1.0 - ADAM_B1) * gv
        nv = ADAM_B2 * v_ref[...] + (1.0 - ADAM_B2) * (gv * gv)
        m_hat = nm / (1.0 - ADAM_B1 ** ADAM_STEP)
        v_hat = nv / (1.0 - ADAM_B2 ** ADAM_STEP)
        d_ref[...] = -ADAM_LR * (m_hat / (jnp.sqrt(v_hat) + ADAM_EPS) + ADAM_WD * w_ref[...])
        nm_ref[...] = nm
        nv_ref[...] = nv

    spec = pl.BlockSpec((tr, cols), lambda i: (i, 0))
    return pl.pallas_call(body, name=name, grid=(rows // tr,), in_specs=[spec] * 4, out_specs=[spec] * 3,
                          out_shape=[jax.ShapeDtypeStruct((rows, cols), F32)] * 3, compiler_params=_params(("parallel",)))(w, g, m, v)


WEIGHTS = ['w_in', 's5_lambda_re', 's5_lambda_im', 's5_log_step', 's5_b_re', 's5_b_im', 's5_c_re', 's5_c_im', 's5_d', 's5_glu_w',
           's5_glu_b', 'sgu_norm_g', 'sgu_norm_b', 'sgu_w', 'sgu_b', 'pool_w', 'pool_scale', 'dn_conv_w', 'dn_a_log', 'dn_dt_bias',
           'dn_norm_g', 'w_out', 'ln1_g', 'ln1_b', 'w_up', 'w_down', 'ln2_g', 'ln2_b']
BIG = ['w_in', 's5_glu_w', 'w_out', 'w_up', 'w_down']
SMALL = [n for n in WEIGHTS if n not in BIG]
CB_S5, CB_SGU_U, CB_SGU_V, CB_POOL, CB_DN_Q, CB_DN_GATE = 0, 1, 2, 3, 4, 7


def _pad_lanes(v, width=128):
    return jnp.zeros((1, width), F32).at[0, :v.shape[0]].set(v)


def _layer_consts(p):
    c = _s5_prepare(p['s5_lambda_re'], p['s5_lambda_im'], p['s5_log_step'], p['s5_b_re'], p['s5_b_im'], p['s5_c_re'], p['s5_c_im'])
    tril = jnp.tril(jnp.ones((SGU_CHUNK, SGU_CHUNK), bool))
    wm = jnp.where(tril, p['sgu_w'], 0.0)
    c.update(s5_d=p['s5_d'].reshape(1, GW), glu_b=p['s5_glu_b'].reshape(1, GW), sgu_ng=p['sgu_norm_g'].reshape(1, GW),
             sgu_nb=p['sgu_norm_b'].reshape(1, GW), sgu_w=wm, sgu_wt=jnp.swapaxes(wm, 1, 2),
             sgu_bias=jnp.repeat(p['sgu_b'].T, SGU_HD, axis=1), pool_w=p['pool_w'], pool_scale=p['pool_scale'].reshape(1, GW),
             conv_w=p['dn_conv_w'], alog=_pad_lanes(p['dn_a_log']), dtb=_pad_lanes(p['dn_dt_bias']), dn_ng=p['dn_norm_g'].reshape(1, DN_HD),
             ln1_g=p['ln1_g'].reshape(1, D_MODEL), ln1_b=p['ln1_b'].reshape(1, D_MODEL), ln2_g=p['ln2_g'].reshape(1, D_MODEL),
             ln2_b=p['ln2_b'].reshape(1, D_MODEL))
    return c


def _layer_fwd(xin, w, c, i):
    tag = str(i)
    proj = _matmul(xin, w['w_in'], mode="nn", name="proj" + tag, tn=1408)
    s5, xre, xim = _s5_fwd(proj, c['bbre'], c['bbim'], c['ccre'], c['ccim'], c['s5_d'], c['cf'], w['s5_glu_w'], c['glu_b'], cb=CB_S5)
    sgu = _sgu_fwd(proj, c['sgu_ng'], c['sgu_nb'], c['sgu_w'], c['sgu_bias'], cbu=CB_SGU_U, cbv=CB_SGU_V)
    pool, pooled = _pool_fwd(proj, c['pool_w'], c['pool_scale'], cb=CB_POOL)
    qn, kn, vs, gb = _dn_prep_fwd(proj, c['conv_w'], c['alog'], c['dtb'], cbq=CB_DN_Q)
    dn, raw, states = _dn_chunk_fwd(qn, kn, vs, gb, proj, c['dn_ng'], cb_gate=CB_DN_GATE)
    mixed = jnp.concatenate([s5, sgu, pool, dn], axis=1)
    h1 = _matmul(mixed, w['w_out'], mode="nn", name="mix_out" + tag, e=xin, epi=lambda r, e: r + ALPHA * e)
    x1 = _ln_fwd(h1, c['ln1_g'], c['ln1_b'], name="ln1_" + tag)
    a = _matmul(x1, w['w_up'], mode="nn", name="mlp_up" + tag)
    h2 = _matmul(a, w['w_down'], mode="nn", name="mlp_down" + tag, a_fn=_relu2, e=x1, epi=lambda r, e: r + ALPHA * e)
    x2 = _ln_fwd(h2, c['ln2_g'], c['ln2_b'], name="ln2_" + tag)
    saved = dict(xin=xin, proj=proj, xre=xre, xim=xim, pooled=pooled, qn=qn, kn=kn, vs=vs, gb=gb, raw=raw, states=states, mixed=mixed,
                 h1=h1, x1=x1, a=a, h2=h2)
    return x2, saved


def _layer_bwd(dx2, s, w, c, p, i):
    tag = str(i)
    dh2, dln2g, dln2b = _ln_bwd(dx2, s['h2'], c['ln2_g'], name="ln2_bwd" + tag)
    dw_down = _matmul(s['a'], dh2, mode="tn", name="dw_down" + tag, a_fn=_relu2)
    da = _matmul(dh2, w['w_down'], mode="nt", name="d_hidden" + tag, e=s['a'], epi=lambda r, e: r * (2.0 * jnp.maximum(e, 0.0)))
    dw_up = _matmul(s['x1'], da, mode="tn", name="dw_up" + tag)
    dx1 = _matmul(da, w['w_up'], mode="nt", name="dx_mlp" + tag, e=dh2, epi=lambda r, e: r + ALPHA * e)
    dh1, dln1g, dln1b = _ln_bwd(dx1, s['h1'], c['ln1_g'], name="ln1_bwd" + tag)
    dw_out = _matmul(s['mixed'], dh1, mode="tn", name="dw_out" + tag)
    dmixed = _matmul(dh1, w['w_out'], mode="nt", name="d_mixed" + tag)
    proj = s['proj']
    (du, dglu_w, dglu_b, dd, dccre, dccim, dbbre, dbbim, sre, sim) = _s5_bwd(
        dmixed, proj, s['xre'], s['xim'], c['bbre'], c['bbim'], c['ccre'], c['ccim'], c['s5_d'], c['cr'], w['s5_glu_w'], c['glu_b'],
        cb_dy=0, cb=CB_S5)
    dlam_re, dlam_im, dlog_step, db_re, db_im, dc_re, dc_im = _s5_param_grads(
        p['s5_lambda_re'], p['s5_lambda_im'], p['s5_log_step'], p['s5_b_re'], p['s5_b_im'], dbbre, dbbim, dccre, dccim, sre, sim)
    dzu, dzv, dsgu_w, dsgu_bias, dsgu_ng, dsgu_nb = _sgu_bwd(dmixed, proj, c['sgu_ng'], c['sgu_nb'], c['sgu_w'], c['sgu_wt'], c['sgu_bias'],
                                                            cb=1, cbu=CB_SGU_U, cbv=CB_SGU_V)
    dp, dpool_w, dpool_scale = _pool_bwd(dmixed, s['pooled'], c['pool_w'], c['pool_scale'], cb=2)
    dqn, dkn, dvs, dgate, dgb, ddn_ng = _dn_chunk_bwd(dmixed, s['qn'], s['kn'], s['vs'], s['gb'], proj, s['raw'], s['states'], c['dn_ng'],
                                                     cb_dy=3, cb_gate=CB_DN_GATE)
    dcq, dck, dcv, dab, dalog, ddtb = _dn_prep_bwd_a(proj, c['conv_w'], c['alog'], c['dtb'], dqn, dkn, dvs, dgb, cbq=CB_DN_Q)
    dq, dk, dv, dconv_w = _dn_prep_bwd_b(proj, c['conv_w'], dcq, dck, dcv, cbq=CB_DN_Q)
    dproj = jnp.concatenate([du, dzu, dzv, dp, dq, dk, dv, dgate, dab], axis=1)
    dw_in = _matmul(s['xin'], dproj, mode="tn", name="dw_in" + tag, tn=1408)
    dxin = _matmul(dproj, w['w_in'], mode="nt", name="dx_in" + tag, tk=384, e=dh1, epi=lambda r, e: r + ALPHA * e)
    tril = jnp.tril(jnp.ones((SGU_CHUNK, SGU_CHUNK), bool))
    big = dict(w_in=dw_in, s5_glu_w=dglu_w, w_out=dw_out, w_up=dw_up, w_down=dw_down)
    small = dict(
        s5_lambda_re=dlam_re, s5_lambda_im=dlam_im, s5_log_step=dlog_step, s5_b_re=db_re, s5_b_im=db_im, s5_c_re=dc_re, s5_c_im=dc_im,
        s5_d=dd.reshape(S5_G, S5_H), s5_glu_b=dglu_b[0], sgu_norm_g=dsgu_ng[0], sgu_norm_b=dsgu_nb[0],
        sgu_w=jnp.where(tril, dsgu_w, 0.0), sgu_b=dsgu_bias.reshape(SGU_CHUNK, SGU_HEADS, SGU_HD).sum(-1).T, pool_w=dpool_w,
        pool_scale=dpool_scale[0], dn_conv_w=dconv_w, dn_a_log=dalog[0, :DN_HEADS], dn_dt_bias=ddtb[0, :DN_HEADS], dn_norm_g=ddn_ng[0],
        ln1_g=dln1g[0], ln1_b=dln1b[0], ln2_g=dln2g[0], ln2_b=dln2b[0])
    return dxin, big, small


def _pack(arrs):
    flat = jnp.concatenate([a.reshape(-1) for a in arrs])
    n = flat.shape[0]
    m = -(-n // 1024) * 1024
    return jnp.pad(flat, (0, m - n)).reshape(m // 128, 128)


def _sum_all(stacked):
    return _sum_kernel([(stacked, d) for d in range(stacked.shape[0])], name="small_sum")


def _unpack(packed, like):
    flat, out, off = packed.reshape(-1), [], 0
    for a in like:
        n = math.prod(a.shape)
        out.append(flat[off:off + n].reshape(a.shape))
        off += n
    return out


def kernel(x, w_in, s5_lambda_re, s5_lambda_im, s5_log_step, s5_b_re, s5_b_im, s5_c_re, s5_c_im, s5_d, s5_glu_w, s5_glu_b, sgu_norm_g, sgu_norm_b, sgu_w, sgu_b, pool_w, pool_scale, dn_conv_w, dn_a_log, dn_dt_bias, dn_norm_g, w_out, ln1_g, ln1_b, w_up, w_down, ln2_g, ln2_b, loss_target, m_w_in, m_s5_lambda_re, m_s5_lambda_im, m_s5_log_step, m_s5_b_re, m_s5_b_im, m_s5_c_re, m_s5_c_im, m_s5_d, m_s5_glu_w, m_s5_glu_b, m_sgu_norm_g, m_sgu_norm_b, m_sgu_w, m_sgu_b, m_pool_w, m_pool_scale, m_dn_conv_w, m_dn_a_log, m_dn_dt_bias, m_dn_norm_g, m_w_out, m_ln1_g, m_ln1_b, m_w_up, m_w_down, m_ln2_g, m_ln2_b, v_w_in, v_s5_lambda_re, v_s5_lambda_im, v_s5_log_step, v_s5_b_re, v_s5_b_im, v_s5_c_re, v_s5_c_im, v_s5_d, v_s5_glu_w, v_s5_glu_b, v_sgu_norm_g, v_sgu_norm_b, v_sgu_w, v_sgu_b, v_pool_w, v_pool_scale, v_dn_conv_w, v_dn_a_log, v_dn_dt_bias, v_dn_norm_g, v_w_out, v_ln1_g, v_ln1_b, v_w_up, v_w_down, v_ln2_g, v_ln2_b):
    given = dict(locals())
    xs, ys = lax.axis_index("x"), lax.axis_index("y")
    chip = 2 * xs + ys
    t = given['x'].shape[1]
    x0 = given['x'].reshape(t, D_MODEL)
    target = given['loss_target'].reshape(t, D_MODEL)

    gathered = dict(zip(BIG, _allgather_chips([given[n].astype(MXU_DTYPE).reshape(-1, given[n].shape[-1]) for n in BIG])))
    conv_local = given['dn_conv_w']
    conv_all = _allgather_all(_pack([conv_local]))
    n_conv = math.prod(conv_local.shape)
    conv_full = jnp.concatenate([conv_all[2 * j].reshape(-1)[:n_conv].reshape(conv_local.shape) for j in range(4)], axis=-1)

    def layer_weights(i):
        def rows(n, per):
            return gathered[n][:, i * per:(i + 1) * per, :]
        w_in_full = jnp.transpose(rows('w_in', D_MODEL), (1, 0, 2)).reshape(D_MODEL, IN_COLS)
        return dict(w_in=jnp.pad(w_in_full, ((0, 0), (0, IN_PAD - IN_COLS))), s5_glu_w=rows('s5_glu_w', GW // 4).reshape(GW, GW),
                    w_out=rows('w_out', D_MODEL // 4).reshape(D_MODEL, D_MODEL),
                    w_up=jnp.transpose(rows('w_up', D_MODEL), (1, 0, 2)).reshape(D_MODEL, D_FF),
                    w_down=rows('w_down', D_FF // 4).reshape(D_FF, D_MODEL))

    def layer_params(i):
        p = {n: given[n][i] for n in SMALL}
        p['dn_conv_w'] = conv_full[i]
        return p

    ws = [layer_weights(i) for i in range(DEPTH)]
    ps = [layer_params(i) for i in range(DEPTH)]
    cs = [_layer_consts(p) for p in ps]

    xcur, saved = x0, []
    for i in range(DEPTH):
        xcur, s = _layer_fwd(xcur, ws[i], cs[i], i)
        saved.append(s)
    dx, colsum = _loss_head(xcur, target)
    loss = lax.psum(0.5 * jnp.sum(colsum) / D_MODEL, ("x", "y", "c"))

    bigs, smalls = [None] * DEPTH, [None] * DEPTH
    for i in reversed(range(DEPTH)):
        dx, bigs[i], smalls[i] = _layer_bwd(dx, saved[i], ws[i], cs[i], ps[i], i)
    grad_x = dx.reshape(1, t, D_MODEL)

    def by_rows(g, per):
        return g.reshape(4, per, g.shape[-1])

    def by_cols(g, cols):
        return jnp.transpose(g.reshape(g.shape[0], 4, cols), (1, 0, 2))

    blocks = dict(
        w_in=[by_cols(b['w_in'][:, :IN_COLS], IN_COLS // 4) for b in bigs], s5_glu_w=[by_rows(b['s5_glu_w'], GW // 4) for b in bigs],
        w_out=[by_rows(b['w_out'], D_MODEL // 4) for b in bigs], w_up=[by_cols(b['w_up'], D_MODEL) for b in bigs],
        w_down=[by_rows(b['w_down'], D_FF // 4) for b in bigs])
    reduced = dict(zip(BIG, _reduce_scatter([jnp.concatenate(blocks[n], axis=1) for n in BIG], BIG)))

    small_full = [jnp.stack([smalls[i][n] for i in range(DEPTH)]) for n in SMALL]
    grads = dict(zip(SMALL, _unpack(_sum_all(_allgather_all(_pack(small_full))), small_full)))
    grads['dn_conv_w'] = lax.dynamic_slice_in_dim(grads['dn_conv_w'], chip * conv_local.shape[-1], conv_local.shape[-1], axis=2)
    for n in BIG:
        grads[n] = reduced[n].reshape(given[n].shape)

    delta, new_m, new_v = {}, {}, {}
    for n in BIG:
        cols = given[n].shape[-1]
        d, nm, nv = _adamw(given[n].reshape(-1, cols), reduced[n], given['m_' + n].reshape(-1, cols), given['v_' + n].reshape(-1, cols),
                           name="adamw_" + n)
        delta[n], new_m[n], new_v[n] = d.reshape(given[n].shape), nm.reshape(given[n].shape), nv.reshape(given[n].shape)
    like = [given[n] for n in SMALL]
    d, nm, nv = _adamw(_pack(like), _pack([grads[n] for n in SMALL]), _pack([given['m_' + n] for n in SMALL]),
                       _pack([given['v_' + n] for n in SMALL]), name="adamw_small")
    for out, packed in ((delta, d), (new_m, nm), (new_v, nv)):
        out.update(zip(SMALL, _unpack(packed, like)))
    return (loss, grad_x, *[grads[n] for n in WEIGHTS], *[delta[n] for n in WEIGHTS], *[new_m[n] for n in WEIGHTS],
            *[new_v[n] for n in WEIGHTS])
```

```python
import functools
import math

import jax
import jax.numpy as jnp
from jax import lax
from jax.experimental import pallas as pl
from jax.experimental.pallas import tpu as pltpu

F32 = jnp.float32
MXU_DTYPE = jnp.bfloat16
HI = lax.Precision.HIGHEST

D_MODEL = 2048
DEPTH = 2
GW = 512
S5_H = 16
S5_G = GW // S5_H
S5_P = 64
S5_N = S5_G * S5_P
SGU_CHUNK = 128
SGU_HEADS = 8
SGU_HD = GW // SGU_HEADS
POOL_WINDOWS = (2, 4, 8, 16)
POOL_GD = 128
DN_HD = 128
DN_HEADS = 4
DN_CONV = 4
DN_CHUNK = 64
D_FF = 4 * D_MODEL
IN_COLS = 4104
IN_PAD = 4224
LN_EPS = 1e-5
RMS_EPS = 1e-6
L2_EPS = 1e-6
ALPHA = (2 * DEPTH) ** 0.25
ADAM_LR, ADAM_B1, ADAM_B2, ADAM_EPS, ADAM_WD, ADAM_STEP = 0.001, 0.9, 0.999, 1e-08, 0.01, 10

VMEM_LIMIT = 56 * 1024 * 1024
MESH = pl.DeviceIdType.MESH


def _params(sem=None, vmem=VMEM_LIMIT):
    return pltpu.CompilerParams(dimension_semantics=sem, vmem_limit_bytes=vmem)


def _full(shape):
    nd = len(shape)
    return pl.BlockSpec(shape, lambda *_: (0,) * nd)


def _dot(a, b, dims=(((1,), (0,)), ((), ())), exact=False):
    if exact:
        return lax.dot_general(a, b, dims, precision=HI, preferred_element_type=F32)
    return lax.dot_general(a.astype(MXU_DTYPE), b.astype(MXU_DTYPE), dims, preferred_element_type=F32)


NN = (((1,), (0,)), ((), ()))
NT = (((1,), (1,)), ((), ()))
TN = (((0,), (0,)), ((), ()))


def _gelu(x):
    c = math.sqrt(2.0 / math.pi)
    return 0.5 * x * (1.0 + jnp.tanh(c * (x + 0.044715 * x * x * x)))


def _gelu_grad(x):
    c = math.sqrt(2.0 / math.pi)
    t = jnp.tanh(c * (x + 0.044715 * x * x * x))
    return 0.5 * (1.0 + t) + 0.5 * x * (1.0 - t * t) * c * (1.0 + 3.0 * 0.044715 * x * x)


def _sigmoid(x):
    return 1.0 / (1.0 + jnp.exp(-x))


def _relu2(x):
    r = jnp.maximum(x, 0.0)
    return r * r


def _matmul(a, b, *, mode, name, e=None, epi=None, out_dtypes=(F32,), tm=1024, tn=1024, tk=512):
    if mode == "nn":
        (m, k), n = a.shape, b.shape[1]
    elif mode == "nt":
        (m, k), n = a.shape, b.shape[0]
    else:
        (k, m), n = a.shape, b.shape[1]
    tm, tn, tk = min(tm, m), min(tn, n), min(tk, k)
    assert m % tm == 0 and n % tn == 0 and k % tk == 0, (name, m, n, k, tm, tn, tk)
    nk, nout = k // tk, len(out_dtypes)
    dims = {"nn": NN, "nt": NT, "tn": TN}[mode]
    a_spec = pl.BlockSpec((tk, tm), lambda i, j, l: (l, i)) if mode == "tn" else pl.BlockSpec((tm, tk), lambda i, j, l: (i, l))
    b_spec = pl.BlockSpec((tn, tk), lambda i, j, l: (j, l)) if mode == "nt" else pl.BlockSpec((tk, tn), lambda i, j, l: (l, j))
    o_spec = pl.BlockSpec((tm, tn), lambda i, j, l: (i, j))

    def body(*refs):
        a_ref, b_ref = refs[:2]
        e_ref = refs[2] if e is not None else None
        o_refs = refs[len(refs) - 1 - nout:len(refs) - 1]
        acc = refs[-1]
        l = pl.program_id(2)
        d = _dot(a_ref[...], b_ref[...], dims)

        def finish(r):
            outs = (r,) if epi is None else epi(r, None if e_ref is None else e_ref[...])
            for o_ref, o, dt in zip(o_refs, outs, out_dtypes, strict=True):
                o_ref[...] = o.astype(dt)

        if nk == 1:
            finish(d)
        else:
            @pl.when(l == 0)
            def _():
                acc[...] = d

            @pl.when((l > 0) & (l < nk - 1))
            def _():
                acc[...] += d

            @pl.when(l == nk - 1)
            def _():
                finish(acc[...] + d)

    ins, specs = [a, b], [a_spec, b_spec]
    if e is not None:
        ins.append(e)
        specs.append(o_spec)
    return pl.pallas_call(
        body, name=name, grid=(m // tm, n // tn, nk), in_specs=specs, out_specs=[o_spec] * nout,
        out_shape=[jax.ShapeDtypeStruct((m, n), dt) for dt in out_dtypes], scratch_shapes=[pltpu.VMEM((tm, tn), F32)],
        compiler_params=_params(("parallel", "parallel", "arbitrary")))(*ins)


def _ln_fwd(h, g, b, *, name, tr=256):
    t, d = h.shape

    def body(h_ref, g_ref, b_ref, o_ref, o16_ref):
        x = h_ref[...]
        mu = jnp.mean(x, axis=-1, keepdims=True)
        xc = x - mu
        var = jnp.mean(xc * xc, axis=-1, keepdims=True)
        y = xc * lax.rsqrt(var + LN_EPS) * g_ref[...] + b_ref[...]
        o_ref[...] = y
        o16_ref[...] = y.astype(MXU_DTYPE)

    row = pl.BlockSpec((tr, d), lambda i: (i, 0))
    return pl.pallas_call(body, name=name, grid=(t // tr,), in_specs=[row, _full((1, d)), _full((1, d))], out_specs=[row, row],
                          out_shape=[jax.ShapeDtypeStruct((t, d), F32), jax.ShapeDtypeStruct((t, d), MXU_DTYPE)],
                          compiler_params=_params(("parallel",)))(h, g, b)


def _ln_bwd(dy, h, g, *, name, tr=256):
    t, d = h.shape

    def body(dy_ref, h_ref, g_ref, dh_ref, dh16_ref, dg_ref, db_ref):
        @pl.when(pl.program_id(0) == 0)
        def _():
            dg_ref[...] = jnp.zeros_like(dg_ref)
            db_ref[...] = jnp.zeros_like(db_ref)

        x, dyv = h_ref[...], dy_ref[...]
        mu = jnp.mean(x, axis=-1, keepdims=True)
        xc = x - mu
        rstd = lax.rsqrt(jnp.mean(xc * xc, axis=-1, keepdims=True) + LN_EPS)
        xh = xc * rstd
        w = dyv * g_ref[...]
        dh = rstd * (w - jnp.mean(w, axis=-1, keepdims=True) - xh * jnp.mean(w * xh, axis=-1, keepdims=True))
        dh_ref[...] = dh
        dh16_ref[...] = dh.astype(MXU_DTYPE)
        dg_ref[...] += jnp.sum(dyv * xh, axis=0, keepdims=True)
        db_ref[...] += jnp.sum(dyv, axis=0, keepdims=True)

    row = pl.BlockSpec((tr, d), lambda i: (i, 0))
    vec = _full((1, d))
    return pl.pallas_call(
        body, name=name, grid=(t // tr,), in_specs=[row, row, vec], out_specs=[row, row, vec, vec],
        out_shape=[jax.ShapeDtypeStruct((t, d), F32), jax.ShapeDtypeStruct((t, d), MXU_DTYPE), jax.ShapeDtypeStruct((1, d), F32),
                   jax.ShapeDtypeStruct((1, d), F32)],
        compiler_params=_params(("arbitrary",)))(dy, h, g)


def _loss_head(y, target, *, tr=256):
    t, d = y.shape

    def body(y_ref, t_ref, dy_ref, s_ref):
        @pl.when(pl.program_id(0) == 0)
        def _():
            s_ref[...] = jnp.zeros_like(s_ref)

        err = y_ref[...] - t_ref[...]
        dy_ref[...] = err * (1.0 / d)
        s_ref[...] += jnp.sum(err * err, axis=0, keepdims=True)

    row = pl.BlockSpec((tr, d), lambda i: (i, 0))
    return pl.pallas_call(
        body, name="loss_head", grid=(t // tr,), in_specs=[row, row], out_specs=[row, _full((1, d))],
        out_shape=[jax.ShapeDtypeStruct((t, d), F32), jax.ShapeDtypeStruct((1, d), F32)],
        compiler_params=_params(("arbitrary",)))(y, target)


def _cols(tb, width, cb):
    return pl.BlockSpec((tb, width), lambda i: (i, cb))


def _cols_rev(tb, width, cb, nb):
    return pl.BlockSpec((tb, width), lambda i: (nb - 1 - i, cb))


POOL_HALO = 16


def _pool_fwd(proj, w, scale, *, cb, tb=512):
    t = proj.shape[0]
    tb = min(tb, t)

    def body(p_ref, w_ref, s_ref, o_ref, pooled_ref, ext):
        i = pl.program_id(0)

        @pl.when(i == 0)
        def _():
            ext[pl.ds(0, POOL_HALO), :] = jnp.zeros((POOL_HALO, GW), F32)

        p = p_ref[...]
        ext[pl.ds(POOL_HALO, tb), :] = p
        pos = (i * tb + lax.broadcasted_iota(jnp.int32, (tb, 1), 0) + 1).astype(F32)
        for gi, win in enumerate(POOL_WINDOWS):
            c0 = gi * POOL_GD
            s = p[:, c0:c0 + POOL_GD]
            for k in range(1, win):
                s = s + ext[pl.ds(POOL_HALO - k, tb), pl.ds(c0, POOL_GD)]
            pooled = s / jnp.minimum(pos, float(win)) - p[:, c0:c0 + POOL_GD]
            pooled_ref[:, pl.ds(c0, POOL_GD)] = pooled
            o_ref[:, pl.ds(c0, POOL_GD)] = (_dot(pooled, w_ref[gi]) * s_ref[:, pl.ds(c0, POOL_GD)]).astype(o_ref.dtype)
        ext[pl.ds(0, POOL_HALO), :] = p[tb - POOL_HALO:, :]

    row = pl.BlockSpec((tb, GW), lambda i: (i, 0))
    return pl.pallas_call(
        body, name="pool_fwd", grid=(t // tb,),
        in_specs=[_cols(tb, GW, cb), _full((4, POOL_GD, POOL_GD)), _full((1, GW))], out_specs=[row, row],
        out_shape=[jax.ShapeDtypeStruct((t, GW), MXU_DTYPE), jax.ShapeDtypeStruct((t, GW), F32)],
        scratch_shapes=[pltpu.VMEM((tb + POOL_HALO, GW), F32)],
        compiler_params=_params(("arbitrary",)))(proj, w, scale)


def _pool_bwd(dmixed, pooled, w, scale, *, cb, tb=512):
    t = pooled.shape[0]
    tb = min(tb, t)
    nb = t // tb

    def body(dy_ref, pooled_ref, w_ref, s_ref, dp_ref, dw_ref, ds_ref, ext):
        i = pl.program_id(0)

        @pl.when(i == 0)
        def _():
            ext[pl.ds(tb, POOL_HALO), :] = jnp.zeros((POOL_HALO, GW), F32)
            dw_ref[...] = jnp.zeros_like(dw_ref)
            ds_ref[...] = jnp.zeros_like(ds_ref)

        dy = dy_ref[...]
        pos = ((nb - 1 - i) * tb + lax.broadcasted_iota(jnp.int32, (tb, 1), 0) + 1).astype(F32)
        dpool_all = []
        for gi, win in enumerate(POOL_WINDOWS):
            c0 = gi * POOL_GD
            pg = pooled_ref[:, pl.ds(c0, POOL_GD)]
            dyg = dy[:, c0:c0 + POOL_GD]
            ds_ref[:, pl.ds(c0, POOL_GD)] += jnp.sum(dyg * _dot(pg, w_ref[gi]), axis=0, keepdims=True)
            dmp = dyg * s_ref[:, pl.ds(c0, POOL_GD)]
            dw_ref[gi] += _dot(pg, dmp, TN)
            dpool = _dot(dmp, w_ref[gi], NT)
            dpool_all.append(dpool)
            ext[pl.ds(0, tb), pl.ds(c0, POOL_GD)] = dpool / jnp.minimum(pos, float(win))
        for gi, win in enumerate(POOL_WINDOWS):
            c0 = gi * POOL_GD
            s = ext[pl.ds(0, tb), pl.ds(c0, POOL_GD)]
            for k in range(1, win):
                s = s + ext[pl.ds(k, tb), pl.ds(c0, POOL_GD)]
            dp_ref[:, pl.ds(c0, POOL_GD)] = (s - dpool_all[gi]).astype(dp_ref.dtype)
        ext[pl.ds(tb, POOL_HALO), :] = ext[pl.ds(0, POOL_HALO), :]

    row = pl.BlockSpec((tb, GW), lambda i: (nb - 1 - i, 0))
    return pl.pallas_call(
        body, name="pool_bwd", grid=(nb,),
        in_specs=[_cols_rev(tb, GW, cb, nb), row, _full((4, POOL_GD, POOL_GD)), _full((1, GW))],
        out_specs=[row, _full((4, POOL_GD, POOL_GD)), _full((1, GW))],
        out_shape=[jax.ShapeDtypeStruct((t, GW), MXU_DTYPE), jax.ShapeDtypeStruct((4, POOL_GD, POOL_GD), F32),
                   jax.ShapeDtypeStruct((1, GW), F32)],
        scratch_shapes=[pltpu.VMEM((tb + POOL_HALO, GW), F32)], compiler_params=_params(("arbitrary",)))(dmixed, pooled, w, scale)


def _sgu_core(zu, zv, ng, nb, w_ref, bias):
    tb = zu.shape[0]
    u = _gelu(zu)
    v0 = _gelu(zv)
    mu = jnp.mean(v0, axis=-1, keepdims=True)
    vc = v0 - mu
    rstd = lax.rsqrt(jnp.mean(vc * vc, axis=-1, keepdims=True) + LN_EPS)
    xh = vc * rstd
    vn = xh * ng + nb
    low = lax.broadcasted_iota(jnp.int32, (SGU_CHUNK, 2 * SGU_HD), 1) < SGU_HD
    rows = []
    for n in range(tb // SGU_CHUNK):
        pairs = []
        for j in range(SGU_HEADS // 2):
            vp = vn[n * SGU_CHUNK:(n + 1) * SGU_CHUNK, j * 128:(j + 1) * 128]
            pairs.append(jnp.where(low, _dot(w_ref[2 * j], vp), _dot(w_ref[2 * j + 1], vp)))
        rows.append(jnp.concatenate(pairs, axis=1) + bias)
    mixed = jnp.concatenate(rows, axis=0)
    return u, xh, rstd, vn, mixed


def _sgu_fwd(proj, ng, nb, w, bias, *, cbu, cbv, tb=512):
    t = proj.shape[0]
    tb = min(tb, t)

    def body(zu_ref, zv_ref, ng_ref, nb_ref, w_ref, bias_ref, o_ref):
        u, _, _, _, mixed = _sgu_core(zu_ref[...], zv_ref[...], ng_ref[...], nb_ref[...], w_ref, bias_ref[...])
        o_ref[...] = (u * mixed).astype(o_ref.dtype)

    vec = _full((1, GW))
    return pl.pallas_call(
        body, name="sgu_fwd", grid=(t // tb,),
        in_specs=[_cols(tb, GW, cbu), _cols(tb, GW, cbv), vec, vec, _full((8, 128, 128)), _full((128, GW))],
        out_specs=pl.BlockSpec((tb, GW), lambda i: (i, 0)), out_shape=jax.ShapeDtypeStruct((t, GW), MXU_DTYPE),
        compiler_params=_params(("parallel",)))(proj, proj, ng, nb, w, bias)


def _sgu_bwd(dmixed, proj, ng, nb, w, wt, bias, *, cb, cbu, cbv, tb=512):
    t = proj.shape[0]
    tb = min(tb, t)

    def body(dy_ref, zu_ref, zv_ref, ng_ref, nb_ref, w_ref, wt_ref, bias_ref, dzu_ref, dzv_ref, dw_ref, dbias_ref, dng_ref, dnb_ref):
        @pl.when(pl.program_id(0) == 0)
        def _():
            dw_ref[...] = jnp.zeros_like(dw_ref)
            dbias_ref[...] = jnp.zeros_like(dbias_ref)
            dng_ref[...] = jnp.zeros_like(dng_ref)
            dnb_ref[...] = jnp.zeros_like(dnb_ref)

        zu, zv, dy = zu_ref[...], zv_ref[...], dy_ref[...]
        u, xh, rstd, vn, mixed = _sgu_core(zu, zv, ng_ref[...], nb_ref[...], w_ref, bias_ref[...])
        dzu_ref[...] = (dy * mixed * _gelu_grad(zu)).astype(dzu_ref.dtype)
        dmix = dy * u
        low = lax.broadcasted_iota(jnp.int32, (SGU_CHUNK, 2 * SGU_HD), 1) < SGU_HD
        dbias = jnp.zeros((SGU_CHUNK, GW), F32)
        rows = []
        for n in range(tb // SGU_CHUNK):
            dm = dmix[n * SGU_CHUNK:(n + 1) * SGU_CHUNK, :]
            dbias = dbias + dm
            pairs = []
            for j in range(SGU_HEADS // 2):
                dmp = dm[:, j * 128:(j + 1) * 128]
                vp = vn[n * SGU_CHUNK:(n + 1) * SGU_CHUNK, j * 128:(j + 1) * 128]
                dw_ref[2 * j] += _dot(jnp.where(low, dmp, 0.0), vp, NT)
                dw_ref[2 * j + 1] += _dot(jnp.where(low, 0.0, dmp), vp, NT)
                pairs.append(jnp.where(low, _dot(wt_ref[2 * j], dmp), _dot(wt_ref[2 * j + 1], dmp)))
            rows.append(jnp.concatenate(pairs, axis=1))
        dbias_ref[...] += dbias
        dvn = jnp.concatenate(rows, axis=0)
        dng_ref[...] += jnp.sum(dvn * xh, axis=0, keepdims=True)
        dnb_ref[...] += jnp.sum(dvn, axis=0, keepdims=True)
        wv = dvn * ng_ref[...]
        dv0 = rstd * (wv - jnp.mean(wv, axis=-1, keepdims=True) - xh * jnp.mean(wv * xh, axis=-1, keepdims=True))
        dzv_ref[...] = (dv0 * _gelu_grad(zv)).astype(dzv_ref.dtype)

    vec = _full((1, GW))
    row = pl.BlockSpec((tb, GW), lambda i: (i, 0))
    mat = _full((8, 128, 128))
    return pl.pallas_call(
        body, name="sgu_bwd", grid=(t // tb,),
        in_specs=[_cols(tb, GW, cb), _cols(tb, GW, cbu), _cols(tb, GW, cbv), vec, vec, mat, mat, _full((128, GW))],
        out_specs=[row, row, mat, _full((128, GW)), vec, vec],
        out_shape=[jax.ShapeDtypeStruct((t, GW), MXU_DTYPE)] * 2 + [jax.ShapeDtypeStruct((8, 128, 128), F32),
                   jax.ShapeDtypeStruct((128, GW), F32), jax.ShapeDtypeStruct((1, GW), F32), jax.ShapeDtypeStruct((1, GW), F32)],
        compiler_params=_params(("arbitrary",)))(dmixed, proj, proj, ng, nb, w, wt, bias)


S5_KB = 4
SUB = 8


def _s5_discretize(lam_re, lam_im, log_step, b_re, b_im):
    step = jnp.exp(log_step)[:, None]
    mag = jnp.exp(lam_re * step)
    lr, li = mag * jnp.cos(lam_im * step), mag * jnp.sin(lam_im * step)
    den = lam_re * lam_re + lam_im * lam_im
    fr = ((lr - 1.0) * lam_re + li * lam_im) / den
    fi = (li * lam_re - (lr - 1.0) * lam_im) / den
    return lr, li, fr[:, :, None] * b_re - fi[:, :, None] * b_im, fr[:, :, None] * b_im + fi[:, :, None] * b_re


def _cpow(lr, li, n):
    rr, ri = lr, li
    for _ in range(n - 1):
        rr, ri = rr * lr - ri * li, rr * li + ri * lr
    return rr, ri


def _s5_scan_consts(lr, li, reverse):
    lr, li = lr.reshape(1, S5_N), (-li if reverse else li).reshape(1, S5_N)
    row = jnp.arange(SUB)[:, None]
    out = []
    for s in (1, 2, 4):
        pr, pi = _cpow(lr, li, s)
        keep = (row < SUB - s) if reverse else (row >= s)
        out += [jnp.where(keep, pr, 0.0), jnp.where(keep, pi, 0.0)]
    cr, ci = [], []
    for i in range(SUB):
        pr, pi = _cpow(lr, li, SUB - i if reverse else i + 1)
        cr.append(pr)
        ci.append(pi)
    out += [jnp.concatenate(cr, axis=0), jnp.concatenate(ci, axis=0)]
    return jnp.stack(out)


def _s5_blockdiag_in(b):
    bt = jnp.swapaxes(b, 1, 2).reshape(S5_KB, 8, S5_H, S5_P)
    eye = jnp.eye(8, dtype=b.dtype)
    return jnp.einsum("kghp,gj->kghjp", bt, eye).reshape(S5_KB, 128, 512)


def _s5_blockdiag_in_extract(bb):
    x = bb.reshape(S5_KB, 8, S5_H, 8, S5_P)
    d = jnp.einsum("kghgp->kghp", x).reshape(S5_G, S5_H, S5_P)
    return jnp.swapaxes(d, 1, 2)


def _s5_blockdiag_out(c):
    ct = jnp.swapaxes(c, 1, 2).reshape(S5_KB, 8, S5_P, S5_H)
    eye = jnp.eye(8, dtype=c.dtype)
    return jnp.einsum("kgph,gj->kgpjh", ct, eye).reshape(S5_KB, 512, 128)


def _s5_blockdiag_out_extract(cc):
    x = cc.reshape(S5_KB, 8, S5_P, 8, S5_H)
    d = jnp.einsum("kgpgh->kgph", x).reshape(S5_G, S5_P, S5_H)
    return jnp.swapaxes(d, 1, 2)


def _s5_tile_scan(a, b, c_ref, carry, reverse):
    for si, s in enumerate((1, 2, 4)):
        sh = SUB - s if reverse else s
        ar, br = pltpu.roll(a, sh, 0), pltpu.roll(b, sh, 0)
        mr, mi = c_ref[2 * si], c_ref[2 * si + 1]
        a, b = a + mr * ar - mi * br, b + mr * br + mi * ar
    pr, pi = c_ref[6], c_ref[7]
    cr, ci = carry
    return a + pr * cr - pi * ci, b + pr * ci + pi * cr


def _s5_readout(xre_ref, xim_ref, ccre_ref, ccim_ref):
    return jnp.concatenate(
        [_dot(xre_ref[:, pl.ds(512 * k, 512)], ccre_ref[k], exact=True) - _dot(xim_ref[:, pl.ds(512 * k, 512)], ccim_ref[k], exact=True)
         for k in range(S5_KB)], axis=1)


def _s5_fwd(proj, bbre, bbim, ccre, ccim, dvec, consts, glu_w, glu_b, *, cb, tb=256):
    t = proj.shape[0]
    tb = min(tb, t)
    nt = tb // SUB

    def body(u_ref, bbre_ref, bbim_ref, ccre_ref, ccim_ref, d_ref, c_ref, w_ref, b_ref, o_ref, xre_ref, xim_ref, car):
        @pl.when(pl.program_id(0) == 0)
        def _():
            car[...] = jnp.zeros_like(car)

        u = u_ref[...]
        for k in range(S5_KB):
            uk = u[:, 128 * k:128 * (k + 1)]
            xre_ref[:, pl.ds(512 * k, 512)] = _dot(uk, bbre_ref[k], exact=True)
            xim_ref[:, pl.ds(512 * k, 512)] = _dot(uk, bbim_ref[k], exact=True)

        def tile(r, carry):
            sl = pl.ds(pl.multiple_of(r * SUB, SUB), SUB)
            a, b = _s5_tile_scan(xre_ref[sl, :], xim_ref[sl, :], c_ref, carry, False)
            xre_ref[sl, :] = a
            xim_ref[sl, :] = b
            return a[SUB - 1:SUB, :], b[SUB - 1:SUB, :]

        cr, ci = lax.fori_loop(0, nt, tile, (car[0:1, :], car[1:2, :]))
        car[0:1, :] = cr
        car[1:2, :] = ci
        ys = _s5_readout(xre_ref, xim_ref, ccre_ref, ccim_ref) + d_ref[...] * u
        yg = _gelu(ys)
        o_ref[...] = (yg * _sigmoid(_dot(yg, w_ref[...]) + b_ref[...])).astype(o_ref.dtype)

    row = pl.BlockSpec((tb, GW), lambda i: (i, 0))
    xrow = pl.BlockSpec((tb, S5_N), lambda i: (i, 0))
    vec = _full((1, GW))
    return pl.pallas_call(
        body, name="s5_fwd", grid=(t // tb,),
        in_specs=[_cols(tb, GW, cb), _full((4, 128, 512)), _full((4, 128, 512)), _full((4, 512, 128)), _full((4, 512, 128)),
                  vec, _full((8, SUB, S5_N)), _full((GW, GW)), vec],
        out_specs=[row, xrow, xrow],
        out_shape=[jax.ShapeDtypeStruct((t, GW), MXU_DTYPE), jax.ShapeDtypeStruct((t, S5_N), F32), jax.ShapeDtypeStruct((t, S5_N), F32)],
        scratch_shapes=[pltpu.VMEM((SUB, S5_N), F32)], compiler_params=_params(("arbitrary",)))(
            proj, bbre, bbim, ccre, ccim, dvec, consts, glu_w, glu_b)


def _s5_bwd(dmixed, proj, xre, xim, bbre, bbim, ccre, ccim, dvec, consts, glu_w, glu_b, *, cb_dy, cb, tb=256):
    t = proj.shape[0]
    tb = min(tb, t)
    nb = t // tb
    nt = tb // SUB

    def body(dy_ref, u_ref, xre_ref, xim_ref, bbre_ref, bbim_ref, ccre_ref, ccim_ref, d_ref, c_ref, w_ref, b_ref,
             du_ref, dw_ref, db_ref, dd_ref, dccre_ref, dccim_ref, dbbre_ref, dbbim_ref, sre_ref, sim_ref, are, aim, car):
        @pl.when(pl.program_id(0) == 0)
        def _():
            car[...] = jnp.zeros_like(car)
            for r in (dw_ref, db_ref, dd_ref, dccre_ref, dccim_ref, dbbre_ref, dbbim_ref, sre_ref, sim_ref):
                r[...] = jnp.zeros_like(r)

        u, dy = u_ref[...], dy_ref[...]
        ys = _s5_readout(xre_ref, xim_ref, ccre_ref, ccim_ref) + d_ref[...] * u
        yg = _gelu(ys)
        sg = _sigmoid(_dot(yg, w_ref[...]) + b_ref[...])
        dz = dy * yg * sg * (1.0 - sg)
        dyg = dy * sg + _dot(dz, w_ref[...], NT)
        dw_ref[...] += _dot(yg, dz, TN)
        db_ref[...] += jnp.sum(dz, axis=0, keepdims=True)
        dys = dyg * _gelu_grad(ys)
        dd_ref[...] += jnp.sum(dys * u, axis=0, keepdims=True)
        for k in range(S5_KB):
            dk = dys[:, 128 * k:128 * (k + 1)]
            lanes = pl.ds(512 * k, 512)
            are[:, lanes] = _dot(dk, ccre_ref[k], NT, exact=True)
            aim[:, lanes] = -_dot(dk, ccim_ref[k], NT, exact=True)
            dccre_ref[k] += _dot(xre_ref[:, lanes], dk, TN, exact=True)
            dccim_ref[k] -= _dot(xim_ref[:, lanes], dk, TN, exact=True)

        def tile(j, carry):
            sl = pl.ds(pl.multiple_of((nt - 1 - j) * SUB, SUB), SUB)
            gr, gi = are[sl, :], aim[sl, :]
            a, b = _s5_tile_scan(gr, gi, c_ref, carry, True)
            are[sl, :] = a
            aim[sl, :] = b
            er, ei = a - gr, b - gi
            xr, xi = xre_ref[sl, :], xim_ref[sl, :]
            sre_ref[...] += xr * er + xi * ei
            sim_ref[...] += xr * ei - xi * er
            return a[0:1, :], b[0:1, :]

        cr, ci = lax.fori_loop(0, nt, tile, (car[0:1, :], car[1:2, :]))
        car[0:1, :] = cr
        car[1:2, :] = ci
        dus = []
        for k in range(S5_KB):
            uk = u[:, 128 * k:128 * (k + 1)]
            lanes = pl.ds(512 * k, 512)
            dbbre_ref[k] += _dot(uk, are[:, lanes], TN, exact=True)
            dbbim_ref[k] += _dot(uk, aim[:, lanes], TN, exact=True)
            dus.append(_dot(are[:, lanes], bbre_ref[k], NT, exact=True) + _dot(aim[:, lanes], bbim_ref[k], NT, exact=True))
        du_ref[...] = (d_ref[...] * dys + jnp.concatenate(dus, axis=1)).astype(du_ref.dtype)

    row = pl.BlockSpec((tb, GW), lambda i: (nb - 1 - i, 0))
    xrow = pl.BlockSpec((tb, S5_N), lambda i: (nb - 1 - i, 0))
    vec = _full((1, GW))
    bbs, ccs = _full((4, 128, 512)), _full((4, 512, 128))
    sds = jax.ShapeDtypeStruct
    return pl.pallas_call(
        body, name="s5_bwd", grid=(nb,),
        in_specs=[_cols_rev(tb, GW, cb_dy, nb), _cols_rev(tb, GW, cb, nb), xrow, xrow, bbs, bbs, ccs, ccs, vec,
                  _full((8, SUB, S5_N)), _full((GW, GW)), vec],
        out_specs=[row, _full((GW, GW)), vec, vec, ccs, ccs, bbs, bbs, _full((SUB, S5_N)), _full((SUB, S5_N))],
        out_shape=[sds((t, GW), MXU_DTYPE), sds((GW, GW), F32), sds((1, GW), F32), sds((1, GW), F32), sds((4, 512, 128), F32),
                   sds((4, 512, 128), F32), sds((4, 128, 512), F32), sds((4, 128, 512), F32), sds((SUB, S5_N), F32),
                   sds((SUB, S5_N), F32)],
        scratch_shapes=[pltpu.VMEM((tb, S5_N), F32), pltpu.VMEM((tb, S5_N), F32), pltpu.VMEM((SUB, S5_N), F32)],
        compiler_params=_params(("arbitrary",)))(dmixed, proj, xre, xim, bbre, bbim, ccre, ccim, dvec, consts, glu_w, glu_b)


def _s5_prepare(lam_re, lam_im, log_step, b_re, b_im, c_re, c_im):
    lr, li, bbr, bbi = _s5_discretize(lam_re, lam_im, log_step, b_re, b_im)
    return dict(bbre=_s5_blockdiag_in(bbr), bbim=_s5_blockdiag_in(bbi), ccre=_s5_blockdiag_out(c_re), ccim=_s5_blockdiag_out(c_im),
                cf=_s5_scan_consts(lr, li, False), cr=_s5_scan_consts(lr, li, True))


def _s5_param_grads(lam_re, lam_im, log_step, b_re, b_im, dbbre, dbbim, dccre, dccim, sre, sim):
    (lr, li, _, _), vjp = jax.vjp(_s5_discretize, lam_re, lam_im, log_step, b_re, b_im)
    sr, si = jnp.sum(sre, axis=0).reshape(S5_G, S5_P), jnp.sum(sim, axis=0).reshape(S5_G, S5_P)
    den = lr * lr + li * li
    glr, gli = (sr * lr - si * li) / den, (si * lr + sr * li) / den
    dlam_re, dlam_im, dlog_step, db_re, db_im = vjp((glr, gli, _s5_blockdiag_in_extract(dbbre), _s5_blockdiag_in_extract(dbbim)))
    return dlam_re, dlam_im, dlog_step, db_re, db_im, _s5_blockdiag_out_extract(dccre), _s5_blockdiag_out_extract(dccim)


HALO = 8
AB_CB = 4096 // 128
Q_SCALE = DN_HD ** -0.5


def _halo_prev(tb, width, cb):
    return pl.BlockSpec((HALO, width), lambda i: (jnp.maximum(i * (tb // HALO) - 1, 0), cb))


def _halo_next(tb, width, cb, nrows):
    last = nrows // HALO - 1
    return pl.BlockSpec((HALO, width), lambda i: (jnp.minimum((i + 1) * (tb // HALO), last), cb))


def _silu_parts(c):
    sg = _sigmoid(c)
    return c * sg, sg * (1.0 + c * (1.0 - sg))


def _softplus(x):
    return jnp.maximum(x, 0.0) + jnp.log(1.0 + jnp.exp(-jnp.abs(x)))


def _dn_conv(x_ref, halo_ref, w_ref, part, ext, first):
    tb = x_ref.shape[0]
    ext[pl.ds(0, HALO), :] = jnp.where(first, 0.0, halo_ref[...])
    ext[pl.ds(HALO, tb), :] = x_ref[...]
    c = None
    for j in range(DN_CONV):
        term = w_ref[pl.ds(j, 1), pl.ds(512 * part, 512)] * ext[pl.ds(HALO - (DN_CONV - 1) + j, tb), :]
        c = term if c is None else c + term
    return c


def _dn_gb(ab, alog, dtb):
    lane = lax.broadcasted_iota(jnp.int32, ab.shape, 1)
    pre = ab + dtb
    g = -jnp.exp(alog) * _softplus(pre)
    beta = _sigmoid(ab)
    return jnp.where(lane < DN_HEADS, g, jnp.where(lane < 2 * DN_HEADS, beta, 0.0)), pre, beta


def _dn_prep_fwd(proj, conv_w, alog, dtb, *, cbq, tb=512):
    t = proj.shape[0]
    tb = min(tb, t)

    def body(xq, xk, xv, hq, hk, hv, ab_ref, w_ref, alog_ref, dtb_ref, qn_ref, kn_ref, vs_ref, gb_ref, ext):
        first = pl.program_id(0) == 0
        for part, (x_ref, h_ref, o_ref) in enumerate(((xq, hq, qn_ref), (xk, hk, kn_ref), (xv, hv, vs_ref))):
            s, _ = _silu_parts(_dn_conv(x_ref, h_ref, w_ref, part, ext, first))
            if part < 2:
                scale = Q_SCALE if part == 0 else 1.0
                for h in range(DN_HEADS):
                    sh = s[:, DN_HD * h:DN_HD * (h + 1)]
                    rn = lax.rsqrt(jnp.sum(sh * sh, axis=-1, keepdims=True) + L2_EPS)
                    o_ref[:, pl.ds(DN_HD * h, DN_HD)] = sh * (rn * scale)
            else:
                o_ref[...] = s
        gb_ref[...] = _dn_gb(ab_ref[...], alog_ref[...], dtb_ref[...])[0]

    row = pl.BlockSpec((tb, GW), lambda i: (i, 0))
    small = pl.BlockSpec((tb, 128), lambda i: (i, 0))
    v128 = _full((1, 128))
    sds = jax.ShapeDtypeStruct
    return pl.pallas_call(
        body, name="dn_prep_fwd", grid=(t // tb,),
        in_specs=[_cols(tb, GW, cbq), _cols(tb, GW, cbq + 1), _cols(tb, GW, cbq + 2),
                  _halo_prev(tb, GW, cbq), _halo_prev(tb, GW, cbq + 1), _halo_prev(tb, GW, cbq + 2),
                  _cols(tb, 128, AB_CB), _full((DN_CONV, 3 * GW)), v128, v128],
        out_specs=[row, row, row, small],
        out_shape=[sds((t, GW), F32)] * 3 + [sds((t, 128), F32)],
        scratch_shapes=[pltpu.VMEM((tb + HALO, GW), F32)], compiler_params=_params(("parallel",)))(
            proj, proj, proj, proj, proj, proj, proj, conv_w, alog, dtb)


def _dn_prep_bwd_a(proj, conv_w, alog, dtb, dqn, dkn, dvs, dgb, *, cbq, tb=512):
    t = proj.shape[0]
    tb = min(tb, t)

    def body(xq, xk, xv, hq, hk, hv, ab_ref, w_ref, alog_ref, dtb_ref, dqn_ref, dkn_ref, dvs_ref, dgb_ref,
             dcq_ref, dck_ref, dcv_ref, dab_ref, dalog_ref, ddtb_ref, ext):
        first = pl.program_id(0) == 0

        @pl.when(first)
        def _():
            dalog_ref[...] = jnp.zeros_like(dalog_ref)
            ddtb_ref[...] = jnp.zeros_like(ddtb_ref)

        for part, (x_ref, h_ref, d_ref, o_ref) in enumerate(((xq, hq, dqn_ref, dcq_ref), (xk, hk, dkn_ref, dck_ref), (xv, hv, dvs_ref, dcv_ref))):
            s, ds_dc = _silu_parts(_dn_conv(x_ref, h_ref, w_ref, part, ext, first))
            d = d_ref[...]
            if part < 2:
                scale = Q_SCALE if part == 0 else 1.0
                for h in range(DN_HEADS):
                    lanes = slice(DN_HD * h, DN_HD * (h + 1))
                    sh, dh = s[:, lanes], d[:, lanes]
                    rn = lax.rsqrt(jnp.sum(sh * sh, axis=-1, keepdims=True) + L2_EPS)
                    dsh = scale * (rn * dh - sh * (rn * rn * rn) * jnp.sum(dh * sh, axis=-1, keepdims=True))
                    o_ref[:, pl.ds(DN_HD * h, DN_HD)] = dsh * ds_dc[:, lanes]
            else:
                o_ref[...] = d * ds_dc
        ab, dgb_v = ab_ref[...], dgb_ref[...]
        gb, pre, beta = _dn_gb(ab, alog_ref[...], dtb_ref[...])
        lane = lax.broadcasted_iota(jnp.int32, ab.shape, 1)
        is_g = lane < DN_HEADS
        da = jnp.where(is_g, dgb_v * (-jnp.exp(alog_ref[...])) * _sigmoid(pre), 0.0)
        db = jnp.where((lane >= DN_HEADS) & (lane < 2 * DN_HEADS), dgb_v * beta * (1.0 - beta), 0.0)
        dab_ref[...] = (da + db).astype(dab_ref.dtype)
        ddtb_ref[...] += jnp.sum(da, axis=0, keepdims=True)
        dalog_ref[...] += jnp.sum(jnp.where(is_g, dgb_v * gb, 0.0), axis=0, keepdims=True)

    row = pl.BlockSpec((tb, GW), lambda i: (i, 0))
    small = pl.BlockSpec((tb, 128), lambda i: (i, 0))
    v128 = _full((1, 128))
    sds = jax.ShapeDtypeStruct
    return pl.pallas_call(
        body, name="dn_prep_bwd_a", grid=(t // tb,),
        in_specs=[_cols(tb, GW, cbq), _cols(tb, GW, cbq + 1), _cols(tb, GW, cbq + 2),
                  _halo_prev(tb, GW, cbq), _halo_prev(tb, GW, cbq + 1), _halo_prev(tb, GW, cbq + 2),
                  _cols(tb, 128, AB_CB), _full((DN_CONV, 3 * GW)), v128, v128, row, row, row, small],
        out_specs=[row, row, row, small, v128, v128],
        out_shape=[sds((t, GW), F32)] * 3 + [sds((t, 128), MXU_DTYPE), sds((1, 128), F32), sds((1, 128), F32)],
        scratch_shapes=[pltpu.VMEM((tb + HALO, GW), F32)], compiler_params=_params(("arbitrary",)))(
            proj, proj, proj, proj, proj, proj, proj, conv_w, alog, dtb, dqn, dkn, dvs, dgb)


def _dn_prep_bwd_b(proj, conv_w, dcq, dck, dcv, *, cbq, tb=512):
    t = proj.shape[0]
    tb = min(tb, t)
    nb = t // tb

    def body(xq, xk, xv, hq, hk, hv, dq_in, dk_in, dv_in, nq, nk, nv, w_ref, dq_ref, dk_ref, dv_ref, dw_ref, ext):
        i = pl.program_id(0)
        first, last = i == 0, i == nb - 1

        @pl.when(first)
        def _():
            dw_ref[...] = jnp.zeros_like(dw_ref)

        for part, (x_ref, h_ref, d_ref, n_ref, o_ref) in enumerate(
                ((xq, hq, dq_in, nq, dq_ref), (xk, hk, dk_in, nk, dk_ref), (xv, hv, dv_in, nv, dv_ref))):
            lanes = pl.ds(512 * part, 512)
            d = d_ref[...]
            ext[pl.ds(0, HALO), :] = jnp.where(first, 0.0, h_ref[...])
            ext[pl.ds(HALO, tb), :] = x_ref[...]
            for j in range(DN_CONV):
                xs = ext[pl.ds(HALO - (DN_CONV - 1) + j, tb), :]
                dw_ref[pl.ds(j, 1), lanes] += jnp.sum(d * xs, axis=0, keepdims=True)
            ext[pl.ds(0, tb), :] = d
            ext[pl.ds(tb, HALO), :] = jnp.where(last, 0.0, n_ref[...])
            acc = None
            for j in range(DN_CONV):
                term = w_ref[pl.ds(j, 1), lanes] * ext[pl.ds(DN_CONV - 1 - j, tb), :]
                acc = term if acc is None else acc + term
            o_ref[...] = acc.astype(o_ref.dtype)

    row = pl.BlockSpec((tb, GW), lambda i: (i, 0))
    nxt = _halo_next(tb, GW, 0, t)
    sds = jax.ShapeDtypeStruct
    return pl.pallas_call(
        body, name="dn_prep_bwd_b", grid=(nb,),
        in_specs=[_cols(tb, GW, cbq), _cols(tb, GW, cbq + 1), _cols(tb, GW, cbq + 2),
                  _halo_prev(tb, GW, cbq), _halo_prev(tb, GW, cbq + 1), _halo_prev(tb, GW, cbq + 2),
                  row, row, row, nxt, nxt, nxt, _full((DN_CONV, 3 * GW))],
        out_specs=[row, row, row, _full((DN_CONV, 3 * GW))],
        out_shape=[sds((t, GW), MXU_DTYPE)] * 3 + [sds((DN_CONV, 3 * GW), F32)],
        scratch_shapes=[pltpu.VMEM((tb + HALO, GW), F32)], compiler_params=_params(("arbitrary",)))(
            proj, proj, proj, proj, proj, proj, dcq, dck, dcv, dcq, dck, dcv, conv_w)


def _tri_inv(n, eye):
    x = eye - n
    p = _dot(n, n, exact=True)
    for _ in range(4):
        x = x + _dot(x, p, exact=True)
        p = _dot(p, p, exact=True)
    return x + _dot(x, p, exact=True)


def _dn_masks():
    r = lax.broadcasted_iota(jnp.int32, (DN_CHUNK, DN_CHUNK), 0)
    c = lax.broadcasted_iota(jnp.int32, (DN_CHUNK, DN_CHUNK), 1)
    return r == c, r >= c, r > c, r <= c


def _to_row(col, eye):
    return jnp.sum(jnp.where(eye, col, 0.0), axis=0, keepdims=True)


def _dn_chunk_math(q, k, v, g, beta, s, masks):
    eye, tril, strict, triu = masks
    g_row = _to_row(g, eye)
    gc = jnp.sum(jnp.where(tril, g_row, 0.0), axis=1, keepdims=True)
    gc_row = jnp.sum(jnp.where(triu, g, 0.0), axis=0, keepdims=True)
    decay = jnp.where(tril, jnp.exp(jnp.minimum(gc - gc_row, 0.0)), 0.0)
    eg = jnp.exp(gc)
    gl = gc[DN_CHUNK - 1:DN_CHUNK, :]
    eg_last = jnp.exp(gl)
    etail = jnp.exp(gl - gc)
    kb, vb = k * beta, v * beta
    kk = _dot(kb, k, NT)
    tm = _tri_inv(jnp.where(strict, kk * decay, 0.0), eye.astype(F32))
    kbg = kb * eg
    u = _dot(tm, vb)
    wm = _dot(tm, kbg)
    qk = _dot(q, k, NT)
    qkd = qk * decay
    qg, ktail = q * eg, k * etail
    v_new = u - _dot(wm, s)
    return dict(decay=decay, eg=eg, eg_last=eg_last, etail=etail, kb=kb, vb=vb, kk=kk, tm=tm, kbg=kbg, wm=wm, qk=qk, qkd=qkd,
                qg=qg, ktail=ktail, v_new=v_new)


def _dn_chunk_fwd(qn, kn, vs, gb, proj, norm_g, *, cb_gate):
    t = qn.shape[0]
    nc = t // DN_CHUNK

    def body(q_ref, k_ref, v_ref, gb_ref, gate_ref, ng_ref, o_ref, raw_ref, st_ref, s_ref):
        @pl.when(pl.program_id(0) == 0)
        def _():
            s_ref[...] = jnp.zeros_like(s_ref)

        masks = _dn_masks()
        gbv = gb_ref[...]
        for h in range(DN_HEADS):
            lanes = pl.ds(DN_HD * h, DN_HD)
            s = s_ref[h]
            st_ref[0, h] = s
            m = _dn_chunk_math(q_ref[:, lanes], k_ref[:, lanes], v_ref[:, lanes], gbv[:, h:h + 1],
                               gbv[:, DN_HEADS + h:DN_HEADS + h + 1], s, masks)
            o = _dot(m["qg"], s) + _dot(m["qkd"], m["v_new"])
            s_ref[h] = s * m["eg_last"] + _dot(m["ktail"], m["v_new"], TN)
            raw_ref[:, lanes] = o
            r = lax.rsqrt(jnp.mean(o * o, axis=-1, keepdims=True) + RMS_EPS)
            gt = gate_ref[:, lanes]
            o_ref[:, lanes] = (o * r * ng_ref[...] * (gt * _sigmoid(gt))).astype(o_ref.dtype)

    row = pl.BlockSpec((DN_CHUNK, GW), lambda i: (i, 0))
    sds = jax.ShapeDtypeStruct
    return pl.pallas_call(
        body, name="dn_chunk_fwd", grid=(nc,),
        in_specs=[row, row, row, pl.BlockSpec((DN_CHUNK, 128), lambda i: (i, 0)), _cols(DN_CHUNK, GW, cb_gate), _full((1, DN_HD))],
        out_specs=[row, row, pl.BlockSpec((1, DN_HEADS, DN_HD, DN_HD), lambda i: (i, 0, 0, 0))],
        out_shape=[sds((t, GW), MXU_DTYPE), sds((t, GW), F32), sds((nc, DN_HEADS, DN_HD, DN_HD), F32)],
        scratch_shapes=[pltpu.VMEM((DN_HEADS, DN_HD, DN_HD), F32)], compiler_params=_params(("arbitrary",)))(
            qn, kn, vs, gb, proj, norm_g)


def _dn_chunk_bwd(dmixed, qn, kn, vs, gb, proj, raw, states, norm_g, *, cb_dy, cb_gate):
    t = qn.shape[0]
    nc = t // DN_CHUNK

    def body(dy_ref, q_ref, k_ref, v_ref, gb_ref, gate_ref, raw_ref, st_ref, ng_ref,
             dq_ref, dk_ref, dv_ref, dgate_ref, dgb_ref, dng_ref, ds_ref):
        @pl.when(pl.program_id(0) == 0)
        def _():
            ds_ref[...] = jnp.zeros_like(ds_ref)
            dng_ref[...] = jnp.zeros_like(dng_ref)

        masks = _dn_masks()
        eye, tril, strict, triu = masks
        gbv = gb_ref[...]
        lane = lax.broadcasted_iota(jnp.int32, (DN_CHUNK, 128), 1)
        last_row = lax.broadcasted_iota(jnp.int32, (DN_CHUNK, 1), 0) == DN_CHUNK - 1
        dgb = jnp.zeros((DN_CHUNK, 128), F32)
        for h in range(DN_HEADS):
            lanes = pl.ds(DN_HD * h, DN_HD)
            q, k, v = q_ref[:, lanes], k_ref[:, lanes], v_ref[:, lanes]
            beta = gbv[:, DN_HEADS + h:DN_HEADS + h + 1]
            s, ds_out = st_ref[0, h], ds_ref[h]
            o, gt, dy, ng = raw_ref[:, lanes], gate_ref[:, lanes], dy_ref[:, lanes], ng_ref[...]
            r = lax.rsqrt(jnp.mean(o * o, axis=-1, keepdims=True) + RMS_EPS)
            sil, dsil = _silu_parts(gt)
            d_on = dy * sil
            dgate_ref[:, lanes] = (dy * (o * r * ng) * dsil).astype(dgate_ref.dtype)
            dng_ref[...] += jnp.sum(d_on * o * r, axis=0, keepdims=True)
            w = d_on * ng
            do = r * w - o * (r * r * r) * jnp.mean(w * o, axis=-1, keepdims=True)
            m = _dn_chunk_math(q, k, v, gbv[:, h:h + 1], beta, s, masks)
            decay, eg, tm = m["decay"], m["eg"], m["tm"]
            d_vnew = _dot(m["qkd"], do, TN) + _dot(m["ktail"], ds_out)
            d_qkd = jnp.where(tril, _dot(do, m["v_new"], NT), 0.0)
            d_qg = _dot(do, s, NT)
            ds_ref[h] = _dot(m["qg"], do, TN) + m["eg_last"] * ds_out - _dot(m["wm"], d_vnew, TN)
            d_eglast = jnp.sum(jnp.sum(s * ds_out, axis=1, keepdims=True), axis=0, keepdims=True)
            d_ktail = _dot(m["v_new"], ds_out, NT)
            d_wm = -_dot(d_vnew, s, NT)
            dq = d_qg * eg
            dk = d_ktail * m["etail"]
            tail_term = jnp.sum(d_ktail * m["ktail"], axis=1, keepdims=True)
            dgc = jnp.sum(d_qg * m["qg"], axis=1, keepdims=True) - tail_term
            dgl = jnp.sum(tail_term, axis=0, keepdims=True) + d_eglast * m["eg_last"]
            dqk_dec = d_qkd * decay
            dq = dq + _dot(dqk_dec, k)
            dk = dk + _dot(dqk_dec, q, TN)
            ddecay = d_qkd * m["qk"]
            d_tm = _dot(d_vnew, m["vb"], NT) + _dot(d_wm, m["kbg"], NT)
            d_vb = _dot(tm, d_vnew, TN)
            d_kbg = _dot(tm, d_wm, TN)
            d_kb = d_kbg * eg
            dgc = dgc + jnp.sum(d_kbg * m["kbg"], axis=1, keepdims=True)
            d_n = jnp.where(strict, -_dot(_dot(tm, d_tm, TN, exact=True), tm, NT, exact=True), 0.0)
            d_kk = d_n * decay
            d_kb = d_kb + _dot(d_kk, k)
            dk = dk + _dot(d_kk, m["kb"], TN)
            ddecay = ddecay + d_n * m["kk"]
            dk = dk + d_kb * beta
            dbeta = jnp.sum(d_kb * k, axis=1, keepdims=True) + jnp.sum(d_vb * v, axis=1, keepdims=True)
            dv_ref[:, lanes] = d_vb * beta
            dq_ref[:, lanes] = dq
            dk_ref[:, lanes] = dk
            dd = ddecay * decay
            dgc = dgc + jnp.sum(dd, axis=1, keepdims=True) + jnp.where(last_row, dgl, 0.0)
            dgc_row = _to_row(dgc, eye) - jnp.sum(dd, axis=0, keepdims=True)
            dg = jnp.sum(jnp.where(triu, dgc_row, 0.0), axis=1, keepdims=True)
            dgb = dgb + jnp.where(lane == h, dg, 0.0) + jnp.where(lane == DN_HEADS + h, dbeta, 0.0)
        dgb_ref[...] = dgb

    row = pl.BlockSpec((DN_CHUNK, GW), lambda i: (nc - 1 - i, 0))
    small = pl.BlockSpec((DN_CHUNK, 128), lambda i: (nc - 1 - i, 0))
    sds = jax.ShapeDtypeStruct
    return pl.pallas_call(
        body, name="dn_chunk_bwd", grid=(nc,),
        in_specs=[_cols_rev(DN_CHUNK, GW, cb_dy, nc), row, row, row, small, _cols_rev(DN_CHUNK, GW, cb_gate, nc), row,
                  pl.BlockSpec((1, DN_HEADS, DN_HD, DN_HD), lambda i: (nc - 1 - i, 0, 0, 0)), _full((1, DN_HD))],
        out_specs=[row, row, row, row, small, _full((1, DN_HD))],
        out_shape=[sds((t, GW), F32)] * 3 + [sds((t, GW), MXU_DTYPE), sds((t, 128), F32), sds((1, DN_HD), F32)],
        scratch_shapes=[pltpu.VMEM((DN_HEADS, DN_HD, DN_HD), F32)], compiler_params=_params(("arbitrary",)))(
            dmixed, qn, kn, vs, gb, proj, raw, states, norm_g)


ANY = pl.BlockSpec(memory_space=pl.ANY)
PAIR_SPLIT = 4


def _place():
    x, y, c = lax.axis_index("x"), lax.axis_index("y"), lax.axis_index("c")
    chips = [(1 - x, y), (x, 1 - y), (1 - x, 1 - y)]
    return x, y, c, chips


def _remote(src, dst, send_sem, recv_sem, to):
    return pltpu.make_async_remote_copy(src_ref=src, dst_ref=dst, send_sem=send_sem, recv_sem=recv_sem, device_id=to,
                                        device_id_type=MESH)


def _allgather_chips(arrs):
    n = len(arrs)

    def body(*refs):
        ins, outs = refs[:n], refs[n:2 * n]
        send_sems, recv_sems = refs[2 * n:]
        x, y, c, chips = _place()
        me = 2 * x + y
        sibling = (x, y, 1 - c)
        sends, fwds = [], []
        for a in range(n):
            half = ins[a].shape[0] // 2
            mine = pl.ds(c * half, half)
            for k, chip in enumerate(chips):
                s = _remote(ins[a].at[mine], outs[a].at[me, mine], send_sems.at[a, k], recv_sems.at[a, k], (*chip, c))
                s.start()
                sends.append(s)
        for a in range(n):
            half = ins[a].shape[0] // 2
            mine, other = pl.ds(c * half, half), pl.ds((1 - c) * half, half)
            for k, (cx, cy) in enumerate(chips):
                j = 2 * cx + cy
                got = outs[a].at[j, mine]
                _remote(got, got, send_sems.at[a, k], recv_sems.at[a, k], (cx, cy, c)).wait_recv()
                f = _remote(got, got, send_sems.at[a, 3 + k], recv_sems.at[a, 3 + k], sibling)
                f.start()
                fwds.append(f)
        for a in range(n):
            half = ins[a].shape[0] // 2
            other = pl.ds((1 - c) * half, half)
            for k, (cx, cy) in enumerate(chips):
                got = outs[a].at[2 * cx + cy, other]
                _remote(got, got, send_sems.at[a, 3 + k], recv_sems.at[a, 3 + k], sibling).wait_recv()
        for s in sends + fwds:
            s.wait_send()

    return pl.pallas_call(
        body, name="allgather_weights", in_specs=[ANY] * n, out_specs=[ANY] * n,
        out_shape=[jax.ShapeDtypeStruct((4,) + a.shape, a.dtype) for a in arrs],
        scratch_shapes=[pltpu.SemaphoreType.DMA((n, 6)), pltpu.SemaphoreType.DMA((n, 6))])(*arrs)


def _pair_exchange(gbs):
    n = len(gbs)

    def body(*refs):
        ins, got_refs = refs[:n], refs[n:2 * n]
        send_sems, recv_sems = refs[2 * n:]
        x, y, c, _ = _place()
        work = []
        for a in range(n):
            half = ins[a].shape[1] // 2
            piece = half // PAIR_SPLIT
            for r in range(PAIR_SPLIT):
                s = _remote(ins[a].at[:, pl.ds((1 - c) * half + r * piece, piece)], got_refs[a].at[:, pl.ds(r * piece, piece)],
                            send_sems.at[a, r], recv_sems.at[a, r], (x, y, 1 - c))
                s.start()
                work.append(s)
        for s in work:
            s.wait()

    return pl.pallas_call(
        body, name="grad_pair_exchange", in_specs=[ANY] * n, out_specs=[ANY] * n,
        out_shape=[jax.ShapeDtypeStruct((4, g.shape[1] // 2, g.shape[2]), g.dtype) for g in gbs],
        scratch_shapes=[pltpu.SemaphoreType.DMA((n, PAIR_SPLIT)), pltpu.SemaphoreType.DMA((n, PAIR_SPLIT))])(*gbs)


def _chip_exchange(ps):
    n = len(ps)

    def body(*refs):
        ins, got_refs = refs[:n], refs[n:2 * n]
        send_sems, recv_sems = refs[2 * n:]
        x, y, c, chips = _place()
        work = []
        for a in range(n):
            for k, (cx, cy) in enumerate(chips):
                s = _remote(ins[a].at[2 * cx + cy], got_refs[a].at[k], send_sems.at[a, k], recv_sems.at[a, k], (cx, cy, c))
                s.start()
                work.append(s)
        for s in work:
            s.wait()

    return pl.pallas_call(
        body, name="grad_chip_exchange", in_specs=[ANY] * n, out_specs=[ANY] * n,
        out_shape=[jax.ShapeDtypeStruct((3,) + p.shape[1:], p.dtype) for p in ps],
        scratch_shapes=[pltpu.SemaphoreType.DMA((n, 3)), pltpu.SemaphoreType.DMA((n, 3))])(*ps)


def _pair_join(bufs):
    n = len(bufs)

    def body(*refs):
        outs = refs[n:2 * n]
        send_sems, recv_sems = refs[2 * n:]
        x, y, c, _ = _place()
        work = []
        for a in range(n):
            s = _remote(outs[a].at[c], outs[a].at[c], send_sems.at[a], recv_sems.at[a], (x, y, 1 - c))
            s.start()
            work.append(s)
        for s in work:
            s.wait()

    return pl.pallas_call(
        body, name="grad_pair_join", in_specs=[ANY] * n, out_specs=[ANY] * n,
        out_shape=[jax.ShapeDtypeStruct(b.shape, b.dtype) for b in bufs], input_output_aliases={a: a for a in range(n)},
        scratch_shapes=[pltpu.SemaphoreType.DMA((n,)), pltpu.SemaphoreType.DMA((n,))])(*bufs)


def _pair_sum(gb, got, place, *, name, block_bytes=1 << 20):
    _, r, cols = gb.shape
    half = r // 2
    tr = _row_tile(half, cols, block_bytes)

    def body(place_ref, g_ref, got_ref, o_ref):
        o_ref[...] = g_ref[...] + got_ref[...]

    blk = pl.BlockSpec((None, tr, cols), lambda j, i, p: (j, i, 0))
    grid_spec = pltpu.PrefetchScalarGridSpec(
        num_scalar_prefetch=1, grid=(4, half // tr),
        in_specs=[pl.BlockSpec((None, None, tr, cols), lambda j, i, p: (j, p[0], i, 0)), blk], out_specs=blk)
    return pl.pallas_call(body, name=name, grid_spec=grid_spec, out_shape=jax.ShapeDtypeStruct((4, half, cols), F32),
                          compiler_params=_params(("parallel", "parallel")))(place, gb.reshape(4, 2, half, cols), got)


def _chip_sum(p, got, place, *, name, block_bytes=1 << 20):
    _, h, cols = p.shape
    tr = _row_tile(h, cols, block_bytes)

    def body(place_ref, p_ref, g0, g1, g2, o_ref):
        o_ref[...] = p_ref[...] + g0[...] + g1[...] + g2[...]

    def got_spec(k):
        return pl.BlockSpec((None, tr, cols), functools.partial(lambda i, pr, k: (k, i, 0), k=k))

    grid_spec = pltpu.PrefetchScalarGridSpec(
        num_scalar_prefetch=1, grid=(h // tr,),
        in_specs=[pl.BlockSpec((None, tr, cols), lambda i, pr: (pr[1], i, 0)), got_spec(0), got_spec(1), got_spec(2)],
        out_specs=pl.BlockSpec((None, tr, cols), lambda i, pr: (pr[0], i, 0)))
    return pl.pallas_call(body, name=name, grid_spec=grid_spec, out_shape=jax.ShapeDtypeStruct((2, h, cols), F32),
                          compiler_params=_params(("parallel",)))(place, p, got, got, got)


def _allgather_all(v):
    def body(v_ref, out_ref, send_sems, recv_sems, local_sem):
        x, y, c, chips = _place()
        me, sibling = (x, y, c), (x, y, 1 - c)

        def rows(px, py, pc):
            return out_ref.at[4 * px + 2 * py + pc]

        def copy(k, block, to, src=None):
            return _remote(rows(*block) if src is None else src, rows(*block), send_sems.at[k], recv_sems.at[k], to)

        mine = pltpu.make_async_copy(v_ref, rows(*me), local_sem)
        mine.start()
        first = [copy(0, me, sibling, src=v_ref)] + [copy(1 + j, me, (*chip, c), src=v_ref) for j, chip in enumerate(chips)]
        for cp in first:
            cp.start()
        passed = [copy(4 + j, (*chip, c), sibling) for j, chip in enumerate(chips)]
        for j, chip in enumerate(chips):
            copy(1 + j, (*chip, c), me).wait_recv()
            passed[j].start()
        copy(0, sibling, me).wait_recv()
        for j, chip in enumerate(chips):
            copy(4 + j, (*chip, 1 - c), me).wait_recv()
        for cp in first + passed:
            cp.wait_send()
        mine.wait()

    return pl.pallas_call(
        body, name="allgather_small", in_specs=[ANY], out_specs=ANY, out_shape=jax.ShapeDtypeStruct((8,) + v.shape, v.dtype),
        scratch_shapes=[pltpu.SemaphoreType.DMA((7,)), pltpu.SemaphoreType.DMA((7,)), pltpu.SemaphoreType.DMA],
        )(v)


def _row_tile(rows, cols, limit_bytes):
    for d in range(1, rows + 1):
        if rows % d == 0 and (rows // d) % 8 == 0 and (rows // d) * cols * 4 <= limit_bytes:
            return rows // d
    return rows


def _sum_kernel(parts, *, name, block_bytes=1 << 20):
    n = len(parts)
    rows, cols = parts[0][0].shape[1:] if isinstance(parts[0], tuple) else parts[0].shape
    tr = _row_tile(rows, cols, block_bytes)
    ins, specs = [], []
    for part in parts:
        if isinstance(part, tuple):
            ins.append(part[0])
            specs.append(pl.BlockSpec((None, tr, cols), functools.partial(lambda i, s: (s, i, 0), s=part[1])))
        else:
            ins.append(part)
            specs.append(pl.BlockSpec((tr, cols), lambda i: (i, 0)))

    def body(*refs):
        acc = refs[0][...]
        for r in refs[1:n]:
            acc = acc + r[...]
        refs[n][...] = acc

    return pl.pallas_call(body, name=name, grid=(rows // tr,), in_specs=specs, out_specs=pl.BlockSpec((tr, cols), lambda i: (i, 0)),
                          out_shape=jax.ShapeDtypeStruct((rows, cols), F32), compiler_params=_params(("parallel",)))(*ins)


def _reduce_scatter(gbs, names, place):
    got = _pair_exchange(gbs)
    ps = [_pair_sum(g, r, place, name="pair_sum_" + nm) for g, r, nm in zip(gbs, got, names)]
    others = _chip_exchange(ps)
    bufs = [_chip_sum(p, t, place, name="chip_sum_" + nm) for p, t, nm in zip(ps, others, names)]
    return [j.reshape(-1, j.shape[-1]) for j in _pair_join(bufs)]


def _adamw(w, g, m, v, *, name, block_bytes=1 << 20):
    rows, cols = w.shape
    tr = _row_tile(rows, cols, block_bytes)

    def body(w_ref, g_ref, m_ref, v_ref, d_ref, nm_ref, nv_ref):
        gv = g_ref[...]
        nm = ADAM_B1 * m_ref[...] + (1.0 - ADAM_B1) * gv
        nv = ADAM_B2 * v_ref[...] + (1.0 - ADAM_B2) * (gv * gv)
        m_hat = nm / (1.0 - ADAM_B1 ** ADAM_STEP)
        v_hat = nv / (1.0 - ADAM_B2 ** ADAM_STEP)
        d_ref[...] = -ADAM_LR * (m_hat / (jnp.sqrt(v_hat) + ADAM_EPS) + ADAM_WD * w_ref[...])
        nm_ref[...] = nm
        nv_ref[...] = nv

    spec = pl.BlockSpec((tr, cols), lambda i: (i, 0))
    return pl.pallas_call(body, name=name, grid=(rows // tr,), in_specs=[spec] * 4, out_specs=[spec] * 3,
                          out_shape=[jax.ShapeDtypeStruct((rows, cols), F32)] * 3, compiler_params=_params(("parallel",)))(w, g, m, v)


WEIGHTS = ['w_in', 's5_lambda_re', 's5_lambda_im', 's5_log_step', 's5_b_re', 's5_b_im', 's5_c_re', 's5_c_im', 's5_d', 's5_glu_w',
           's5_glu_b', 'sgu_norm_g', 'sgu_norm_b', 'sgu_w', 'sgu_b', 'pool_w', 'pool_scale', 'dn_conv_w', 'dn_a_log', 'dn_dt_bias',
           'dn_norm_g', 'w_out', 'ln1_g', 'ln1_b', 'w_up', 'w_down', 'ln2_g', 'ln2_b']
BIG = ['w_in', 's5_glu_w', 'w_out', 'w_up', 'w_down']
SMALL = [n for n in WEIGHTS if n not in BIG]
CB_S5, CB_SGU_U, CB_SGU_V, CB_POOL, CB_DN_Q, CB_DN_GATE = 0, 1, 2, 3, 4, 7


def _pad_lanes(v, width=128):
    return jnp.zeros((1, width), F32).at[0, :v.shape[0]].set(v)


def _layer_consts(p):
    c = _s5_prepare(p['s5_lambda_re'], p['s5_lambda_im'], p['s5_log_step'], p['s5_b_re'], p['s5_b_im'], p['s5_c_re'], p['s5_c_im'])
    tril = jnp.tril(jnp.ones((SGU_CHUNK, SGU_CHUNK), bool))
    wm = jnp.where(tril, p['sgu_w'], 0.0)
    c.update(s5_d=p['s5_d'].reshape(1, GW), glu_b=p['s5_glu_b'].reshape(1, GW), sgu_ng=p['sgu_norm_g'].reshape(1, GW),
             sgu_nb=p['sgu_norm_b'].reshape(1, GW), sgu_w=wm, sgu_wt=jnp.swapaxes(wm, 1, 2),
             sgu_bias=jnp.repeat(p['sgu_b'].T, SGU_HD, axis=1), pool_w=p['pool_w'], pool_scale=p['pool_scale'].reshape(1, GW),
             conv_w=p['dn_conv_w'], alog=_pad_lanes(p['dn_a_log']), dtb=_pad_lanes(p['dn_dt_bias']), dn_ng=p['dn_norm_g'].reshape(1, DN_HD),
             ln1_g=p['ln1_g'].reshape(1, D_MODEL), ln1_b=p['ln1_b'].reshape(1, D_MODEL), ln2_g=p['ln2_g'].reshape(1, D_MODEL),
             ln2_b=p['ln2_b'].reshape(1, D_MODEL))
    return c


def _layer_fwd(xin, xin16, w, c, i):
    tag = str(i)
    residual = lambda r, e: (r + ALPHA * e,)
    (proj,) = _matmul(xin16, w['w_in'], mode="nn", name="proj" + tag, tn=1408)
    s5, xre, xim = _s5_fwd(proj, c['bbre'], c['bbim'], c['ccre'], c['ccim'], c['s5_d'], c['cf'], w['s5_glu_w'], c['glu_b'], cb=CB_S5)
    sgu = _sgu_fwd(proj, c['sgu_ng'], c['sgu_nb'], c['sgu_w'], c['sgu_bias'], cbu=CB_SGU_U, cbv=CB_SGU_V)
    pool, pooled = _pool_fwd(proj, c['pool_w'], c['pool_scale'], cb=CB_POOL)
    qn, kn, vs, gb = _dn_prep_fwd(proj, c['conv_w'], c['alog'], c['dtb'], cbq=CB_DN_Q)
    dn, raw, states = _dn_chunk_fwd(qn, kn, vs, gb, proj, c['dn_ng'], cb_gate=CB_DN_GATE)
    mixed = jnp.concatenate([s5, sgu, pool, dn], axis=1)
    (h1,) = _matmul(mixed, w['w_out'], mode="nn", name="mix_out" + tag, e=xin, epi=residual)
    x1, x1_16 = _ln_fwd(h1, c['ln1_g'], c['ln1_b'], name="ln1_" + tag)
    a, hidden = _matmul(x1_16, w['w_up'], mode="nn", name="mlp_up" + tag, epi=lambda r, e: (r, _relu2(r)), out_dtypes=(F32, MXU_DTYPE),
                        tm=2048)
    (h2,) = _matmul(hidden, w['w_down'], mode="nn", name="mlp_down" + tag, e=x1, epi=residual, tm=2048)
    x2, x2_16 = _ln_fwd(h2, c['ln2_g'], c['ln2_b'], name="ln2_" + tag)
    saved = dict(xin16=xin16, proj=proj, xre=xre, xim=xim, pooled=pooled, qn=qn, kn=kn, vs=vs, gb=gb, raw=raw, states=states,
                 mixed=mixed, h1=h1, x1_16=x1_16, a=a, hidden=hidden, h2=h2)
    return x2, x2_16, saved


def _layer_bwd(dx2, s, w, c, p, i):
    tag = str(i)
    residual = lambda r, e: (r + ALPHA * e,)
    dh2, dh2_16, dln2g, dln2b = _ln_bwd(dx2, s['h2'], c['ln2_g'], name="ln2_bwd" + tag)
    (dw_down,) = _matmul(s['hidden'], dh2_16, mode="tn", name="dw_down" + tag, tk=1024)
    (da,) = _matmul(dh2_16, w['w_down'], mode="nt", name="d_hidden" + tag, e=s['a'], epi=lambda r, e: (r * (2.0 * jnp.maximum(e, 0.0)),),
                    out_dtypes=(MXU_DTYPE,), tm=2048)
    (dw_up,) = _matmul(s['x1_16'], da, mode="tn", name="dw_up" + tag, tk=1024)
    (dx1,) = _matmul(da, w['w_up'], mode="nt", name="dx_mlp" + tag, e=dh2, epi=residual, tm=2048)
    dh1, dh1_16, dln1g, dln1b = _ln_bwd(dx1, s['h1'], c['ln1_g'], name="ln1_bwd" + tag)
    (dw_out,) = _matmul(s['mixed'], dh1_16, mode="tn", name="dw_out" + tag, tk=1024)
    (dmixed,) = _matmul(dh1_16, w['w_out'], mode="nt", name="d_mixed" + tag, tm=2048)
    proj = s['proj']
    (du, dglu_w, dglu_b, dd, dccre, dccim, dbbre, dbbim, sre, sim) = _s5_bwd(
        dmixed, proj, s['xre'], s['xim'], c['bbre'], c['bbim'], c['ccre'], c['ccim'], c['s5_d'], c['cr'], w['s5_glu_w'], c['glu_b'],
        cb_dy=0, cb=CB_S5)
    dlam_re, dlam_im, dlog_step, db_re, db_im, dc_re, dc_im = _s5_param_grads(
        p['s5_lambda_re'], p['s5_lambda_im'], p['s5_log_step'], p['s5_b_re'], p['s5_b_im'], dbbre, dbbim, dccre, dccim, sre, sim)
    dzu, dzv, dsgu_w, dsgu_bias, dsgu_ng, dsgu_nb = _sgu_bwd(dmixed, proj, c['sgu_ng'], c['sgu_nb'], c['sgu_w'], c['sgu_wt'], c['sgu_bias'],
                                                            cb=1, cbu=CB_SGU_U, cbv=CB_SGU_V)
    dp, dpool_w, dpool_scale = _pool_bwd(dmixed, s['pooled'], c['pool_w'], c['pool_scale'], cb=2)
    dqn, dkn, dvs, dgate, dgb, ddn_ng = _dn_chunk_bwd(dmixed, s['qn'], s['kn'], s['vs'], s['gb'], proj, s['raw'], s['states'], c['dn_ng'],
                                                     cb_dy=3, cb_gate=CB_DN_GATE)
    dcq, dck, dcv, dab, dalog, ddtb = _dn_prep_bwd_a(proj, c['conv_w'], c['alog'], c['dtb'], dqn, dkn, dvs, dgb, cbq=CB_DN_Q)
    dq, dk, dv, dconv_w = _dn_prep_bwd_b(proj, c['conv_w'], dcq, dck, dcv, cbq=CB_DN_Q)
    dproj = jnp.concatenate([du, dzu, dzv, dp, dq, dk, dv, dgate, dab], axis=1)
    (dw_in,) = _matmul(s['xin16'], dproj, mode="tn", name="dw_in" + tag, tn=1408, tk=1024)
    (dxin,) = _matmul(dproj, w['w_in'], mode="nt", name="dx_in" + tag, tk=384, e=dh1, epi=residual, tm=2048)
    tril = jnp.tril(jnp.ones((SGU_CHUNK, SGU_CHUNK), bool))
    big = dict(w_in=dw_in, s5_glu_w=dglu_w, w_out=dw_out, w_up=dw_up, w_down=dw_down)
    small = dict(
        s5_lambda_re=dlam_re, s5_lambda_im=dlam_im, s5_log_step=dlog_step, s5_b_re=db_re, s5_b_im=db_im, s5_c_re=dc_re, s5_c_im=dc_im,
        s5_d=dd.reshape(S5_G, S5_H), s5_glu_b=dglu_b[0], sgu_norm_g=dsgu_ng[0], sgu_norm_b=dsgu_nb[0],
        sgu_w=jnp.where(tril, dsgu_w, 0.0), sgu_b=dsgu_bias.reshape(SGU_CHUNK, SGU_HEADS, SGU_HD).sum(-1).T, pool_w=dpool_w,
        pool_scale=dpool_scale[0], dn_conv_w=dconv_w, dn_a_log=dalog[0, :DN_HEADS], dn_dt_bias=ddtb[0, :DN_HEADS], dn_norm_g=ddn_ng[0],
        ln1_g=dln1g[0], ln1_b=dln1b[0], ln2_g=dln2g[0], ln2_b=dln2b[0])
    return dxin, big, small


def _pack(arrs):
    flat = jnp.concatenate([a.reshape(-1) for a in arrs])
    n = flat.shape[0]
    m = -(-n // 1024) * 1024
    return jnp.pad(flat, (0, m - n)).reshape(m // 128, 128)


def _sum_all(stacked):
    return _sum_kernel([(stacked, d) for d in range(stacked.shape[0])], name="small_sum")


def _unpack(packed, like):
    flat, out, off = packed.reshape(-1), [], 0
    for a in like:
        n = math.prod(a.shape)
        out.append(flat[off:off + n].reshape(a.shape))
        off += n
    return out


def kernel(x, w_in, s5_lambda_re, s5_lambda_im, s5_log_step, s5_b_re, s5_b_im, s5_c_re, s5_c_im, s5_d, s5_glu_w, s5_glu_b, sgu_norm_g, sgu_norm_b, sgu_w, sgu_b, pool_w, pool_scale, dn_conv_w, dn_a_log, dn_dt_bias, dn_norm_g, w_out, ln1_g, ln1_b, w_up, w_down, ln2_g, ln2_b, loss_target, m_w_in, m_s5_lambda_re, m_s5_lambda_im, m_s5_log_step, m_s5_b_re, m_s5_b_im, m_s5_c_re, m_s5_c_im, m_s5_d, m_s5_glu_w, m_s5_glu_b, m_sgu_norm_g, m_sgu_norm_b, m_sgu_w, m_sgu_b, m_pool_w, m_pool_scale, m_dn_conv_w, m_dn_a_log, m_dn_dt_bias, m_dn_norm_g, m_w_out, m_ln1_g, m_ln1_b, m_w_up, m_w_down, m_ln2_g, m_ln2_b, v_w_in, v_s5_lambda_re, v_s5_lambda_im, v_s5_log_step, v_s5_b_re, v_s5_b_im, v_s5_c_re, v_s5_c_im, v_s5_d, v_s5_glu_w, v_s5_glu_b, v_sgu_norm_g, v_sgu_norm_b, v_sgu_w, v_sgu_b, v_pool_w, v_pool_scale, v_dn_conv_w, v_dn_a_log, v_dn_dt_bias, v_dn_norm_g, v_w_out, v_ln1_g, v_ln1_b, v_w_up, v_w_down, v_ln2_g, v_ln2_b):
    given = dict(locals())
    xs, ys = lax.axis_index("x"), lax.axis_index("y")
    chip = 2 * xs + ys
    t = given['x'].shape[1]
    x0 = given['x'].reshape(t, D_MODEL)
    target = given['loss_target'].reshape(t, D_MODEL)

    mine16 = [given[n].astype(MXU_DTYPE).reshape(-1, given[n].shape[-1]) for n in BIG]
    gathered = {n: lax.dynamic_update_slice(buf, own[None], (chip, 0, 0)) for n, buf, own in zip(BIG, _allgather_chips(mine16), mine16)}
    conv_local = given['dn_conv_w']
    conv_all = _allgather_all(_pack([conv_local]))
    n_conv = math.prod(conv_local.shape)
    conv_full = jnp.concatenate([conv_all[2 * j].reshape(-1)[:n_conv].reshape(conv_local.shape) for j in range(4)], axis=-1)

    def layer_weights(i):
        def rows(n, per):
            return gathered[n][:, i * per:(i + 1) * per, :]
        w_in_full = jnp.transpose(rows('w_in', D_MODEL), (1, 0, 2)).reshape(D_MODEL, IN_COLS)
        return dict(w_in=jnp.pad(w_in_full, ((0, 0), (0, IN_PAD - IN_COLS))), s5_glu_w=rows('s5_glu_w', GW // 4).reshape(GW, GW),
                    w_out=rows('w_out', D_MODEL // 4).reshape(D_MODEL, D_MODEL),
                    w_up=jnp.transpose(rows('w_up', D_MODEL), (1, 0, 2)).reshape(D_MODEL, D_FF),
                    w_down=rows('w_down', D_FF // 4).reshape(D_FF, D_MODEL))

    def layer_params(i):
        p = {n: given[n][i] for n in SMALL}
        p['dn_conv_w'] = conv_full[i]
        return p

    ws = [layer_weights(i) for i in range(DEPTH)]
    ps = [layer_params(i) for i in range(DEPTH)]
    cs = [_layer_consts(p) for p in ps]

    xcur, xcur16, saved = x0, x0.astype(MXU_DTYPE), []
    for i in range(DEPTH):
        xcur, xcur16, s = _layer_fwd(xcur, xcur16, ws[i], cs[i], i)
        saved.append(s)
    dx, colsum = _loss_head(xcur, target)
    loss = lax.psum(0.5 * jnp.sum(colsum) / D_MODEL, ("x", "y", "c"))

    bigs, smalls = [None] * DEPTH, [None] * DEPTH
    for i in reversed(range(DEPTH)):
        dx, bigs[i], smalls[i] = _layer_bwd(dx, saved[i], ws[i], cs[i], ps[i], i)
    grad_x = dx.reshape(1, t, D_MODEL)

    def by_rows(g, per):
        return g.reshape(4, per, g.shape[-1])

    def by_cols(g, cols):
        return jnp.transpose(g.reshape(g.shape[0], 4, cols), (1, 0, 2))

    blocks = dict(
        w_in=[by_cols(b['w_in'][:, :IN_COLS], IN_COLS // 4) for b in bigs], s5_glu_w=[by_rows(b['s5_glu_w'], GW // 4) for b in bigs],
        w_out=[by_rows(b['w_out'], D_MODEL // 4) for b in bigs], w_up=[by_cols(b['w_up'], D_MODEL) for b in bigs],
        w_down=[by_rows(b['w_down'], D_FF // 4) for b in bigs])
    place = jnp.stack([lax.axis_index("c"), chip]).astype(jnp.int32)
    reduced = dict(zip(BIG, _reduce_scatter([jnp.concatenate(blocks[n], axis=1) for n in BIG], BIG, place)))

    small_full = [jnp.stack([smalls[i][n] for i in range(DEPTH)]) for n in SMALL]
    grads = dict(zip(SMALL, _unpack(_sum_all(_allgather_all(_pack(small_full))), small_full)))
    grads['dn_conv_w'] = lax.dynamic_slice_in_dim(grads['dn_conv_w'], chip * conv_local.shape[-1], conv_local.shape[-1], axis=2)
    for n in BIG:
        grads[n] = reduced[n].reshape(given[n].shape)

    delta, new_m, new_v = {}, {}, {}
    for n in BIG:
        cols = given[n].shape[-1]
        d, nm, nv = _adamw(given[n].reshape(-1, cols), reduced[n], given['m_' + n].reshape(-1, cols), given['v_' + n].reshape(-1, cols),
                           name="adamw_" + n)
        delta[n], new_m[n], new_v[n] = d.reshape(given[n].shape), nm.reshape(given[n].shape), nv.reshape(given[n].shape)
    like = [given[n] for n in SMALL]
    d, nm, nv = _adamw(_pack(like), _pack([grads[n] for n in SMALL]), _pack([given['m_' + n] for n in SMALL]),
                       _pack([given['v_' + n] for n in SMALL]), name="adamw_small")
    for out, packed in ((delta, d), (new_m, nm), (new_v, nv)):
        out.update(zip(SMALL, _unpack(packed, like)))
    return (loss, grad_x, *[grads[n] for n in WEIGHTS], *[delta[n] for n in WEIGHTS], *[new_m[n] for n in WEIGHTS],
            *[new_v[n] for n in WEIGHTS])
```

```python
import functools
import math

import jax
import jax.numpy as jnp
from jax import lax
from jax.experimental import pallas as pl
from jax.experimental.pallas import tpu as pltpu

F32 = jnp.float32
MXU_DTYPE = jnp.bfloat16
HI = lax.Precision.HIGHEST

D_MODEL = 2048
DEPTH = 2
GW = 512
S5_H = 16
S5_G = GW // S5_H
S5_P = 64
S5_N = S5_G * S5_P
SGU_CHUNK = 128
SGU_HEADS = 8
SGU_HD = GW // SGU_HEADS
POOL_WINDOWS = (2, 4, 8, 16)
POOL_GD = 128
DN_HD = 128
DN_HEADS = 4
DN_CONV = 4
DN_CHUNK = 64
D_FF = 4 * D_MODEL
IN_COLS = 4104
IN_PAD = 4224
LN_EPS = 1e-5
RMS_EPS = 1e-6
L2_EPS = 1e-6
ALPHA = (2 * DEPTH) ** 0.25
ADAM_LR, ADAM_B1, ADAM_B2, ADAM_EPS, ADAM_WD, ADAM_STEP = 0.001, 0.9, 0.999, 1e-08, 0.01, 10

VMEM_LIMIT = 56 * 1024 * 1024
MESH = pl.DeviceIdType.MESH


def _params(sem=None, vmem=VMEM_LIMIT):
    return pltpu.CompilerParams(dimension_semantics=sem, vmem_limit_bytes=vmem)


def _full(shape):
    nd = len(shape)
    return pl.BlockSpec(shape, lambda *_: (0,) * nd)


def _split(a):
    hi = a.astype(MXU_DTYPE)
    return hi, (a - hi.astype(F32)).astype(MXU_DTYPE)


def _dot(a, b, dims=(((1,), (0,)), ((), ())), exact=False):
    if exact and MXU_DTYPE == F32:
        return lax.dot_general(a, b, dims, precision=HI, preferred_element_type=F32)
    if exact:
        (ah, al), (bh, bl) = _split(a), _split(b)
        return (lax.dot_general(ah, bh, dims, preferred_element_type=F32) + lax.dot_general(al, bh, dims, preferred_element_type=F32)
                + lax.dot_general(ah, bl, dims, preferred_element_type=F32))
    return lax.dot_general(a.astype(MXU_DTYPE), b.astype(MXU_DTYPE), dims, preferred_element_type=F32)


NN = (((1,), (0,)), ((), ()))
NT = (((1,), (1,)), ((), ()))
TN = (((0,), (0,)), ((), ()))


def _gelu(x):
    c = math.sqrt(2.0 / math.pi)
    return 0.5 * x * (1.0 + jnp.tanh(c * (x + 0.044715 * x * x * x)))


def _gelu_grad(x):
    c = math.sqrt(2.0 / math.pi)
    t = jnp.tanh(c * (x + 0.044715 * x * x * x))
    return 0.5 * (1.0 + t) + 0.5 * x * (1.0 - t * t) * c * (1.0 + 3.0 * 0.044715 * x * x)


def _sigmoid(x):
    return 1.0 / (1.0 + jnp.exp(-x))


def _relu2(x):
    r = jnp.maximum(x, 0.0)
    return r * r


def _matmul(a, b, *, mode, name, e=None, epi=None, out_dtypes=(F32,), tm=1024, tn=1024, tk=512):
    if mode == "nn":
        (m, k), n = a.shape, b.shape[1]
    elif mode == "nt":
        (m, k), n = a.shape, b.shape[0]
    else:
        (k, m), n = a.shape, b.shape[1]
    tm, tn, tk = min(tm, m), min(tn, n), min(tk, k)
    assert m % tm == 0 and n % tn == 0 and k % tk == 0, (name, m, n, k, tm, tn, tk)
    nk, nout = k // tk, len(out_dtypes)
    dims = {"nn": NN, "nt": NT, "tn": TN}[mode]
    a_spec = pl.BlockSpec((tk, tm), lambda i, j, l: (l, i)) if mode == "tn" else pl.BlockSpec((tm, tk), lambda i, j, l: (i, l))
    b_spec = pl.BlockSpec((tn, tk), lambda i, j, l: (j, l)) if mode == "nt" else pl.BlockSpec((tk, tn), lambda i, j, l: (l, j))
    o_spec = pl.BlockSpec((tm, tn), lambda i, j, l: (i, j))

    def body(*refs):
        a_ref, b_ref = refs[:2]
        e_ref = refs[2] if e is not None else None
        o_refs = refs[len(refs) - 1 - nout:len(refs) - 1]
        acc = refs[-1]
        l = pl.program_id(2)
        d = _dot(a_ref[...], b_ref[...], dims)

        def finish(r):
            outs = (r,) if epi is None else epi(r, None if e_ref is None else e_ref[...])
            for o_ref, o, dt in zip(o_refs, outs, out_dtypes, strict=True):
                o_ref[...] = o.astype(dt)

        if nk == 1:
            finish(d)
        else:
            @pl.when(l == 0)
            def _():
                acc[...] = d

            @pl.when((l > 0) & (l < nk - 1))
            def _():
                acc[...] += d

            @pl.when(l == nk - 1)
            def _():
                finish(acc[...] + d)

    ins, specs = [a, b], [a_spec, b_spec]
    if e is not None:
        ins.append(e)
        specs.append(o_spec)
    return pl.pallas_call(
        body, name=name, grid=(m // tm, n // tn, nk), in_specs=specs, out_specs=[o_spec] * nout,
        out_shape=[jax.ShapeDtypeStruct((m, n), dt) for dt in out_dtypes], scratch_shapes=[pltpu.VMEM((tm, tn), F32)],
        compiler_params=_params(("parallel", "parallel", "arbitrary")))(*ins)


def _ln_fwd(h, g, b, *, name, tr=256):
    t, d = h.shape

    def body(h_ref, g_ref, b_ref, o_ref, o16_ref):
        x = h_ref[...]
        mu = jnp.mean(x, axis=-1, keepdims=True)
        xc = x - mu
        var = jnp.mean(xc * xc, axis=-1, keepdims=True)
        y = xc * lax.rsqrt(var + LN_EPS) * g_ref[...] + b_ref[...]
        o_ref[...] = y
        o16_ref[...] = y.astype(MXU_DTYPE)

    row = pl.BlockSpec((tr, d), lambda i: (i, 0))
    return pl.pallas_call(body, name=name, grid=(t // tr,), in_specs=[row, _full((1, d)), _full((1, d))], out_specs=[row, row],
                          out_shape=[jax.ShapeDtypeStruct((t, d), F32), jax.ShapeDtypeStruct((t, d), MXU_DTYPE)],
                          compiler_params=_params(("parallel",)))(h, g, b)


def _ln_bwd(dy, h, g, *, name, tr=256):
    t, d = h.shape

    def body(dy_ref, h_ref, g_ref, dh_ref, dh16_ref, dg_ref, db_ref):
        @pl.when(pl.program_id(0) == 0)
        def _():
            dg_ref[...] = jnp.zeros_like(dg_ref)
            db_ref[...] = jnp.zeros_like(db_ref)

        x, dyv = h_ref[...], dy_ref[...]
        mu = jnp.mean(x, axis=-1, keepdims=True)
        xc = x - mu
        rstd = lax.rsqrt(jnp.mean(xc * xc, axis=-1, keepdims=True) + LN_EPS)
        xh = xc * rstd
        w = dyv * g_ref[...]
        dh = rstd * (w - jnp.mean(w, axis=-1, keepdims=True) - xh * jnp.mean(w * xh, axis=-1, keepdims=True))
        dh_ref[...] = dh
        dh16_ref[...] = dh.astype(MXU_DTYPE)
        dg_ref[...] += jnp.sum(dyv * xh, axis=0, keepdims=True)
        db_ref[...] += jnp.sum(dyv, axis=0, keepdims=True)

    row = pl.BlockSpec((tr, d), lambda i: (i, 0))
    vec = _full((1, d))
    return pl.pallas_call(
        body, name=name, grid=(t // tr,), in_specs=[row, row, vec], out_specs=[row, row, vec, vec],
        out_shape=[jax.ShapeDtypeStruct((t, d), F32), jax.ShapeDtypeStruct((t, d), MXU_DTYPE), jax.ShapeDtypeStruct((1, d), F32),
                   jax.ShapeDtypeStruct((1, d), F32)],
        compiler_params=_params(("arbitrary",)))(dy, h, g)


def _loss_head(y, target, *, tr=256):
    t, d = y.shape

    def body(y_ref, t_ref, dy_ref, s_ref):
        @pl.when(pl.program_id(0) == 0)
        def _():
            s_ref[...] = jnp.zeros_like(s_ref)

        err = y_ref[...] - t_ref[...]
        dy_ref[...] = err * (1.0 / d)
        s_ref[...] += jnp.sum(err * err, axis=0, keepdims=True)

    row = pl.BlockSpec((tr, d), lambda i: (i, 0))
    return pl.pallas_call(
        body, name="loss_head", grid=(t // tr,), in_specs=[row, row], out_specs=[row, _full((1, d))],
        out_shape=[jax.ShapeDtypeStruct((t, d), F32), jax.ShapeDtypeStruct((1, d), F32)],
        compiler_params=_params(("arbitrary",)))(y, target)


def _cols(tb, width, cb):
    return pl.BlockSpec((tb, width), lambda i: (i, cb))


def _cols_rev(tb, width, cb, nb):
    return pl.BlockSpec((tb, width), lambda i: (nb - 1 - i, cb))


POOL_HALO = 16


def _pool_fwd(proj, w, scale, *, cb, tb=512):
    t = proj.shape[0]
    tb = min(tb, t)

    def body(p_ref, w_ref, s_ref, o_ref, pooled_ref, ext):
        i = pl.program_id(0)

        @pl.when(i == 0)
        def _():
            ext[pl.ds(0, POOL_HALO), :] = jnp.zeros((POOL_HALO, GW), F32)

        p = p_ref[...]
        ext[pl.ds(POOL_HALO, tb), :] = p
        pos = (i * tb + lax.broadcasted_iota(jnp.int32, (tb, 1), 0) + 1).astype(F32)
        for gi, win in enumerate(POOL_WINDOWS):
            c0 = gi * POOL_GD
            s = p[:, c0:c0 + POOL_GD]
            for k in range(1, win):
                s = s + ext[pl.ds(POOL_HALO - k, tb), pl.ds(c0, POOL_GD)]
            pooled = s / jnp.minimum(pos, float(win)) - p[:, c0:c0 + POOL_GD]
            pooled_ref[:, pl.ds(c0, POOL_GD)] = pooled
            o_ref[:, pl.ds(c0, POOL_GD)] = (_dot(pooled, w_ref[gi]) * s_ref[:, pl.ds(c0, POOL_GD)]).astype(o_ref.dtype)
        ext[pl.ds(0, POOL_HALO), :] = p[tb - POOL_HALO:, :]

    row = pl.BlockSpec((tb, GW), lambda i: (i, 0))
    return pl.pallas_call(
        body, name="pool_fwd", grid=(t // tb,),
        in_specs=[_cols(tb, GW, cb), _full((4, POOL_GD, POOL_GD)), _full((1, GW))], out_specs=[row, row],
        out_shape=[jax.ShapeDtypeStruct((t, GW), MXU_DTYPE), jax.ShapeDtypeStruct((t, GW), F32)],
        scratch_shapes=[pltpu.VMEM((tb + POOL_HALO, GW), F32)],
        compiler_params=_params(("arbitrary",)))(proj, w, scale)


def _pool_bwd(dmixed, pooled, w, scale, *, cb, tb=512):
    t = pooled.shape[0]
    tb = min(tb, t)
    nb = t // tb

    def body(dy_ref, pooled_ref, w_ref, s_ref, dp_ref, dw_ref, ds_ref, ext):
        i = pl.program_id(0)

        @pl.when(i == 0)
        def _():
            ext[pl.ds(tb, POOL_HALO), :] = jnp.zeros((POOL_HALO, GW), F32)
            dw_ref[...] = jnp.zeros_like(dw_ref)
            ds_ref[...] = jnp.zeros_like(ds_ref)

        dy = dy_ref[...]
        pos = ((nb - 1 - i) * tb + lax.broadcasted_iota(jnp.int32, (tb, 1), 0) + 1).astype(F32)
        dpool_all = []
        for gi, win in enumerate(POOL_WINDOWS):
            c0 = gi * POOL_GD
            pg = pooled_ref[:, pl.ds(c0, POOL_GD)]
            dyg = dy[:, c0:c0 + POOL_GD]
            ds_ref[:, pl.ds(c0, POOL_GD)] += jnp.sum(dyg * _dot(pg, w_ref[gi]), axis=0, keepdims=True)
            dmp = dyg * s_ref[:, pl.ds(c0, POOL_GD)]
            dw_ref[gi] += _dot(pg, dmp, TN)
            dpool = _dot(dmp, w_ref[gi], NT)
            dpool_all.append(dpool)
            ext[pl.ds(0, tb), pl.ds(c0, POOL_GD)] = dpool / jnp.minimum(pos, float(win))
        for gi, win in enumerate(POOL_WINDOWS):
            c0 = gi * POOL_GD
            s = ext[pl.ds(0, tb), pl.ds(c0, POOL_GD)]
            for k in range(1, win):
                s = s + ext[pl.ds(k, tb), pl.ds(c0, POOL_GD)]
            dp_ref[:, pl.ds(c0, POOL_GD)] = (s - dpool_all[gi]).astype(dp_ref.dtype)
        ext[pl.ds(tb, POOL_HALO), :] = ext[pl.ds(0, POOL_HALO), :]

    row = pl.BlockSpec((tb, GW), lambda i: (nb - 1 - i, 0))
    return pl.pallas_call(
        body, name="pool_bwd", grid=(nb,),
        in_specs=[_cols_rev(tb, GW, cb, nb), row, _full((4, POOL_GD, POOL_GD)), _full((1, GW))],
        out_specs=[row, _full((4, POOL_GD, POOL_GD)), _full((1, GW))],
        out_shape=[jax.ShapeDtypeStruct((t, GW), MXU_DTYPE), jax.ShapeDtypeStruct((4, POOL_GD, POOL_GD), F32),
                   jax.ShapeDtypeStruct((1, GW), F32)],
        scratch_shapes=[pltpu.VMEM((tb + POOL_HALO, GW), F32)], compiler_params=_params(("arbitrary",)))(dmixed, pooled, w, scale)


def _sgu_core(zu, zv, ng, nb, w_ref, bias):
    tb = zu.shape[0]
    u = _gelu(zu)
    v0 = _gelu(zv)
    mu = jnp.mean(v0, axis=-1, keepdims=True)
    vc = v0 - mu
    rstd = lax.rsqrt(jnp.mean(vc * vc, axis=-1, keepdims=True) + LN_EPS)
    xh = vc * rstd
    vn = xh * ng + nb
    low = lax.broadcasted_iota(jnp.int32, (SGU_CHUNK, 2 * SGU_HD), 1) < SGU_HD
    rows = []
    for n in range(tb // SGU_CHUNK):
        pairs = []
        for j in range(SGU_HEADS // 2):
            vp = vn[n * SGU_CHUNK:(n + 1) * SGU_CHUNK, j * 128:(j + 1) * 128]
            pairs.append(jnp.where(low, _dot(w_ref[2 * j], vp), _dot(w_ref[2 * j + 1], vp)))
        rows.append(jnp.concatenate(pairs, axis=1) + bias)
    mixed = jnp.concatenate(rows, axis=0)
    return u, xh, rstd, vn, mixed


def _sgu_fwd(proj, ng, nb, w, bias, *, cbu, cbv, tb=512):
    t = proj.shape[0]
    tb = min(tb, t)

    def body(zu_ref, zv_ref, ng_ref, nb_ref, w_ref, bias_ref, o_ref):
        u, _, _, _, mixed = _sgu_core(zu_ref[...], zv_ref[...], ng_ref[...], nb_ref[...], w_ref, bias_ref[...])
        o_ref[...] = (u * mixed).astype(o_ref.dtype)

    vec = _full((1, GW))
    return pl.pallas_call(
        body, name="sgu_fwd", grid=(t // tb,),
        in_specs=[_cols(tb, GW, cbu), _cols(tb, GW, cbv), vec, vec, _full((8, 128, 128)), _full((128, GW))],
        out_specs=pl.BlockSpec((tb, GW), lambda i: (i, 0)), out_shape=jax.ShapeDtypeStruct((t, GW), MXU_DTYPE),
        compiler_params=_params(("parallel",)))(proj, proj, ng, nb, w, bias)


def _sgu_bwd(dmixed, proj, ng, nb, w, wt, bias, *, cb, cbu, cbv, tb=512):
    t = proj.shape[0]
    tb = min(tb, t)

    def body(dy_ref, zu_ref, zv_ref, ng_ref, nb_ref, w_ref, wt_ref, bias_ref, dzu_ref, dzv_ref, dw_ref, dbias_ref, dng_ref, dnb_ref):
        @pl.when(pl.program_id(0) == 0)
        def _():
            dw_ref[...] = jnp.zeros_like(dw_ref)
            dbias_ref[...] = jnp.zeros_like(dbias_ref)
            dng_ref[...] = jnp.zeros_like(dng_ref)
            dnb_ref[...] = jnp.zeros_like(dnb_ref)

        zu, zv, dy = zu_ref[...], zv_ref[...], dy_ref[...]
        u, xh, rstd, vn, mixed = _sgu_core(zu, zv, ng_ref[...], nb_ref[...], w_ref, bias_ref[...])
        dzu_ref[...] = (dy * mixed * _gelu_grad(zu)).astype(dzu_ref.dtype)
        dmix = dy * u
        low = lax.broadcasted_iota(jnp.int32, (SGU_CHUNK, 2 * SGU_HD), 1) < SGU_HD
        dbias = jnp.zeros((SGU_CHUNK, GW), F32)
        rows = []
        for n in range(tb // SGU_CHUNK):
            dm = dmix[n * SGU_CHUNK:(n + 1) * SGU_CHUNK, :]
            dbias = dbias + dm
            pairs = []
            for j in range(SGU_HEADS // 2):
                dmp = dm[:, j * 128:(j + 1) * 128]
                vp = vn[n * SGU_CHUNK:(n + 1) * SGU_CHUNK, j * 128:(j + 1) * 128]
                dw_ref[2 * j] += _dot(jnp.where(low, dmp, 0.0), vp, NT)
                dw_ref[2 * j + 1] += _dot(jnp.where(low, 0.0, dmp), vp, NT)
                pairs.append(jnp.where(low, _dot(wt_ref[2 * j], dmp), _dot(wt_ref[2 * j + 1], dmp)))
            rows.append(jnp.concatenate(pairs, axis=1))
        dbias_ref[...] += dbias
        dvn = jnp.concatenate(rows, axis=0)
        dng_ref[...] += jnp.sum(dvn * xh, axis=0, keepdims=True)
        dnb_ref[...] += jnp.sum(dvn, axis=0, keepdims=True)
        wv = dvn * ng_ref[...]
        dv0 = rstd * (wv - jnp.mean(wv, axis=-1, keepdims=True) - xh * jnp.mean(wv * xh, axis=-1, keepdims=True))
        dzv_ref[...] = (dv0 * _gelu_grad(zv)).astype(dzv_ref.dtype)

    vec = _full((1, GW))
    row = pl.BlockSpec((tb, GW), lambda i: (i, 0))
    mat = _full((8, 128, 128))
    return pl.pallas_call(
        body, name="sgu_bwd", grid=(t // tb,),
        in_specs=[_cols(tb, GW, cb), _cols(tb, GW, cbu), _cols(tb, GW, cbv), vec, vec, mat, mat, _full((128, GW))],
        out_specs=[row, row, mat, _full((128, GW)), vec, vec],
        out_shape=[jax.ShapeDtypeStruct((t, GW), MXU_DTYPE)] * 2 + [jax.ShapeDtypeStruct((8, 128, 128), F32),
                   jax.ShapeDtypeStruct((128, GW), F32), jax.ShapeDtypeStruct((1, GW), F32), jax.ShapeDtypeStruct((1, GW), F32)],
        compiler_params=_params(("arbitrary",)))(dmixed, proj, proj, ng, nb, w, wt, bias)


S5_KB = 4
SUB = 8


def _s5_discretize(lam_re, lam_im, log_step, b_re, b_im):
    step = jnp.exp(log_step)[:, None]
    mag = jnp.exp(lam_re * step)
    lr, li = mag * jnp.cos(lam_im * step), mag * jnp.sin(lam_im * step)
    den = lam_re * lam_re + lam_im * lam_im
    fr = ((lr - 1.0) * lam_re + li * lam_im) / den
    fi = (li * lam_re - (lr - 1.0) * lam_im) / den
    return lr, li, fr[:, :, None] * b_re - fi[:, :, None] * b_im, fr[:, :, None] * b_im + fi[:, :, None] * b_re


def _cpow(lr, li, n):
    rr, ri = lr, li
    for _ in range(n - 1):
        rr, ri = rr * lr - ri * li, rr * li + ri * lr
    return rr, ri


def _s5_scan_consts(lr, li, reverse):
    lr, li = lr.reshape(1, S5_N), (-li if reverse else li).reshape(1, S5_N)
    row = jnp.arange(SUB)[:, None]
    out = []
    for s in (1, 2, 4):
        pr, pi = _cpow(lr, li, s)
        keep = (row < SUB - s) if reverse else (row >= s)
        out += [jnp.where(keep, pr, 0.0), jnp.where(keep, pi, 0.0)]
    cr, ci = [], []
    for i in range(SUB):
        pr, pi = _cpow(lr, li, SUB - i if reverse else i + 1)
        cr.append(pr)
        ci.append(pi)
    out += [jnp.concatenate(cr, axis=0), jnp.concatenate(ci, axis=0)]
    return jnp.stack(out)


def _s5_blockdiag_in(b):
    bt = jnp.swapaxes(b, 1, 2).reshape(S5_KB, 8, S5_H, S5_P)
    eye = jnp.eye(8, dtype=b.dtype)
    return jnp.einsum("kghp,gj->kghjp", bt, eye).reshape(S5_KB, 128, 512)


def _s5_blockdiag_in_extract(bb):
    x = bb.reshape(S5_KB, 8, S5_H, 8, S5_P)
    d = jnp.einsum("kghgp->kghp", x).reshape(S5_G, S5_H, S5_P)
    return jnp.swapaxes(d, 1, 2)


def _s5_blockdiag_out(c):
    ct = jnp.swapaxes(c, 1, 2).reshape(S5_KB, 8, S5_P, S5_H)
    eye = jnp.eye(8, dtype=c.dtype)
    return jnp.einsum("kgph,gj->kgpjh", ct, eye).reshape(S5_KB, 512, 128)


def _s5_blockdiag_out_extract(cc):
    x = cc.reshape(S5_KB, 8, S5_P, 8, S5_H)
    d = jnp.einsum("kgpgh->kgph", x).reshape(S5_G, S5_P, S5_H)
    return jnp.swapaxes(d, 1, 2)


def _s5_tile_scan(a, b, c_ref, carry, reverse):
    for si, s in enumerate((1, 2, 4)):
        sh = SUB - s if reverse else s
        ar, br = pltpu.roll(a, sh, 0), pltpu.roll(b, sh, 0)
        mr, mi = c_ref[2 * si], c_ref[2 * si + 1]
        a, b = a + mr * ar - mi * br, b + mr * br + mi * ar
    pr, pi = c_ref[6], c_ref[7]
    cr, ci = carry
    return a + pr * cr - pi * ci, b + pr * ci + pi * cr


def _s5_readout(xre_ref, xim_ref, ccre_ref, ccim_ref):
    return jnp.concatenate(
        [_dot(xre_ref[:, pl.ds(512 * k, 512)], ccre_ref[k], exact=True) - _dot(xim_ref[:, pl.ds(512 * k, 512)], ccim_ref[k], exact=True)
         for k in range(S5_KB)], axis=1)


def _s5_fwd(proj, bbre, bbim, ccre, ccim, dvec, consts, glu_w, glu_b, *, cb, tb=256):
    t = proj.shape[0]
    tb = min(tb, t)
    nt = tb // SUB

    def body(u_ref, bbre_ref, bbim_ref, ccre_ref, ccim_ref, d_ref, c_ref, w_ref, b_ref, o_ref, xre_ref, xim_ref, car):
        @pl.when(pl.program_id(0) == 0)
        def _():
            car[...] = jnp.zeros_like(car)

        u = u_ref[...]
        for k in range(S5_KB):
            uk = u[:, 128 * k:128 * (k + 1)]
            xre_ref[:, pl.ds(512 * k, 512)] = _dot(uk, bbre_ref[k], exact=True)
            xim_ref[:, pl.ds(512 * k, 512)] = _dot(uk, bbim_ref[k], exact=True)

        def tile(r, carry):
            sl = pl.ds(pl.multiple_of(r * SUB, SUB), SUB)
            a, b = _s5_tile_scan(xre_ref[sl, :], xim_ref[sl, :], c_ref, carry, False)
            xre_ref[sl, :] = a
            xim_ref[sl, :] = b
            return a[SUB - 1:SUB, :], b[SUB - 1:SUB, :]

        cr, ci = lax.fori_loop(0, nt, tile, (car[0:1, :], car[1:2, :]))
        car[0:1, :] = cr
        car[1:2, :] = ci
        ys = _s5_readout(xre_ref, xim_ref, ccre_ref, ccim_ref) + d_ref[...] * u
        yg = _gelu(ys)
        o_ref[...] = (yg * _sigmoid(_dot(yg, w_ref[...]) + b_ref[...])).astype(o_ref.dtype)

    row = pl.BlockSpec((tb, GW), lambda i: (i, 0))
    xrow = pl.BlockSpec((tb, S5_N), lambda i: (i, 0))
    vec = _full((1, GW))
    return pl.pallas_call(
        body, name="s5_fwd", grid=(t // tb,),
        in_specs=[_cols(tb, GW, cb), _full((4, 128, 512)), _full((4, 128, 512)), _full((4, 512, 128)), _full((4, 512, 128)),
                  vec, _full((8, SUB, S5_N)), _full((GW, GW)), vec],
        out_specs=[row, xrow, xrow],
        out_shape=[jax.ShapeDtypeStruct((t, GW), MXU_DTYPE), jax.ShapeDtypeStruct((t, S5_N), F32), jax.ShapeDtypeStruct((t, S5_N), F32)],
        scratch_shapes=[pltpu.VMEM((SUB, S5_N), F32)], compiler_params=_params(("arbitrary",)))(
            proj, bbre, bbim, ccre, ccim, dvec, consts, glu_w, glu_b)


def _s5_bwd(dmixed, proj, xre, xim, bbre, bbim, ccre, ccim, dvec, consts, glu_w, glu_b, *, cb_dy, cb, tb=256):
    t = proj.shape[0]
    tb = min(tb, t)
    nb = t // tb
    nt = tb // SUB

    def body(dy_ref, u_ref, xre_ref, xim_ref, bbre_ref, bbim_ref, ccre_ref, ccim_ref, d_ref, c_ref, w_ref, b_ref,
             du_ref, dw_ref, db_ref, dd_ref, dccre_ref, dccim_ref, dbbre_ref, dbbim_ref, sre_ref, sim_ref, are, aim, car):
        @pl.when(pl.program_id(0) == 0)
        def _():
            car[...] = jnp.zeros_like(car)
            for r in (dw_ref, db_ref, dd_ref, dccre_ref, dccim_ref, dbbre_ref, dbbim_ref, sre_ref, sim_ref):
                r[...] = jnp.zeros_like(r)

        u, dy = u_ref[...], dy_ref[...]
        ys = _s5_readout(xre_ref, xim_ref, ccre_ref, ccim_ref) + d_ref[...] * u
        yg = _gelu(ys)
        sg = _sigmoid(_dot(yg, w_ref[...]) + b_ref[...])
        dz = dy * yg * sg * (1.0 - sg)
        dyg = dy * sg + _dot(dz, w_ref[...], NT)
        dw_ref[...] += _dot(yg, dz, TN)
        db_ref[...] += jnp.sum(dz, axis=0, keepdims=True)
        dys = dyg * _gelu_grad(ys)
        dd_ref[...] += jnp.sum(dys * u, axis=0, keepdims=True)
        for k in range(S5_KB):
            dk = dys[:, 128 * k:128 * (k + 1)]
            lanes = pl.ds(512 * k, 512)
            are[:, lanes] = _dot(dk, ccre_ref[k], NT, exact=True)
            aim[:, lanes] = -_dot(dk, ccim_ref[k], NT, exact=True)
            dccre_ref[k] += _dot(xre_ref[:, lanes], dk, TN, exact=True)
            dccim_ref[k] -= _dot(xim_ref[:, lanes], dk, TN, exact=True)

        def tile(j, carry):
            sl = pl.ds(pl.multiple_of((nt - 1 - j) * SUB, SUB), SUB)
            gr, gi = are[sl, :], aim[sl, :]
            a, b = _s5_tile_scan(gr, gi, c_ref, carry, True)
            are[sl, :] = a
            aim[sl, :] = b
            er, ei = a - gr, b - gi
            xr, xi = xre_ref[sl, :], xim_ref[sl, :]
            sre_ref[...] += xr * er + xi * ei
            sim_ref[...] += xr * ei - xi * er
            return a[0:1, :], b[0:1, :]

        cr, ci = lax.fori_loop(0, nt, tile, (car[0:1, :], car[1:2, :]))
        car[0:1, :] = cr
        car[1:2, :] = ci
        dus = []
        for k in range(S5_KB):
            uk = u[:, 128 * k:128 * (k + 1)]
            lanes = pl.ds(512 * k, 512)
            dbbre_ref[k] += _dot(uk, are[:, lanes], TN, exact=True)
            dbbim_ref[k] += _dot(uk, aim[:, lanes], TN, exact=True)
            dus.append(_dot(are[:, lanes], bbre_ref[k], NT, exact=True) + _dot(aim[:, lanes], bbim_ref[k], NT, exact=True))
        du_ref[...] = (d_ref[...] * dys + jnp.concatenate(dus, axis=1)).astype(du_ref.dtype)

    row = pl.BlockSpec((tb, GW), lambda i: (nb - 1 - i, 0))
    xrow = pl.BlockSpec((tb, S5_N), lambda i: (nb - 1 - i, 0))
    vec = _full((1, GW))
    bbs, ccs = _full((4, 128, 512)), _full((4, 512, 128))
    sds = jax.ShapeDtypeStruct
    return pl.pallas_call(
        body, name="s5_bwd", grid=(nb,),
        in_specs=[_cols_rev(tb, GW, cb_dy, nb), _cols_rev(tb, GW, cb, nb), xrow, xrow, bbs, bbs, ccs, ccs, vec,
                  _full((8, SUB, S5_N)), _full((GW, GW)), vec],
        out_specs=[row, _full((GW, GW)), vec, vec, ccs, ccs, bbs, bbs, _full((SUB, S5_N)), _full((SUB, S5_N))],
        out_shape=[sds((t, GW), MXU_DTYPE), sds((GW, GW), F32), sds((1, GW), F32), sds((1, GW), F32), sds((4, 512, 128), F32),
                   sds((4, 512, 128), F32), sds((4, 128, 512), F32), sds((4, 128, 512), F32), sds((SUB, S5_N), F32),
                   sds((SUB, S5_N), F32)],
        scratch_shapes=[pltpu.VMEM((tb, S5_N), F32), pltpu.VMEM((tb, S5_N), F32), pltpu.VMEM((SUB, S5_N), F32)],
        compiler_params=_params(("arbitrary",)))(dmixed, proj, xre, xim, bbre, bbim, ccre, ccim, dvec, consts, glu_w, glu_b)


def _s5_prepare(lam_re, lam_im, log_step, b_re, b_im, c_re, c_im):
    lr, li, bbr, bbi = _s5_discretize(lam_re, lam_im, log_step, b_re, b_im)
    return dict(bbre=_s5_blockdiag_in(bbr), bbim=_s5_blockdiag_in(bbi), ccre=_s5_blockdiag_out(c_re), ccim=_s5_blockdiag_out(c_im),
                cf=_s5_scan_consts(lr, li, False), cr=_s5_scan_consts(lr, li, True))


def _s5_param_grads(lam_re, lam_im, log_step, b_re, b_im, dbbre, dbbim, dccre, dccim, sre, sim):
    (lr, li, _, _), vjp = jax.vjp(_s5_discretize, lam_re, lam_im, log_step, b_re, b_im)
    sr, si = jnp.sum(sre, axis=0).reshape(S5_G, S5_P), jnp.sum(sim, axis=0).reshape(S5_G, S5_P)
    den = lr * lr + li * li
    glr, gli = (sr * lr - si * li) / den, (si * lr + sr * li) / den
    dlam_re, dlam_im, dlog_step, db_re, db_im = vjp((glr, gli, _s5_blockdiag_in_extract(dbbre), _s5_blockdiag_in_extract(dbbim)))
    return dlam_re, dlam_im, dlog_step, db_re, db_im, _s5_blockdiag_out_extract(dccre), _s5_blockdiag_out_extract(dccim)


HALO = 8
AB_CB = 4096 // 128
Q_SCALE = DN_HD ** -0.5


def _halo_prev(tb, width, cb):
    return pl.BlockSpec((HALO, width), lambda i: (jnp.maximum(i * (tb // HALO) - 1, 0), cb))


def _halo_next(tb, width, cb, nrows):
    last = nrows // HALO - 1
    return pl.BlockSpec((HALO, width), lambda i: (jnp.minimum((i + 1) * (tb // HALO), last), cb))


def _silu_parts(c):
    sg = _sigmoid(c)
    return c * sg, sg * (1.0 + c * (1.0 - sg))


def _softplus(x):
    return jnp.maximum(x, 0.0) + jnp.log(1.0 + jnp.exp(-jnp.abs(x)))


def _dn_conv(x_ref, halo_ref, w_ref, part, ext, first):
    tb = x_ref.shape[0]
    ext[pl.ds(0, HALO), :] = jnp.where(first, 0.0, halo_ref[...])
    ext[pl.ds(HALO, tb), :] = x_ref[...]
    c = None
    for j in range(DN_CONV):
        term = w_ref[pl.ds(j, 1), pl.ds(512 * part, 512)] * ext[pl.ds(HALO - (DN_CONV - 1) + j, tb), :]
        c = term if c is None else c + term
    return c


def _dn_gb(ab, alog, dtb):
    lane = lax.broadcasted_iota(jnp.int32, ab.shape, 1)
    pre = ab + dtb
    g = -jnp.exp(alog) * _softplus(pre)
    beta = _sigmoid(ab)
    return jnp.where(lane < DN_HEADS, g, jnp.where(lane < 2 * DN_HEADS, beta, 0.0)), pre, beta


def _dn_prep_fwd(proj, conv_w, alog, dtb, *, cbq, tb=512):
    t = proj.shape[0]
    tb = min(tb, t)

    def body(xq, xk, xv, hq, hk, hv, ab_ref, w_ref, alog_ref, dtb_ref, qn_ref, kn_ref, vs_ref, gb_ref, ext):
        first = pl.program_id(0) == 0
        for part, (x_ref, h_ref, o_ref) in enumerate(((xq, hq, qn_ref), (xk, hk, kn_ref), (xv, hv, vs_ref))):
            s, _ = _silu_parts(_dn_conv(x_ref, h_ref, w_ref, part, ext, first))
            if part < 2:
                scale = Q_SCALE if part == 0 else 1.0
                for h in range(DN_HEADS):
                    sh = s[:, DN_HD * h:DN_HD * (h + 1)]
                    rn = lax.rsqrt(jnp.sum(sh * sh, axis=-1, keepdims=True) + L2_EPS)
                    o_ref[:, pl.ds(DN_HD * h, DN_HD)] = sh * (rn * scale)
            else:
                o_ref[...] = s
        gb_ref[...] = _dn_gb(ab_ref[...], alog_ref[...], dtb_ref[...])[0]

    row = pl.BlockSpec((tb, GW), lambda i: (i, 0))
    small = pl.BlockSpec((tb, 128), lambda i: (i, 0))
    v128 = _full((1, 128))
    sds = jax.ShapeDtypeStruct
    return pl.pallas_call(
        body, name="dn_prep_fwd", grid=(t // tb,),
        in_specs=[_cols(tb, GW, cbq), _cols(tb, GW, cbq + 1), _cols(tb, GW, cbq + 2),
                  _halo_prev(tb, GW, cbq), _halo_prev(tb, GW, cbq + 1), _halo_prev(tb, GW, cbq + 2),
                  _cols(tb, 128, AB_CB), _full((DN_CONV, 3 * GW)), v128, v128],
        out_specs=[row, row, row, small],
        out_shape=[sds((t, GW), F32)] * 3 + [sds((t, 128), F32)],
        scratch_shapes=[pltpu.VMEM((tb + HALO, GW), F32)], compiler_params=_params(("parallel",)))(
            proj, proj, proj, proj, proj, proj, proj, conv_w, alog, dtb)


def _dn_prep_bwd_a(proj, conv_w, alog, dtb, dqn, dkn, dvs, dgb, *, cbq, tb=512):
    t = proj.shape[0]
    tb = min(tb, t)

    def body(xq, xk, xv, hq, hk, hv, ab_ref, w_ref, alog_ref, dtb_ref, dqn_ref, dkn_ref, dvs_ref, dgb_ref,
             dcq_ref, dck_ref, dcv_ref, dab_ref, dalog_ref, ddtb_ref, ext):
        first = pl.program_id(0) == 0

        @pl.when(first)
        def _():
            dalog_ref[...] = jnp.zeros_like(dalog_ref)
            ddtb_ref[...] = jnp.zeros_like(ddtb_ref)

        for part, (x_ref, h_ref, d_ref, o_ref) in enumerate(((xq, hq, dqn_ref, dcq_ref), (xk, hk, dkn_ref, dck_ref), (xv, hv, dvs_ref, dcv_ref))):
            s, ds_dc = _silu_parts(_dn_conv(x_ref, h_ref, w_ref, part, ext, first))
            d = d_ref[...]
            if part < 2:
                scale = Q_SCALE if part == 0 else 1.0
                for h in range(DN_HEADS):
                    lanes = slice(DN_HD * h, DN_HD * (h + 1))
                    sh, dh = s[:, lanes], d[:, lanes]
                    rn = lax.rsqrt(jnp.sum(sh * sh, axis=-1, keepdims=True) + L2_EPS)
                    dsh = scale * (rn * dh - sh * (rn * rn * rn) * jnp.sum(dh * sh, axis=-1, keepdims=True))
                    o_ref[:, pl.ds(DN_HD * h, DN_HD)] = dsh * ds_dc[:, lanes]
            else:
                o_ref[...] = d * ds_dc
        ab, dgb_v = ab_ref[...], dgb_ref[...]
        gb, pre, beta = _dn_gb(ab, alog_ref[...], dtb_ref[...])
        lane = lax.broadcasted_iota(jnp.int32, ab.shape, 1)
        is_g = lane < DN_HEADS
        da = jnp.where(is_g, dgb_v * (-jnp.exp(alog_ref[...])) * _sigmoid(pre), 0.0)
        db = jnp.where((lane >= DN_HEADS) & (lane < 2 * DN_HEADS), dgb_v * beta * (1.0 - beta), 0.0)
        dab_ref[...] = (da + db).astype(dab_ref.dtype)
        ddtb_ref[...] += jnp.sum(da, axis=0, keepdims=True)
        dalog_ref[...] += jnp.sum(jnp.where(is_g, dgb_v * gb, 0.0), axis=0, keepdims=True)

    row = pl.BlockSpec((tb, GW), lambda i: (i, 0))
    small = pl.BlockSpec((tb, 128), lambda i: (i, 0))
    v128 = _full((1, 128))
    sds = jax.ShapeDtypeStruct
    return pl.pallas_call(
        body, name="dn_prep_bwd_a", grid=(t // tb,),
        in_specs=[_cols(tb, GW, cbq), _cols(tb, GW, cbq + 1), _cols(tb, GW, cbq + 2),
                  _halo_prev(tb, GW, cbq), _halo_prev(tb, GW, cbq + 1), _halo_prev(tb, GW, cbq + 2),
                  _cols(tb, 128, AB_CB), _full((DN_CONV, 3 * GW)), v128, v128, row, row, row, small],
        out_specs=[row, row, row, small, v128, v128],
        out_shape=[sds((t, GW), F32)] * 3 + [sds((t, 128), MXU_DTYPE), sds((1, 128), F32), sds((1, 128), F32)],
        scratch_shapes=[pltpu.VMEM((tb + HALO, GW), F32)], compiler_params=_params(("arbitrary",)))(
            proj, proj, proj, proj, proj, proj, proj, conv_w, alog, dtb, dqn, dkn, dvs, dgb)


def _dn_prep_bwd_b(proj, conv_w, dcq, dck, dcv, *, cbq, tb=512):
    t = proj.shape[0]
    tb = min(tb, t)
    nb = t // tb

    def body(xq, xk, xv, hq, hk, hv, dq_in, dk_in, dv_in, nq, nk, nv, w_ref, dq_ref, dk_ref, dv_ref, dw_ref, ext):
        i = pl.program_id(0)
        first, last = i == 0, i == nb - 1

        @pl.when(first)
        def _():
            dw_ref[...] = jnp.zeros_like(dw_ref)

        for part, (x_ref, h_ref, d_ref, n_ref, o_ref) in enumerate(
                ((xq, hq, dq_in, nq, dq_ref), (xk, hk, dk_in, nk, dk_ref), (xv, hv, dv_in, nv, dv_ref))):
            lanes = pl.ds(512 * part, 512)
            d = d_ref[...]
            ext[pl.ds(0, HALO), :] = jnp.where(first, 0.0, h_ref[...])
            ext[pl.ds(HALO, tb), :] = x_ref[...]
            for j in range(DN_CONV):
                xs = ext[pl.ds(HALO - (DN_CONV - 1) + j, tb), :]
                dw_ref[pl.ds(j, 1), lanes] += jnp.sum(d * xs, axis=0, keepdims=True)
            ext[pl.ds(0, tb), :] = d
            ext[pl.ds(tb, HALO), :] = jnp.where(last, 0.0, n_ref[...])
            acc = None
            for j in range(DN_CONV):
                term = w_ref[pl.ds(j, 1), lanes] * ext[pl.ds(DN_CONV - 1 - j, tb), :]
                acc = term if acc is None else acc + term
            o_ref[...] = acc.astype(o_ref.dtype)

    row = pl.BlockSpec((tb, GW), lambda i: (i, 0))
    nxt = _halo_next(tb, GW, 0, t)
    sds = jax.ShapeDtypeStruct
    return pl.pallas_call(
        body, name="dn_prep_bwd_b", grid=(nb,),
        in_specs=[_cols(tb, GW, cbq), _cols(tb, GW, cbq + 1), _cols(tb, GW, cbq + 2),
                  _halo_prev(tb, GW, cbq), _halo_prev(tb, GW, cbq + 1), _halo_prev(tb, GW, cbq + 2),
                  row, row, row, nxt, nxt, nxt, _full((DN_CONV, 3 * GW))],
        out_specs=[row, row, row, _full((DN_CONV, 3 * GW))],
        out_shape=[sds((t, GW), MXU_DTYPE)] * 3 + [sds((DN_CONV, 3 * GW), F32)],
        scratch_shapes=[pltpu.VMEM((tb + HALO, GW), F32)], compiler_params=_params(("arbitrary",)))(
            proj, proj, proj, proj, proj, proj, dcq, dck, dcv, dcq, dck, dcv, conv_w)


CAT = DN_HEADS * DN_CHUNK
DN_LOCAL_CHUNKS = 2


def _iota_div(shape, axis, width):
    return jnp.right_shift(lax.broadcasted_iota(jnp.int32, shape, axis), width.bit_length() - 1)


def _dn_masks():
    r = lax.broadcasted_iota(jnp.int32, (DN_CHUNK, CAT), 0)
    c = jnp.bitwise_and(lax.broadcasted_iota(jnp.int32, (DN_CHUNK, CAT), 1), DN_CHUNK - 1)
    wide = _iota_div((CAT, GW), 0, DN_CHUNK) == _iota_div((CAT, GW), 1, DN_HD)
    square = _iota_div((CAT, CAT), 0, DN_CHUNK) == _iota_div((CAT, CAT), 1, DN_CHUNK)
    return dict(eye=r == c, tril=r >= c, strict=r > c, triu=r <= c, wide=wide, square=square)


def _stack4(x):
    return jnp.concatenate([x, x, x, x], axis=0)


def _diag_blocks(x, mask):
    return jnp.where(mask, _stack4(x), 0.0)


def _fold_blocks(x, mask):
    x = jnp.where(mask, x, 0.0)
    return x[0:64] + x[64:128] + x[128:192] + x[192:256]


def _expand(cols, base, width):
    head = _iota_div((cols.shape[0], DN_HEADS * width), 1, width)
    out = jnp.zeros((cols.shape[0], DN_HEADS * width), F32)
    for h in range(DN_HEADS):
        out = jnp.where(head == h, cols[:, base + h:base + h + 1], out)
    return out


def _head_sums(x, width):
    if width == DN_HD:
        return [jnp.sum(x[:, DN_HD * h:DN_HD * (h + 1)], axis=1, keepdims=True) for h in range(DN_HEADS)]
    head = _iota_div(x.shape, 1, width)
    return [jnp.sum(jnp.where(head == h, x, 0.0), axis=1, keepdims=True) for h in range(DN_HEADS)]


def _cumsum_rows(x):
    row = lax.broadcasted_iota(jnp.int32, x.shape, 0)
    for s in (1, 2, 4, 8, 16, 32):
        x = x + jnp.where(row >= s, pltpu.roll(x, s, 0), 0.0)
    return x


def _tri_inv(n, square):
    col = jnp.bitwise_and(lax.broadcasted_iota(jnp.int32, n.shape, 1), DN_CHUNK - 1)
    x = jnp.where(lax.broadcasted_iota(jnp.int32, n.shape, 0) == col, 1.0, 0.0) - n
    p = _dot(n, _diag_blocks(n, square), exact=True)
    for _ in range(4):
        pd = _diag_blocks(p, square)
        x = x + _dot(x, pd, exact=True)
        p = _dot(p, pd, exact=True)
    return x + _dot(x, _diag_blocks(p, square), exact=True)


def _dn_local_math(q, k, v, gbv, m, tm=None):
    gc = _cumsum_rows(gbv)
    gc_cat, gc_wide = _expand(gc, 0, DN_CHUNK), _expand(gc, 0, DN_HD)
    gc_row = jnp.sum(jnp.where(m["eye"], gc_cat, 0.0), axis=0, keepdims=True)
    decay = jnp.where(m["tril"], jnp.exp(jnp.minimum(gc_cat - gc_row, 0.0)), 0.0)
    eg = jnp.exp(gc_wide)
    gl = _expand(gc[DN_CHUNK - 1:DN_CHUNK, :], 0, DN_HD)
    etail = jnp.exp(gl - gc_wide)
    beta = _expand(gbv, DN_HEADS, DN_HD)
    kb, vb = k * beta, v * beta
    k_rows = _diag_blocks(k, m["wide"])
    kk = _dot(kb, k_rows, NT)
    if tm is None:
        tm = _tri_inv(jnp.where(m["strict"], kk * decay, 0.0), m["square"])
    return dict(decay=decay, eg=eg, eg_last=jnp.exp(gl), etail=etail, beta=beta, kb=kb, vb=vb, kk=kk, tm=tm, kbg=kb * eg,
                qk=_dot(q, k_rows, NT), qg=q * eg, ktail=k * etail, k_rows=k_rows)


def _dn_local_fwd(qn, kn, vs, gb):
    t = qn.shape[0]
    rows = DN_CHUNK * DN_LOCAL_CHUNKS

    def body(q_ref, k_ref, v_ref, gb_ref, u_ref, wm_ref, qg_ref, kt_ref, qkd_ref, tm_ref):
        masks = _dn_masks()
        for n in range(DN_LOCAL_CHUNKS):
            rs = pl.ds(DN_CHUNK * n, DN_CHUNK)
            m = _dn_local_math(q_ref[rs, :], k_ref[rs, :], v_ref[rs, :], gb_ref[rs, :], masks)
            u_ref[rs, :] = _dot(m["tm"], _diag_blocks(m["vb"], masks["wide"]))
            wm_ref[rs, :] = _dot(m["tm"], _diag_blocks(m["kbg"], masks["wide"])).astype(wm_ref.dtype)
            qg_ref[rs, :] = m["qg"].astype(qg_ref.dtype)
            kt_ref[rs, :] = m["ktail"].astype(kt_ref.dtype)
            qkd = (m["qk"] * m["decay"]).astype(qkd_ref.dtype)
            for h in range(DN_HEADS):
                qkd_ref[rs, pl.ds(DN_HD * h, DN_CHUNK)] = qkd[:, DN_CHUNK * h:DN_CHUNK * (h + 1)]
            tm_ref[rs, :] = m["tm"]

    row = pl.BlockSpec((rows, GW), lambda i: (i, 0))
    sds = jax.ShapeDtypeStruct
    return pl.pallas_call(
        body, name="dn_local_fwd", grid=(t // rows,), in_specs=[row, row, row, pl.BlockSpec((rows, 128), lambda i: (i, 0))],
        out_specs=[row] * 5 + [pl.BlockSpec((rows, CAT), lambda i: (i, 0))],
        out_shape=[sds((t, GW), F32)] + [sds((t, GW), MXU_DTYPE)] * 4 + [sds((t, CAT), F32)],
        compiler_params=_params(("parallel",)))(qn, kn, vs, gb)


def _dn_eg_last(gbv, h):
    return jnp.exp(jnp.sum(gbv[:, h:h + 1], axis=0, keepdims=True))


def _dn_seq_fwd(u, wm, qg, ktail, qkd, gb, proj, norm_g, *, cb_gate):
    t = u.shape[0]
    nc = t // DN_CHUNK

    def body(u_ref, wm_ref, qg_ref, kt_ref, qkd_ref, gb_ref, gate_ref, ng_ref, o_ref, raw_ref, vn_ref, st_ref, s_ref):
        @pl.when(pl.program_id(0) == 0)
        def _():
            s_ref[...] = jnp.zeros_like(s_ref)

        gbv = gb_ref[...]
        for h in range(DN_HEADS):
            lanes = pl.ds(DN_HD * h, DN_HD)
            s = s_ref[h]
            st_ref[0, h] = s
            v_new = u_ref[:, lanes] - _dot(wm_ref[:, lanes], s)
            s_ref[h] = s * _dn_eg_last(gbv, h) + _dot(kt_ref[:, lanes], v_new, TN)
            o = _dot(qg_ref[:, lanes], s) + _dot(qkd_ref[:, pl.ds(DN_HD * h, DN_CHUNK)], v_new)
            vn_ref[:, lanes] = v_new.astype(vn_ref.dtype)
            raw_ref[:, lanes] = o
            r = lax.rsqrt(jnp.mean(o * o, axis=-1, keepdims=True) + RMS_EPS)
            gt = gate_ref[:, lanes]
            o_ref[:, lanes] = (o * r * ng_ref[...] * (gt * _sigmoid(gt))).astype(o_ref.dtype)

    row = pl.BlockSpec((DN_CHUNK, GW), lambda i: (i, 0))
    sds = jax.ShapeDtypeStruct
    return pl.pallas_call(
        body, name="dn_seq_fwd", grid=(nc,),
        in_specs=[row] * 5 + [pl.BlockSpec((DN_CHUNK, 128), lambda i: (i, 0)), _cols(DN_CHUNK, GW, cb_gate), _full((1, DN_HD))],
        out_specs=[row, row, row, pl.BlockSpec((1, DN_HEADS, DN_HD, DN_HD), lambda i: (i, 0, 0, 0))],
        out_shape=[sds((t, GW), MXU_DTYPE), sds((t, GW), F32), sds((t, GW), MXU_DTYPE), sds((nc, DN_HEADS, DN_HD, DN_HD), F32)],
        scratch_shapes=[pltpu.VMEM((DN_HEADS, DN_HD, DN_HD), F32)], compiler_params=_params(("arbitrary",)))(
            u, wm, qg, ktail, qkd, gb, proj, norm_g)


def _dn_seq_bwd(dmixed, raw, proj, norm_g, gb, wm, qg, ktail, qkd, v_new, states, *, cb_dy, cb_gate):
    t = raw.shape[0]
    nc = t // DN_CHUNK

    def body(dy_ref, raw_ref, gate_ref, ng_ref, gb_ref, wm_ref, qg_ref, kt_ref, qkd_ref, vn_ref, st_ref,
             dgate_ref, dng_ref, do_ref, dvn_ref, dwm_ref, dqg_ref, dkt_ref, degl_ref, ds_ref):
        @pl.when(pl.program_id(0) == 0)
        def _():
            ds_ref[...] = jnp.zeros_like(ds_ref)
            dng_ref[...] = jnp.zeros_like(dng_ref)

        gbv = gb_ref[...]
        lane = lax.broadcasted_iota(jnp.int32, (1, 128), 1)
        degl = jnp.zeros((1, 128), F32)
        for h in range(DN_HEADS):
            lanes = pl.ds(DN_HD * h, DN_HD)
            s, ds_out = st_ref[0, h], ds_ref[h]
            o, gt, dy, ng = raw_ref[:, lanes], gate_ref[:, lanes], dy_ref[:, lanes], ng_ref[...]
            r = lax.rsqrt(jnp.mean(o * o, axis=-1, keepdims=True) + RMS_EPS)
            sil, dsil = _silu_parts(gt)
            d_on = dy * sil
            dgate_ref[:, lanes] = (dy * (o * r * ng) * dsil).astype(dgate_ref.dtype)
            dng_ref[...] += jnp.sum(d_on * o * r, axis=0, keepdims=True)
            w = d_on * ng
            do = r * w - o * (r * r * r) * jnp.mean(w * o, axis=-1, keepdims=True)
            d_vnew = _dot(qkd_ref[:, pl.ds(DN_HD * h, DN_CHUNK)], do, TN) + _dot(kt_ref[:, lanes], ds_out)
            ds_ref[h] = _dot(qg_ref[:, lanes], do, TN) + _dn_eg_last(gbv, h) * ds_out - _dot(wm_ref[:, lanes], d_vnew, TN)
            do_ref[:, lanes] = do.astype(do_ref.dtype)
            dvn_ref[:, lanes] = d_vnew.astype(dvn_ref.dtype)
            dwm_ref[:, lanes] = (-_dot(d_vnew, s, NT)).astype(dwm_ref.dtype)
            dqg_ref[:, lanes] = _dot(do, s, NT)
            dkt_ref[:, lanes] = _dot(vn_ref[:, lanes], ds_out, NT)
            d_eglast = jnp.sum(jnp.sum(s * ds_out, axis=1, keepdims=True), axis=0, keepdims=True)
            degl = degl + jnp.where(lane == h, d_eglast, 0.0)
        degl_ref[0] = degl

    row = pl.BlockSpec((DN_CHUNK, GW), lambda i: (nc - 1 - i, 0))
    small = pl.BlockSpec((DN_CHUNK, 128), lambda i: (nc - 1 - i, 0))
    sds = jax.ShapeDtypeStruct
    return pl.pallas_call(
        body, name="dn_seq_bwd", grid=(nc,),
        in_specs=[_cols_rev(DN_CHUNK, GW, cb_dy, nc), row, _cols_rev(DN_CHUNK, GW, cb_gate, nc), _full((1, DN_HD)), small,
                  row, row, row, row, row, pl.BlockSpec((1, DN_HEADS, DN_HD, DN_HD), lambda i: (nc - 1 - i, 0, 0, 0))],
        out_specs=[row, _full((1, DN_HD)), row, row, row, row, row, pl.BlockSpec((1, 1, 128), lambda i: (nc - 1 - i, 0, 0))],
        out_shape=[sds((t, GW), MXU_DTYPE), sds((1, DN_HD), F32), sds((t, GW), MXU_DTYPE), sds((t, GW), MXU_DTYPE),
                   sds((t, GW), MXU_DTYPE), sds((t, GW), F32), sds((t, GW), F32), sds((nc, 1, 128), F32)],
        scratch_shapes=[pltpu.VMEM((DN_HEADS, DN_HD, DN_HD), F32)], compiler_params=_params(("arbitrary",)))(
            dmixed, raw, proj, norm_g, gb, wm, qg, ktail, qkd, v_new, states)


def _dn_local_bwd(qn, kn, vs, gb, tm, v_new, do, d_vnew, d_wm, d_qg, d_ktail, d_eglast):
    t = qn.shape[0]
    rows = DN_CHUNK * DN_LOCAL_CHUNKS

    def body(q_ref, k_ref, v_ref, gb_ref, tm_ref, vn_ref, do_ref, dvn_ref, dwm_ref, dqg_ref, dkt_ref, degl_ref,
             dq_ref, dk_ref, dv_ref, dgb_ref):
        masks = _dn_masks()
        wide, square = masks["wide"], masks["square"]
        lane = lax.broadcasted_iota(jnp.int32, (DN_CHUNK, 128), 1)
        last_row = lax.broadcasted_iota(jnp.int32, (DN_CHUNK, 1), 0) == DN_CHUNK - 1
        for n in range(DN_LOCAL_CHUNKS):
            rs = pl.ds(DN_CHUNK * n, DN_CHUNK)
            q, k, v, tm = q_ref[rs, :], k_ref[rs, :], v_ref[rs, :], tm_ref[rs, :]
            m = _dn_local_math(q, k, v, gb_ref[rs, :], masks, tm=tm)
            decay, eg, k_rows = m["decay"], m["eg"], m["k_rows"]
            d_vnew, d_wm, d_qg, d_ktail = dvn_ref[rs, :], dwm_ref[rs, :], dqg_ref[rs, :], dkt_ref[rs, :]
            deglv = degl_ref[n]
            dq = d_qg * eg
            dk = d_ktail * m["etail"]
            tails = _head_sums(d_ktail * m["ktail"], DN_HD)
            dgcs = _head_sums(d_qg * m["qg"], DN_HD)
            d_qkd = jnp.where(masks["tril"], _dot(do_ref[rs, :], _diag_blocks(vn_ref[rs, :], wide), NT), 0.0)
            dqk_dec = d_qkd * decay
            dq = dq + _dot(dqk_dec, k_rows)
            dk = dk + _fold_blocks(_dot(dqk_dec, q, TN), wide)
            ddecay = d_qkd * m["qk"]
            d_tm = _dot(d_vnew, _diag_blocks(m["vb"], wide), NT) + _dot(d_wm, _diag_blocks(m["kbg"], wide), NT)
            d_vb = _fold_blocks(_dot(tm, d_vnew, TN), wide)
            d_kbg = _fold_blocks(_dot(tm, d_wm, TN), wide)
            d_kb = d_kbg * eg
            kbgs = _head_sums(d_kbg * m["kbg"], DN_HD)
            x = _fold_blocks(_dot(tm, d_tm, TN, exact=True), square)
            d_n = jnp.where(masks["strict"], -_dot(x, _diag_blocks(tm, square), NT, exact=True), 0.0)
            d_kk = d_n * decay
            d_kb = d_kb + _dot(d_kk, k_rows)
            dk = dk + _fold_blocks(_dot(d_kk, m["kb"], TN), wide)
            ddecay = ddecay + d_n * m["kk"]
            dk = dk + d_kb * m["beta"]
            dbetas = [a + b for a, b in zip(_head_sums(d_kb * k, DN_HD), _head_sums(d_vb * v, DN_HD))]
            dv_ref[rs, :] = d_vb * m["beta"]
            dq_ref[rs, :] = dq
            dk_ref[rs, :] = dk
            dd = ddecay * decay
            row_sums = _head_sums(dd, DN_CHUNK)
            dgc_cols = jnp.zeros((DN_CHUNK, 128), F32)
            for h in range(DN_HEADS):
                dgl = jnp.sum(tails[h], axis=0, keepdims=True) + deglv[:, h:h + 1] * m["eg_last"][:, DN_HD * h:DN_HD * h + 1]
                dgc_cols = jnp.where(lane == h, dgcs[h] - tails[h] + kbgs[h] + row_sums[h] + jnp.where(last_row, dgl, 0.0), dgc_cols)
            dgc_row = (jnp.sum(jnp.where(masks["eye"], _expand(dgc_cols, 0, DN_CHUNK), 0.0), axis=0, keepdims=True)
                       - jnp.sum(dd, axis=0, keepdims=True))
            dgs = _head_sums(jnp.where(masks["triu"], dgc_row, 0.0), DN_CHUNK)
            dgb = jnp.zeros((DN_CHUNK, 128), F32)
            for h in range(DN_HEADS):
                dgb = jnp.where(lane == h, dgs[h], jnp.where(lane == DN_HEADS + h, dbetas[h], dgb))
            dgb_ref[rs, :] = dgb

    row = pl.BlockSpec((rows, GW), lambda i: (i, 0))
    small = pl.BlockSpec((rows, 128), lambda i: (i, 0))
    sds = jax.ShapeDtypeStruct
    return pl.pallas_call(
        body, name="dn_local_bwd", grid=(t // rows,),
        in_specs=[row, row, row, small, pl.BlockSpec((rows, CAT), lambda i: (i, 0))] + [row] * 6
        + [pl.BlockSpec((DN_LOCAL_CHUNKS, 1, 128), lambda i: (i, 0, 0))],
        out_specs=[row, row, row, small], out_shape=[sds((t, GW), F32)] * 3 + [sds((t, 128), F32)],
        compiler_params=_params(("parallel",)))(qn, kn, vs, gb, tm, v_new, do, d_vnew, d_wm, d_qg, d_ktail, d_eglast)


ANY = pl.BlockSpec(memory_space=pl.ANY)
PAIR_SPLIT = 4


def _place():
    x, y, c = lax.axis_index("x"), lax.axis_index("y"), lax.axis_index("c")
    chips = [(1 - x, y), (x, 1 - y), (1 - x, 1 - y)]
    return x, y, c, chips


def _remote(src, dst, send_sem, recv_sem, to):
    return pltpu.make_async_remote_copy(src_ref=src, dst_ref=dst, send_sem=send_sem, recv_sem=recv_sem, device_id=to,
                                        device_id_type=MESH)


def _allgather_chips(arrs):
    n = len(arrs)

    def body(*refs):
        ins, outs = refs[:n], refs[n:2 * n]
        send_sems, recv_sems = refs[2 * n:]
        x, y, c, chips = _place()
        me = 2 * x + y
        sibling = (x, y, 1 - c)
        sends, fwds = [], []
        for a in range(n):
            half = ins[a].shape[0] // 2
            mine = pl.ds(c * half, half)
            for k, chip in enumerate(chips):
                s = _remote(ins[a].at[mine], outs[a].at[me, mine], send_sems.at[a, k], recv_sems.at[a, k], (*chip, c))
                s.start()
                sends.append(s)
        for a in range(n):
            half = ins[a].shape[0] // 2
            mine, other = pl.ds(c * half, half), pl.ds((1 - c) * half, half)
            for k, (cx, cy) in enumerate(chips):
                j = 2 * cx + cy
                got = outs[a].at[j, mine]
                _remote(got, got, send_sems.at[a, k], recv_sems.at[a, k], (cx, cy, c)).wait_recv()
                f = _remote(got, got, send_sems.at[a, 3 + k], recv_sems.at[a, 3 + k], sibling)
                f.start()
                fwds.append(f)
        for a in range(n):
            half = ins[a].shape[0] // 2
            other = pl.ds((1 - c) * half, half)
            for k, (cx, cy) in enumerate(chips):
                got = outs[a].at[2 * cx + cy, other]
                _remote(got, got, send_sems.at[a, 3 + k], recv_sems.at[a, 3 + k], sibling).wait_recv()
        for s in sends + fwds:
            s.wait_send()

    return pl.pallas_call(
        body, name="allgather_weights", in_specs=[ANY] * n, out_specs=[ANY] * n,
        out_shape=[jax.ShapeDtypeStruct((4,) + a.shape, a.dtype) for a in arrs],
        scratch_shapes=[pltpu.SemaphoreType.DMA((n, 6)), pltpu.SemaphoreType.DMA((n, 6))])(*arrs)


def _pair_exchange(gbs):
    n = len(gbs)

    def body(*refs):
        ins, got_refs = refs[:n], refs[n:2 * n]
        send_sems, recv_sems = refs[2 * n:]
        x, y, c, _ = _place()
        work = []
        for a in range(n):
            half = ins[a].shape[1] // 2
            piece = half // PAIR_SPLIT
            for r in range(PAIR_SPLIT):
                s = _remote(ins[a].at[:, pl.ds((1 - c) * half + r * piece, piece)], got_refs[a].at[:, pl.ds(r * piece, piece)],
                            send_sems.at[a, r], recv_sems.at[a, r], (x, y, 1 - c))
                s.start()
                work.append(s)
        for s in work:
            s.wait()

    return pl.pallas_call(
        body, name="grad_pair_exchange", in_specs=[ANY] * n, out_specs=[ANY] * n,
        out_shape=[jax.ShapeDtypeStruct((4, g.shape[1] // 2, g.shape[2]), g.dtype) for g in gbs],
        scratch_shapes=[pltpu.SemaphoreType.DMA((n, PAIR_SPLIT)), pltpu.SemaphoreType.DMA((n, PAIR_SPLIT))])(*gbs)


def _chip_exchange(ps):
    n = len(ps)

    def body(*refs):
        ins, got_refs = refs[:n], refs[n:2 * n]
        send_sems, recv_sems = refs[2 * n:]
        x, y, c, chips = _place()
        work = []
        for a in range(n):
            for k, (cx, cy) in enumerate(chips):
                s = _remote(ins[a].at[2 * cx + cy], got_refs[a].at[k], send_sems.at[a, k], recv_sems.at[a, k], (cx, cy, c))
                s.start()
                work.append(s)
        for s in work:
            s.wait()

    return pl.pallas_call(
        body, name="grad_chip_exchange", in_specs=[ANY] * n, out_specs=[ANY] * n,
        out_shape=[jax.ShapeDtypeStruct((3,) + p.shape[1:], p.dtype) for p in ps],
        scratch_shapes=[pltpu.SemaphoreType.DMA((n, 3)), pltpu.SemaphoreType.DMA((n, 3))])(*ps)


def _pair_join(bufs):
    n = len(bufs)

    def body(*refs):
        outs = refs[n:2 * n]
        send_sems, recv_sems = refs[2 * n:]
        x, y, c, _ = _place()
        work = []
        for a in range(n):
            s = _remote(outs[a].at[c], outs[a].at[c], send_sems.at[a], recv_sems.at[a], (x, y, 1 - c))
            s.start()
            work.append(s)
        for s in work:
            s.wait()

    return pl.pallas_call(
        body, name="grad_pair_join", in_specs=[ANY] * n, out_specs=[ANY] * n,
        out_shape=[jax.ShapeDtypeStruct(b.shape, b.dtype) for b in bufs], input_output_aliases={a: a for a in range(n)},
        scratch_shapes=[pltpu.SemaphoreType.DMA((n,)), pltpu.SemaphoreType.DMA((n,))])(*bufs)


def _pair_sum(gb, got, place, *, name, block_bytes=1 << 20):
    _, r, cols = gb.shape
    half = r // 2
    tr = _row_tile(half, cols, block_bytes)

    def body(place_ref, g_ref, got_ref, o_ref):
        o_ref[...] = g_ref[...] + got_ref[...]

    blk = pl.BlockSpec((None, tr, cols), lambda j, i, p: (j, i, 0))
    grid_spec = pltpu.PrefetchScalarGridSpec(
        num_scalar_prefetch=1, grid=(4, half // tr),
        in_specs=[pl.BlockSpec((None, None, tr, cols), lambda j, i, p: (j, p[0], i, 0)), blk], out_specs=blk)
    return pl.pallas_call(body, name=name, grid_spec=grid_spec, out_shape=jax.ShapeDtypeStruct((4, half, cols), F32),
                          compiler_params=_params(("parallel", "parallel")))(place, gb.reshape(4, 2, half, cols), got)


def _chip_sum(p, got, place, *, name, block_bytes=1 << 20):
    _, h, cols = p.shape
    tr = _row_tile(h, cols, block_bytes)

    def body(place_ref, p_ref, g0, g1, g2, o_ref):
        o_ref[...] = p_ref[...] + g0[...] + g1[...] + g2[...]

    def got_spec(k):
        return pl.BlockSpec((None, tr, cols), functools.partial(lambda i, pr, k: (k, i, 0), k=k))

    grid_spec = pltpu.PrefetchScalarGridSpec(
        num_scalar_prefetch=1, grid=(h // tr,),
        in_specs=[pl.BlockSpec((None, tr, cols), lambda i, pr: (pr[1], i, 0)), got_spec(0), got_spec(1), got_spec(2)],
        out_specs=pl.BlockSpec((None, tr, cols), lambda i, pr: (pr[0], i, 0)))
    return pl.pallas_call(body, name=name, grid_spec=grid_spec, out_shape=jax.ShapeDtypeStruct((2, h, cols), F32),
                          compiler_params=_params(("parallel",)))(place, p, got, got, got)


def _allgather_all(v):
    def body(v_ref, out_ref, send_sems, recv_sems, local_sem):
        x, y, c, chips = _place()
        me, sibling = (x, y, c), (x, y, 1 - c)

        def rows(px, py, pc):
            return out_ref.at[4 * px + 2 * py + pc]

        def copy(k, block, to, src=None):
            return _remote(rows(*block) if src is None else src, rows(*block), send_sems.at[k], recv_sems.at[k], to)

        mine = pltpu.make_async_copy(v_ref, rows(*me), local_sem)
        mine.start()
        first = [copy(0, me, sibling, src=v_ref)] + [copy(1 + j, me, (*chip, c), src=v_ref) for j, chip in enumerate(chips)]
        for cp in first:
            cp.start()
        passed = [copy(4 + j, (*chip, c), sibling) for j, chip in enumerate(chips)]
        for j, chip in enumerate(chips):
            copy(1 + j, (*chip, c), me).wait_recv()
            passed[j].start()
        copy(0, sibling, me).wait_recv()
        for j, chip in enumerate(chips):
            copy(4 + j, (*chip, 1 - c), me).wait_recv()
        for cp in first + passed:
            cp.wait_send()
        mine.wait()

    return pl.pallas_call(
        body, name="allgather_small", in_specs=[ANY], out_specs=ANY, out_shape=jax.ShapeDtypeStruct((8,) + v.shape, v.dtype),
        scratch_shapes=[pltpu.SemaphoreType.DMA((7,)), pltpu.SemaphoreType.DMA((7,)), pltpu.SemaphoreType.DMA],
        )(v)


def _row_tile(rows, cols, limit_bytes):
    for d in range(1, rows + 1):
        if rows % d == 0 and (rows // d) % 8 == 0 and (rows // d) * cols * 4 <= limit_bytes:
            return rows // d
    return rows


def _sum_kernel(parts, *, name, block_bytes=1 << 20):
    n = len(parts)
    rows, cols = parts[0][0].shape[1:] if isinstance(parts[0], tuple) else parts[0].shape
    tr = _row_tile(rows, cols, block_bytes)
    ins, specs = [], []
    for part in parts:
        if isinstance(part, tuple):
            ins.append(part[0])
            specs.append(pl.BlockSpec((None, tr, cols), functools.partial(lambda i, s: (s, i, 0), s=part[1])))
        else:
            ins.append(part)
            specs.append(pl.BlockSpec((tr, cols), lambda i: (i, 0)))

    def body(*refs):
        acc = refs[0][...]
        for r in refs[1:n]:
            acc = acc + r[...]
        refs[n][...] = acc

    return pl.pallas_call(body, name=name, grid=(rows // tr,), in_specs=specs, out_specs=pl.BlockSpec((tr, cols), lambda i: (i, 0)),
                          out_shape=jax.ShapeDtypeStruct((rows, cols), F32), compiler_params=_params(("parallel",)))(*ins)


def _reduce_scatter(gbs, names, place):
    got = _pair_exchange(gbs)
    ps = [_pair_sum(g, r, place, name="pair_sum_" + nm) for g, r, nm in zip(gbs, got, names)]
    others = _chip_exchange(ps)
    bufs = [_chip_sum(p, t, place, name="chip_sum_" + nm) for p, t, nm in zip(ps, others, names)]
    return [j.reshape(-1, j.shape[-1]) for j in _pair_join(bufs)]


def _adamw(w, g, m, v, *, name, block_bytes=1 << 20):
    rows, cols = w.shape
    tr = _row_tile(rows, cols, block_bytes)

    def body(w_ref, g_ref, m_ref, v_ref, d_ref, nm_ref, nv_ref):
        gv = g_ref[...]
        nm = ADAM_B1 * m_ref[...] + (1.0 - ADAM_B1) * gv
        nv = ADAM_B2 * v_ref[...] + (1.0 - ADAM_B2) * (gv * gv)
        m_hat = nm / (1.0 - ADAM_B1 ** ADAM_STEP)
        v_hat = nv / (1.0 - ADAM_B2 ** ADAM_STEP)
        d_ref[...] = -ADAM_LR * (m_hat / (jnp.sqrt(v_hat) + ADAM_EPS) + ADAM_WD * w_ref[...])
        nm_ref[...] = nm
        nv_ref[...] = nv

    spec = pl.BlockSpec((tr, cols), lambda i: (i, 0))
    return pl.pallas_call(body, name=name, grid=(rows // tr,), in_specs=[spec] * 4, out_specs=[spec] * 3,
                          out_shape=[jax.ShapeDtypeStruct((rows, cols), F32)] * 3, compiler_params=_params(("parallel",)))(w, g, m, v)


WEIGHTS = ['w_in', 's5_lambda_re', 's5_lambda_im', 's5_log_step', 's5_b_re', 's5_b_im', 's5_c_re', 's5_c_im', 's5_d', 's5_glu_w',
           's5_glu_b', 'sgu_norm_g', 'sgu_norm_b', 'sgu_w', 'sgu_b', 'pool_w', 'pool_scale', 'dn_conv_w', 'dn_a_log', 'dn_dt_bias',
           'dn_norm_g', 'w_out', 'ln1_g', 'ln1_b', 'w_up', 'w_down', 'ln2_g', 'ln2_b']
BIG = ['w_in', 's5_glu_w', 'w_out', 'w_up', 'w_down']
SMALL = [n for n in WEIGHTS if n not in BIG]
CB_S5, CB_SGU_U, CB_SGU_V, CB_POOL, CB_DN_Q, CB_DN_GATE = 0, 1, 2, 3, 4, 7
KT = 2048


def _pad_lanes(v, width=128):
    return jnp.zeros((1, width), F32).at[0, :v.shape[0]].set(v)


def _layer_consts(p):
    c = _s5_prepare(p['s5_lambda_re'], p['s5_lambda_im'], p['s5_log_step'], p['s5_b_re'], p['s5_b_im'], p['s5_c_re'], p['s5_c_im'])
    tril = jnp.tril(jnp.ones((SGU_CHUNK, SGU_CHUNK), bool))
    wm = jnp.where(tril, p['sgu_w'], 0.0)
    c.update(s5_d=p['s5_d'].reshape(1, GW), glu_b=p['s5_glu_b'].reshape(1, GW), sgu_ng=p['sgu_norm_g'].reshape(1, GW),
             sgu_nb=p['sgu_norm_b'].reshape(1, GW), sgu_w=wm, sgu_wt=jnp.swapaxes(wm, 1, 2),
             sgu_bias=jnp.repeat(p['sgu_b'].T, SGU_HD, axis=1), pool_w=p['pool_w'], pool_scale=p['pool_scale'].reshape(1, GW),
             conv_w=p['dn_conv_w'], alog=_pad_lanes(p['dn_a_log']), dtb=_pad_lanes(p['dn_dt_bias']), dn_ng=p['dn_norm_g'].reshape(1, DN_HD),
             ln1_g=p['ln1_g'].reshape(1, D_MODEL), ln1_b=p['ln1_b'].reshape(1, D_MODEL), ln2_g=p['ln2_g'].reshape(1, D_MODEL),
             ln2_b=p['ln2_b'].reshape(1, D_MODEL))
    return c


def _layer_fwd(xin, xin16, w, c, i):
    tag = str(i)
    residual = lambda r, e: (r + ALPHA * e,)
    (proj,) = _matmul(xin16, w['w_in'], mode="nn", name="proj" + tag, tn=1408, tk=KT)
    s5, xre, xim = _s5_fwd(proj, c['bbre'], c['bbim'], c['ccre'], c['ccim'], c['s5_d'], c['cf'], w['s5_glu_w'], c['glu_b'], cb=CB_S5)
    sgu = _sgu_fwd(proj, c['sgu_ng'], c['sgu_nb'], c['sgu_w'], c['sgu_bias'], cbu=CB_SGU_U, cbv=CB_SGU_V)
    pool, pooled = _pool_fwd(proj, c['pool_w'], c['pool_scale'], cb=CB_POOL)
    qn, kn, vs, gb = _dn_prep_fwd(proj, c['conv_w'], c['alog'], c['dtb'], cbq=CB_DN_Q)
    u, wm, qg, ktail, qkd, tm = _dn_local_fwd(qn, kn, vs, gb)
    dn, raw, v_new, states = _dn_seq_fwd(u, wm, qg, ktail, qkd, gb, proj, c['dn_ng'], cb_gate=CB_DN_GATE)
    mixed = jnp.concatenate([s5, sgu, pool, dn], axis=1)
    (h1,) = _matmul(mixed, w['w_out'], mode="nn", name="mix_out" + tag, e=xin, epi=residual, tk=KT)
    x1, x1_16 = _ln_fwd(h1, c['ln1_g'], c['ln1_b'], name="ln1_" + tag)
    (hidden,) = _matmul(x1_16, w['w_up'], mode="nn", name="mlp_up" + tag, epi=lambda r, e: (_relu2(r),), out_dtypes=(MXU_DTYPE,), tk=KT)
    (h2,) = _matmul(hidden, w['w_down'], mode="nn", name="mlp_down" + tag, e=x1, epi=residual, tk=KT)
    x2, x2_16 = _ln_fwd(h2, c['ln2_g'], c['ln2_b'], name="ln2_" + tag)
    saved = dict(xin16=xin16, proj=proj, xre=xre, xim=xim, pooled=pooled, qn=qn, kn=kn, vs=vs, gb=gb, raw=raw, states=states,
                 wm=wm, qg=qg, ktail=ktail, qkd=qkd, tm=tm, v_new=v_new, mixed=mixed, h1=h1, x1_16=x1_16, hidden=hidden, h2=h2)
    return x2, x2_16, saved


def _layer_bwd(dx2, s, w, c, p, i):
    tag = str(i)
    residual = lambda r, e: (r + ALPHA * e,)
    dh2, dh2_16, dln2g, dln2b = _ln_bwd(dx2, s['h2'], c['ln2_g'], name="ln2_bwd" + tag)
    (dw_down,) = _matmul(s['hidden'], dh2_16, mode="tn", name="dw_down" + tag, tk=KT)
    (da,) = _matmul(dh2_16, w['w_down'], mode="nt", name="d_hidden" + tag, e=s['hidden'],
                    epi=lambda r, e: (r * (2.0 * jnp.sqrt(e.astype(F32))),), out_dtypes=(MXU_DTYPE,), tk=KT)
    (dw_up,) = _matmul(s['x1_16'], da, mode="tn", name="dw_up" + tag, tk=KT)
    (dx1,) = _matmul(da, w['w_up'], mode="nt", name="dx_mlp" + tag, e=dh2, epi=residual, tk=KT)
    dh1, dh1_16, dln1g, dln1b = _ln_bwd(dx1, s['h1'], c['ln1_g'], name="ln1_bwd" + tag)
    (dw_out,) = _matmul(s['mixed'], dh1_16, mode="tn", name="dw_out" + tag, tk=KT)
    (dmixed,) = _matmul(dh1_16, w['w_out'], mode="nt", name="d_mixed" + tag, tk=KT)
    proj = s['proj']
    (du, dglu_w, dglu_b, dd, dccre, dccim, dbbre, dbbim, sre, sim) = _s5_bwd(
        dmixed, proj, s['xre'], s['xim'], c['bbre'], c['bbim'], c['ccre'], c['ccim'], c['s5_d'], c['cr'], w['s5_glu_w'], c['glu_b'],
        cb_dy=0, cb=CB_S5)
    dlam_re, dlam_im, dlog_step, db_re, db_im, dc_re, dc_im = _s5_param_grads(
        p['s5_lambda_re'], p['s5_lambda_im'], p['s5_log_step'], p['s5_b_re'], p['s5_b_im'], dbbre, dbbim, dccre, dccim, sre, sim)
    dzu, dzv, dsgu_w, dsgu_bias, dsgu_ng, dsgu_nb = _sgu_bwd(dmixed, proj, c['sgu_ng'], c['sgu_nb'], c['sgu_w'], c['sgu_wt'], c['sgu_bias'],
                                                            cb=1, cbu=CB_SGU_U, cbv=CB_SGU_V)
    dp, dpool_w, dpool_scale = _pool_bwd(dmixed, s['pooled'], c['pool_w'], c['pool_scale'], cb=2)
    dgate, ddn_ng, do, d_vnew, d_wm, d_qg, d_ktail, d_eglast = _dn_seq_bwd(
        dmixed, s['raw'], proj, c['dn_ng'], s['gb'], s['wm'], s['qg'], s['ktail'], s['qkd'], s['v_new'], s['states'],
        cb_dy=3, cb_gate=CB_DN_GATE)
    dqn, dkn, dvs, dgb = _dn_local_bwd(s['qn'], s['kn'], s['vs'], s['gb'], s['tm'], s['v_new'], do, d_vnew, d_wm, d_qg, d_ktail, d_eglast)
    dcq, dck, dcv, dab, dalog, ddtb = _dn_prep_bwd_a(proj, c['conv_w'], c['alog'], c['dtb'], dqn, dkn, dvs, dgb, cbq=CB_DN_Q)
    dq, dk, dv, dconv_w = _dn_prep_bwd_b(proj, c['conv_w'], dcq, dck, dcv, cbq=CB_DN_Q)
    dproj = jnp.concatenate([du, dzu, dzv, dp, dq, dk, dv, dgate, dab], axis=1)
    (dw_in,) = _matmul(s['xin16'], dproj, mode="tn", name="dw_in" + tag, tn=1408, tk=KT)
    (dxin,) = _matmul(dproj, w['w_in'], mode="nt", name="dx_in" + tag, tk=1408, e=dh1, epi=residual)
    tril = jnp.tril(jnp.ones((SGU_CHUNK, SGU_CHUNK), bool))
    big = dict(w_in=dw_in, s5_glu_w=dglu_w, w_out=dw_out, w_up=dw_up, w_down=dw_down)
    small = dict(
        s5_lambda_re=dlam_re, s5_lambda_im=dlam_im, s5_log_step=dlog_step, s5_b_re=db_re, s5_b_im=db_im, s5_c_re=dc_re, s5_c_im=dc_im,
        s5_d=dd.reshape(S5_G, S5_H), s5_glu_b=dglu_b[0], sgu_norm_g=dsgu_ng[0], sgu_norm_b=dsgu_nb[0],
        sgu_w=jnp.where(tril, dsgu_w, 0.0), sgu_b=dsgu_bias.reshape(SGU_CHUNK, SGU_HEADS, SGU_HD).sum(-1).T, pool_w=dpool_w,
        pool_scale=dpool_scale[0], dn_conv_w=dconv_w, dn_a_log=dalog[0, :DN_HEADS], dn_dt_bias=ddtb[0, :DN_HEADS], dn_norm_g=ddn_ng[0],
        ln1_g=dln1g[0], ln1_b=dln1b[0], ln2_g=dln2g[0], ln2_b=dln2b[0])
    return dxin, big, small


def _pack(arrs):
    flat = jnp.concatenate([a.reshape(-1) for a in arrs])
    n = flat.shape[0]
    m = -(-n // 1024) * 1024
    return jnp.pad(flat, (0, m - n)).reshape(m // 128, 128)


def _sum_all(stacked):
    return _sum_kernel([(stacked, d) for d in range(stacked.shape[0])], name="small_sum")


def _unpack(packed, like):
    flat, out, off = packed.reshape(-1), [], 0
    for a in like:
        n = math.prod(a.shape)
        out.append(flat[off:off + n].reshape(a.shape))
        off += n
    return out


def kernel(x, w_in, s5_lambda_re, s5_lambda_im, s5_log_step, s5_b_re, s5_b_im, s5_c_re, s5_c_im, s5_d, s5_glu_w, s5_glu_b, sgu_norm_g, sgu_norm_b, sgu_w, sgu_b, pool_w, pool_scale, dn_conv_w, dn_a_log, dn_dt_bias, dn_norm_g, w_out, ln1_g, ln1_b, w_up, w_down, ln2_g, ln2_b, loss_target, m_w_in, m_s5_lambda_re, m_s5_lambda_im, m_s5_log_step, m_s5_b_re, m_s5_b_im, m_s5_c_re, m_s5_c_im, m_s5_d, m_s5_glu_w, m_s5_glu_b, m_sgu_norm_g, m_sgu_norm_b, m_sgu_w, m_sgu_b, m_pool_w, m_pool_scale, m_dn_conv_w, m_dn_a_log, m_dn_dt_bias, m_dn_norm_g, m_w_out, m_ln1_g, m_ln1_b, m_w_up, m_w_down, m_ln2_g, m_ln2_b, v_w_in, v_s5_lambda_re, v_s5_lambda_im, v_s5_log_step, v_s5_b_re, v_s5_b_im, v_s5_c_re, v_s5_c_im, v_s5_d, v_s5_glu_w, v_s5_glu_b, v_sgu_norm_g, v_sgu_norm_b, v_sgu_w, v_sgu_b, v_pool_w, v_pool_scale, v_dn_conv_w, v_dn_a_log, v_dn_dt_bias, v_dn_norm_g, v_w_out, v_ln1_g, v_ln1_b, v_w_up, v_w_down, v_ln2_g, v_ln2_b):
    given = dict(locals())
    xs, ys = lax.axis_index("x"), lax.axis_index("y")
    chip = 2 * xs + ys
    t = given['x'].shape[1]
    x0 = given['x'].reshape(t, D_MODEL)
    target = given['loss_target'].reshape(t, D_MODEL)

    mine16 = [given[n].astype(MXU_DTYPE).reshape(-1, given[n].shape[-1]) for n in BIG]
    gathered = {n: lax.dynamic_update_slice(buf, own[None], (chip, 0, 0)) for n, buf, own in zip(BIG, _allgather_chips(mine16), mine16)}
    conv_local = given['dn_conv_w']
    conv_all = _allgather_all(_pack([conv_local]))
    n_conv = math.prod(conv_local.shape)
    conv_full = jnp.concatenate([conv_all[2 * j].reshape(-1)[:n_conv].reshape(conv_local.shape) for j in range(4)], axis=-1)

    def layer_weights(i):
        def rows(n, per):
            return gathered[n][:, i * per:(i + 1) * per, :]
        w_in_full = jnp.transpose(rows('w_in', D_MODEL), (1, 0, 2)).reshape(D_MODEL, IN_COLS)
        return dict(w_in=jnp.pad(w_in_full, ((0, 0), (0, IN_PAD - IN_COLS))), s5_glu_w=rows('s5_glu_w', GW // 4).reshape(GW, GW),
                    w_out=rows('w_out', D_MODEL // 4).reshape(D_MODEL, D_MODEL),
                    w_up=jnp.transpose(rows('w_up', D_MODEL), (1, 0, 2)).reshape(D_MODEL, D_FF),
                    w_down=rows('w_down', D_FF // 4).reshape(D_FF, D_MODEL))

    def layer_params(i):
        p = {n: given[n][i] for n in SMALL}
        p['dn_conv_w'] = conv_full[i]
        return p

    ws = [layer_weights(i) for i in range(DEPTH)]
    ps = [layer_params(i) for i in range(DEPTH)]
    cs = [_layer_consts(p) for p in ps]

    xcur, xcur16, saved = x0, x0.astype(MXU_DTYPE), []
    for i in range(DEPTH):
        xcur, xcur16, s = _layer_fwd(xcur, xcur16, ws[i], cs[i], i)
        saved.append(s)
    dx, colsum = _loss_head(xcur, target)
    loss = lax.psum(0.5 * jnp.sum(colsum) / D_MODEL, ("x", "y", "c"))

    bigs, smalls = [None] * DEPTH, [None] * DEPTH
    for i in reversed(range(DEPTH)):
        dx, bigs[i], smalls[i] = _layer_bwd(dx, saved[i], ws[i], cs[i], ps[i], i)
    grad_x = dx.reshape(1, t, D_MODEL)

    def by_rows(g, per):
        return g.reshape(4, per, g.shape[-1])

    def by_cols(g, cols):
        return jnp.transpose(g.reshape(g.shape[0], 4, cols), (1, 0, 2))

    blocks = dict(
        w_in=[by_cols(b['w_in'][:, :IN_COLS], IN_COLS // 4) for b in bigs], s5_glu_w=[by_rows(b['s5_glu_w'], GW // 4) for b in bigs],
        w_out=[by_rows(b['w_out'], D_MODEL // 4) for b in bigs], w_up=[by_cols(b['w_up'], D_MODEL) for b in bigs],
        w_down=[by_rows(b['w_down'], D_FF // 4) for b in bigs])
    place = jnp.stack([lax.axis_index("c"), chip]).astype(jnp.int32)
    reduced = dict(zip(BIG, _reduce_scatter([jnp.concatenate(blocks[n], axis=1) for n in BIG], BIG, place)))

    small_full = [jnp.stack([smalls[i][n] for i in range(DEPTH)]) for n in SMALL]
    grads = dict(zip(SMALL, _unpack(_sum_all(_allgather_all(_pack(small_full))), small_full)))
    grads['dn_conv_w'] = lax.dynamic_slice_in_dim(grads['dn_conv_w'], chip * conv_local.shape[-1], conv_local.shape[-1], axis=2)
    for n in BIG:
        grads[n] = reduced[n].reshape(given[n].shape)

    delta, new_m, new_v = {}, {}, {}
    for n in BIG:
        cols = given[n].shape[-1]
        d, nm, nv = _adamw(given[n].reshape(-1, cols), reduced[n], given['m_' + n].reshape(-1, cols), given['v_' + n].reshape(-1, cols),
                           name="adamw_" + n)
        delta[n], new_m[n], new_v[n] = d.reshape(given[n].shape), nm.reshape(given[n].shape), nv.reshape(given[n].shape)
    like = [given[n] for n in SMALL]
    d, nm, nv = _adamw(_pack(like), _pack([grads[n] for n in SMALL]), _pack([given['m_' + n] for n in SMALL]),
                       _pack([given['v_' + n] for n in SMALL]), name="adamw_small")
    for out, packed in ((delta, d), (new_m, nm), (new_v, nv)):
        out.update(zip(SMALL, _unpack(packed, like)))
    return (loss, grad_x, *[grads[n] for n in WEIGHTS], *[delta[n] for n in WEIGHTS], *[new_m[n] for n in WEIGHTS],
            *[new_v[n] for n in WEIGHTS])
```

```python
import functools
import math

import jax
import jax.numpy as jnp
from jax import lax
from jax.experimental import pallas as pl
from jax.experimental.pallas import tpu as pltpu

F32 = jnp.float32
MXU_DTYPE = jnp.bfloat16
HI = lax.Precision.HIGHEST

D_MODEL = 2048
DEPTH = 2
GW = 512
S5_H = 16
S5_G = GW // S5_H
S5_P = 64
S5_N = S5_G * S5_P
SGU_CHUNK = 128
SGU_HEADS = 8
SGU_HD = GW // SGU_HEADS
POOL_WINDOWS = (2, 4, 8, 16)
POOL_GD = 128
DN_HD = 128
DN_HEADS = 4
DN_CONV = 4
DN_CHUNK = 64
D_FF = 4 * D_MODEL
IN_COLS = 4104
IN_PAD = 4224
LN_EPS = 1e-5
RMS_EPS = 1e-6
L2_EPS = 1e-6
ALPHA = (2 * DEPTH) ** 0.25
ADAM_LR, ADAM_B1, ADAM_B2, ADAM_EPS, ADAM_WD, ADAM_STEP = 0.001, 0.9, 0.999, 1e-08, 0.01, 10

VMEM_LIMIT = 56 * 1024 * 1024
MESH = pl.DeviceIdType.MESH


def _params(sem=None, vmem=VMEM_LIMIT):
    return pltpu.CompilerParams(dimension_semantics=sem, vmem_limit_bytes=vmem)


def _full(shape):
    nd = len(shape)
    return pl.BlockSpec(shape, lambda *_: (0,) * nd)


def _split(a):
    hi = a.astype(MXU_DTYPE)
    return hi, (a - hi.astype(F32)).astype(MXU_DTYPE)


def _dot(a, b, dims=(((1,), (0,)), ((), ())), exact=False):
    if exact and MXU_DTYPE == F32:
        return lax.dot_general(a, b, dims, precision=HI, preferred_element_type=F32)
    if exact:
        (ah, al), (bh, bl) = _split(a), _split(b)
        return (lax.dot_general(ah, bh, dims, preferred_element_type=F32) + lax.dot_general(al, bh, dims, preferred_element_type=F32)
                + lax.dot_general(ah, bl, dims, preferred_element_type=F32))
    return lax.dot_general(a.astype(MXU_DTYPE), b.astype(MXU_DTYPE), dims, preferred_element_type=F32)


NN = (((1,), (0,)), ((), ()))
NT = (((1,), (1,)), ((), ()))
TN = (((0,), (0,)), ((), ()))


def _gelu(x):
    c = math.sqrt(2.0 / math.pi)
    return 0.5 * x * (1.0 + jnp.tanh(c * (x + 0.044715 * x * x * x)))


def _gelu_grad(x):
    c = math.sqrt(2.0 / math.pi)
    t = jnp.tanh(c * (x + 0.044715 * x * x * x))
    return 0.5 * (1.0 + t) + 0.5 * x * (1.0 - t * t) * c * (1.0 + 3.0 * 0.044715 * x * x)


def _sigmoid(x):
    return 1.0 / (1.0 + jnp.exp(-x))


def _relu2(x):
    r = jnp.maximum(x, 0.0)
    return r * r


class _Carry:
    def __init__(self, ins, out_shapes, nsem, start, finish):
        self.ins, self.out_shapes, self.nsem, self.start, self.finish = list(ins), list(out_shapes), nsem, start, finish

    def sems(self):
        return [pltpu.SemaphoreType.DMA((self.nsem,)), pltpu.SemaphoreType.DMA((self.nsem,))]


def _run_carry(carry, name):
    n_in, n_out = len(carry.ins), len(carry.out_shapes)

    def body(*refs):
        parts = refs[:n_in], refs[n_in:n_in + n_out], refs[-2], refs[-1]
        carry.start(*parts)
        carry.finish(*parts)

    any_spec = pl.BlockSpec(memory_space=pl.ANY)
    return pl.pallas_call(body, name=name, in_specs=[any_spec] * n_in, out_specs=[any_spec] * n_out, out_shape=carry.out_shapes,
                          scratch_shapes=carry.sems())(*carry.ins)


def _matmul(a, b, *, mode, name, e=None, epi=None, out_dtypes=(F32,), tm=1024, tn=1024, tk=512, carry=None):
    if mode == "nn":
        (m, k), n = a.shape, b.shape[1]
    elif mode == "nt":
        (m, k), n = a.shape, b.shape[0]
    else:
        (k, m), n = a.shape, b.shape[1]
    tm, tn, tk = min(tm, m), min(tn, n), min(tk, k)
    assert m % tm == 0 and n % tn == 0 and k % tk == 0, (name, m, n, k, tm, tn, tk)
    nk, nout = k // tk, len(out_dtypes)
    dims = {"nn": NN, "nt": NT, "tn": TN}[mode]
    a_spec = pl.BlockSpec((tk, tm), lambda i, j, l: (l, i)) if mode == "tn" else pl.BlockSpec((tm, tk), lambda i, j, l: (i, l))
    b_spec = pl.BlockSpec((tn, tk), lambda i, j, l: (j, l)) if mode == "nt" else pl.BlockSpec((tk, tn), lambda i, j, l: (l, j))
    o_spec = pl.BlockSpec((tm, tn), lambda i, j, l: (i, j))

    n_in = 2 + (e is not None)
    c_in, c_out = (len(carry.ins), len(carry.out_shapes)) if carry is not None else (0, 0)
    gm, gn = m // tm, n // tn

    def body(*refs):
        a_ref, b_ref = refs[:2]
        e_ref = refs[2] if e is not None else None
        o_refs = refs[n_in + c_in:n_in + c_in + nout]
        acc = refs[n_in + c_in + nout + c_out]
        l = pl.program_id(2)
        if carry is not None:
            parts = refs[n_in:n_in + c_in], refs[n_in + c_in + nout:n_in + c_in + nout + c_out], refs[-2], refs[-1]
            step = (pl.program_id(0) * gn + pl.program_id(1)) * nk + l

            @pl.when(step == 0)
            def _():
                carry.start(*parts)

        d = _dot(a_ref[...], b_ref[...], dims)

        def finish(r):
            outs = (r,) if epi is None else epi(r, None if e_ref is None else e_ref[...])
            for o_ref, o, dt in zip(o_refs, outs, out_dtypes, strict=True):
                o_ref[...] = o.astype(dt)

        if nk == 1:
            finish(d)
        else:
            @pl.when(l == 0)
            def _():
                acc[...] = d

            @pl.when((l > 0) & (l < nk - 1))
            def _():
                acc[...] += d

            @pl.when(l == nk - 1)
            def _():
                finish(acc[...] + d)

        if carry is not None:
            @pl.when(step == gm * gn * nk - 1)
            def _():
                carry.finish(*parts)

    ins, specs = [a, b], [a_spec, b_spec]
    if e is not None:
        ins.append(e)
        specs.append(o_spec)
    out_shape = [jax.ShapeDtypeStruct((m, n), dt) for dt in out_dtypes]
    out_specs, scratch = [o_spec] * nout, [pltpu.VMEM((tm, tn), F32)]
    if carry is not None:
        any_spec = pl.BlockSpec(memory_space=pl.ANY)
        ins, specs = ins + carry.ins, specs + [any_spec] * c_in
        out_shape, out_specs, scratch = out_shape + carry.out_shapes, out_specs + [any_spec] * c_out, scratch + carry.sems()
    sem = ("parallel", "parallel", "arbitrary") if carry is None else ("arbitrary",) * 3
    res = pl.pallas_call(body, name=name, grid=(gm, gn, nk), in_specs=specs, out_specs=out_specs, out_shape=out_shape,
                         scratch_shapes=scratch, compiler_params=_params(sem))(*ins)
    return tuple(res) if carry is None else (tuple(res[:nout]), list(res[nout:]))


def _ln_fwd(h, g, b, *, name, tr=256):
    t, d = h.shape

    def body(h_ref, g_ref, b_ref, o_ref, o16_ref):
        x = h_ref[...]
        mu = jnp.mean(x, axis=-1, keepdims=True)
        xc = x - mu
        var = jnp.mean(xc * xc, axis=-1, keepdims=True)
        y = xc * lax.rsqrt(var + LN_EPS) * g_ref[...] + b_ref[...]
        o_ref[...] = y
        o16_ref[...] = y.astype(MXU_DTYPE)

    row = pl.BlockSpec((tr, d), lambda i: (i, 0))
    return pl.pallas_call(body, name=name, grid=(t // tr,), in_specs=[row, _full((1, d)), _full((1, d))], out_specs=[row, row],
                          out_shape=[jax.ShapeDtypeStruct((t, d), F32), jax.ShapeDtypeStruct((t, d), MXU_DTYPE)],
                          compiler_params=_params(("parallel",)))(h, g, b)


def _ln_bwd(dy, h, g, *, name, tr=256):
    t, d = h.shape

    def body(dy_ref, h_ref, g_ref, dh_ref, dh16_ref, dg_ref, db_ref):
        @pl.when(pl.program_id(0) == 0)
        def _():
            dg_ref[...] = jnp.zeros_like(dg_ref)
            db_ref[...] = jnp.zeros_like(db_ref)

        x, dyv = h_ref[...], dy_ref[...]
        mu = jnp.mean(x, axis=-1, keepdims=True)
        xc = x - mu
        rstd = lax.rsqrt(jnp.mean(xc * xc, axis=-1, keepdims=True) + LN_EPS)
        xh = xc * rstd
        w = dyv * g_ref[...]
        dh = rstd * (w - jnp.mean(w, axis=-1, keepdims=True) - xh * jnp.mean(w * xh, axis=-1, keepdims=True))
        dh_ref[...] = dh
        dh16_ref[...] = dh.astype(MXU_DTYPE)
        dg_ref[...] += jnp.sum(dyv * xh, axis=0, keepdims=True)
        db_ref[...] += jnp.sum(dyv, axis=0, keepdims=True)

    row = pl.BlockSpec((tr, d), lambda i: (i, 0))
    vec = _full((1, d))
    return pl.pallas_call(
        body, name=name, grid=(t // tr,), in_specs=[row, row, vec], out_specs=[row, row, vec, vec],
        out_shape=[jax.ShapeDtypeStruct((t, d), F32), jax.ShapeDtypeStruct((t, d), MXU_DTYPE), jax.ShapeDtypeStruct((1, d), F32),
                   jax.ShapeDtypeStruct((1, d), F32)],
        compiler_params=_params(("arbitrary",)))(dy, h, g)


def _loss_head(y, target, *, tr=256):
    t, d = y.shape

    def body(y_ref, t_ref, dy_ref, s_ref):
        @pl.when(pl.program_id(0) == 0)
        def _():
            s_ref[...] = jnp.zeros_like(s_ref)

        err = y_ref[...] - t_ref[...]
        dy_ref[...] = err * (1.0 / d)
        s_ref[...] += jnp.sum(err * err, axis=0, keepdims=True)

    row = pl.BlockSpec((tr, d), lambda i: (i, 0))
    return pl.pallas_call(
        body, name="loss_head", grid=(t // tr,), in_specs=[row, row], out_specs=[row, _full((1, d))],
        out_shape=[jax.ShapeDtypeStruct((t, d), F32), jax.ShapeDtypeStruct((1, d), F32)],
        compiler_params=_params(("arbitrary",)))(y, target)


def _cols(tb, width, cb):
    return pl.BlockSpec((tb, width), lambda i: (i, cb))


def _cols_rev(tb, width, cb, nb):
    return pl.BlockSpec((tb, width), lambda i: (nb - 1 - i, cb))


POOL_HALO = 16


def _pool_fwd(proj, w, scale, *, cb, tb=512):
    t = proj.shape[0]
    tb = min(tb, t)

    def body(p_ref, w_ref, s_ref, o_ref, pooled_ref, ext):
        i = pl.program_id(0)

        @pl.when(i == 0)
        def _():
            ext[pl.ds(0, POOL_HALO), :] = jnp.zeros((POOL_HALO, GW), F32)

        p = p_ref[...]
        ext[pl.ds(POOL_HALO, tb), :] = p
        pos = (i * tb + lax.broadcasted_iota(jnp.int32, (tb, 1), 0) + 1).astype(F32)
        for gi, win in enumerate(POOL_WINDOWS):
            c0 = gi * POOL_GD
            s = p[:, c0:c0 + POOL_GD]
            for k in range(1, win):
                s = s + ext[pl.ds(POOL_HALO - k, tb), pl.ds(c0, POOL_GD)]
            pooled = s / jnp.minimum(pos, float(win)) - p[:, c0:c0 + POOL_GD]
            pooled_ref[:, pl.ds(c0, POOL_GD)] = pooled
            o_ref[:, pl.ds(c0, POOL_GD)] = (_dot(pooled, w_ref[gi]) * s_ref[:, pl.ds(c0, POOL_GD)]).astype(o_ref.dtype)
        ext[pl.ds(0, POOL_HALO), :] = p[tb - POOL_HALO:, :]

    row = pl.BlockSpec((tb, GW), lambda i: (i, 0))
    return pl.pallas_call(
        body, name="pool_fwd", grid=(t // tb,),
        in_specs=[_cols(tb, GW, cb), _full((4, POOL_GD, POOL_GD)), _full((1, GW))], out_specs=[row, row],
        out_shape=[jax.ShapeDtypeStruct((t, GW), MXU_DTYPE), jax.ShapeDtypeStruct((t, GW), F32)],
        scratch_shapes=[pltpu.VMEM((tb + POOL_HALO, GW), F32)],
        compiler_params=_params(("arbitrary",)))(proj, w, scale)


def _pool_bwd(dmixed, pooled, w, scale, *, cb, tb=512):
    t = pooled.shape[0]
    tb = min(tb, t)
    nb = t // tb

    def body(dy_ref, pooled_ref, w_ref, s_ref, dp_ref, dw_ref, ds_ref, ext):
        i = pl.program_id(0)

        @pl.when(i == 0)
        def _():
            ext[pl.ds(tb, POOL_HALO), :] = jnp.zeros((POOL_HALO, GW), F32)
            dw_ref[...] = jnp.zeros_like(dw_ref)
            ds_ref[...] = jnp.zeros_like(ds_ref)

        dy = dy_ref[...]
        pos = ((nb - 1 - i) * tb + lax.broadcasted_iota(jnp.int32, (tb, 1), 0) + 1).astype(F32)
        dpool_all = []
        for gi, win in enumerate(POOL_WINDOWS):
            c0 = gi * POOL_GD
            pg = pooled_ref[:, pl.ds(c0, POOL_GD)]
            dyg = dy[:, c0:c0 + POOL_GD]
            ds_ref[:, pl.ds(c0, POOL_GD)] += jnp.sum(dyg * _dot(pg, w_ref[gi]), axis=0, keepdims=True)
            dmp = dyg * s_ref[:, pl.ds(c0, POOL_GD)]
            dw_ref[gi] += _dot(pg, dmp, TN)
            dpool = _dot(dmp, w_ref[gi], NT)
            dpool_all.append(dpool)
            ext[pl.ds(0, tb), pl.ds(c0, POOL_GD)] = dpool / jnp.minimum(pos, float(win))
        for gi, win in enumerate(POOL_WINDOWS):
            c0 = gi * POOL_GD
            s = ext[pl.ds(0, tb), pl.ds(c0, POOL_GD)]
            for k in range(1, win):
                s = s + ext[pl.ds(k, tb), pl.ds(c0, POOL_GD)]
            dp_ref[:, pl.ds(c0, POOL_GD)] = (s - dpool_all[gi]).astype(dp_ref.dtype)
        ext[pl.ds(tb, POOL_HALO), :] = ext[pl.ds(0, POOL_HALO), :]

    row = pl.BlockSpec((tb, GW), lambda i: (nb - 1 - i, 0))
    return pl.pallas_call(
        body, name="pool_bwd", grid=(nb,),
        in_specs=[_cols_rev(tb, GW, cb, nb), row, _full((4, POOL_GD, POOL_GD)), _full((1, GW))],
        out_specs=[row, _full((4, POOL_GD, POOL_GD)), _full((1, GW))],
        out_shape=[jax.ShapeDtypeStruct((t, GW), MXU_DTYPE), jax.ShapeDtypeStruct((4, POOL_GD, POOL_GD), F32),
                   jax.ShapeDtypeStruct((1, GW), F32)],
        scratch_shapes=[pltpu.VMEM((tb + POOL_HALO, GW), F32)], compiler_params=_params(("arbitrary",)))(dmixed, pooled, w, scale)


def _sgu_core(zu, zv, ng, nb, w_ref, bias):
    tb = zu.shape[0]
    u = _gelu(zu)
    v0 = _gelu(zv)
    mu = jnp.mean(v0, axis=-1, keepdims=True)
    vc = v0 - mu
    rstd = lax.rsqrt(jnp.mean(vc * vc, axis=-1, keepdims=True) + LN_EPS)
    xh = vc * rstd
    vn = xh * ng + nb
    low = lax.broadcasted_iota(jnp.int32, (SGU_CHUNK, 2 * SGU_HD), 1) < SGU_HD
    rows = []
    for n in range(tb // SGU_CHUNK):
        pairs = []
        for j in range(SGU_HEADS // 2):
            vp = vn[n * SGU_CHUNK:(n + 1) * SGU_CHUNK, j * 128:(j + 1) * 128]
            pairs.append(jnp.where(low, _dot(w_ref[2 * j], vp), _dot(w_ref[2 * j + 1], vp)))
        rows.append(jnp.concatenate(pairs, axis=1) + bias)
    mixed = jnp.concatenate(rows, axis=0)
    return u, xh, rstd, vn, mixed


def _sgu_fwd(proj, ng, nb, w, bias, *, cbu, cbv, tb=512):
    t = proj.shape[0]
    tb = min(tb, t)

    def body(zu_ref, zv_ref, ng_ref, nb_ref, w_ref, bias_ref, o_ref):
        u, _, _, _, mixed = _sgu_core(zu_ref[...], zv_ref[...], ng_ref[...], nb_ref[...], w_ref, bias_ref[...])
        o_ref[...] = (u * mixed).astype(o_ref.dtype)

    vec = _full((1, GW))
    return pl.pallas_call(
        body, name="sgu_fwd", grid=(t // tb,),
        in_specs=[_cols(tb, GW, cbu), _cols(tb, GW, cbv), vec, vec, _full((8, 128, 128)), _full((128, GW))],
        out_specs=pl.BlockSpec((tb, GW), lambda i: (i, 0)), out_shape=jax.ShapeDtypeStruct((t, GW), MXU_DTYPE),
        compiler_params=_params(("parallel",)))(proj, proj, ng, nb, w, bias)


def _sgu_bwd(dmixed, proj, ng, nb, w, wt, bias, *, cb, cbu, cbv, tb=512):
    t = proj.shape[0]
    tb = min(tb, t)

    def body(dy_ref, zu_ref, zv_ref, ng_ref, nb_ref, w_ref, wt_ref, bias_ref, dzu_ref, dzv_ref, dw_ref, dbias_ref, dng_ref, dnb_ref):
        @pl.when(pl.program_id(0) == 0)
        def _():
            dw_ref[...] = jnp.zeros_like(dw_ref)
            dbias_ref[...] = jnp.zeros_like(dbias_ref)
            dng_ref[...] = jnp.zeros_like(dng_ref)
            dnb_ref[...] = jnp.zeros_like(dnb_ref)

        zu, zv, dy = zu_ref[...], zv_ref[...], dy_ref[...]
        u, xh, rstd, vn, mixed = _sgu_core(zu, zv, ng_ref[...], nb_ref[...], w_ref, bias_ref[...])
        dzu_ref[...] = (dy * mixed * _gelu_grad(zu)).astype(dzu_ref.dtype)
        dmix = dy * u
        low = lax.broadcasted_iota(jnp.int32, (SGU_CHUNK, 2 * SGU_HD), 1) < SGU_HD
        dbias = jnp.zeros((SGU_CHUNK, GW), F32)
        rows = []
        for n in range(tb // SGU_CHUNK):
            dm = dmix[n * SGU_CHUNK:(n + 1) * SGU_CHUNK, :]
            dbias = dbias + dm
            pairs = []
            for j in range(SGU_HEADS // 2):
                dmp = dm[:, j * 128:(j + 1) * 128]
                vp = vn[n * SGU_CHUNK:(n + 1) * SGU_CHUNK, j * 128:(j + 1) * 128]
                dw_ref[2 * j] += _dot(jnp.where(low, dmp, 0.0), vp, NT)
                dw_ref[2 * j + 1] += _dot(jnp.where(low, 0.0, dmp), vp, NT)
                pairs.append(jnp.where(low, _dot(wt_ref[2 * j], dmp), _dot(wt_ref[2 * j + 1], dmp)))
            rows.append(jnp.concatenate(pairs, axis=1))
        dbias_ref[...] += dbias
        dvn = jnp.concatenate(rows, axis=0)
        dng_ref[...] += jnp.sum(dvn * xh, axis=0, keepdims=True)
        dnb_ref[...] += jnp.sum(dvn, axis=0, keepdims=True)
        wv = dvn * ng_ref[...]
        dv0 = rstd * (wv - jnp.mean(wv, axis=-1, keepdims=True) - xh * jnp.mean(wv * xh, axis=-1, keepdims=True))
        dzv_ref[...] = (dv0 * _gelu_grad(zv)).astype(dzv_ref.dtype)

    vec = _full((1, GW))
    row = pl.BlockSpec((tb, GW), lambda i: (i, 0))
    mat = _full((8, 128, 128))
    return pl.pallas_call(
        body, name="sgu_bwd", grid=(t // tb,),
        in_specs=[_cols(tb, GW, cb), _cols(tb, GW, cbu), _cols(tb, GW, cbv), vec, vec, mat, mat, _full((128, GW))],
        out_specs=[row, row, mat, _full((128, GW)), vec, vec],
        out_shape=[jax.ShapeDtypeStruct((t, GW), MXU_DTYPE)] * 2 + [jax.ShapeDtypeStruct((8, 128, 128), F32),
                   jax.ShapeDtypeStruct((128, GW), F32), jax.ShapeDtypeStruct((1, GW), F32), jax.ShapeDtypeStruct((1, GW), F32)],
        compiler_params=_params(("arbitrary",)))(dmixed, proj, proj, ng, nb, w, wt, bias)


S5_KB = 4
SUB = 8


def _s5_discretize(lam_re, lam_im, log_step, b_re, b_im):
    step = jnp.exp(log_step)[:, None]
    mag = jnp.exp(lam_re * step)
    lr, li = mag * jnp.cos(lam_im * step), mag * jnp.sin(lam_im * step)
    den = lam_re * lam_re + lam_im * lam_im
    fr = ((lr - 1.0) * lam_re + li * lam_im) / den
    fi = (li * lam_re - (lr - 1.0) * lam_im) / den
    return lr, li, fr[:, :, None] * b_re - fi[:, :, None] * b_im, fr[:, :, None] * b_im + fi[:, :, None] * b_re


def _cpow(lr, li, n):
    rr, ri = lr, li
    for _ in range(n - 1):
        rr, ri = rr * lr - ri * li, rr * li + ri * lr
    return rr, ri


def _s5_scan_consts(lr, li, reverse):
    lr, li = lr.reshape(1, S5_N), (-li if reverse else li).reshape(1, S5_N)
    row = jnp.arange(SUB)[:, None]
    out = []
    for s in (1, 2, 4):
        pr, pi = _cpow(lr, li, s)
        keep = (row < SUB - s) if reverse else (row >= s)
        out += [jnp.where(keep, pr, 0.0), jnp.where(keep, pi, 0.0)]
    cr, ci = [], []
    for i in range(SUB):
        pr, pi = _cpow(lr, li, SUB - i if reverse else i + 1)
        cr.append(pr)
        ci.append(pi)
    out += [jnp.concatenate(cr, axis=0), jnp.concatenate(ci, axis=0)]
    return jnp.stack(out)


def _s5_blockdiag_in(b):
    bt = jnp.swapaxes(b, 1, 2).reshape(S5_KB, 8, S5_H, S5_P)
    eye = jnp.eye(8, dtype=b.dtype)
    return jnp.einsum("kghp,gj->kghjp", bt, eye).reshape(S5_KB, 128, 512)


def _s5_blockdiag_in_extract(bb):
    x = bb.reshape(S5_KB, 8, S5_H, 8, S5_P)
    d = jnp.einsum("kghgp->kghp", x).reshape(S5_G, S5_H, S5_P)
    return jnp.swapaxes(d, 1, 2)


def _s5_blockdiag_out(c):
    ct = jnp.swapaxes(c, 1, 2).reshape(S5_KB, 8, S5_P, S5_H)
    eye = jnp.eye(8, dtype=c.dtype)
    return jnp.einsum("kgph,gj->kgpjh", ct, eye).reshape(S5_KB, 512, 128)


def _s5_blockdiag_out_extract(cc):
    x = cc.reshape(S5_KB, 8, S5_P, 8, S5_H)
    d = jnp.einsum("kgpgh->kgph", x).reshape(S5_G, S5_P, S5_H)
    return jnp.swapaxes(d, 1, 2)


def _s5_tile_scan(a, b, c_ref, carry, reverse):
    for si, s in enumerate((1, 2, 4)):
        sh = SUB - s if reverse else s
        ar, br = pltpu.roll(a, sh, 0), pltpu.roll(b, sh, 0)
        mr, mi = c_ref[2 * si], c_ref[2 * si + 1]
        a, b = a + mr * ar - mi * br, b + mr * br + mi * ar
    pr, pi = c_ref[6], c_ref[7]
    cr, ci = carry
    return a + pr * cr - pi * ci, b + pr * ci + pi * cr


def _s5_readout(xre_ref, xim_ref, ccre_ref, ccim_ref):
    return jnp.concatenate(
        [_dot(xre_ref[:, pl.ds(512 * k, 512)], ccre_ref[k], exact=True) - _dot(xim_ref[:, pl.ds(512 * k, 512)], ccim_ref[k], exact=True)
         for k in range(S5_KB)], axis=1)


def _s5_fwd(proj, bbre, bbim, ccre, ccim, dvec, consts, glu_w, glu_b, *, cb, tb=256):
    t = proj.shape[0]
    tb = min(tb, t)
    nt = tb // SUB

    def body(u_ref, bbre_ref, bbim_ref, ccre_ref, ccim_ref, d_ref, c_ref, w_ref, b_ref, o_ref, xre_ref, xim_ref, car):
        @pl.when(pl.program_id(0) == 0)
        def _():
            car[...] = jnp.zeros_like(car)

        u = u_ref[...]
        for k in range(S5_KB):
            uk = u[:, 128 * k:128 * (k + 1)]
            xre_ref[:, pl.ds(512 * k, 512)] = _dot(uk, bbre_ref[k], exact=True)
            xim_ref[:, pl.ds(512 * k, 512)] = _dot(uk, bbim_ref[k], exact=True)

        def tile(r, carry):
            sl = pl.ds(pl.multiple_of(r * SUB, SUB), SUB)
            a, b = _s5_tile_scan(xre_ref[sl, :], xim_ref[sl, :], c_ref, carry, False)
            xre_ref[sl, :] = a
            xim_ref[sl, :] = b
            return a[SUB - 1:SUB, :], b[SUB - 1:SUB, :]

        cr, ci = lax.fori_loop(0, nt, tile, (car[0:1, :], car[1:2, :]))
        car[0:1, :] = cr
        car[1:2, :] = ci
        ys = _s5_readout(xre_ref, xim_ref, ccre_ref, ccim_ref) + d_ref[...] * u
        yg = _gelu(ys)
        o_ref[...] = (yg * _sigmoid(_dot(yg, w_ref[...]) + b_ref[...])).astype(o_ref.dtype)

    row = pl.BlockSpec((tb, GW), lambda i: (i, 0))
    xrow = pl.BlockSpec((tb, S5_N), lambda i: (i, 0))
    vec = _full((1, GW))
    return pl.pallas_call(
        body, name="s5_fwd", grid=(t // tb,),
        in_specs=[_cols(tb, GW, cb), _full((4, 128, 512)), _full((4, 128, 512)), _full((4, 512, 128)), _full((4, 512, 128)),
                  vec, _full((8, SUB, S5_N)), _full((GW, GW)), vec],
        out_specs=[row, xrow, xrow],
        out_shape=[jax.ShapeDtypeStruct((t, GW), MXU_DTYPE), jax.ShapeDtypeStruct((t, S5_N), F32), jax.ShapeDtypeStruct((t, S5_N), F32)],
        scratch_shapes=[pltpu.VMEM((SUB, S5_N), F32)], compiler_params=_params(("arbitrary",)))(
            proj, bbre, bbim, ccre, ccim, dvec, consts, glu_w, glu_b)


def _s5_bwd(dmixed, proj, xre, xim, bbre, bbim, ccre, ccim, dvec, consts, glu_w, glu_b, *, cb_dy, cb, tb=256):
    t = proj.shape[0]
    tb = min(tb, t)
    nb = t // tb
    nt = tb // SUB

    def body(dy_ref, u_ref, xre_ref, xim_ref, bbre_ref, bbim_ref, ccre_ref, ccim_ref, d_ref, c_ref, w_ref, b_ref,
             du_ref, dw_ref, db_ref, dd_ref, dccre_ref, dccim_ref, dbbre_ref, dbbim_ref, sre_ref, sim_ref, are, aim, car):
        @pl.when(pl.program_id(0) == 0)
        def _():
            car[...] = jnp.zeros_like(car)
            for r in (dw_ref, db_ref, dd_ref, dccre_ref, dccim_ref, dbbre_ref, dbbim_ref, sre_ref, sim_ref):
                r[...] = jnp.zeros_like(r)

        u, dy = u_ref[...], dy_ref[...]
        ys = _s5_readout(xre_ref, xim_ref, ccre_ref, ccim_ref) + d_ref[...] * u
        yg = _gelu(ys)
        sg = _sigmoid(_dot(yg, w_ref[...]) + b_ref[...])
        dz = dy * yg * sg * (1.0 - sg)
        dyg = dy * sg + _dot(dz, w_ref[...], NT)
        dw_ref[...] += _dot(yg, dz, TN)
        db_ref[...] += jnp.sum(dz, axis=0, keepdims=True)
        dys = dyg * _gelu_grad(ys)
        dd_ref[...] += jnp.sum(dys * u, axis=0, keepdims=True)
        for k in range(S5_KB):
            dk = dys[:, 128 * k:128 * (k + 1)]
            lanes = pl.ds(512 * k, 512)
            are[:, lanes] = _dot(dk, ccre_ref[k], NT, exact=True)
            aim[:, lanes] = -_dot(dk, ccim_ref[k], NT, exact=True)
            dccre_ref[k] += _dot(xre_ref[:, lanes], dk, TN, exact=True)
            dccim_ref[k] -= _dot(xim_ref[:, lanes], dk, TN, exact=True)

        def tile(j, carry):
            sl = pl.ds(pl.multiple_of((nt - 1 - j) * SUB, SUB), SUB)
            gr, gi = are[sl, :], aim[sl, :]
            a, b = _s5_tile_scan(gr, gi, c_ref, carry, True)
            are[sl, :] = a
            aim[sl, :] = b
            er, ei = a - gr, b - gi
            xr, xi = xre_ref[sl, :], xim_ref[sl, :]
            sre_ref[...] += xr * er + xi * ei
            sim_ref[...] += xr * ei - xi * er
            return a[0:1, :], b[0:1, :]

        cr, ci = lax.fori_loop(0, nt, tile, (car[0:1, :], car[1:2, :]))
        car[0:1, :] = cr
        car[1:2, :] = ci
        dus = []
        for k in range(S5_KB):
            uk = u[:, 128 * k:128 * (k + 1)]
            lanes = pl.ds(512 * k, 512)
            dbbre_ref[k] += _dot(uk, are[:, lanes], TN, exact=True)
            dbbim_ref[k] += _dot(uk, aim[:, lanes], TN, exact=True)
            dus.append(_dot(are[:, lanes], bbre_ref[k], NT, exact=True) + _dot(aim[:, lanes], bbim_ref[k], NT, exact=True))
        du_ref[...] = (d_ref[...] * dys + jnp.concatenate(dus, axis=1)).astype(du_ref.dtype)

    row = pl.BlockSpec((tb, GW), lambda i: (nb - 1 - i, 0))
    xrow = pl.BlockSpec((tb, S5_N), lambda i: (nb - 1 - i, 0))
    vec = _full((1, GW))
    bbs, ccs = _full((4, 128, 512)), _full((4, 512, 128))
    sds = jax.ShapeDtypeStruct
    return pl.pallas_call(
        body, name="s5_bwd", grid=(nb,),
        in_specs=[_cols_rev(tb, GW, cb_dy, nb), _cols_rev(tb, GW, cb, nb), xrow, xrow, bbs, bbs, ccs, ccs, vec,
                  _full((8, SUB, S5_N)), _full((GW, GW)), vec],
        out_specs=[row, _full((GW, GW)), vec, vec, ccs, ccs, bbs, bbs, _full((SUB, S5_N)), _full((SUB, S5_N))],
        out_shape=[sds((t, GW), MXU_DTYPE), sds((GW, GW), F32), sds((1, GW), F32), sds((1, GW), F32), sds((4, 512, 128), F32),
                   sds((4, 512, 128), F32), sds((4, 128, 512), F32), sds((4, 128, 512), F32), sds((SUB, S5_N), F32),
                   sds((SUB, S5_N), F32)],
        scratch_shapes=[pltpu.VMEM((tb, S5_N), F32), pltpu.VMEM((tb, S5_N), F32), pltpu.VMEM((SUB, S5_N), F32)],
        compiler_params=_params(("arbitrary",)))(dmixed, proj, xre, xim, bbre, bbim, ccre, ccim, dvec, consts, glu_w, glu_b)


def _s5_prepare(lam_re, lam_im, log_step, b_re, b_im, c_re, c_im):
    lr, li, bbr, bbi = _s5_discretize(lam_re, lam_im, log_step, b_re, b_im)
    return dict(bbre=_s5_blockdiag_in(bbr), bbim=_s5_blockdiag_in(bbi), ccre=_s5_blockdiag_out(c_re), ccim=_s5_blockdiag_out(c_im),
                cf=_s5_scan_consts(lr, li, False), cr=_s5_scan_consts(lr, li, True))


def _s5_param_grads(lam_re, lam_im, log_step, b_re, b_im, dbbre, dbbim, dccre, dccim, sre, sim):
    (lr, li, _, _), vjp = jax.vjp(_s5_discretize, lam_re, lam_im, log_step, b_re, b_im)
    sr, si = jnp.sum(sre, axis=0).reshape(S5_G, S5_P), jnp.sum(sim, axis=0).reshape(S5_G, S5_P)
    den = lr * lr + li * li
    glr, gli = (sr * lr - si * li) / den, (si * lr + sr * li) / den
    dlam_re, dlam_im, dlog_step, db_re, db_im = vjp((glr, gli, _s5_blockdiag_in_extract(dbbre), _s5_blockdiag_in_extract(dbbim)))
    return dlam_re, dlam_im, dlog_step, db_re, db_im, _s5_blockdiag_out_extract(dccre), _s5_blockdiag_out_extract(dccim)


HALO = 8
AB_CB = 4096 // 128
Q_SCALE = DN_HD ** -0.5


def _halo_prev(tb, width, cb):
    return pl.BlockSpec((HALO, width), lambda i: (jnp.maximum(i * (tb // HALO) - 1, 0), cb))


def _halo_next(tb, width, cb, nrows):
    last = nrows // HALO - 1
    return pl.BlockSpec((HALO, width), lambda i: (jnp.minimum((i + 1) * (tb // HALO), last), cb))


def _silu_parts(c):
    sg = _sigmoid(c)
    return c * sg, sg * (1.0 + c * (1.0 - sg))


def _softplus(x):
    return jnp.maximum(x, 0.0) + jnp.log(1.0 + jnp.exp(-jnp.abs(x)))


def _dn_conv(x_ref, halo_ref, w_ref, part, ext, first):
    tb = x_ref.shape[0]
    ext[pl.ds(0, HALO), :] = jnp.where(first, 0.0, halo_ref[...])
    ext[pl.ds(HALO, tb), :] = x_ref[...]
    c = None
    for j in range(DN_CONV):
        term = w_ref[pl.ds(j, 1), pl.ds(512 * part, 512)] * ext[pl.ds(HALO - (DN_CONV - 1) + j, tb), :]
        c = term if c is None else c + term
    return c


def _dn_gb(ab, alog, dtb):
    lane = lax.broadcasted_iota(jnp.int32, ab.shape, 1)
    pre = ab + dtb
    g = -jnp.exp(alog) * _softplus(pre)
    beta = _sigmoid(ab)
    return jnp.where(lane < DN_HEADS, g, jnp.where(lane < 2 * DN_HEADS, beta, 0.0)), pre, beta


def _dn_prep_fwd(proj, conv_w, alog, dtb, *, cbq, tb=512):
    t = proj.shape[0]
    tb = min(tb, t)

    def body(xq, xk, xv, hq, hk, hv, ab_ref, w_ref, alog_ref, dtb_ref, qn_ref, kn_ref, vs_ref, gb_ref, ext):
        first = pl.program_id(0) == 0
        for part, (x_ref, h_ref, o_ref) in enumerate(((xq, hq, qn_ref), (xk, hk, kn_ref), (xv, hv, vs_ref))):
            s, _ = _silu_parts(_dn_conv(x_ref, h_ref, w_ref, part, ext, first))
            if part < 2:
                scale = Q_SCALE if part == 0 else 1.0
                for h in range(DN_HEADS):
                    sh = s[:, DN_HD * h:DN_HD * (h + 1)]
                    rn = lax.rsqrt(jnp.sum(sh * sh, axis=-1, keepdims=True) + L2_EPS)
                    o_ref[:, pl.ds(DN_HD * h, DN_HD)] = sh * (rn * scale)
            else:
                o_ref[...] = s
        gb_ref[...] = _dn_gb(ab_ref[...], alog_ref[...], dtb_ref[...])[0]

    row = pl.BlockSpec((tb, GW), lambda i: (i, 0))
    small = pl.BlockSpec((tb, 128), lambda i: (i, 0))
    v128 = _full((1, 128))
    sds = jax.ShapeDtypeStruct
    return pl.pallas_call(
        body, name="dn_prep_fwd", grid=(t // tb,),
        in_specs=[_cols(tb, GW, cbq), _cols(tb, GW, cbq + 1), _cols(tb, GW, cbq + 2),
                  _halo_prev(tb, GW, cbq), _halo_prev(tb, GW, cbq + 1), _halo_prev(tb, GW, cbq + 2),
                  _cols(tb, 128, AB_CB), _full((DN_CONV, 3 * GW)), v128, v128],
        out_specs=[row, row, row, small],
        out_shape=[sds((t, GW), F32)] * 3 + [sds((t, 128), F32)],
        scratch_shapes=[pltpu.VMEM((tb + HALO, GW), F32)], compiler_params=_params(("parallel",)))(
            proj, proj, proj, proj, proj, proj, proj, conv_w, alog, dtb)


def _dn_prep_bwd_a(proj, conv_w, alog, dtb, dqn, dkn, dvs, dgb, *, cbq, tb=512):
    t = proj.shape[0]
    tb = min(tb, t)

    def body(xq, xk, xv, hq, hk, hv, ab_ref, w_ref, alog_ref, dtb_ref, dqn_ref, dkn_ref, dvs_ref, dgb_ref,
             dcq_ref, dck_ref, dcv_ref, dab_ref, dalog_ref, ddtb_ref, ext):
        first = pl.program_id(0) == 0

        @pl.when(first)
        def _():
            dalog_ref[...] = jnp.zeros_like(dalog_ref)
            ddtb_ref[...] = jnp.zeros_like(ddtb_ref)

        for part, (x_ref, h_ref, d_ref, o_ref) in enumerate(((xq, hq, dqn_ref, dcq_ref), (xk, hk, dkn_ref, dck_ref), (xv, hv, dvs_ref, dcv_ref))):
            s, ds_dc = _silu_parts(_dn_conv(x_ref, h_ref, w_ref, part, ext, first))
            d = d_ref[...]
            if part < 2:
                scale = Q_SCALE if part == 0 else 1.0
                for h in range(DN_HEADS):
                    lanes = slice(DN_HD * h, DN_HD * (h + 1))
                    sh, dh = s[:, lanes], d[:, lanes]
                    rn = lax.rsqrt(jnp.sum(sh * sh, axis=-1, keepdims=True) + L2_EPS)
                    dsh = scale * (rn * dh - sh * (rn * rn * rn) * jnp.sum(dh * sh, axis=-1, keepdims=True))
                    o_ref[:, pl.ds(DN_HD * h, DN_HD)] = dsh * ds_dc[:, lanes]
            else:
                o_ref[...] = d * ds_dc
        ab, dgb_v = ab_ref[...], dgb_ref[...]
        gb, pre, beta = _dn_gb(ab, alog_ref[...], dtb_ref[...])
        lane = lax.broadcasted_iota(jnp.int32, ab.shape, 1)
        is_g = lane < DN_HEADS
        da = jnp.where(is_g, dgb_v * (-jnp.exp(alog_ref[...])) * _sigmoid(pre), 0.0)
        db = jnp.where((lane >= DN_HEADS) & (lane < 2 * DN_HEADS), dgb_v * beta * (1.0 - beta), 0.0)
        dab_ref[...] = (da + db).astype(dab_ref.dtype)
        ddtb_ref[...] += jnp.sum(da, axis=0, keepdims=True)
        dalog_ref[...] += jnp.sum(jnp.where(is_g, dgb_v * gb, 0.0), axis=0, keepdims=True)

    row = pl.BlockSpec((tb, GW), lambda i: (i, 0))
    small = pl.BlockSpec((tb, 128), lambda i: (i, 0))
    v128 = _full((1, 128))
    sds = jax.ShapeDtypeStruct
    return pl.pallas_call(
        body, name="dn_prep_bwd_a", grid=(t // tb,),
        in_specs=[_cols(tb, GW, cbq), _cols(tb, GW, cbq + 1), _cols(tb, GW, cbq + 2),
                  _halo_prev(tb, GW, cbq), _halo_prev(tb, GW, cbq + 1), _halo_prev(tb, GW, cbq + 2),
                  _cols(tb, 128, AB_CB), _full((DN_CONV, 3 * GW)), v128, v128, row, row, row, small],
        out_specs=[row, row, row, small, v128, v128],
        out_shape=[sds((t, GW), F32)] * 3 + [sds((t, 128), MXU_DTYPE), sds((1, 128), F32), sds((1, 128), F32)],
        scratch_shapes=[pltpu.VMEM((tb + HALO, GW), F32)], compiler_params=_params(("arbitrary",)))(
            proj, proj, proj, proj, proj, proj, proj, conv_w, alog, dtb, dqn, dkn, dvs, dgb)


def _dn_prep_bwd_b(proj, conv_w, dcq, dck, dcv, *, cbq, tb=512):
    t = proj.shape[0]
    tb = min(tb, t)
    nb = t // tb

    def body(xq, xk, xv, hq, hk, hv, dq_in, dk_in, dv_in, nq, nk, nv, w_ref, dq_ref, dk_ref, dv_ref, dw_ref, ext):
        i = pl.program_id(0)
        first, last = i == 0, i == nb - 1

        @pl.when(first)
        def _():
            dw_ref[...] = jnp.zeros_like(dw_ref)

        for part, (x_ref, h_ref, d_ref, n_ref, o_ref) in enumerate(
                ((xq, hq, dq_in, nq, dq_ref), (xk, hk, dk_in, nk, dk_ref), (xv, hv, dv_in, nv, dv_ref))):
            lanes = pl.ds(512 * part, 512)
            d = d_ref[...]
            ext[pl.ds(0, HALO), :] = jnp.where(first, 0.0, h_ref[...])
            ext[pl.ds(HALO, tb), :] = x_ref[...]
            for j in range(DN_CONV):
                xs = ext[pl.ds(HALO - (DN_CONV - 1) + j, tb), :]
                dw_ref[pl.ds(j, 1), lanes] += jnp.sum(d * xs, axis=0, keepdims=True)
            ext[pl.ds(0, tb), :] = d
            ext[pl.ds(tb, HALO), :] = jnp.where(last, 0.0, n_ref[...])
            acc = None
            for j in range(DN_CONV):
                term = w_ref[pl.ds(j, 1), lanes] * ext[pl.ds(DN_CONV - 1 - j, tb), :]
                acc = term if acc is None else acc + term
            o_ref[...] = acc.astype(o_ref.dtype)

    row = pl.BlockSpec((tb, GW), lambda i: (i, 0))
    nxt = _halo_next(tb, GW, 0, t)
    sds = jax.ShapeDtypeStruct
    return pl.pallas_call(
        body, name="dn_prep_bwd_b", grid=(nb,),
        in_specs=[_cols(tb, GW, cbq), _cols(tb, GW, cbq + 1), _cols(tb, GW, cbq + 2),
                  _halo_prev(tb, GW, cbq), _halo_prev(tb, GW, cbq + 1), _halo_prev(tb, GW, cbq + 2),
                  row, row, row, nxt, nxt, nxt, _full((DN_CONV, 3 * GW))],
        out_specs=[row, row, row, _full((DN_CONV, 3 * GW))],
        out_shape=[sds((t, GW), MXU_DTYPE)] * 3 + [sds((DN_CONV, 3 * GW), F32)],
        scratch_shapes=[pltpu.VMEM((tb + HALO, GW), F32)], compiler_params=_params(("arbitrary",)))(
            proj, proj, proj, proj, proj, proj, dcq, dck, dcv, dcq, dck, dcv, conv_w)


CAT = DN_HEADS * DN_CHUNK
DN_LOCAL_CHUNKS = 2


def _iota_div(shape, axis, width):
    return jnp.right_shift(lax.broadcasted_iota(jnp.int32, shape, axis), width.bit_length() - 1)


def _dn_masks():
    r = lax.broadcasted_iota(jnp.int32, (DN_CHUNK, CAT), 0)
    c = jnp.bitwise_and(lax.broadcasted_iota(jnp.int32, (DN_CHUNK, CAT), 1), DN_CHUNK - 1)
    wide = _iota_div((CAT, GW), 0, DN_CHUNK) == _iota_div((CAT, GW), 1, DN_HD)
    square = _iota_div((CAT, CAT), 0, DN_CHUNK) == _iota_div((CAT, CAT), 1, DN_CHUNK)
    return dict(eye=r == c, tril=r >= c, strict=r > c, triu=r <= c, wide=wide, square=square)


def _stack4(x):
    return jnp.concatenate([x, x, x, x], axis=0)


def _diag_blocks(x, mask):
    return jnp.where(mask, _stack4(x), 0.0)


def _fold_blocks(x, mask):
    x = jnp.where(mask, x, 0.0)
    return x[0:64] + x[64:128] + x[128:192] + x[192:256]


def _expand(cols, base, width):
    head = _iota_div((cols.shape[0], DN_HEADS * width), 1, width)
    out = jnp.zeros((cols.shape[0], DN_HEADS * width), F32)
    for h in range(DN_HEADS):
        out = jnp.where(head == h, cols[:, base + h:base + h + 1], out)
    return out


def _head_sums(x, width):
    if width == DN_HD:
        return [jnp.sum(x[:, DN_HD * h:DN_HD * (h + 1)], axis=1, keepdims=True) for h in range(DN_HEADS)]
    head = _iota_div(x.shape, 1, width)
    return [jnp.sum(jnp.where(head == h, x, 0.0), axis=1, keepdims=True) for h in range(DN_HEADS)]


def _cumsum_rows(x):
    row = lax.broadcasted_iota(jnp.int32, x.shape, 0)
    for s in (1, 2, 4, 8, 16, 32):
        x = x + jnp.where(row >= s, pltpu.roll(x, s, 0), 0.0)
    return x


def _tri_inv(n, square):
    col = jnp.bitwise_and(lax.broadcasted_iota(jnp.int32, n.shape, 1), DN_CHUNK - 1)
    x = jnp.where(lax.broadcasted_iota(jnp.int32, n.shape, 0) == col, 1.0, 0.0) - n
    p = _dot(n, _diag_blocks(n, square), exact=True)
    for _ in range(4):
        pd = _diag_blocks(p, square)
        x = x + _dot(x, pd, exact=True)
        p = _dot(p, pd, exact=True)
    return x + _dot(x, _diag_blocks(p, square), exact=True)


def _dn_local_math(q, k, v, gbv, m, tm=None):
    gc = _cumsum_rows(gbv)
    gc_cat, gc_wide = _expand(gc, 0, DN_CHUNK), _expand(gc, 0, DN_HD)
    gc_row = jnp.sum(jnp.where(m["eye"], gc_cat, 0.0), axis=0, keepdims=True)
    decay = jnp.where(m["tril"], jnp.exp(jnp.minimum(gc_cat - gc_row, 0.0)), 0.0)
    eg = jnp.exp(gc_wide)
    gl = _expand(gc[DN_CHUNK - 1:DN_CHUNK, :], 0, DN_HD)
    etail = jnp.exp(gl - gc_wide)
    beta = _expand(gbv, DN_HEADS, DN_HD)
    kb, vb = k * beta, v * beta
    k_rows = _diag_blocks(k, m["wide"])
    kk = _dot(kb, k_rows, NT)
    if tm is None:
        tm = _tri_inv(jnp.where(m["strict"], kk * decay, 0.0), m["square"])
    return dict(decay=decay, eg=eg, eg_last=jnp.exp(gl), etail=etail, beta=beta, kb=kb, vb=vb, kk=kk, tm=tm, kbg=kb * eg,
                qk=_dot(q, k_rows, NT), qg=q * eg, ktail=k * etail, k_rows=k_rows)


def _dn_local_fwd(qn, kn, vs, gb):
    t = qn.shape[0]
    rows = DN_CHUNK * DN_LOCAL_CHUNKS

    def body(q_ref, k_ref, v_ref, gb_ref, u_ref, wm_ref, qg_ref, kt_ref, qkd_ref, tm_ref):
        masks = _dn_masks()
        for n in range(DN_LOCAL_CHUNKS):
            rs = pl.ds(DN_CHUNK * n, DN_CHUNK)
            m = _dn_local_math(q_ref[rs, :], k_ref[rs, :], v_ref[rs, :], gb_ref[rs, :], masks)
            u_ref[rs, :] = _dot(m["tm"], _diag_blocks(m["vb"], masks["wide"]))
            wm_ref[rs, :] = _dot(m["tm"], _diag_blocks(m["kbg"], masks["wide"])).astype(wm_ref.dtype)
            qg_ref[rs, :] = m["qg"].astype(qg_ref.dtype)
            kt_ref[rs, :] = m["ktail"].astype(kt_ref.dtype)
            qkd = (m["qk"] * m["decay"]).astype(qkd_ref.dtype)
            for h in range(DN_HEADS):
                qkd_ref[rs, pl.ds(DN_HD * h, DN_CHUNK)] = qkd[:, DN_CHUNK * h:DN_CHUNK * (h + 1)]
            tm_ref[rs, :] = m["tm"]

    row = pl.BlockSpec((rows, GW), lambda i: (i, 0))
    sds = jax.ShapeDtypeStruct
    return pl.pallas_call(
        body, name="dn_local_fwd", grid=(t // rows,), in_specs=[row, row, row, pl.BlockSpec((rows, 128), lambda i: (i, 0))],
        out_specs=[row] * 5 + [pl.BlockSpec((rows, CAT), lambda i: (i, 0))],
        out_shape=[sds((t, GW), F32)] + [sds((t, GW), MXU_DTYPE)] * 4 + [sds((t, CAT), F32)],
        compiler_params=_params(("parallel",)))(qn, kn, vs, gb)


def _dn_eg_last(gbv, h):
    return jnp.exp(jnp.sum(gbv[:, h:h + 1], axis=0, keepdims=True))


def _dn_seq_fwd(u, wm, qg, ktail, qkd, gb, proj, norm_g, *, cb_gate):
    t = u.shape[0]
    nc = t // DN_CHUNK

    def body(u_ref, wm_ref, qg_ref, kt_ref, qkd_ref, gb_ref, gate_ref, ng_ref, o_ref, raw_ref, vn_ref, st_ref, s_ref):
        @pl.when(pl.program_id(0) == 0)
        def _():
            s_ref[...] = jnp.zeros_like(s_ref)

        gbv = gb_ref[...]
        for h in range(DN_HEADS):
            lanes = pl.ds(DN_HD * h, DN_HD)
            s = s_ref[h]
            st_ref[0, h] = s
            v_new = u_ref[:, lanes] - _dot(wm_ref[:, lanes], s)
            s_ref[h] = s * _dn_eg_last(gbv, h) + _dot(kt_ref[:, lanes], v_new, TN)
            o = _dot(qg_ref[:, lanes], s) + _dot(qkd_ref[:, pl.ds(DN_HD * h, DN_CHUNK)], v_new)
            vn_ref[:, lanes] = v_new.astype(vn_ref.dtype)
            raw_ref[:, lanes] = o
            r = lax.rsqrt(jnp.mean(o * o, axis=-1, keepdims=True) + RMS_EPS)
            gt = gate_ref[:, lanes]
            o_ref[:, lanes] = (o * r * ng_ref[...] * (gt * _sigmoid(gt))).astype(o_ref.dtype)

    row = pl.BlockSpec((DN_CHUNK, GW), lambda i: (i, 0))
    sds = jax.ShapeDtypeStruct
    return pl.pallas_call(
        body, name="dn_seq_fwd", grid=(nc,),
        in_specs=[row] * 5 + [pl.BlockSpec((DN_CHUNK, 128), lambda i: (i, 0)), _cols(DN_CHUNK, GW, cb_gate), _full((1, DN_HD))],
        out_specs=[row, row, row, pl.BlockSpec((1, DN_HEADS, DN_HD, DN_HD), lambda i: (i, 0, 0, 0))],
        out_shape=[sds((t, GW), MXU_DTYPE), sds((t, GW), F32), sds((t, GW), MXU_DTYPE), sds((nc, DN_HEADS, DN_HD, DN_HD), F32)],
        scratch_shapes=[pltpu.VMEM((DN_HEADS, DN_HD, DN_HD), F32)], compiler_params=_params(("arbitrary",)))(
            u, wm, qg, ktail, qkd, gb, proj, norm_g)


def _dn_seq_bwd(dmixed, raw, proj, norm_g, gb, wm, qg, ktail, qkd, v_new, states, *, cb_dy, cb_gate):
    t = raw.shape[0]
    nc = t // DN_CHUNK

    def body(dy_ref, raw_ref, gate_ref, ng_ref, gb_ref, wm_ref, qg_ref, kt_ref, qkd_ref, vn_ref, st_ref,
             dgate_ref, dng_ref, do_ref, dvn_ref, dwm_ref, dqg_ref, dkt_ref, degl_ref, ds_ref):
        @pl.when(pl.program_id(0) == 0)
        def _():
            ds_ref[...] = jnp.zeros_like(ds_ref)
            dng_ref[...] = jnp.zeros_like(dng_ref)

        gbv = gb_ref[...]
        lane = lax.broadcasted_iota(jnp.int32, (1, 128), 1)
        degl = jnp.zeros((1, 128), F32)
        for h in range(DN_HEADS):
            lanes = pl.ds(DN_HD * h, DN_HD)
            s, ds_out = st_ref[0, h], ds_ref[h]
            o, gt, dy, ng = raw_ref[:, lanes], gate_ref[:, lanes], dy_ref[:, lanes], ng_ref[...]
            r = lax.rsqrt(jnp.mean(o * o, axis=-1, keepdims=True) + RMS_EPS)
            sil, dsil = _silu_parts(gt)
            d_on = dy * sil
            dgate_ref[:, lanes] = (dy * (o * r * ng) * dsil).astype(dgate_ref.dtype)
            dng_ref[...] += jnp.sum(d_on * o * r, axis=0, keepdims=True)
            w = d_on * ng
            do = r * w - o * (r * r * r) * jnp.mean(w * o, axis=-1, keepdims=True)
            d_vnew = _dot(qkd_ref[:, pl.ds(DN_HD * h, DN_CHUNK)], do, TN) + _dot(kt_ref[:, lanes], ds_out)
            ds_ref[h] = _dot(qg_ref[:, lanes], do, TN) + _dn_eg_last(gbv, h) * ds_out - _dot(wm_ref[:, lanes], d_vnew, TN)
            do_ref[:, lanes] = do.astype(do_ref.dtype)
            dvn_ref[:, lanes] = d_vnew.astype(dvn_ref.dtype)
            dwm_ref[:, lanes] = (-_dot(d_vnew, s, NT)).astype(dwm_ref.dtype)
            dqg_ref[:, lanes] = _dot(do, s, NT)
            dkt_ref[:, lanes] = _dot(vn_ref[:, lanes], ds_out, NT)
            d_eglast = jnp.sum(jnp.sum(s * ds_out, axis=1, keepdims=True), axis=0, keepdims=True)
            degl = degl + jnp.where(lane == h, d_eglast, 0.0)
        degl_ref[0] = degl

    row = pl.BlockSpec((DN_CHUNK, GW), lambda i: (nc - 1 - i, 0))
    small = pl.BlockSpec((DN_CHUNK, 128), lambda i: (nc - 1 - i, 0))
    sds = jax.ShapeDtypeStruct
    return pl.pallas_call(
        body, name="dn_seq_bwd", grid=(nc,),
        in_specs=[_cols_rev(DN_CHUNK, GW, cb_dy, nc), row, _cols_rev(DN_CHUNK, GW, cb_gate, nc), _full((1, DN_HD)), small,
                  row, row, row, row, row, pl.BlockSpec((1, DN_HEADS, DN_HD, DN_HD), lambda i: (nc - 1 - i, 0, 0, 0))],
        out_specs=[row, _full((1, DN_HD)), row, row, row, row, row, pl.BlockSpec((1, 1, 128), lambda i: (nc - 1 - i, 0, 0))],
        out_shape=[sds((t, GW), MXU_DTYPE), sds((1, DN_HD), F32), sds((t, GW), MXU_DTYPE), sds((t, GW), MXU_DTYPE),
                   sds((t, GW), MXU_DTYPE), sds((t, GW), F32), sds((t, GW), F32), sds((nc, 1, 128), F32)],
        scratch_shapes=[pltpu.VMEM((DN_HEADS, DN_HD, DN_HD), F32)], compiler_params=_params(("arbitrary",)))(
            dmixed, raw, proj, norm_g, gb, wm, qg, ktail, qkd, v_new, states)


def _dn_local_bwd(qn, kn, vs, gb, tm, v_new, do, d_vnew, d_wm, d_qg, d_ktail, d_eglast):
    t = qn.shape[0]
    rows = DN_CHUNK * DN_LOCAL_CHUNKS

    def body(q_ref, k_ref, v_ref, gb_ref, tm_ref, vn_ref, do_ref, dvn_ref, dwm_ref, dqg_ref, dkt_ref, degl_ref,
             dq_ref, dk_ref, dv_ref, dgb_ref):
        masks = _dn_masks()
        wide, square = masks["wide"], masks["square"]
        lane = lax.broadcasted_iota(jnp.int32, (DN_CHUNK, 128), 1)
        last_row = lax.broadcasted_iota(jnp.int32, (DN_CHUNK, 1), 0) == DN_CHUNK - 1
        for n in range(DN_LOCAL_CHUNKS):
            rs = pl.ds(DN_CHUNK * n, DN_CHUNK)
            q, k, v, tm = q_ref[rs, :], k_ref[rs, :], v_ref[rs, :], tm_ref[rs, :]
            m = _dn_local_math(q, k, v, gb_ref[rs, :], masks, tm=tm)
            decay, eg, k_rows = m["decay"], m["eg"], m["k_rows"]
            d_vnew, d_wm, d_qg, d_ktail = dvn_ref[rs, :], dwm_ref[rs, :], dqg_ref[rs, :], dkt_ref[rs, :]
            deglv = degl_ref[n]
            dq = d_qg * eg
            dk = d_ktail * m["etail"]
            tails = _head_sums(d_ktail * m["ktail"], DN_HD)
            dgcs = _head_sums(d_qg * m["qg"], DN_HD)
            d_qkd = jnp.where(masks["tril"], _dot(do_ref[rs, :], _diag_blocks(vn_ref[rs, :], wide), NT), 0.0)
            dqk_dec = d_qkd * decay
            dq = dq + _dot(dqk_dec, k_rows)
            dk = dk + _fold_blocks(_dot(dqk_dec, q, TN), wide)
            ddecay = d_qkd * m["qk"]
            d_tm = _dot(d_vnew, _diag_blocks(m["vb"], wide), NT) + _dot(d_wm, _diag_blocks(m["kbg"], wide), NT)
            d_vb = _fold_blocks(_dot(tm, d_vnew, TN), wide)
            d_kbg = _fold_blocks(_dot(tm, d_wm, TN), wide)
            d_kb = d_kbg * eg
            kbgs = _head_sums(d_kbg * m["kbg"], DN_HD)
            x = _fold_blocks(_dot(tm, d_tm, TN, exact=True), square)
            d_n = jnp.where(masks["strict"], -_dot(x, _diag_blocks(tm, square), NT, exact=True), 0.0)
            d_kk = d_n * decay
            d_kb = d_kb + _dot(d_kk, k_rows)
            dk = dk + _fold_blocks(_dot(d_kk, m["kb"], TN), wide)
            ddecay = ddecay + d_n * m["kk"]
            dk = dk + d_kb * m["beta"]
            dbetas = [a + b for a, b in zip(_head_sums(d_kb * k, DN_HD), _head_sums(d_vb * v, DN_HD))]
            dv_ref[rs, :] = d_vb * m["beta"]
            dq_ref[rs, :] = dq
            dk_ref[rs, :] = dk
            dd = ddecay * decay
            row_sums = _head_sums(dd, DN_CHUNK)
            dgc_cols = jnp.zeros((DN_CHUNK, 128), F32)
            for h in range(DN_HEADS):
                dgl = jnp.sum(tails[h], axis=0, keepdims=True) + deglv[:, h:h + 1] * m["eg_last"][:, DN_HD * h:DN_HD * h + 1]
                dgc_cols = jnp.where(lane == h, dgcs[h] - tails[h] + kbgs[h] + row_sums[h] + jnp.where(last_row, dgl, 0.0), dgc_cols)
            dgc_row = (jnp.sum(jnp.where(masks["eye"], _expand(dgc_cols, 0, DN_CHUNK), 0.0), axis=0, keepdims=True)
                       - jnp.sum(dd, axis=0, keepdims=True))
            dgs = _head_sums(jnp.where(masks["triu"], dgc_row, 0.0), DN_CHUNK)
            dgb = jnp.zeros((DN_CHUNK, 128), F32)
            for h in range(DN_HEADS):
                dgb = jnp.where(lane == h, dgs[h], jnp.where(lane == DN_HEADS + h, dbetas[h], dgb))
            dgb_ref[rs, :] = dgb

    row = pl.BlockSpec((rows, GW), lambda i: (i, 0))
    small = pl.BlockSpec((rows, 128), lambda i: (i, 0))
    sds = jax.ShapeDtypeStruct
    return pl.pallas_call(
        body, name="dn_local_bwd", grid=(t // rows,),
        in_specs=[row, row, row, small, pl.BlockSpec((rows, CAT), lambda i: (i, 0))] + [row] * 6
        + [pl.BlockSpec((DN_LOCAL_CHUNKS, 1, 128), lambda i: (i, 0, 0))],
        out_specs=[row, row, row, small], out_shape=[sds((t, GW), F32)] * 3 + [sds((t, 128), F32)],
        compiler_params=_params(("parallel",)))(qn, kn, vs, gb, tm, v_new, do, d_vnew, d_wm, d_qg, d_ktail, d_eglast)


ANY = pl.BlockSpec(memory_space=pl.ANY)
PAIR_SPLIT = 4


def _place():
    x, y, c = lax.axis_index("x"), lax.axis_index("y"), lax.axis_index("c")
    chips = [(1 - x, y), (x, 1 - y), (1 - x, 1 - y)]
    return x, y, c, chips


def _remote(src, dst, send_sem, recv_sem, to):
    return pltpu.make_async_remote_copy(src_ref=src, dst_ref=dst, send_sem=send_sem, recv_sem=recv_sem, device_id=to,
                                        device_id_type=MESH)


def _carry_allgather(arrs):
    n = len(arrs)

    def sends(ins, outs, send_sems, recv_sems):
        x, y, c, chips = _place()
        me = 2 * x + y
        out = []
        for a in range(n):
            half = ins[a].shape[0] // 2
            mine = pl.ds(c * half, half)
            out += [_remote(ins[a].at[mine], outs[a].at[me, mine], send_sems.at[6 * a + k], recv_sems.at[6 * a + k], (*chip, c))
                    for k, chip in enumerate(chips)]
        return out

    def start(*parts):
        for s in sends(*parts):
            s.start()

    def finish(ins, outs, send_sems, recv_sems):
        x, y, c, chips = _place()
        sibling = (x, y, 1 - c)
        fwds = []
        for a in range(n):
            half = ins[a].shape[0] // 2
            mine = pl.ds(c * half, half)
            for k, (cx, cy) in enumerate(chips):
                got = outs[a].at[2 * cx + cy, mine]
                _remote(got, got, send_sems.at[6 * a + k], recv_sems.at[6 * a + k], (cx, cy, c)).wait_recv()
                f = _remote(got, got, send_sems.at[6 * a + 3 + k], recv_sems.at[6 * a + 3 + k], sibling)
                f.start()
                fwds.append(f)
        for a in range(n):
            half = ins[a].shape[0] // 2
            other = pl.ds((1 - c) * half, half)
            for k, (cx, cy) in enumerate(chips):
                got = outs[a].at[2 * cx + cy, other]
                _remote(got, got, send_sems.at[6 * a + 3 + k], recv_sems.at[6 * a + 3 + k], sibling).wait_recv()
        for s in sends(ins, outs, send_sems, recv_sems) + fwds:
            s.wait_send()

    return _Carry(arrs, [jax.ShapeDtypeStruct((4,) + a.shape, a.dtype) for a in arrs], 6 * n, start, finish)


def _pair_exchange(gbs, *, name):
    n = len(gbs)

    def body(*refs):
        ins, got_refs = refs[:n], refs[n:2 * n]
        send_sems, recv_sems = refs[2 * n:]
        x, y, c, _ = _place()
        work = []
        for a in range(n):
            half = ins[a].shape[1] // 2
            piece = half // PAIR_SPLIT
            for r in range(PAIR_SPLIT):
                s = _remote(ins[a].at[:, pl.ds((1 - c) * half + r * piece, piece)], got_refs[a].at[:, pl.ds(r * piece, piece)],
                            send_sems.at[a, r], recv_sems.at[a, r], (x, y, 1 - c))
                s.start()
                work.append(s)
        for s in work:
            s.wait()

    return pl.pallas_call(
        body, name=name, in_specs=[ANY] * n, out_specs=[ANY] * n,
        out_shape=[jax.ShapeDtypeStruct((4, g.shape[1] // 2, g.shape[2]), g.dtype) for g in gbs],
        scratch_shapes=[pltpu.SemaphoreType.DMA((n, PAIR_SPLIT)), pltpu.SemaphoreType.DMA((n, PAIR_SPLIT))])(*gbs)


def _carry_chip_exchange(ps):
    n = len(ps)

    def copies(ins, outs, send_sems, recv_sems):
        x, y, c, chips = _place()
        return [_remote(ins[a].at[2 * cx + cy], outs[a].at[k], send_sems.at[3 * a + k], recv_sems.at[3 * a + k], (cx, cy, c))
                for a in range(n) for k, (cx, cy) in enumerate(chips)]

    def start(*parts):
        for s in copies(*parts):
            s.start()

    def finish(*parts):
        for s in copies(*parts):
            s.wait()

    return _Carry(ps, [jax.ShapeDtypeStruct((3,) + p.shape[1:], p.dtype) for p in ps], 3 * n, start, finish)


def _pair_join(bufs, *, name):
    n = len(bufs)

    def body(*refs):
        outs = refs[n:2 * n]
        send_sems, recv_sems = refs[2 * n:]
        x, y, c, _ = _place()
        work = []
        for a in range(n):
            s = _remote(outs[a].at[c], outs[a].at[c], send_sems.at[a], recv_sems.at[a], (x, y, 1 - c))
            s.start()
            work.append(s)
        for s in work:
            s.wait()

    return pl.pallas_call(
        body, name=name, in_specs=[ANY] * n, out_specs=[ANY] * n,
        out_shape=[jax.ShapeDtypeStruct(b.shape, b.dtype) for b in bufs], input_output_aliases={a: a for a in range(n)},
        scratch_shapes=[pltpu.SemaphoreType.DMA((n,)), pltpu.SemaphoreType.DMA((n,))])(*bufs)


def _pair_sum(gb, got, place, *, name, block_bytes=1 << 20):
    _, r, cols = gb.shape
    half = r // 2
    tr = _row_tile(half, cols, block_bytes)

    def body(place_ref, g_ref, got_ref, o_ref):
        o_ref[...] = g_ref[...] + got_ref[...]

    blk = pl.BlockSpec((None, tr, cols), lambda j, i, p: (j, i, 0))
    grid_spec = pltpu.PrefetchScalarGridSpec(
        num_scalar_prefetch=1, grid=(4, half // tr),
        in_specs=[pl.BlockSpec((None, None, tr, cols), lambda j, i, p: (j, p[0], i, 0)), blk], out_specs=blk)
    return pl.pallas_call(body, name=name, grid_spec=grid_spec, out_shape=jax.ShapeDtypeStruct((4, half, cols), F32),
                          compiler_params=_params(("parallel", "parallel")))(place, gb.reshape(4, 2, half, cols), got)


def _chip_sum(p, got, place, *, name, block_bytes=1 << 20):
    _, h, cols = p.shape
    tr = _row_tile(h, cols, block_bytes)

    def body(place_ref, p_ref, g0, g1, g2, o_ref):
        o_ref[...] = p_ref[...] + g0[...] + g1[...] + g2[...]

    def got_spec(k):
        return pl.BlockSpec((None, tr, cols), functools.partial(lambda i, pr, k: (k, i, 0), k=k))

    grid_spec = pltpu.PrefetchScalarGridSpec(
        num_scalar_prefetch=1, grid=(h // tr,),
        in_specs=[pl.BlockSpec((None, tr, cols), lambda i, pr: (pr[1], i, 0)), got_spec(0), got_spec(1), got_spec(2)],
        out_specs=pl.BlockSpec((None, tr, cols), lambda i, pr: (pr[0], i, 0)))
    return pl.pallas_call(body, name=name, grid_spec=grid_spec, out_shape=jax.ShapeDtypeStruct((2, h, cols), F32),
                          compiler_params=_params(("parallel",)))(place, p, got, got, got)


def _allgather_all(v):
    def body(v_ref, out_ref, send_sems, recv_sems, local_sem):
        x, y, c, chips = _place()
        me, sibling = (x, y, c), (x, y, 1 - c)

        def rows(px, py, pc):
            return out_ref.at[4 * px + 2 * py + pc]

        def copy(k, block, to, src=None):
            return _remote(rows(*block) if src is None else src, rows(*block), send_sems.at[k], recv_sems.at[k], to)

        mine = pltpu.make_async_copy(v_ref, rows(*me), local_sem)
        mine.start()
        first = [copy(0, me, sibling, src=v_ref)] + [copy(1 + j, me, (*chip, c), src=v_ref) for j, chip in enumerate(chips)]
        for cp in first:
            cp.start()
        passed = [copy(4 + j, (*chip, c), sibling) for j, chip in enumerate(chips)]
        for j, chip in enumerate(chips):
            copy(1 + j, (*chip, c), me).wait_recv()
            passed[j].start()
        copy(0, sibling, me).wait_recv()
        for j, chip in enumerate(chips):
            copy(4 + j, (*chip, 1 - c), me).wait_recv()
        for cp in first + passed:
            cp.wait_send()
        mine.wait()

    return pl.pallas_call(
        body, name="allgather_small", in_specs=[ANY], out_specs=ANY, out_shape=jax.ShapeDtypeStruct((8,) + v.shape, v.dtype),
        scratch_shapes=[pltpu.SemaphoreType.DMA((7,)), pltpu.SemaphoreType.DMA((7,)), pltpu.SemaphoreType.DMA],
        )(v)


def _row_tile(rows, cols, limit_bytes):
    for d in range(1, rows + 1):
        if rows % d == 0 and (rows // d) % 8 == 0 and (rows // d) * cols * 4 <= limit_bytes:
            return rows // d
    return rows


def _sum_kernel(parts, *, name, block_bytes=1 << 20):
    n = len(parts)
    rows, cols = parts[0][0].shape[1:] if isinstance(parts[0], tuple) else parts[0].shape
    tr = _row_tile(rows, cols, block_bytes)
    ins, specs = [], []
    for part in parts:
        if isinstance(part, tuple):
            ins.append(part[0])
            specs.append(pl.BlockSpec((None, tr, cols), functools.partial(lambda i, s: (s, i, 0), s=part[1])))
        else:
            ins.append(part)
            specs.append(pl.BlockSpec((tr, cols), lambda i: (i, 0)))

    def body(*refs):
        acc = refs[0][...]
        for r in refs[1:n]:
            acc = acc + r[...]
        refs[n][...] = acc

    return pl.pallas_call(body, name=name, grid=(rows // tr,), in_specs=specs, out_specs=pl.BlockSpec((tr, cols), lambda i: (i, 0)),
                          out_shape=jax.ShapeDtypeStruct((rows, cols), F32), compiler_params=_params(("parallel",)))(*ins)


def _adamw_math(w, gv, m, v):
    nm = ADAM_B1 * m + (1.0 - ADAM_B1) * gv
    nv = ADAM_B2 * v + (1.0 - ADAM_B2) * (gv * gv)
    m_hat = nm / (1.0 - ADAM_B1 ** ADAM_STEP)
    v_hat = nv / (1.0 - ADAM_B2 ** ADAM_STEP)
    return -ADAM_LR * (m_hat / (jnp.sqrt(v_hat) + ADAM_EPS) + ADAM_WD * w), nm, nv


def _adamw_layers(w, g0, g1, m, v, *, name, block_bytes=1 << 20):
    _, rows, cols = w.shape
    tr = _row_tile(rows, cols, block_bytes)

    def body(w_ref, g0_ref, g1_ref, m_ref, v_ref, g_ref, d_ref, nm_ref, nv_ref):
        gv = jnp.where(pl.program_id(0) == 0, g0_ref[...], g1_ref[...])
        g_ref[...] = gv
        d_ref[...], nm_ref[...], nv_ref[...] = _adamw_math(w_ref[...], gv, m_ref[...], v_ref[...])

    both = pl.BlockSpec((None, tr, cols), lambda l, i: (l, i, 0))
    specs = [both, pl.BlockSpec((tr, cols), lambda l, i: (i * (1 - l), 0)), pl.BlockSpec((tr, cols), lambda l, i: (i * l, 0)), both, both]
    return pl.pallas_call(body, name=name, grid=(2, rows // tr), in_specs=specs, out_specs=[both] * 4,
                          out_shape=[jax.ShapeDtypeStruct(w.shape, F32)] * 4, compiler_params=_params(("arbitrary", "arbitrary")))(
                              w, g0, g1, m, v)


def _adamw(w, g, m, v, *, name, block_bytes=1 << 20):
    rows, cols = w.shape
    tr = _row_tile(rows, cols, block_bytes)

    def body(w_ref, g_ref, m_ref, v_ref, d_ref, nm_ref, nv_ref):
        d_ref[...], nm_ref[...], nv_ref[...] = _adamw_math(w_ref[...], g_ref[...], m_ref[...], v_ref[...])

    spec = pl.BlockSpec((tr, cols), lambda i: (i, 0))
    return pl.pallas_call(body, name=name, grid=(rows // tr,), in_specs=[spec] * 4, out_specs=[spec] * 3,
                          out_shape=[jax.ShapeDtypeStruct((rows, cols), F32)] * 3, compiler_params=_params(("parallel",)))(w, g, m, v)


WEIGHTS = ['w_in', 's5_lambda_re', 's5_lambda_im', 's5_log_step', 's5_b_re', 's5_b_im', 's5_c_re', 's5_c_im', 's5_d', 's5_glu_w',
           's5_glu_b', 'sgu_norm_g', 'sgu_norm_b', 'sgu_w', 'sgu_b', 'pool_w', 'pool_scale', 'dn_conv_w', 'dn_a_log', 'dn_dt_bias',
           'dn_norm_g', 'w_out', 'ln1_g', 'ln1_b', 'w_up', 'w_down', 'ln2_g', 'ln2_b']
BIG = ['w_in', 's5_glu_w', 'w_out', 'w_up', 'w_down']
SMALL = [n for n in WEIGHTS if n not in BIG]
CB_S5, CB_SGU_U, CB_SGU_V, CB_POOL, CB_DN_Q, CB_DN_GATE = 0, 1, 2, 3, 4, 7
KT = 2048


def _pad_lanes(v, width=128):
    return jnp.zeros((1, width), F32).at[0, :v.shape[0]].set(v)


def _layer_consts(p):
    c = _s5_prepare(p['s5_lambda_re'], p['s5_lambda_im'], p['s5_log_step'], p['s5_b_re'], p['s5_b_im'], p['s5_c_re'], p['s5_c_im'])
    tril = jnp.tril(jnp.ones((SGU_CHUNK, SGU_CHUNK), bool))
    wm = jnp.where(tril, p['sgu_w'], 0.0)
    c.update(s5_d=p['s5_d'].reshape(1, GW), glu_b=p['s5_glu_b'].reshape(1, GW), sgu_ng=p['sgu_norm_g'].reshape(1, GW),
             sgu_nb=p['sgu_norm_b'].reshape(1, GW), sgu_w=wm, sgu_wt=jnp.swapaxes(wm, 1, 2),
             sgu_bias=jnp.repeat(p['sgu_b'].T, SGU_HD, axis=1), pool_w=p['pool_w'], pool_scale=p['pool_scale'].reshape(1, GW),
             conv_w=p['dn_conv_w'], alog=_pad_lanes(p['dn_a_log']), dtb=_pad_lanes(p['dn_dt_bias']), dn_ng=p['dn_norm_g'].reshape(1, DN_HD),
             ln1_g=p['ln1_g'].reshape(1, D_MODEL), ln1_b=p['ln1_b'].reshape(1, D_MODEL), ln2_g=p['ln2_g'].reshape(1, D_MODEL),
             ln2_b=p['ln2_b'].reshape(1, D_MODEL))
    return c


def _layer_fwd(xin, xin16, w, c, i, carries):
    tag = str(i)
    residual = lambda r, e: (r + ALPHA * e,)

    def mm(a, b_name, *, name, **kw):
        if name not in carries:
            return _matmul(a, w[b_name], name=name + tag, **kw)
        carry, done = carries[name]
        outs, extra = _matmul(a, w[b_name], name=name + tag, carry=carry, **kw)
        done(extra)
        return outs

    (proj,) = mm(xin16, 'w_in', mode="nn", name="proj", tn=1408, tk=KT)
    s5, xre, xim = _s5_fwd(proj, c['bbre'], c['bbim'], c['ccre'], c['ccim'], c['s5_d'], c['cf'], w['s5_glu_w'], c['glu_b'], cb=CB_S5)
    sgu = _sgu_fwd(proj, c['sgu_ng'], c['sgu_nb'], c['sgu_w'], c['sgu_bias'], cbu=CB_SGU_U, cbv=CB_SGU_V)
    pool, pooled = _pool_fwd(proj, c['pool_w'], c['pool_scale'], cb=CB_POOL)
    qn, kn, vs, gb = _dn_prep_fwd(proj, c['conv_w'], c['alog'], c['dtb'], cbq=CB_DN_Q)
    u, wm, qg, ktail, qkd, tm = _dn_local_fwd(qn, kn, vs, gb)
    dn, raw, v_new, states = _dn_seq_fwd(u, wm, qg, ktail, qkd, gb, proj, c['dn_ng'], cb_gate=CB_DN_GATE)
    mixed = jnp.concatenate([s5, sgu, pool, dn], axis=1)
    (h1,) = mm(mixed, 'w_out', mode="nn", name="mix_out", e=xin, epi=residual, tk=KT)
    x1, x1_16 = _ln_fwd(h1, c['ln1_g'], c['ln1_b'], name="ln1_" + tag)
    (hidden,) = mm(x1_16, 'w_up', mode="nn", name="mlp_up", epi=lambda r, e: (_relu2(r),), out_dtypes=(MXU_DTYPE,), tk=KT)
    (h2,) = mm(hidden, 'w_down', mode="nn", name="mlp_down", e=x1, epi=residual, tk=KT)
    x2, x2_16 = _ln_fwd(h2, c['ln2_g'], c['ln2_b'], name="ln2_" + tag)
    saved = dict(xin16=xin16, proj=proj, xre=xre, xim=xim, pooled=pooled, qn=qn, kn=kn, vs=vs, gb=gb, raw=raw, states=states,
                 wm=wm, qg=qg, ktail=ktail, qkd=qkd, tm=tm, v_new=v_new, mixed=mixed, h1=h1, x1_16=x1_16, hidden=hidden, h2=h2)
    return x2, x2_16, saved


def _by_rows(g):
    return g.reshape(4, g.shape[0] // 4, g.shape[1])


def _by_cols(g):
    return jnp.transpose(g.reshape(g.shape[0], 4, g.shape[1] // 4), (1, 0, 2))


def _layer_bwd(dx2, s, w, c, p, i, place):
    tag = str(i)
    residual = lambda r, e: (r + ALPHA * e,)
    reduced = {}

    def pair(blocks, names):
        got = _pair_exchange(blocks, name="grad_pair_exchange_" + names[0] + tag)
        return [_pair_sum(g, r, place, name="pair_sum_" + nm + tag) for g, r, nm in zip(blocks, got, names)]

    def riding(ps, names, a, b, **kw):
        outs, got = _matmul(a, b, carry=_carry_chip_exchange(ps), **kw)
        bufs = _pair_join([_chip_sum(q, t, place, name="chip_sum_" + nm + tag) for q, t, nm in zip(ps, got, names)],
                          name="grad_pair_join_" + names[0] + tag)
        for nm, buf in zip(names, bufs):
            reduced[nm] = buf.reshape(-1, buf.shape[-1])
        return outs

    dh2, dh2_16, dln2g, dln2b = _ln_bwd(dx2, s['h2'], c['ln2_g'], name="ln2_bwd" + tag)
    (dw_down,) = _matmul(s['hidden'], dh2_16, mode="tn", name="dw_down" + tag, tk=KT)
    p_down = pair([_by_rows(dw_down)], ['w_down'])
    (da,) = _matmul(dh2_16, w['w_down'], mode="nt", name="d_hidden" + tag, e=s['hidden'],
                    epi=lambda r, e: (r * (2.0 * jnp.sqrt(e.astype(F32))),), out_dtypes=(MXU_DTYPE,), tk=KT)
    (dw_up,) = riding(p_down, ['w_down'], s['x1_16'], da, mode="tn", name="dw_up" + tag, tk=KT)
    p_up = pair([_by_cols(dw_up)], ['w_up'])
    (dx1,) = riding(p_up, ['w_up'], da, w['w_up'], mode="nt", name="dx_mlp" + tag, e=dh2, epi=residual, tk=KT)
    dh1, dh1_16, dln1g, dln1b = _ln_bwd(dx1, s['h1'], c['ln1_g'], name="ln1_bwd" + tag)
    (dw_out,) = _matmul(s['mixed'], dh1_16, mode="tn", name="dw_out" + tag, tk=KT)
    p_out = pair([_by_rows(dw_out)], ['w_out'])
    (dmixed,) = riding(p_out, ['w_out'], dh1_16, w['w_out'], mode="nt", name="d_mixed" + tag, tk=KT)
    proj = s['proj']
    (du, dglu_w, dglu_b, dd, dccre, dccim, dbbre, dbbim, sre, sim) = _s5_bwd(
        dmixed, proj, s['xre'], s['xim'], c['bbre'], c['bbim'], c['ccre'], c['ccim'], c['s5_d'], c['cr'], w['s5_glu_w'], c['glu_b'],
        cb_dy=0, cb=CB_S5)
    dlam_re, dlam_im, dlog_step, db_re, db_im, dc_re, dc_im = _s5_param_grads(
        p['s5_lambda_re'], p['s5_lambda_im'], p['s5_log_step'], p['s5_b_re'], p['s5_b_im'], dbbre, dbbim, dccre, dccim, sre, sim)
    dzu, dzv, dsgu_w, dsgu_bias, dsgu_ng, dsgu_nb = _sgu_bwd(dmixed, proj, c['sgu_ng'], c['sgu_nb'], c['sgu_w'], c['sgu_wt'], c['sgu_bias'],
                                                            cb=1, cbu=CB_SGU_U, cbv=CB_SGU_V)
    dp, dpool_w, dpool_scale = _pool_bwd(dmixed, s['pooled'], c['pool_w'], c['pool_scale'], cb=2)
    dgate, ddn_ng, do, d_vnew, d_wm, d_qg, d_ktail, d_eglast = _dn_seq_bwd(
        dmixed, s['raw'], proj, c['dn_ng'], s['gb'], s['wm'], s['qg'], s['ktail'], s['qkd'], s['v_new'], s['states'],
        cb_dy=3, cb_gate=CB_DN_GATE)
    dqn, dkn, dvs, dgb = _dn_local_bwd(s['qn'], s['kn'], s['vs'], s['gb'], s['tm'], s['v_new'], do, d_vnew, d_wm, d_qg, d_ktail, d_eglast)
    dcq, dck, dcv, dab, dalog, ddtb = _dn_prep_bwd_a(proj, c['conv_w'], c['alog'], c['dtb'], dqn, dkn, dvs, dgb, cbq=CB_DN_Q)
    dq, dk, dv, dconv_w = _dn_prep_bwd_b(proj, c['conv_w'], dcq, dck, dcv, cbq=CB_DN_Q)
    dproj = jnp.concatenate([du, dzu, dzv, dp, dq, dk, dv, dgate, dab], axis=1)
    (dw_in,) = _matmul(s['xin16'], dproj, mode="tn", name="dw_in" + tag, tn=1408, tk=KT)
    p_in = pair([_by_cols(dw_in[:, :IN_COLS]), _by_rows(dglu_w)], ['w_in', 's5_glu_w'])
    (dxin,) = riding(p_in, ['w_in', 's5_glu_w'], dproj, w['w_in'], mode="nt", name="dx_in" + tag, tk=1408, e=dh1, epi=residual)
    tril = jnp.tril(jnp.ones((SGU_CHUNK, SGU_CHUNK), bool))
    small = dict(
        s5_lambda_re=dlam_re, s5_lambda_im=dlam_im, s5_log_step=dlog_step, s5_b_re=db_re, s5_b_im=db_im, s5_c_re=dc_re, s5_c_im=dc_im,
        s5_d=dd.reshape(S5_G, S5_H), s5_glu_b=dglu_b[0], sgu_norm_g=dsgu_ng[0], sgu_norm_b=dsgu_nb[0],
        sgu_w=jnp.where(tril, dsgu_w, 0.0), sgu_b=dsgu_bias.reshape(SGU_CHUNK, SGU_HEADS, SGU_HD).sum(-1).T, pool_w=dpool_w,
        pool_scale=dpool_scale[0], dn_conv_w=dconv_w, dn_a_log=dalog[0, :DN_HEADS], dn_dt_bias=ddtb[0, :DN_HEADS], dn_norm_g=ddn_ng[0],
        ln1_g=dln1g[0], ln1_b=dln1b[0], ln2_g=dln2g[0], ln2_b=dln2b[0])
    return dxin, reduced, small


def _pack(arrs):
    flat = jnp.concatenate([a.reshape(-1) for a in arrs])
    n = flat.shape[0]
    m = -(-n // 1024) * 1024
    return jnp.pad(flat, (0, m - n)).reshape(m // 128, 128)


def _sum_all(stacked):
    return _sum_kernel([(stacked, d) for d in range(stacked.shape[0])], name="small_sum")


def _unpack(packed, like):
    flat, out, off = packed.reshape(-1), [], 0
    for a in like:
        n = math.prod(a.shape)
        out.append(flat[off:off + n].reshape(a.shape))
        off += n
    return out


def kernel(x, w_in, s5_lambda_re, s5_lambda_im, s5_log_step, s5_b_re, s5_b_im, s5_c_re, s5_c_im, s5_d, s5_glu_w, s5_glu_b, sgu_norm_g, sgu_norm_b, sgu_w, sgu_b, pool_w, pool_scale, dn_conv_w, dn_a_log, dn_dt_bias, dn_norm_g, w_out, ln1_g, ln1_b, w_up, w_down, ln2_g, ln2_b, loss_target, m_w_in, m_s5_lambda_re, m_s5_lambda_im, m_s5_log_step, m_s5_b_re, m_s5_b_im, m_s5_c_re, m_s5_c_im, m_s5_d, m_s5_glu_w, m_s5_glu_b, m_sgu_norm_g, m_sgu_norm_b, m_sgu_w, m_sgu_b, m_pool_w, m_pool_scale, m_dn_conv_w, m_dn_a_log, m_dn_dt_bias, m_dn_norm_g, m_w_out, m_ln1_g, m_ln1_b, m_w_up, m_w_down, m_ln2_g, m_ln2_b, v_w_in, v_s5_lambda_re, v_s5_lambda_im, v_s5_log_step, v_s5_b_re, v_s5_b_im, v_s5_c_re, v_s5_c_im, v_s5_d, v_s5_glu_w, v_s5_glu_b, v_sgu_norm_g, v_sgu_norm_b, v_sgu_w, v_sgu_b, v_pool_w, v_pool_scale, v_dn_conv_w, v_dn_a_log, v_dn_dt_bias, v_dn_norm_g, v_w_out, v_ln1_g, v_ln1_b, v_w_up, v_w_down, v_ln2_g, v_ln2_b):
    given = dict(locals())
    xs, ys = lax.axis_index("x"), lax.axis_index("y")
    chip = 2 * xs + ys
    t = given['x'].shape[1]
    x0 = given['x'].reshape(t, D_MODEL)
    target = given['loss_target'].reshape(t, D_MODEL)

    assert DEPTH == 2
    place = jnp.stack([lax.axis_index("c"), chip]).astype(jnp.int32)
    conv_local = given['dn_conv_w']
    conv_all = _allgather_all(_pack([conv_local]))
    n_conv = math.prod(conv_local.shape)
    conv_full = jnp.concatenate([conv_all[2 * j].reshape(-1)[:n_conv].reshape(conv_local.shape) for j in range(4)], axis=-1)

    ws = [dict(), dict()]

    def whole(n, blocks):
        if n == 'w_in':
            return jnp.pad(jnp.transpose(blocks, (1, 0, 2)).reshape(D_MODEL, IN_COLS), ((0, 0), (0, IN_PAD - IN_COLS)))
        if n == 'w_up':
            return jnp.transpose(blocks, (1, 0, 2)).reshape(D_MODEL, D_FF)
        return blocks.reshape(-1, blocks.shape[-1])

    def gather(items):
        own = [given[n][i].astype(MXU_DTYPE) for n, i in items]

        def done(bufs):
            for (n, i), buf, mine in zip(items, bufs, own):
                ws[i][n] = whole(n, lax.dynamic_update_slice(buf, mine[None], (chip, 0, 0)))
        return _carry_allgather(own), done

    first, first_done = gather([('w_in', 0), ('s5_glu_w', 0), ('w_out', 0)])
    first_done(_run_carry(first, "allgather_first"))
    carries = [dict(proj=gather([('w_up', 0)]), mix_out=gather([('w_down', 0)]),
                    mlp_up=gather([('w_in', 1), ('s5_glu_w', 1), ('w_out', 1)]), mlp_down=gather([('w_up', 1)])),
               dict(proj=gather([('w_down', 1)]))]

    def layer_params(i):
        p = {n: given[n][i] for n in SMALL}
        p['dn_conv_w'] = conv_full[i]
        return p

    ps = [layer_params(i) for i in range(DEPTH)]
    cs = [_layer_consts(p) for p in ps]

    xcur, xcur16, saved = x0, x0.astype(MXU_DTYPE), []
    for i in range(DEPTH):
        xcur, xcur16, s = _layer_fwd(xcur, xcur16, ws[i], cs[i], i, carries[i])
        saved.append(s)
    dx, colsum = _loss_head(xcur, target)
    loss = lax.psum(0.5 * jnp.sum(colsum) / D_MODEL, ("x", "y", "c"))

    reduced, smalls = [None] * DEPTH, [None] * DEPTH
    for i in reversed(range(DEPTH)):
        dx, reduced[i], smalls[i] = _layer_bwd(dx, saved[i], ws[i], cs[i], ps[i], i, place)
    grad_x = dx.reshape(1, t, D_MODEL)

    small_full = [jnp.stack([smalls[i][n] for i in range(DEPTH)]) for n in SMALL]
    grads = dict(zip(SMALL, _unpack(_sum_all(_allgather_all(_pack(small_full))), small_full)))
    grads['dn_conv_w'] = lax.dynamic_slice_in_dim(grads['dn_conv_w'], chip * conv_local.shape[-1], conv_local.shape[-1], axis=2)

    delta, new_m, new_v = {}, {}, {}
    for n in BIG:
        grads[n], delta[n], new_m[n], new_v[n] = _adamw_layers(given[n], reduced[0][n], reduced[1][n], given['m_' + n], given['v_' + n],
                                                               name="adamw_" + n)
    like = [given[n] for n in SMALL]
    d, nm, nv = _adamw(_pack(like), _pack([grads[n] for n in SMALL]), _pack([given['m_' + n] for n in SMALL]),
                       _pack([given['v_' + n] for n in SMALL]), name="adamw_small")
    for out, packed in ((delta, d), (new_m, nm), (new_v, nv)):
        out.update(zip(SMALL, _unpack(packed, like)))
    return (loss, grad_x, *[grads[n] for n in WEIGHTS], *[delta[n] for n in WEIGHTS], *[new_m[n] for n in WEIGHTS],
            *[new_v[n] for n in WEIGHTS])
```

```python
import functools
import math

import jax
import jax.numpy as jnp
from jax import lax
from jax.experimental import pallas as pl
from jax.experimental.pallas import tpu as pltpu

F32 = jnp.float32
MXU_DTYPE = jnp.bfloat16
HI = lax.Precision.HIGHEST

D_MODEL = 2048
DEPTH = 2
GW = 512
S5_H = 16
S5_G = GW // S5_H
S5_P = 64
S5_N = S5_G * S5_P
SGU_CHUNK = 128
SGU_HEADS = 8
SGU_HD = GW // SGU_HEADS
POOL_WINDOWS = (2, 4, 8, 16)
POOL_GD = 128
DN_HD = 128
DN_HEADS = 4
DN_CONV = 4
DN_CHUNK = 64
D_FF = 4 * D_MODEL
IN_COLS = 4104
IN_PAD = 4224
LN_EPS = 1e-5
RMS_EPS = 1e-6
L2_EPS = 1e-6
ALPHA = (2 * DEPTH) ** 0.25
ADAM_LR, ADAM_B1, ADAM_B2, ADAM_EPS, ADAM_WD, ADAM_STEP = 0.001, 0.9, 0.999, 1e-08, 0.01, 10

VMEM_LIMIT = 56 * 1024 * 1024
MESH = pl.DeviceIdType.MESH


def _params(sem=None, vmem=VMEM_LIMIT):
    return pltpu.CompilerParams(dimension_semantics=sem, vmem_limit_bytes=vmem)


def _full(shape):
    nd = len(shape)
    return pl.BlockSpec(shape, lambda *_: (0,) * nd)


def _split(a):
    hi = a.astype(MXU_DTYPE)
    return hi, (a - hi.astype(F32)).astype(MXU_DTYPE)


def _dot(a, b, dims=(((1,), (0,)), ((), ())), exact=False):
    if exact and MXU_DTYPE == F32:
        return lax.dot_general(a, b, dims, precision=HI, preferred_element_type=F32)
    if exact:
        (ah, al), (bh, bl) = _split(a), _split(b)
        return (lax.dot_general(ah, bh, dims, preferred_element_type=F32) + lax.dot_general(al, bh, dims, preferred_element_type=F32)
                + lax.dot_general(ah, bl, dims, preferred_element_type=F32))
    return lax.dot_general(a.astype(MXU_DTYPE), b.astype(MXU_DTYPE), dims, preferred_element_type=F32)


NN = (((1,), (0,)), ((), ()))
NT = (((1,), (1,)), ((), ()))
TN = (((0,), (0,)), ((), ()))


def _gelu(x):
    c = math.sqrt(2.0 / math.pi)
    return 0.5 * x * (1.0 + jnp.tanh(c * (x + 0.044715 * x * x * x)))


def _gelu_grad(x):
    c = math.sqrt(2.0 / math.pi)
    t = jnp.tanh(c * (x + 0.044715 * x * x * x))
    return 0.5 * (1.0 + t) + 0.5 * x * (1.0 - t * t) * c * (1.0 + 3.0 * 0.044715 * x * x)


def _sigmoid(x):
    return 1.0 / (1.0 + jnp.exp(-x))


def _relu2(x):
    r = jnp.maximum(x, 0.0)
    return r * r


class _Carry:
    def __init__(self, ins, out_shapes, nsem, start, finish):
        self.ins, self.out_shapes, self.nsem, self.start, self.finish = list(ins), list(out_shapes), nsem, start, finish

    def sems(self):
        return [pltpu.SemaphoreType.DMA((self.nsem,)), pltpu.SemaphoreType.DMA((self.nsem,))]


def _run_carry(carry, name):
    n_in, n_out = len(carry.ins), len(carry.out_shapes)

    def body(*refs):
        parts = refs[:n_in], refs[n_in:n_in + n_out], refs[-2], refs[-1]
        carry.start(*parts)
        carry.finish(*parts)

    any_spec = pl.BlockSpec(memory_space=pl.ANY)
    return pl.pallas_call(body, name=name, in_specs=[any_spec] * n_in, out_specs=[any_spec] * n_out, out_shape=carry.out_shapes,
                          scratch_shapes=carry.sems())(*carry.ins)


def _matmul(a, b, *, mode, name, e=None, epi=None, out_dtypes=(F32,), tm=1024, tn=1024, tk=512, carry=None, b_blocked=False,
            out_blocked=False):
    if mode == "nn":
        (m, k), n = a.shape, (4 * b.shape[2] if b_blocked else b.shape[1])
    elif mode == "nt":
        m, n, k = a.shape[0], b.shape[-2], a.shape[1]
    else:
        (k, m), n = a.shape, b.shape[1]
    tm, tn, tk = min(tm, m), min(tn, n), min(tk, k)
    if b_blocked or out_blocked:
        tn, tk = min(tn, n // 4), (min(tk, k // 4) if mode == "nt" and b_blocked else tk)
    assert m % tm == 0 and n % tn == 0 and k % tk == 0, (name, m, n, k, tm, tn, tk)
    nk, nout = k // tk, len(out_dtypes)
    dims = {"nn": NN, "nt": NT, "tn": TN}[mode]
    a_spec = pl.BlockSpec((tk, tm), lambda i, j, l: (l, i)) if mode == "tn" else pl.BlockSpec((tm, tk), lambda i, j, l: (i, l))
    nb, kb = max(n // 4 // tn, 1), max(k // 4 // tk, 1)

    def split(idx, per):
        return lax.div(idx, jnp.int32(per)), lax.rem(idx, jnp.int32(per))

    if b_blocked and mode == "nn":
        b_spec = pl.BlockSpec((None, tk, tn), lambda i, j, l: (split(j, nb)[0], l, split(j, nb)[1]))
    elif b_blocked:
        b_spec = pl.BlockSpec((None, tn, tk), lambda i, j, l: (split(l, kb)[0], j, split(l, kb)[1]))
    else:
        b_spec = pl.BlockSpec((tn, tk), lambda i, j, l: (j, l)) if mode == "nt" else pl.BlockSpec((tk, tn), lambda i, j, l: (l, j))
    if out_blocked:
        o_spec = pl.BlockSpec((None, tm, tn), lambda i, j, l: (split(j, nb)[0], i, split(j, nb)[1]))
    else:
        o_spec = pl.BlockSpec((tm, tn), lambda i, j, l: (i, j))
    assert not (out_blocked and e is not None)
    o_shape = (4, m, n // 4) if out_blocked else (m, n)

    n_in = 2 + (e is not None)
    c_in, c_out = (len(carry.ins), len(carry.out_shapes)) if carry is not None else (0, 0)
    gm, gn = m // tm, n // tn

    def body(*refs):
        a_ref, b_ref = refs[:2]
        e_ref = refs[2] if e is not None else None
        o_refs = refs[n_in + c_in:n_in + c_in + nout]
        acc = refs[n_in + c_in + nout + c_out]
        l = pl.program_id(2)
        if carry is not None:
            parts = refs[n_in:n_in + c_in], refs[n_in + c_in + nout:n_in + c_in + nout + c_out], refs[-2], refs[-1]
            step = (pl.program_id(0) * gn + pl.program_id(1)) * nk + l

            @pl.when(step == 0)
            def _():
                carry.start(*parts)

        d = _dot(a_ref[...], b_ref[...], dims)

        def finish(r):
            outs = (r,) if epi is None else epi(r, None if e_ref is None else e_ref[...])
            for o_ref, o, dt in zip(o_refs, outs, out_dtypes, strict=True):
                o_ref[...] = o.astype(dt)

        if nk == 1:
            finish(d)
        else:
            @pl.when(l == 0)
            def _():
                acc[...] = d

            @pl.when((l > 0) & (l < nk - 1))
            def _():
                acc[...] += d

            @pl.when(l == nk - 1)
            def _():
                finish(acc[...] + d)

        if carry is not None:
            @pl.when(step == gm * gn * nk - 1)
            def _():
                carry.finish(*parts)

    ins, specs = [a, b], [a_spec, b_spec]
    if e is not None:
        ins.append(e)
        specs.append(o_spec)
    out_shape = [jax.ShapeDtypeStruct(o_shape, dt) for dt in out_dtypes]
    out_specs, scratch = [o_spec] * nout, [pltpu.VMEM((tm, tn), F32)]
    if carry is not None:
        any_spec = pl.BlockSpec(memory_space=pl.ANY)
        ins, specs = ins + carry.ins, specs + [any_spec] * c_in
        out_shape, out_specs, scratch = out_shape + carry.out_shapes, out_specs + [any_spec] * c_out, scratch + carry.sems()
    sem = ("parallel", "parallel", "arbitrary") if carry is None else ("arbitrary",) * 3
    res = pl.pallas_call(body, name=name, grid=(gm, gn, nk), in_specs=specs, out_specs=out_specs, out_shape=out_shape,
                         scratch_shapes=scratch, compiler_params=_params(sem))(*ins)
    return tuple(res) if carry is None else (tuple(res[:nout]), list(res[nout:]))


def _ln_fwd(h, g, b, *, name, tr=256):
    t, d = h.shape

    def body(h_ref, g_ref, b_ref, o_ref, o16_ref):
        x = h_ref[...]
        mu = jnp.mean(x, axis=-1, keepdims=True)
        xc = x - mu
        var = jnp.mean(xc * xc, axis=-1, keepdims=True)
        y = xc * lax.rsqrt(var + LN_EPS) * g_ref[...] + b_ref[...]
        o_ref[...] = y
        o16_ref[...] = y.astype(MXU_DTYPE)

    row = pl.BlockSpec((tr, d), lambda i: (i, 0))
    return pl.pallas_call(body, name=name, grid=(t // tr,), in_specs=[row, _full((1, d)), _full((1, d))], out_specs=[row, row],
                          out_shape=[jax.ShapeDtypeStruct((t, d), F32), jax.ShapeDtypeStruct((t, d), MXU_DTYPE)],
                          compiler_params=_params(("parallel",)))(h, g, b)


def _ln_bwd(dy, h, g, *, name, tr=256):
    t, d = h.shape

    def body(dy_ref, h_ref, g_ref, dh_ref, dh16_ref, dg_ref, db_ref):
        @pl.when(pl.program_id(0) == 0)
        def _():
            dg_ref[...] = jnp.zeros_like(dg_ref)
            db_ref[...] = jnp.zeros_like(db_ref)

        x, dyv = h_ref[...], dy_ref[...]
        mu = jnp.mean(x, axis=-1, keepdims=True)
        xc = x - mu
        rstd = lax.rsqrt(jnp.mean(xc * xc, axis=-1, keepdims=True) + LN_EPS)
        xh = xc * rstd
        w = dyv * g_ref[...]
        dh = rstd * (w - jnp.mean(w, axis=-1, keepdims=True) - xh * jnp.mean(w * xh, axis=-1, keepdims=True))
        dh_ref[...] = dh
        dh16_ref[...] = dh.astype(MXU_DTYPE)
        dg_ref[...] += jnp.sum(dyv * xh, axis=0, keepdims=True)
        db_ref[...] += jnp.sum(dyv, axis=0, keepdims=True)

    row = pl.BlockSpec((tr, d), lambda i: (i, 0))
    vec = _full((1, d))
    return pl.pallas_call(
        body, name=name, grid=(t // tr,), in_specs=[row, row, vec], out_specs=[row, row, vec, vec],
        out_shape=[jax.ShapeDtypeStruct((t, d), F32), jax.ShapeDtypeStruct((t, d), MXU_DTYPE), jax.ShapeDtypeStruct((1, d), F32),
                   jax.ShapeDtypeStruct((1, d), F32)],
        compiler_params=_params(("arbitrary",)))(dy, h, g)


def _loss_head(y, target, *, tr=256):
    t, d = y.shape

    def body(y_ref, t_ref, dy_ref, s_ref):
        @pl.when(pl.program_id(0) == 0)
        def _():
            s_ref[...] = jnp.zeros_like(s_ref)

        err = y_ref[...] - t_ref[...]
        dy_ref[...] = err * (1.0 / d)
        s_ref[...] += jnp.sum(err * err, axis=0, keepdims=True)

    row = pl.BlockSpec((tr, d), lambda i: (i, 0))
    return pl.pallas_call(
        body, name="loss_head", grid=(t // tr,), in_specs=[row, row], out_specs=[row, _full((1, d))],
        out_shape=[jax.ShapeDtypeStruct((t, d), F32), jax.ShapeDtypeStruct((1, d), F32)],
        compiler_params=_params(("arbitrary",)))(y, target)


def _cols(tb, width, cb):
    return pl.BlockSpec((tb, width), lambda i: (i, cb))


def _cols_rev(tb, width, cb, nb):
    return pl.BlockSpec((tb, width), lambda i: (nb - 1 - i, cb))


POOL_HALO = 16


def _pool_fwd(proj, w, scale, *, cb, tb=512):
    t = proj.shape[0]
    tb = min(tb, t)

    def body(p_ref, w_ref, s_ref, o_ref, pooled_ref, ext):
        i = pl.program_id(0)

        @pl.when(i == 0)
        def _():
            ext[pl.ds(0, POOL_HALO), :] = jnp.zeros((POOL_HALO, GW), F32)

        p = p_ref[...]
        ext[pl.ds(POOL_HALO, tb), :] = p
        pos = (i * tb + lax.broadcasted_iota(jnp.int32, (tb, 1), 0) + 1).astype(F32)
        for gi, win in enumerate(POOL_WINDOWS):
            c0 = gi * POOL_GD
            s = p[:, c0:c0 + POOL_GD]
            for k in range(1, win):
                s = s + ext[pl.ds(POOL_HALO - k, tb), pl.ds(c0, POOL_GD)]
            pooled = s / jnp.minimum(pos, float(win)) - p[:, c0:c0 + POOL_GD]
            pooled_ref[:, pl.ds(c0, POOL_GD)] = pooled
            o_ref[:, pl.ds(c0, POOL_GD)] = (_dot(pooled, w_ref[gi]) * s_ref[:, pl.ds(c0, POOL_GD)]).astype(o_ref.dtype)
        ext[pl.ds(0, POOL_HALO), :] = p[tb - POOL_HALO:, :]

    row = pl.BlockSpec((tb, GW), lambda i: (i, 0))
    return pl.pallas_call(
        body, name="pool_fwd", grid=(t // tb,),
        in_specs=[_cols(tb, GW, cb), _full((4, POOL_GD, POOL_GD)), _full((1, GW))], out_specs=[row, row],
        out_shape=[jax.ShapeDtypeStruct((t, GW), MXU_DTYPE), jax.ShapeDtypeStruct((t, GW), F32)],
        scratch_shapes=[pltpu.VMEM((tb + POOL_HALO, GW), F32)],
        compiler_params=_params(("arbitrary",)))(proj, w, scale)


def _pool_bwd(dmixed, pooled, w, scale, *, cb, tb=512):
    t = pooled.shape[0]
    tb = min(tb, t)
    nb = t // tb

    def body(dy_ref, pooled_ref, w_ref, s_ref, dp_ref, dw_ref, ds_ref, ext):
        i = pl.program_id(0)

        @pl.when(i == 0)
        def _():
            ext[pl.ds(tb, POOL_HALO), :] = jnp.zeros((POOL_HALO, GW), F32)
            dw_ref[...] = jnp.zeros_like(dw_ref)
            ds_ref[...] = jnp.zeros_like(ds_ref)

        dy = dy_ref[...]
        pos = ((nb - 1 - i) * tb + lax.broadcasted_iota(jnp.int32, (tb, 1), 0) + 1).astype(F32)
        dpool_all = []
        for gi, win in enumerate(POOL_WINDOWS):
            c0 = gi * POOL_GD
            pg = pooled_ref[:, pl.ds(c0, POOL_GD)]
            dyg = dy[:, c0:c0 + POOL_GD]
            ds_ref[:, pl.ds(c0, POOL_GD)] += jnp.sum(dyg * _dot(pg, w_ref[gi]), axis=0, keepdims=True)
            dmp = dyg * s_ref[:, pl.ds(c0, POOL_GD)]
            dw_ref[gi] += _dot(pg, dmp, TN)
            dpool = _dot(dmp, w_ref[gi], NT)
            dpool_all.append(dpool)
            ext[pl.ds(0, tb), pl.ds(c0, POOL_GD)] = dpool / jnp.minimum(pos, float(win))
        for gi, win in enumerate(POOL_WINDOWS):
            c0 = gi * POOL_GD
            s = ext[pl.ds(0, tb), pl.ds(c0, POOL_GD)]
            for k in range(1, win):
                s = s + ext[pl.ds(k, tb), pl.ds(c0, POOL_GD)]
            dp_ref[:, pl.ds(c0, POOL_GD)] = (s - dpool_all[gi]).astype(dp_ref.dtype)
        ext[pl.ds(tb, POOL_HALO), :] = ext[pl.ds(0, POOL_HALO), :]

    row = pl.BlockSpec((tb, GW), lambda i: (nb - 1 - i, 0))
    return pl.pallas_call(
        body, name="pool_bwd", grid=(nb,),
        in_specs=[_cols_rev(tb, GW, cb, nb), row, _full((4, POOL_GD, POOL_GD)), _full((1, GW))],
        out_specs=[row, _full((4, POOL_GD, POOL_GD)), _full((1, GW))],
        out_shape=[jax.ShapeDtypeStruct((t, GW), MXU_DTYPE), jax.ShapeDtypeStruct((4, POOL_GD, POOL_GD), F32),
                   jax.ShapeDtypeStruct((1, GW), F32)],
        scratch_shapes=[pltpu.VMEM((tb + POOL_HALO, GW), F32)], compiler_params=_params(("arbitrary",)))(dmixed, pooled, w, scale)


def _sgu_core(zu, zv, ng, nb, w_ref, bias):
    tb = zu.shape[0]
    u = _gelu(zu)
    v0 = _gelu(zv)
    mu = jnp.mean(v0, axis=-1, keepdims=True)
    vc = v0 - mu
    rstd = lax.rsqrt(jnp.mean(vc * vc, axis=-1, keepdims=True) + LN_EPS)
    xh = vc * rstd
    vn = xh * ng + nb
    low = lax.broadcasted_iota(jnp.int32, (SGU_CHUNK, 2 * SGU_HD), 1) < SGU_HD
    rows = []
    for n in range(tb // SGU_CHUNK):
        pairs = []
        for j in range(SGU_HEADS // 2):
            vp = vn[n * SGU_CHUNK:(n + 1) * SGU_CHUNK, j * 128:(j + 1) * 128]
            pairs.append(jnp.where(low, _dot(w_ref[2 * j], vp), _dot(w_ref[2 * j + 1], vp)))
        rows.append(jnp.concatenate(pairs, axis=1) + bias)
    mixed = jnp.concatenate(rows, axis=0)
    return u, xh, rstd, vn, mixed


def _sgu_fwd(proj, ng, nb, w, bias, *, cbu, cbv, tb=512):
    t = proj.shape[0]
    tb = min(tb, t)

    def body(zu_ref, zv_ref, ng_ref, nb_ref, w_ref, bias_ref, o_ref):
        u, _, _, _, mixed = _sgu_core(zu_ref[...], zv_ref[...], ng_ref[...], nb_ref[...], w_ref, bias_ref[...])
        o_ref[...] = (u * mixed).astype(o_ref.dtype)

    vec = _full((1, GW))
    return pl.pallas_call(
        body, name="sgu_fwd", grid=(t // tb,),
        in_specs=[_cols(tb, GW, cbu), _cols(tb, GW, cbv), vec, vec, _full((8, 128, 128)), _full((128, GW))],
        out_specs=pl.BlockSpec((tb, GW), lambda i: (i, 0)), out_shape=jax.ShapeDtypeStruct((t, GW), MXU_DTYPE),
        compiler_params=_params(("parallel",)))(proj, proj, ng, nb, w, bias)


def _sgu_bwd(dmixed, proj, ng, nb, w, wt, bias, *, cb, cbu, cbv, tb=512):
    t = proj.shape[0]
    tb = min(tb, t)

    def body(dy_ref, zu_ref, zv_ref, ng_ref, nb_ref, w_ref, wt_ref, bias_ref, dzu_ref, dzv_ref, dw_ref, dbias_ref, dng_ref, dnb_ref):
        @pl.when(pl.program_id(0) == 0)
        def _():
            dw_ref[...] = jnp.zeros_like(dw_ref)
            dbias_ref[...] = jnp.zeros_like(dbias_ref)
            dng_ref[...] = jnp.zeros_like(dng_ref)
            dnb_ref[...] = jnp.zeros_like(dnb_ref)

        zu, zv, dy = zu_ref[...], zv_ref[...], dy_ref[...]
        u, xh, rstd, vn, mixed = _sgu_core(zu, zv, ng_ref[...], nb_ref[...], w_ref, bias_ref[...])
        dzu_ref[...] = (dy * mixed * _gelu_grad(zu)).astype(dzu_ref.dtype)
        dmix = dy * u
        low = lax.broadcasted_iota(jnp.int32, (SGU_CHUNK, 2 * SGU_HD), 1) < SGU_HD
        dbias = jnp.zeros((SGU_CHUNK, GW), F32)
        rows = []
        for n in range(tb // SGU_CHUNK):
            dm = dmix[n * SGU_CHUNK:(n + 1) * SGU_CHUNK, :]
            dbias = dbias + dm
            pairs = []
            for j in range(SGU_HEADS // 2):
                dmp = dm[:, j * 128:(j + 1) * 128]
                vp = vn[n * SGU_CHUNK:(n + 1) * SGU_CHUNK, j * 128:(j + 1) * 128]
                dw_ref[2 * j] += _dot(jnp.where(low, dmp, 0.0), vp, NT)
                dw_ref[2 * j + 1] += _dot(jnp.where(low, 0.0, dmp), vp, NT)
                pairs.append(jnp.where(low, _dot(wt_ref[2 * j], dmp), _dot(wt_ref[2 * j + 1], dmp)))
            rows.append(jnp.concatenate(pairs, axis=1))
        dbias_ref[...] += dbias
        dvn = jnp.concatenate(rows, axis=0)
        dng_ref[...] += jnp.sum(dvn * xh, axis=0, keepdims=True)
        dnb_ref[...] += jnp.sum(dvn, axis=0, keepdims=True)
        wv = dvn * ng_ref[...]
        dv0 = rstd * (wv - jnp.mean(wv, axis=-1, keepdims=True) - xh * jnp.mean(wv * xh, axis=-1, keepdims=True))
        dzv_ref[...] = (dv0 * _gelu_grad(zv)).astype(dzv_ref.dtype)

    vec = _full((1, GW))
    row = pl.BlockSpec((tb, GW), lambda i: (i, 0))
    mat = _full((8, 128, 128))
    return pl.pallas_call(
        body, name="sgu_bwd", grid=(t // tb,),
        in_specs=[_cols(tb, GW, cb), _cols(tb, GW, cbu), _cols(tb, GW, cbv), vec, vec, mat, mat, _full((128, GW))],
        out_specs=[row, row, mat, _full((128, GW)), vec, vec],
        out_shape=[jax.ShapeDtypeStruct((t, GW), MXU_DTYPE)] * 2 + [jax.ShapeDtypeStruct((8, 128, 128), F32),
                   jax.ShapeDtypeStruct((128, GW), F32), jax.ShapeDtypeStruct((1, GW), F32), jax.ShapeDtypeStruct((1, GW), F32)],
        compiler_params=_params(("arbitrary",)))(dmixed, proj, proj, ng, nb, w, wt, bias)


S5_KB = 4
SUB = 8


def _s5_discretize(lam_re, lam_im, log_step, b_re, b_im):
    step = jnp.exp(log_step)[:, None]
    mag = jnp.exp(lam_re * step)
    lr, li = mag * jnp.cos(lam_im * step), mag * jnp.sin(lam_im * step)
    den = lam_re * lam_re + lam_im * lam_im
    fr = ((lr - 1.0) * lam_re + li * lam_im) / den
    fi = (li * lam_re - (lr - 1.0) * lam_im) / den
    return lr, li, fr[:, :, None] * b_re - fi[:, :, None] * b_im, fr[:, :, None] * b_im + fi[:, :, None] * b_re


def _cpow(lr, li, n):
    rr, ri = lr, li
    for _ in range(n - 1):
        rr, ri = rr * lr - ri * li, rr * li + ri * lr
    return rr, ri


def _s5_scan_consts(lr, li, reverse):
    lr, li = lr.reshape(1, S5_N), (-li if reverse else li).reshape(1, S5_N)
    row = jnp.arange(SUB)[:, None]
    out = []
    for s in (1, 2, 4):
        pr, pi = _cpow(lr, li, s)
        keep = (row < SUB - s) if reverse else (row >= s)
        out += [jnp.where(keep, pr, 0.0), jnp.where(keep, pi, 0.0)]
    cr, ci = [], []
    for i in range(SUB):
        pr, pi = _cpow(lr, li, SUB - i if reverse else i + 1)
        cr.append(pr)
        ci.append(pi)
    out += [jnp.concatenate(cr, axis=0), jnp.concatenate(ci, axis=0)]
    return jnp.stack(out)


def _s5_blockdiag_in(b):
    bt = jnp.swapaxes(b, 1, 2).reshape(S5_KB, 8, S5_H, S5_P)
    eye = jnp.eye(8, dtype=b.dtype)
    return jnp.einsum("kghp,gj->kghjp", bt, eye).reshape(S5_KB, 128, 512)


def _s5_blockdiag_in_extract(bb):
    x = bb.reshape(S5_KB, 8, S5_H, 8, S5_P)
    d = jnp.einsum("kghgp->kghp", x).reshape(S5_G, S5_H, S5_P)
    return jnp.swapaxes(d, 1, 2)


def _s5_blockdiag_out(c):
    ct = jnp.swapaxes(c, 1, 2).reshape(S5_KB, 8, S5_P, S5_H)
    eye = jnp.eye(8, dtype=c.dtype)
    return jnp.einsum("kgph,gj->kgpjh", ct, eye).reshape(S5_KB, 512, 128)


def _s5_blockdiag_out_extract(cc):
    x = cc.reshape(S5_KB, 8, S5_P, 8, S5_H)
    d = jnp.einsum("kgpgh->kgph", x).reshape(S5_G, S5_P, S5_H)
    return jnp.swapaxes(d, 1, 2)


def _s5_tile_scan(a, b, c_ref, carry, reverse):
    for si, s in enumerate((1, 2, 4)):
        sh = SUB - s if reverse else s
        ar, br = pltpu.roll(a, sh, 0), pltpu.roll(b, sh, 0)
        mr, mi = c_ref[2 * si], c_ref[2 * si + 1]
        a, b = a + mr * ar - mi * br, b + mr * br + mi * ar
    pr, pi = c_ref[6], c_ref[7]
    cr, ci = carry
    return a + pr * cr - pi * ci, b + pr * ci + pi * cr


def _s5_readout(xre_ref, xim_ref, ccre_ref, ccim_ref):
    return jnp.concatenate(
        [_dot(xre_ref[:, pl.ds(512 * k, 512)], ccre_ref[k]) - _dot(xim_ref[:, pl.ds(512 * k, 512)], ccim_ref[k])
         for k in range(S5_KB)], axis=1)


def _s5_fwd(proj, bbre, bbim, ccre, ccim, dvec, consts, glu_w, glu_b, *, cb, tb=256):
    t = proj.shape[0]
    tb = min(tb, t)
    nt = tb // SUB

    def body(u_ref, bbre_ref, bbim_ref, ccre_ref, ccim_ref, d_ref, c_ref, w_ref, b_ref, o_ref, xre_ref, xim_ref, car):
        @pl.when(pl.program_id(0) == 0)
        def _():
            car[...] = jnp.zeros_like(car)

        u = u_ref[...]
        for k in range(S5_KB):
            uk = u[:, 128 * k:128 * (k + 1)]
            xre_ref[:, pl.ds(512 * k, 512)] = _dot(uk, bbre_ref[k])
            xim_ref[:, pl.ds(512 * k, 512)] = _dot(uk, bbim_ref[k])

        def tile(r, carry):
            sl = pl.ds(pl.multiple_of(r * SUB, SUB), SUB)
            a, b = _s5_tile_scan(xre_ref[sl, :], xim_ref[sl, :], c_ref, carry, False)
            xre_ref[sl, :] = a
            xim_ref[sl, :] = b
            return a[SUB - 1:SUB, :], b[SUB - 1:SUB, :]

        cr, ci = lax.fori_loop(0, nt, tile, (car[0:1, :], car[1:2, :]))
        car[0:1, :] = cr
        car[1:2, :] = ci
        ys = _s5_readout(xre_ref, xim_ref, ccre_ref, ccim_ref) + d_ref[...] * u
        yg = _gelu(ys)
        o_ref[...] = (yg * _sigmoid(_dot(yg, w_ref[...]) + b_ref[...])).astype(o_ref.dtype)

    row = pl.BlockSpec((tb, GW), lambda i: (i, 0))
    xrow = pl.BlockSpec((tb, S5_N), lambda i: (i, 0))
    vec = _full((1, GW))
    return pl.pallas_call(
        body, name="s5_fwd", grid=(t // tb,),
        in_specs=[_cols(tb, GW, cb), _full((4, 128, 512)), _full((4, 128, 512)), _full((4, 512, 128)), _full((4, 512, 128)),
                  vec, _full((8, SUB, S5_N)), _full((GW, GW)), vec],
        out_specs=[row, xrow, xrow],
        out_shape=[jax.ShapeDtypeStruct((t, GW), MXU_DTYPE), jax.ShapeDtypeStruct((t, S5_N), F32), jax.ShapeDtypeStruct((t, S5_N), F32)],
        scratch_shapes=[pltpu.VMEM((SUB, S5_N), F32)], compiler_params=_params(("arbitrary",)))(
            proj, bbre, bbim, ccre, ccim, dvec, consts, glu_w, glu_b)


def _s5_bwd(dmixed, proj, xre, xim, bbre, bbim, ccre, ccim, dvec, consts, glu_w, glu_b, *, cb_dy, cb, tb=256):
    t = proj.shape[0]
    tb = min(tb, t)
    nb = t // tb
    nt = tb // SUB

    def body(dy_ref, u_ref, xre_ref, xim_ref, bbre_ref, bbim_ref, ccre_ref, ccim_ref, d_ref, c_ref, w_ref, b_ref,
             du_ref, dw_ref, db_ref, dd_ref, dccre_ref, dccim_ref, dbbre_ref, dbbim_ref, sre_ref, sim_ref, are, aim, car):
        @pl.when(pl.program_id(0) == 0)
        def _():
            car[...] = jnp.zeros_like(car)
            for r in (dw_ref, db_ref, dd_ref, dccre_ref, dccim_ref, dbbre_ref, dbbim_ref, sre_ref, sim_ref):
                r[...] = jnp.zeros_like(r)

        u, dy = u_ref[...], dy_ref[...]
        ys = _s5_readout(xre_ref, xim_ref, ccre_ref, ccim_ref) + d_ref[...] * u
        yg = _gelu(ys)
        sg = _sigmoid(_dot(yg, w_ref[...]) + b_ref[...])
        dz = dy * yg * sg * (1.0 - sg)
        dyg = dy * sg + _dot(dz, w_ref[...], NT)
        dw_ref[...] += _dot(yg, dz, TN)
        db_ref[...] += jnp.sum(dz, axis=0, keepdims=True)
        dys = dyg * _gelu_grad(ys)
        dd_ref[...] += jnp.sum(dys * u, axis=0, keepdims=True)
        for k in range(S5_KB):
            dk = dys[:, 128 * k:128 * (k + 1)]
            lanes = pl.ds(512 * k, 512)
            are[:, lanes] = _dot(dk, ccre_ref[k], NT)
            aim[:, lanes] = -_dot(dk, ccim_ref[k], NT)
            dccre_ref[k] += _dot(xre_ref[:, lanes], dk, TN)
            dccim_ref[k] -= _dot(xim_ref[:, lanes], dk, TN)

        def tile(j, carry):
            sl = pl.ds(pl.multiple_of((nt - 1 - j) * SUB, SUB), SUB)
            gr, gi = are[sl, :], aim[sl, :]
            a, b = _s5_tile_scan(gr, gi, c_ref, carry, True)
            are[sl, :] = a
            aim[sl, :] = b
            er, ei = a - gr, b - gi
            xr, xi = xre_ref[sl, :], xim_ref[sl, :]
            sre_ref[...] += xr * er + xi * ei
            sim_ref[...] += xr * ei - xi * er
            return a[0:1, :], b[0:1, :]

        cr, ci = lax.fori_loop(0, nt, tile, (car[0:1, :], car[1:2, :]))
        car[0:1, :] = cr
        car[1:2, :] = ci
        dus = []
        for k in range(S5_KB):
            uk = u[:, 128 * k:128 * (k + 1)]
            lanes = pl.ds(512 * k, 512)
            dbbre_ref[k] += _dot(uk, are[:, lanes], TN)
            dbbim_ref[k] += _dot(uk, aim[:, lanes], TN)
            dus.append(_dot(are[:, lanes], bbre_ref[k], NT) + _dot(aim[:, lanes], bbim_ref[k], NT))
        du_ref[...] = (d_ref[...] * dys + jnp.concatenate(dus, axis=1)).astype(du_ref.dtype)

    row = pl.BlockSpec((tb, GW), lambda i: (nb - 1 - i, 0))
    xrow = pl.BlockSpec((tb, S5_N), lambda i: (nb - 1 - i, 0))
    vec = _full((1, GW))
    bbs, ccs = _full((4, 128, 512)), _full((4, 512, 128))
    sds = jax.ShapeDtypeStruct
    return pl.pallas_call(
        body, name="s5_bwd", grid=(nb,),
        in_specs=[_cols_rev(tb, GW, cb_dy, nb), _cols_rev(tb, GW, cb, nb), xrow, xrow, bbs, bbs, ccs, ccs, vec,
                  _full((8, SUB, S5_N)), _full((GW, GW)), vec],
        out_specs=[row, _full((GW, GW)), vec, vec, ccs, ccs, bbs, bbs, _full((SUB, S5_N)), _full((SUB, S5_N))],
        out_shape=[sds((t, GW), MXU_DTYPE), sds((GW, GW), F32), sds((1, GW), F32), sds((1, GW), F32), sds((4, 512, 128), F32),
                   sds((4, 512, 128), F32), sds((4, 128, 512), F32), sds((4, 128, 512), F32), sds((SUB, S5_N), F32),
                   sds((SUB, S5_N), F32)],
        scratch_shapes=[pltpu.VMEM((tb, S5_N), F32), pltpu.VMEM((tb, S5_N), F32), pltpu.VMEM((SUB, S5_N), F32)],
        compiler_params=_params(("arbitrary",)))(dmixed, proj, xre, xim, bbre, bbim, ccre, ccim, dvec, consts, glu_w, glu_b)


def _s5_prepare(lam_re, lam_im, log_step, b_re, b_im, c_re, c_im):
    lr, li, bbr, bbi = _s5_discretize(lam_re, lam_im, log_step, b_re, b_im)
    return dict(bbre=_s5_blockdiag_in(bbr), bbim=_s5_blockdiag_in(bbi), ccre=_s5_blockdiag_out(c_re), ccim=_s5_blockdiag_out(c_im),
                cf=_s5_scan_consts(lr, li, False), cr=_s5_scan_consts(lr, li, True))


def _s5_param_grads(lam_re, lam_im, log_step, b_re, b_im, dbbre, dbbim, dccre, dccim, sre, sim):
    (lr, li, _, _), vjp = jax.vjp(_s5_discretize, lam_re, lam_im, log_step, b_re, b_im)
    sr, si = jnp.sum(sre, axis=0).reshape(S5_G, S5_P), jnp.sum(sim, axis=0).reshape(S5_G, S5_P)
    den = lr * lr + li * li
    glr, gli = (sr * lr - si * li) / den, (si * lr + sr * li) / den
    dlam_re, dlam_im, dlog_step, db_re, db_im = vjp((glr, gli, _s5_blockdiag_in_extract(dbbre), _s5_blockdiag_in_extract(dbbim)))
    return dlam_re, dlam_im, dlog_step, db_re, db_im, _s5_blockdiag_out_extract(dccre), _s5_blockdiag_out_extract(dccim)


HALO = 8
AB_CB = 4096 // 128
Q_SCALE = DN_HD ** -0.5


def _halo_prev(tb, width, cb):
    return pl.BlockSpec((HALO, width), lambda i: (jnp.maximum(i * (tb // HALO) - 1, 0), cb))


def _halo_next(tb, width, cb, nrows):
    last = nrows // HALO - 1
    return pl.BlockSpec((HALO, width), lambda i: (jnp.minimum((i + 1) * (tb // HALO), last), cb))


def _silu_parts(c):
    sg = _sigmoid(c)
    return c * sg, sg * (1.0 + c * (1.0 - sg))


def _softplus(x):
    return jnp.maximum(x, 0.0) + jnp.log(1.0 + jnp.exp(-jnp.abs(x)))


def _dn_conv(x_ref, halo_ref, w_ref, part, ext, first):
    tb = x_ref.shape[0]
    ext[pl.ds(0, HALO), :] = jnp.where(first, 0.0, halo_ref[...])
    ext[pl.ds(HALO, tb), :] = x_ref[...]
    c = None
    for j in range(DN_CONV):
        term = w_ref[pl.ds(j, 1), pl.ds(512 * part, 512)] * ext[pl.ds(HALO - (DN_CONV - 1) + j, tb), :]
        c = term if c is None else c + term
    return c


def _dn_gb(ab, alog, dtb):
    lane = lax.broadcasted_iota(jnp.int32, ab.shape, 1)
    pre = ab + dtb
    g = -jnp.exp(alog) * _softplus(pre)
    beta = _sigmoid(ab)
    return jnp.where(lane < DN_HEADS, g, jnp.where(lane < 2 * DN_HEADS, beta, 0.0)), pre, beta


def _dn_prep_fwd(proj, conv_w, alog, dtb, *, cbq, tb=512):
    t = proj.shape[0]
    tb = min(tb, t)

    def body(xq, xk, xv, hq, hk, hv, ab_ref, w_ref, alog_ref, dtb_ref, qn_ref, kn_ref, vs_ref, gb_ref, ext):
        first = pl.program_id(0) == 0
        for part, (x_ref, h_ref, o_ref) in enumerate(((xq, hq, qn_ref), (xk, hk, kn_ref), (xv, hv, vs_ref))):
            s, _ = _silu_parts(_dn_conv(x_ref, h_ref, w_ref, part, ext, first))
            if part < 2:
                scale = Q_SCALE if part == 0 else 1.0
                for h in range(DN_HEADS):
                    sh = s[:, DN_HD * h:DN_HD * (h + 1)]
                    rn = lax.rsqrt(jnp.sum(sh * sh, axis=-1, keepdims=True) + L2_EPS)
                    o_ref[:, pl.ds(DN_HD * h, DN_HD)] = sh * (rn * scale)
            else:
                o_ref[...] = s
        gb_ref[...] = _dn_gb(ab_ref[...], alog_ref[...], dtb_ref[...])[0]

    row = pl.BlockSpec((tb, GW), lambda i: (i, 0))
    small = pl.BlockSpec((tb, 128), lambda i: (i, 0))
    v128 = _full((1, 128))
    sds = jax.ShapeDtypeStruct
    return pl.pallas_call(
        body, name="dn_prep_fwd", grid=(t // tb,),
        in_specs=[_cols(tb, GW, cbq), _cols(tb, GW, cbq + 1), _cols(tb, GW, cbq + 2),
                  _halo_prev(tb, GW, cbq), _halo_prev(tb, GW, cbq + 1), _halo_prev(tb, GW, cbq + 2),
                  _cols(tb, 128, AB_CB), _full((DN_CONV, 3 * GW)), v128, v128],
        out_specs=[row, row, row, small],
        out_shape=[sds((t, GW), F32)] * 3 + [sds((t, 128), F32)],
        scratch_shapes=[pltpu.VMEM((tb + HALO, GW), F32)], compiler_params=_params(("parallel",)))(
            proj, proj, proj, proj, proj, proj, proj, conv_w, alog, dtb)


def _dn_prep_bwd_a(proj, conv_w, alog, dtb, dqn, dkn, dvs, dgb, *, cbq, tb=512):
    t = proj.shape[0]
    tb = min(tb, t)

    def body(xq, xk, xv, hq, hk, hv, ab_ref, w_ref, alog_ref, dtb_ref, dqn_ref, dkn_ref, dvs_ref, dgb_ref,
             dcq_ref, dck_ref, dcv_ref, dab_ref, dalog_ref, ddtb_ref, ext):
        first = pl.program_id(0) == 0

        @pl.when(first)
        def _():
            dalog_ref[...] = jnp.zeros_like(dalog_ref)
            ddtb_ref[...] = jnp.zeros_like(ddtb_ref)

        for part, (x_ref, h_ref, d_ref, o_ref) in enumerate(((xq, hq, dqn_ref, dcq_ref), (xk, hk, dkn_ref, dck_ref), (xv, hv, dvs_ref, dcv_ref))):
            s, ds_dc = _silu_parts(_dn_conv(x_ref, h_ref, w_ref, part, ext, first))
            d = d_ref[...]
            if part < 2:
                scale = Q_SCALE if part == 0 else 1.0
                for h in range(DN_HEADS):
                    lanes = slice(DN_HD * h, DN_HD * (h + 1))
                    sh, dh = s[:, lanes], d[:, lanes]
                    rn = lax.rsqrt(jnp.sum(sh * sh, axis=-1, keepdims=True) + L2_EPS)
                    dsh = scale * (rn * dh - sh * (rn * rn * rn) * jnp.sum(dh * sh, axis=-1, keepdims=True))
                    o_ref[:, pl.ds(DN_HD * h, DN_HD)] = dsh * ds_dc[:, lanes]
            else:
                o_ref[...] = d * ds_dc
        ab, dgb_v = ab_ref[...], dgb_ref[...]
        gb, pre, beta = _dn_gb(ab, alog_ref[...], dtb_ref[...])
        lane = lax.broadcasted_iota(jnp.int32, ab.shape, 1)
        is_g = lane < DN_HEADS
        da = jnp.where(is_g, dgb_v * (-jnp.exp(alog_ref[...])) * _sigmoid(pre), 0.0)
        db = jnp.where((lane >= DN_HEADS) & (lane < 2 * DN_HEADS), dgb_v * beta * (1.0 - beta), 0.0)
        dab_ref[...] = (da + db).astype(dab_ref.dtype)
        ddtb_ref[...] += jnp.sum(da, axis=0, keepdims=True)
        dalog_ref[...] += jnp.sum(jnp.where(is_g, dgb_v * gb, 0.0), axis=0, keepdims=True)

    row = pl.BlockSpec((tb, GW), lambda i: (i, 0))
    small = pl.BlockSpec((tb, 128), lambda i: (i, 0))
    v128 = _full((1, 128))
    sds = jax.ShapeDtypeStruct
    return pl.pallas_call(
        body, name="dn_prep_bwd_a", grid=(t // tb,),
        in_specs=[_cols(tb, GW, cbq), _cols(tb, GW, cbq + 1), _cols(tb, GW, cbq + 2),
                  _halo_prev(tb, GW, cbq), _halo_prev(tb, GW, cbq + 1), _halo_prev(tb, GW, cbq + 2),
                  _cols(tb, 128, AB_CB), _full((DN_CONV, 3 * GW)), v128, v128, row, row, row, small],
        out_specs=[row, row, row, small, v128, v128],
        out_shape=[sds((t, GW), F32)] * 3 + [sds((t, 128), MXU_DTYPE), sds((1, 128), F32), sds((1, 128), F32)],
        scratch_shapes=[pltpu.VMEM((tb + HALO, GW), F32)], compiler_params=_params(("arbitrary",)))(
            proj, proj, proj, proj, proj, proj, proj, conv_w, alog, dtb, dqn, dkn, dvs, dgb)


def _dn_prep_bwd_b(proj, conv_w, dcq, dck, dcv, *, cbq, tb=512):
    t = proj.shape[0]
    tb = min(tb, t)
    nb = t // tb

    def body(xq, xk, xv, hq, hk, hv, dq_in, dk_in, dv_in, nq, nk, nv, w_ref, dq_ref, dk_ref, dv_ref, dw_ref, ext):
        i = pl.program_id(0)
        first, last = i == 0, i == nb - 1

        @pl.when(first)
        def _():
            dw_ref[...] = jnp.zeros_like(dw_ref)

        for part, (x_ref, h_ref, d_ref, n_ref, o_ref) in enumerate(
                ((xq, hq, dq_in, nq, dq_ref), (xk, hk, dk_in, nk, dk_ref), (xv, hv, dv_in, nv, dv_ref))):
            lanes = pl.ds(512 * part, 512)
            d = d_ref[...]
            ext[pl.ds(0, HALO), :] = jnp.where(first, 0.0, h_ref[...])
            ext[pl.ds(HALO, tb), :] = x_ref[...]
            for j in range(DN_CONV):
                xs = ext[pl.ds(HALO - (DN_CONV - 1) + j, tb), :]
                dw_ref[pl.ds(j, 1), lanes] += jnp.sum(d * xs, axis=0, keepdims=True)
            ext[pl.ds(0, tb), :] = d
            ext[pl.ds(tb, HALO), :] = jnp.where(last, 0.0, n_ref[...])
            acc = None
            for j in range(DN_CONV):
                term = w_ref[pl.ds(j, 1), lanes] * ext[pl.ds(DN_CONV - 1 - j, tb), :]
                acc = term if acc is None else acc + term
            o_ref[...] = acc.astype(o_ref.dtype)

    row = pl.BlockSpec((tb, GW), lambda i: (i, 0))
    nxt = _halo_next(tb, GW, 0, t)
    sds = jax.ShapeDtypeStruct
    return pl.pallas_call(
        body, name="dn_prep_bwd_b", grid=(nb,),
        in_specs=[_cols(tb, GW, cbq), _cols(tb, GW, cbq + 1), _cols(tb, GW, cbq + 2),
                  _halo_prev(tb, GW, cbq), _halo_prev(tb, GW, cbq + 1), _halo_prev(tb, GW, cbq + 2),
                  row, row, row, nxt, nxt, nxt, _full((DN_CONV, 3 * GW))],
        out_specs=[row, row, row, _full((DN_CONV, 3 * GW))],
        out_shape=[sds((t, GW), MXU_DTYPE)] * 3 + [sds((DN_CONV, 3 * GW), F32)],
        scratch_shapes=[pltpu.VMEM((tb + HALO, GW), F32)], compiler_params=_params(("arbitrary",)))(
            proj, proj, proj, proj, proj, proj, dcq, dck, dcv, dcq, dck, dcv, conv_w)


CAT = DN_HEADS * DN_CHUNK
DN_LOCAL_CHUNKS = 2


def _iota_div(shape, axis, width):
    return jnp.right_shift(lax.broadcasted_iota(jnp.int32, shape, axis), width.bit_length() - 1)


def _dn_masks():
    r = lax.broadcasted_iota(jnp.int32, (DN_CHUNK, CAT), 0)
    c = jnp.bitwise_and(lax.broadcasted_iota(jnp.int32, (DN_CHUNK, CAT), 1), DN_CHUNK - 1)
    wide = _iota_div((CAT, GW), 0, DN_CHUNK) == _iota_div((CAT, GW), 1, DN_HD)
    square = _iota_div((CAT, CAT), 0, DN_CHUNK) == _iota_div((CAT, CAT), 1, DN_CHUNK)
    return dict(eye=r == c, tril=r >= c, strict=r > c, triu=r <= c, wide=wide, square=square)


def _stack4(x):
    return jnp.concatenate([x, x, x, x], axis=0)


def _diag_blocks(x, mask):
    return jnp.where(mask, _stack4(x), 0.0)


def _fold_blocks(x, mask):
    x = jnp.where(mask, x, 0.0)
    return x[0:64] + x[64:128] + x[128:192] + x[192:256]


def _expand(cols, base, width):
    head = _iota_div((cols.shape[0], DN_HEADS * width), 1, width)
    out = jnp.zeros((cols.shape[0], DN_HEADS * width), F32)
    for h in range(DN_HEADS):
        out = jnp.where(head == h, cols[:, base + h:base + h + 1], out)
    return out


def _head_sums(x, width):
    if width == DN_HD:
        return [jnp.sum(x[:, DN_HD * h:DN_HD * (h + 1)], axis=1, keepdims=True) for h in range(DN_HEADS)]
    head = _iota_div(x.shape, 1, width)
    return [jnp.sum(jnp.where(head == h, x, 0.0), axis=1, keepdims=True) for h in range(DN_HEADS)]


def _cumsum_rows(x):
    row = lax.broadcasted_iota(jnp.int32, x.shape, 0)
    for s in (1, 2, 4, 8, 16, 32):
        x = x + jnp.where(row >= s, pltpu.roll(x, s, 0), 0.0)
    return x


def _tri_inv(n, square):
    col = jnp.bitwise_and(lax.broadcasted_iota(jnp.int32, n.shape, 1), DN_CHUNK - 1)
    x = jnp.where(lax.broadcasted_iota(jnp.int32, n.shape, 0) == col, 1.0, 0.0) - n
    p = _dot(n, _diag_blocks(n, square), exact=True)
    for _ in range(4):
        pd = _diag_blocks(p, square)
        x = x + _dot(x, pd, exact=True)
        p = _dot(p, pd, exact=True)
    return x + _dot(x, _diag_blocks(p, square), exact=True)


def _dn_local_math(q, k, v, gbv, m, tm=None):
    gc = _cumsum_rows(gbv)
    gc_cat, gc_wide = _expand(gc, 0, DN_CHUNK), _expand(gc, 0, DN_HD)
    gc_row = jnp.sum(jnp.where(m["eye"], gc_cat, 0.0), axis=0, keepdims=True)
    decay = jnp.where(m["tril"], jnp.exp(jnp.minimum(gc_cat - gc_row, 0.0)), 0.0)
    eg = jnp.exp(gc_wide)
    gl = _expand(gc[DN_CHUNK - 1:DN_CHUNK, :], 0, DN_HD)
    etail = jnp.exp(gl - gc_wide)
    beta = _expand(gbv, DN_HEADS, DN_HD)
    kb, vb = k * beta, v * beta
    k_rows = _diag_blocks(k, m["wide"])
    kk = _dot(kb, k_rows, NT)
    if tm is None:
        tm = _tri_inv(jnp.where(m["strict"], kk * decay, 0.0), m["square"])
    return dict(decay=decay, eg=eg, eg_last=jnp.exp(gl), etail=etail, beta=beta, kb=kb, vb=vb, kk=kk, tm=tm, kbg=kb * eg,
                qk=_dot(q, k_rows, NT), qg=q * eg, ktail=k * etail, k_rows=k_rows)


def _dn_local_fwd(qn, kn, vs, gb):
    t = qn.shape[0]
    rows = DN_CHUNK * DN_LOCAL_CHUNKS

    def body(q_ref, k_ref, v_ref, gb_ref, u_ref, wm_ref, qg_ref, kt_ref, qkd_ref, tm_ref):
        masks = _dn_masks()
        for n in range(DN_LOCAL_CHUNKS):
            rs = pl.ds(DN_CHUNK * n, DN_CHUNK)
            m = _dn_local_math(q_ref[rs, :], k_ref[rs, :], v_ref[rs, :], gb_ref[rs, :], masks)
            u_ref[rs, :] = _dot(m["tm"], _diag_blocks(m["vb"], masks["wide"]))
            wm_ref[rs, :] = _dot(m["tm"], _diag_blocks(m["kbg"], masks["wide"])).astype(wm_ref.dtype)
            qg_ref[rs, :] = m["qg"].astype(qg_ref.dtype)
            kt_ref[rs, :] = m["ktail"].astype(kt_ref.dtype)
            qkd = (m["qk"] * m["decay"]).astype(qkd_ref.dtype)
            for h in range(DN_HEADS):
                qkd_ref[rs, pl.ds(DN_HD * h, DN_CHUNK)] = qkd[:, DN_CHUNK * h:DN_CHUNK * (h + 1)]
            tm_ref[rs, :] = m["tm"]

    row = pl.BlockSpec((rows, GW), lambda i: (i, 0))
    sds = jax.ShapeDtypeStruct
    return pl.pallas_call(
        body, name="dn_local_fwd", grid=(t // rows,), in_specs=[row, row, row, pl.BlockSpec((rows, 128), lambda i: (i, 0))],
        out_specs=[row] * 5 + [pl.BlockSpec((rows, CAT), lambda i: (i, 0))],
        out_shape=[sds((t, GW), F32)] + [sds((t, GW), MXU_DTYPE)] * 4 + [sds((t, CAT), F32)],
        compiler_params=_params(("parallel",)))(qn, kn, vs, gb)


def _dn_eg_last(gbv, h):
    return jnp.exp(jnp.sum(gbv[:, h:h + 1], axis=0, keepdims=True))


def _dn_seq_fwd(u, wm, qg, ktail, qkd, gb, proj, norm_g, *, cb_gate):
    t = u.shape[0]
    nc = t // DN_CHUNK

    def body(u_ref, wm_ref, qg_ref, kt_ref, qkd_ref, gb_ref, gate_ref, ng_ref, o_ref, raw_ref, vn_ref, st_ref, s_ref):
        @pl.when(pl.program_id(0) == 0)
        def _():
            s_ref[...] = jnp.zeros_like(s_ref)

        gbv = gb_ref[...]
        for h in range(DN_HEADS):
            lanes = pl.ds(DN_HD * h, DN_HD)
            s = s_ref[h]
            st_ref[0, h] = s
            v_new = u_ref[:, lanes] - _dot(wm_ref[:, lanes], s)
            s_ref[h] = s * _dn_eg_last(gbv, h) + _dot(kt_ref[:, lanes], v_new, TN)
            o = _dot(qg_ref[:, lanes], s) + _dot(qkd_ref[:, pl.ds(DN_HD * h, DN_CHUNK)], v_new)
            vn_ref[:, lanes] = v_new.astype(vn_ref.dtype)
            raw_ref[:, lanes] = o
            r = lax.rsqrt(jnp.mean(o * o, axis=-1, keepdims=True) + RMS_EPS)
            gt = gate_ref[:, lanes]
            o_ref[:, lanes] = (o * r * ng_ref[...] * (gt * _sigmoid(gt))).astype(o_ref.dtype)

    row = pl.BlockSpec((DN_CHUNK, GW), lambda i: (i, 0))
    sds = jax.ShapeDtypeStruct
    return pl.pallas_call(
        body, name="dn_seq_fwd", grid=(nc,),
        in_specs=[row] * 5 + [pl.BlockSpec((DN_CHUNK, 128), lambda i: (i, 0)), _cols(DN_CHUNK, GW, cb_gate), _full((1, DN_HD))],
        out_specs=[row, row, row, pl.BlockSpec((1, DN_HEADS, DN_HD, DN_HD), lambda i: (i, 0, 0, 0))],
        out_shape=[sds((t, GW), MXU_DTYPE), sds((t, GW), F32), sds((t, GW), MXU_DTYPE), sds((nc, DN_HEADS, DN_HD, DN_HD), F32)],
        scratch_shapes=[pltpu.VMEM((DN_HEADS, DN_HD, DN_HD), F32)], compiler_params=_params(("arbitrary",)))(
            u, wm, qg, ktail, qkd, gb, proj, norm_g)


def _dn_seq_bwd(dmixed, raw, proj, norm_g, gb, wm, qg, ktail, qkd, v_new, states, *, cb_dy, cb_gate):
    t = raw.shape[0]
    nc = t // DN_CHUNK

    def body(dy_ref, raw_ref, gate_ref, ng_ref, gb_ref, wm_ref, qg_ref, kt_ref, qkd_ref, vn_ref, st_ref,
             dgate_ref, dng_ref, do_ref, dvn_ref, dwm_ref, dqg_ref, dkt_ref, degl_ref, ds_ref):
        @pl.when(pl.program_id(0) == 0)
        def _():
            ds_ref[...] = jnp.zeros_like(ds_ref)
            dng_ref[...] = jnp.zeros_like(dng_ref)

        gbv = gb_ref[...]
        lane = lax.broadcasted_iota(jnp.int32, (1, 128), 1)
        degl = jnp.zeros((1, 128), F32)
        for h in range(DN_HEADS):
            lanes = pl.ds(DN_HD * h, DN_HD)
            s, ds_out = st_ref[0, h], ds_ref[h]
            o, gt, dy, ng = raw_ref[:, lanes], gate_ref[:, lanes], dy_ref[:, lanes], ng_ref[...]
            r = lax.rsqrt(jnp.mean(o * o, axis=-1, keepdims=True) + RMS_EPS)
            sil, dsil = _silu_parts(gt)
            d_on = dy * sil
            dgate_ref[:, lanes] = (dy * (o * r * ng) * dsil).astype(dgate_ref.dtype)
            dng_ref[...] += jnp.sum(d_on * o * r, axis=0, keepdims=True)
            w = d_on * ng
            do = r * w - o * (r * r * r) * jnp.mean(w * o, axis=-1, keepdims=True)
            d_vnew = _dot(qkd_ref[:, pl.ds(DN_HD * h, DN_CHUNK)], do, TN) + _dot(kt_ref[:, lanes], ds_out)
            ds_ref[h] = _dot(qg_ref[:, lanes], do, TN) + _dn_eg_last(gbv, h) * ds_out - _dot(wm_ref[:, lanes], d_vnew, TN)
            do_ref[:, lanes] = do.astype(do_ref.dtype)
            dvn_ref[:, lanes] = d_vnew.astype(dvn_ref.dtype)
            dwm_ref[:, lanes] = (-_dot(d_vnew, s, NT)).astype(dwm_ref.dtype)
            dqg_ref[:, lanes] = _dot(do, s, NT)
            dkt_ref[:, lanes] = _dot(vn_ref[:, lanes], ds_out, NT)
            d_eglast = jnp.sum(jnp.sum(s * ds_out, axis=1, keepdims=True), axis=0, keepdims=True)
            degl = degl + jnp.where(lane == h, d_eglast, 0.0)
        degl_ref[0] = degl

    row = pl.BlockSpec((DN_CHUNK, GW), lambda i: (nc - 1 - i, 0))
    small = pl.BlockSpec((DN_CHUNK, 128), lambda i: (nc - 1 - i, 0))
    sds = jax.ShapeDtypeStruct
    return pl.pallas_call(
        body, name="dn_seq_bwd", grid=(nc,),
        in_specs=[_cols_rev(DN_CHUNK, GW, cb_dy, nc), row, _cols_rev(DN_CHUNK, GW, cb_gate, nc), _full((1, DN_HD)), small,
                  row, row, row, row, row, pl.BlockSpec((1, DN_HEADS, DN_HD, DN_HD), lambda i: (nc - 1 - i, 0, 0, 0))],
        out_specs=[row, _full((1, DN_HD)), row, row, row, row, row, pl.BlockSpec((1, 1, 128), lambda i: (nc - 1 - i, 0, 0))],
        out_shape=[sds((t, GW), MXU_DTYPE), sds((1, DN_HD), F32), sds((t, GW), MXU_DTYPE), sds((t, GW), MXU_DTYPE),
                   sds((t, GW), MXU_DTYPE), sds((t, GW), F32), sds((t, GW), F32), sds((nc, 1, 128), F32)],
        scratch_shapes=[pltpu.VMEM((DN_HEADS, DN_HD, DN_HD), F32)], compiler_params=_params(("arbitrary",)))(
            dmixed, raw, proj, norm_g, gb, wm, qg, ktail, qkd, v_new, states)


def _dn_local_bwd(qn, kn, vs, gb, tm, v_new, do, d_vnew, d_wm, d_qg, d_ktail, d_eglast):
    t = qn.shape[0]
    rows = DN_CHUNK * DN_LOCAL_CHUNKS

    def body(q_ref, k_ref, v_ref, gb_ref, tm_ref, vn_ref, do_ref, dvn_ref, dwm_ref, dqg_ref, dkt_ref, degl_ref,
             dq_ref, dk_ref, dv_ref, dgb_ref):
        masks = _dn_masks()
        wide, square = masks["wide"], masks["square"]
        lane = lax.broadcasted_iota(jnp.int32, (DN_CHUNK, 128), 1)
        last_row = lax.broadcasted_iota(jnp.int32, (DN_CHUNK, 1), 0) == DN_CHUNK - 1
        for n in range(DN_LOCAL_CHUNKS):
            rs = pl.ds(DN_CHUNK * n, DN_CHUNK)
            q, k, v, tm = q_ref[rs, :], k_ref[rs, :], v_ref[rs, :], tm_ref[rs, :]
            m = _dn_local_math(q, k, v, gb_ref[rs, :], masks, tm=tm)
            decay, eg, k_rows = m["decay"], m["eg"], m["k_rows"]
            d_vnew, d_wm, d_qg, d_ktail = dvn_ref[rs, :], dwm_ref[rs, :], dqg_ref[rs, :], dkt_ref[rs, :]
            deglv = degl_ref[n]
            dq = d_qg * eg
            dk = d_ktail * m["etail"]
            tails = _head_sums(d_ktail * m["ktail"], DN_HD)
            dgcs = _head_sums(d_qg * m["qg"], DN_HD)
            d_qkd = jnp.where(masks["tril"], _dot(do_ref[rs, :], _diag_blocks(vn_ref[rs, :], wide), NT), 0.0)
            dqk_dec = d_qkd * decay
            dq = dq + _dot(dqk_dec, k_rows)
            dk = dk + _fold_blocks(_dot(dqk_dec, q, TN), wide)
            ddecay = d_qkd * m["qk"]
            d_tm = _dot(d_vnew, _diag_blocks(m["vb"], wide), NT) + _dot(d_wm, _diag_blocks(m["kbg"], wide), NT)
            d_vb = _fold_blocks(_dot(tm, d_vnew, TN), wide)
            d_kbg = _fold_blocks(_dot(tm, d_wm, TN), wide)
            d_kb = d_kbg * eg
            kbgs = _head_sums(d_kbg * m["kbg"], DN_HD)
            x = _fold_blocks(_dot(tm, d_tm, TN, exact=True), square)
            d_n = jnp.where(masks["strict"], -_dot(x, _diag_blocks(tm, square), NT, exact=True), 0.0)
            d_kk = d_n * decay
            d_kb = d_kb + _dot(d_kk, k_rows)
            dk = dk + _fold_blocks(_dot(d_kk, m["kb"], TN), wide)
            ddecay = ddecay + d_n * m["kk"]
            dk = dk + d_kb * m["beta"]
            dbetas = [a + b for a, b in zip(_head_sums(d_kb * k, DN_HD), _head_sums(d_vb * v, DN_HD))]
            dv_ref[rs, :] = d_vb * m["beta"]
            dq_ref[rs, :] = dq
            dk_ref[rs, :] = dk
            dd = ddecay * decay
            row_sums = _head_sums(dd, DN_CHUNK)
            dgc_cols = jnp.zeros((DN_CHUNK, 128), F32)
            for h in range(DN_HEADS):
                dgl = jnp.sum(tails[h], axis=0, keepdims=True) + deglv[:, h:h + 1] * m["eg_last"][:, DN_HD * h:DN_HD * h + 1]
                dgc_cols = jnp.where(lane == h, dgcs[h] - tails[h] + kbgs[h] + row_sums[h] + jnp.where(last_row, dgl, 0.0), dgc_cols)
            dgc_row = (jnp.sum(jnp.where(masks["eye"], _expand(dgc_cols, 0, DN_CHUNK), 0.0), axis=0, keepdims=True)
                       - jnp.sum(dd, axis=0, keepdims=True))
            dgs = _head_sums(jnp.where(masks["triu"], dgc_row, 0.0), DN_CHUNK)
            dgb = jnp.zeros((DN_CHUNK, 128), F32)
            for h in range(DN_HEADS):
                dgb = jnp.where(lane == h, dgs[h], jnp.where(lane == DN_HEADS + h, dbetas[h], dgb))
            dgb_ref[rs, :] = dgb

    row = pl.BlockSpec((rows, GW), lambda i: (i, 0))
    small = pl.BlockSpec((rows, 128), lambda i: (i, 0))
    sds = jax.ShapeDtypeStruct
    return pl.pallas_call(
        body, name="dn_local_bwd", grid=(t // rows,),
        in_specs=[row, row, row, small, pl.BlockSpec((rows, CAT), lambda i: (i, 0))] + [row] * 6
        + [pl.BlockSpec((DN_LOCAL_CHUNKS, 1, 128), lambda i: (i, 0, 0))],
        out_specs=[row, row, row, small], out_shape=[sds((t, GW), F32)] * 3 + [sds((t, 128), F32)],
        compiler_params=_params(("parallel",)))(qn, kn, vs, gb, tm, v_new, do, d_vnew, d_wm, d_qg, d_ktail, d_eglast)


ANY = pl.BlockSpec(memory_space=pl.ANY)
PAIR_SPLIT = 4


def _place():
    x, y, c = lax.axis_index("x"), lax.axis_index("y"), lax.axis_index("c")
    chips = [(1 - x, y), (x, 1 - y), (1 - x, 1 - y)]
    return x, y, c, chips


def _remote(src, dst, send_sem, recv_sem, to):
    return pltpu.make_async_remote_copy(src_ref=src, dst_ref=dst, send_sem=send_sem, recv_sem=recv_sem, device_id=to,
                                        device_id_type=MESH)


def _carry_allgather(arrs):
    n = len(arrs)

    def sends(ins, outs, send_sems, recv_sems):
        x, y, c, chips = _place()
        me = 2 * x + y
        out = []
        for a in range(n):
            half = ins[a].shape[0] // 2
            mine = pl.ds(c * half, half)
            out += [_remote(ins[a].at[mine], outs[a].at[me, mine], send_sems.at[6 * a + k], recv_sems.at[6 * a + k], (*chip, c))
                    for k, chip in enumerate(chips)]
        return out

    def start(*parts):
        for s in sends(*parts):
            s.start()

    def finish(ins, outs, send_sems, recv_sems):
        x, y, c, chips = _place()
        sibling = (x, y, 1 - c)
        fwds = []
        for a in range(n):
            half = ins[a].shape[0] // 2
            mine = pl.ds(c * half, half)
            for k, (cx, cy) in enumerate(chips):
                got = outs[a].at[2 * cx + cy, mine]
                _remote(got, got, send_sems.at[6 * a + k], recv_sems.at[6 * a + k], (cx, cy, c)).wait_recv()
                f = _remote(got, got, send_sems.at[6 * a + 3 + k], recv_sems.at[6 * a + 3 + k], sibling)
                f.start()
                fwds.append(f)
        for a in range(n):
            half = ins[a].shape[0] // 2
            other = pl.ds((1 - c) * half, half)
            for k, (cx, cy) in enumerate(chips):
                got = outs[a].at[2 * cx + cy, other]
                _remote(got, got, send_sems.at[6 * a + 3 + k], recv_sems.at[6 * a + 3 + k], sibling).wait_recv()
        for s in sends(ins, outs, send_sems, recv_sems) + fwds:
            s.wait_send()

    return _Carry(arrs, [jax.ShapeDtypeStruct((4,) + a.shape, a.dtype) for a in arrs], 6 * n, start, finish)


def _pair_exchange(gbs, *, name):
    n = len(gbs)

    def body(*refs):
        ins, got_refs = refs[:n], refs[n:2 * n]
        send_sems, recv_sems = refs[2 * n:]
        x, y, c, _ = _place()
        work = []
        for a in range(n):
            half = ins[a].shape[1] // 2
            piece = half // PAIR_SPLIT
            for r in range(PAIR_SPLIT):
                s = _remote(ins[a].at[:, pl.ds((1 - c) * half + r * piece, piece)], got_refs[a].at[:, pl.ds(r * piece, piece)],
                            send_sems.at[a, r], recv_sems.at[a, r], (x, y, 1 - c))
                s.start()
                work.append(s)
        for s in work:
            s.wait()

    return pl.pallas_call(
        body, name=name, in_specs=[ANY] * n, out_specs=[ANY] * n,
        out_shape=[jax.ShapeDtypeStruct((4, g.shape[1] // 2, g.shape[2]), g.dtype) for g in gbs],
        scratch_shapes=[pltpu.SemaphoreType.DMA((n, PAIR_SPLIT)), pltpu.SemaphoreType.DMA((n, PAIR_SPLIT))])(*gbs)


def _carry_chip_exchange(ps):
    n = len(ps)

    def copies(ins, outs, send_sems, recv_sems):
        x, y, c, chips = _place()
        return [_remote(ins[a].at[2 * cx + cy], outs[a].at[k], send_sems.at[3 * a + k], recv_sems.at[3 * a + k], (cx, cy, c))
                for a in range(n) for k, (cx, cy) in enumerate(chips)]

    def start(*parts):
        for s in copies(*parts):
            s.start()

    def finish(*parts):
        for s in copies(*parts):
            s.wait()

    return _Carry(ps, [jax.ShapeDtypeStruct((3,) + p.shape[1:], p.dtype) for p in ps], 3 * n, start, finish)


def _pair_join(bufs, *, name):
    n = len(bufs)

    def body(*refs):
        outs = refs[n:2 * n]
        send_sems, recv_sems = refs[2 * n:]
        x, y, c, _ = _place()
        work = []
        for a in range(n):
            s = _remote(outs[a].at[c], outs[a].at[c], send_sems.at[a], recv_sems.at[a], (x, y, 1 - c))
            s.start()
            work.append(s)
        for s in work:
            s.wait()

    return pl.pallas_call(
        body, name=name, in_specs=[ANY] * n, out_specs=[ANY] * n,
        out_shape=[jax.ShapeDtypeStruct(b.shape, b.dtype) for b in bufs], input_output_aliases={a: a for a in range(n)},
        scratch_shapes=[pltpu.SemaphoreType.DMA((n,)), pltpu.SemaphoreType.DMA((n,))])(*bufs)


def _pair_sum(gb, got, place, *, name, block_bytes=1 << 20):
    _, r, cols = gb.shape
    half = r // 2
    tr = _row_tile(half, cols, block_bytes)

    def body(place_ref, g_ref, got_ref, o_ref):
        o_ref[...] = g_ref[...] + got_ref[...]

    blk = pl.BlockSpec((None, tr, cols), lambda j, i, p: (j, i, 0))
    grid_spec = pltpu.PrefetchScalarGridSpec(
        num_scalar_prefetch=1, grid=(4, half // tr),
        in_specs=[pl.BlockSpec((None, None, tr, cols), lambda j, i, p: (j, p[0], i, 0)), blk], out_specs=blk)
    return pl.pallas_call(body, name=name, grid_spec=grid_spec, out_shape=jax.ShapeDtypeStruct((4, half, cols), F32),
                          compiler_params=_params(("parallel", "parallel")))(place, gb.reshape(4, 2, half, cols), got)


def _chip_sum(p, got, place, *, name, block_bytes=1 << 20):
    _, h, cols = p.shape
    tr = _row_tile(h, cols, block_bytes)

    def body(place_ref, p_ref, g0, g1, g2, o_ref):
        o_ref[...] = p_ref[...] + g0[...] + g1[...] + g2[...]

    def got_spec(k):
        return pl.BlockSpec((None, tr, cols), functools.partial(lambda i, pr, k: (k, i, 0), k=k))

    grid_spec = pltpu.PrefetchScalarGridSpec(
        num_scalar_prefetch=1, grid=(h // tr,),
        in_specs=[pl.BlockSpec((None, tr, cols), lambda i, pr: (pr[1], i, 0)), got_spec(0), got_spec(1), got_spec(2)],
        out_specs=pl.BlockSpec((None, tr, cols), lambda i, pr: (pr[0], i, 0)))
    return pl.pallas_call(body, name=name, grid_spec=grid_spec, out_shape=jax.ShapeDtypeStruct((2, h, cols), F32),
                          compiler_params=_params(("parallel",)))(place, p, got, got, got)


def _allgather_all(v):
    def body(v_ref, out_ref, send_sems, recv_sems, local_sem):
        x, y, c, chips = _place()
        me, sibling = (x, y, c), (x, y, 1 - c)

        def rows(px, py, pc):
            return out_ref.at[4 * px + 2 * py + pc]

        def copy(k, block, to, src=None):
            return _remote(rows(*block) if src is None else src, rows(*block), send_sems.at[k], recv_sems.at[k], to)

        mine = pltpu.make_async_copy(v_ref, rows(*me), local_sem)
        mine.start()
        first = [copy(0, me, sibling, src=v_ref)] + [copy(1 + j, me, (*chip, c), src=v_ref) for j, chip in enumerate(chips)]
        for cp in first:
            cp.start()
        passed = [copy(4 + j, (*chip, c), sibling) for j, chip in enumerate(chips)]
        for j, chip in enumerate(chips):
            copy(1 + j, (*chip, c), me).wait_recv()
            passed[j].start()
        copy(0, sibling, me).wait_recv()
        for j, chip in enumerate(chips):
            copy(4 + j, (*chip, 1 - c), me).wait_recv()
        for cp in first + passed:
            cp.wait_send()
        mine.wait()

    return pl.pallas_call(
        body, name="allgather_small", in_specs=[ANY], out_specs=ANY, out_shape=jax.ShapeDtypeStruct((8,) + v.shape, v.dtype),
        scratch_shapes=[pltpu.SemaphoreType.DMA((7,)), pltpu.SemaphoreType.DMA((7,)), pltpu.SemaphoreType.DMA],
        )(v)


def _row_tile(rows, cols, limit_bytes):
    for d in range(1, rows + 1):
        if rows % d == 0 and (rows // d) % 8 == 0 and (rows // d) * cols * 4 <= limit_bytes:
            return rows // d
    return rows


def _sum_kernel(parts, *, name, block_bytes=1 << 20):
    n = len(parts)
    rows, cols = parts[0][0].shape[1:] if isinstance(parts[0], tuple) else parts[0].shape
    tr = _row_tile(rows, cols, block_bytes)
    ins, specs = [], []
    for part in parts:
        if isinstance(part, tuple):
            ins.append(part[0])
            specs.append(pl.BlockSpec((None, tr, cols), functools.partial(lambda i, s: (s, i, 0), s=part[1])))
        else:
            ins.append(part)
            specs.append(pl.BlockSpec((tr, cols), lambda i: (i, 0)))

    def body(*refs):
        acc = refs[0][...]
        for r in refs[1:n]:
            acc = acc + r[...]
        refs[n][...] = acc

    return pl.pallas_call(body, name=name, grid=(rows // tr,), in_specs=specs, out_specs=pl.BlockSpec((tr, cols), lambda i: (i, 0)),
                          out_shape=jax.ShapeDtypeStruct((rows, cols), F32), compiler_params=_params(("parallel",)))(*ins)


def _adamw_math(w, gv, m, v):
    nm = ADAM_B1 * m + (1.0 - ADAM_B1) * gv
    nv = ADAM_B2 * v + (1.0 - ADAM_B2) * (gv * gv)
    m_hat = nm / (1.0 - ADAM_B1 ** ADAM_STEP)
    v_hat = nv / (1.0 - ADAM_B2 ** ADAM_STEP)
    return -ADAM_LR * (m_hat / (jnp.sqrt(v_hat) + ADAM_EPS) + ADAM_WD * w), nm, nv


def _adamw_layers(w, g0, g1, m, v, *, name, block_bytes=1 << 20):
    _, rows, cols = w.shape
    tr = _row_tile(rows, cols, block_bytes)

    def body(w_ref, g0_ref, g1_ref, m_ref, v_ref, g_ref, d_ref, nm_ref, nv_ref):
        gv = jnp.where(pl.program_id(0) == 0, g0_ref[...], g1_ref[...])
        g_ref[...] = gv
        d_ref[...], nm_ref[...], nv_ref[...] = _adamw_math(w_ref[...], gv, m_ref[...], v_ref[...])

    both = pl.BlockSpec((None, tr, cols), lambda l, i: (l, i, 0))
    specs = [both, pl.BlockSpec((tr, cols), lambda l, i: (i * (1 - l), 0)), pl.BlockSpec((tr, cols), lambda l, i: (i * l, 0)), both, both]
    return pl.pallas_call(body, name=name, grid=(2, rows // tr), in_specs=specs, out_specs=[both] * 4,
                          out_shape=[jax.ShapeDtypeStruct(w.shape, F32)] * 4, compiler_params=_params(("arbitrary", "arbitrary")))(
                              w, g0, g1, m, v)


def _adamw(w, g, m, v, *, name, block_bytes=1 << 20):
    rows, cols = w.shape
    tr = _row_tile(rows, cols, block_bytes)

    def body(w_ref, g_ref, m_ref, v_ref, d_ref, nm_ref, nv_ref):
        d_ref[...], nm_ref[...], nv_ref[...] = _adamw_math(w_ref[...], g_ref[...], m_ref[...], v_ref[...])

    spec = pl.BlockSpec((tr, cols), lambda i: (i, 0))
    return pl.pallas_call(body, name=name, grid=(rows // tr,), in_specs=[spec] * 4, out_specs=[spec] * 3,
                          out_shape=[jax.ShapeDtypeStruct((rows, cols), F32)] * 3, compiler_params=_params(("parallel",)))(w, g, m, v)


WEIGHTS = ['w_in', 's5_lambda_re', 's5_lambda_im', 's5_log_step', 's5_b_re', 's5_b_im', 's5_c_re', 's5_c_im', 's5_d', 's5_glu_w',
           's5_glu_b', 'sgu_norm_g', 'sgu_norm_b', 'sgu_w', 'sgu_b', 'pool_w', 'pool_scale', 'dn_conv_w', 'dn_a_log', 'dn_dt_bias',
           'dn_norm_g', 'w_out', 'ln1_g', 'ln1_b', 'w_up', 'w_down', 'ln2_g', 'ln2_b']
BIG = ['w_in', 's5_glu_w', 'w_out', 'w_up', 'w_down']
SMALL = [n for n in WEIGHTS if n not in BIG]
CB_S5, CB_SGU_U, CB_SGU_V, CB_POOL, CB_DN_Q, CB_DN_GATE = 0, 1, 2, 3, 4, 7
KT = 2048


def _pad_lanes(v, width=128):
    return jnp.zeros((1, width), F32).at[0, :v.shape[0]].set(v)


def _layer_consts(p):
    c = _s5_prepare(p['s5_lambda_re'], p['s5_lambda_im'], p['s5_log_step'], p['s5_b_re'], p['s5_b_im'], p['s5_c_re'], p['s5_c_im'])
    tril = jnp.tril(jnp.ones((SGU_CHUNK, SGU_CHUNK), bool))
    wm = jnp.where(tril, p['sgu_w'], 0.0)
    c.update(s5_d=p['s5_d'].reshape(1, GW), glu_b=p['s5_glu_b'].reshape(1, GW), sgu_ng=p['sgu_norm_g'].reshape(1, GW),
             sgu_nb=p['sgu_norm_b'].reshape(1, GW), sgu_w=wm, sgu_wt=jnp.swapaxes(wm, 1, 2),
             sgu_bias=jnp.repeat(p['sgu_b'].T, SGU_HD, axis=1), pool_w=p['pool_w'], pool_scale=p['pool_scale'].reshape(1, GW),
             conv_w=p['dn_conv_w'], alog=_pad_lanes(p['dn_a_log']), dtb=_pad_lanes(p['dn_dt_bias']), dn_ng=p['dn_norm_g'].reshape(1, DN_HD),
             ln1_g=p['ln1_g'].reshape(1, D_MODEL), ln1_b=p['ln1_b'].reshape(1, D_MODEL), ln2_g=p['ln2_g'].reshape(1, D_MODEL),
             ln2_b=p['ln2_b'].reshape(1, D_MODEL))
    return c


def _layer_fwd(xin, xin16, w, c, i, carries):
    tag = str(i)
    residual = lambda r, e: (r + ALPHA * e,)

    def mm(a, b_name, *, name, **kw):
        if name not in carries:
            return _matmul(a, w[b_name], name=name + tag, **kw)
        carry, done = carries[name]
        outs, extra = _matmul(a, w[b_name], name=name + tag, carry=carry, **kw)
        done(extra)
        return outs

    (proj,) = mm(xin16, 'w_in', mode="nn", name="proj", tn=1408, tk=KT)
    s5, xre, xim = _s5_fwd(proj, c['bbre'], c['bbim'], c['ccre'], c['ccim'], c['s5_d'], c['cf'], w['s5_glu_w'], c['glu_b'], cb=CB_S5)
    sgu = _sgu_fwd(proj, c['sgu_ng'], c['sgu_nb'], c['sgu_w'], c['sgu_bias'], cbu=CB_SGU_U, cbv=CB_SGU_V)
    pool, pooled = _pool_fwd(proj, c['pool_w'], c['pool_scale'], cb=CB_POOL)
    qn, kn, vs, gb = _dn_prep_fwd(proj, c['conv_w'], c['alog'], c['dtb'], cbq=CB_DN_Q)
    u, wm, qg, ktail, qkd, tm = _dn_local_fwd(qn, kn, vs, gb)
    dn, raw, v_new, states = _dn_seq_fwd(u, wm, qg, ktail, qkd, gb, proj, c['dn_ng'], cb_gate=CB_DN_GATE)
    mixed = jnp.concatenate([s5, sgu, pool, dn], axis=1)
    (h1,) = mm(mixed, 'w_out', mode="nn", name="mix_out", e=xin, epi=residual, tk=KT)
    x1, x1_16 = _ln_fwd(h1, c['ln1_g'], c['ln1_b'], name="ln1_" + tag)
    (hidden,) = mm(x1_16, 'w_up', mode="nn", name="mlp_up", epi=lambda r, e: (_relu2(r),), out_dtypes=(MXU_DTYPE,), tk=KT, b_blocked=True)
    (h2,) = mm(hidden, 'w_down', mode="nn", name="mlp_down", e=x1, epi=residual, tk=KT)
    x2, x2_16 = _ln_fwd(h2, c['ln2_g'], c['ln2_b'], name="ln2_" + tag)
    saved = dict(xin16=xin16, proj=proj, xre=xre, xim=xim, pooled=pooled, qn=qn, kn=kn, vs=vs, gb=gb, raw=raw, states=states,
                 wm=wm, qg=qg, ktail=ktail, qkd=qkd, tm=tm, v_new=v_new, mixed=mixed, h1=h1, x1_16=x1_16, hidden=hidden, h2=h2)
    return x2, x2_16, saved


def _by_rows(g):
    return g.reshape(4, g.shape[0] // 4, g.shape[1])


def _by_cols(g):
    return jnp.transpose(g.reshape(g.shape[0], 4, g.shape[1] // 4), (1, 0, 2))


def _layer_bwd(dx2, s, w, c, p, i, place):
    tag = str(i)
    residual = lambda r, e: (r + ALPHA * e,)
    reduced = {}

    def pair(blocks, names):
        got = _pair_exchange(blocks, name="grad_pair_exchange_" + names[0] + tag)
        return [_pair_sum(g, r, place, name="pair_sum_" + nm + tag) for g, r, nm in zip(blocks, got, names)]

    def riding(ps, names, a, b, **kw):
        outs, got = _matmul(a, b, carry=_carry_chip_exchange(ps), **kw)
        bufs = _pair_join([_chip_sum(q, t, place, name="chip_sum_" + nm + tag) for q, t, nm in zip(ps, got, names)],
                          name="grad_pair_join_" + names[0] + tag)
        for nm, buf in zip(names, bufs):
            reduced[nm] = buf.reshape(-1, buf.shape[-1])
        return outs

    dh2, dh2_16, dln2g, dln2b = _ln_bwd(dx2, s['h2'], c['ln2_g'], name="ln2_bwd" + tag)
    (dw_down,) = _matmul(s['hidden'], dh2_16, mode="tn", name="dw_down" + tag, tk=KT)
    p_down = pair([_by_rows(dw_down)], ['w_down'])
    (da,) = _matmul(dh2_16, w['w_down'], mode="nt", name="d_hidden" + tag, e=s['hidden'],
                    epi=lambda r, e: (r * (2.0 * jnp.sqrt(e.astype(F32))),), out_dtypes=(MXU_DTYPE,), tk=KT)
    (dw_up,) = riding(p_down, ['w_down'], s['x1_16'], da, mode="tn", name="dw_up" + tag, tk=KT, out_blocked=True)
    p_up = pair([dw_up], ['w_up'])
    (dx1,) = riding(p_up, ['w_up'], da, w['w_up'], mode="nt", name="dx_mlp" + tag, e=dh2, epi=residual, tk=KT, b_blocked=True)
    dh1, dh1_16, dln1g, dln1b = _ln_bwd(dx1, s['h1'], c['ln1_g'], name="ln1_bwd" + tag)
    (dw_out,) = _matmul(s['mixed'], dh1_16, mode="tn", name="dw_out" + tag, tk=KT)
    p_out = pair([_by_rows(dw_out)], ['w_out'])
    (dmixed,) = riding(p_out, ['w_out'], dh1_16, w['w_out'], mode="nt", name="d_mixed" + tag, tk=KT)
    proj = s['proj']
    (du, dglu_w, dglu_b, dd, dccre, dccim, dbbre, dbbim, sre, sim) = _s5_bwd(
        dmixed, proj, s['xre'], s['xim'], c['bbre'], c['bbim'], c['ccre'], c['ccim'], c['s5_d'], c['cr'], w['s5_glu_w'], c['glu_b'],
        cb_dy=0, cb=CB_S5)
    dlam_re, dlam_im, dlog_step, db_re, db_im, dc_re, dc_im = _s5_param_grads(
        p['s5_lambda_re'], p['s5_lambda_im'], p['s5_log_step'], p['s5_b_re'], p['s5_b_im'], dbbre, dbbim, dccre, dccim, sre, sim)
    dzu, dzv, dsgu_w, dsgu_bias, dsgu_ng, dsgu_nb = _sgu_bwd(dmixed, proj, c['sgu_ng'], c['sgu_nb'], c['sgu_w'], c['sgu_wt'], c['sgu_bias'],
                                                            cb=1, cbu=CB_SGU_U, cbv=CB_SGU_V)
    dp, dpool_w, dpool_scale = _pool_bwd(dmixed, s['pooled'], c['pool_w'], c['pool_scale'], cb=2)
    dgate, ddn_ng, do, d_vnew, d_wm, d_qg, d_ktail, d_eglast = _dn_seq_bwd(
        dmixed, s['raw'], proj, c['dn_ng'], s['gb'], s['wm'], s['qg'], s['ktail'], s['qkd'], s['v_new'], s['states'],
        cb_dy=3, cb_gate=CB_DN_GATE)
    dqn, dkn, dvs, dgb = _dn_local_bwd(s['qn'], s['kn'], s['vs'], s['gb'], s['tm'], s['v_new'], do, d_vnew, d_wm, d_qg, d_ktail, d_eglast)
    dcq, dck, dcv, dab, dalog, ddtb = _dn_prep_bwd_a(proj, c['conv_w'], c['alog'], c['dtb'], dqn, dkn, dvs, dgb, cbq=CB_DN_Q)
    dq, dk, dv, dconv_w = _dn_prep_bwd_b(proj, c['conv_w'], dcq, dck, dcv, cbq=CB_DN_Q)
    dproj = jnp.concatenate([du, dzu, dzv, dp, dq, dk, dv, dgate, dab], axis=1)
    (dw_in,) = _matmul(s['xin16'], dproj, mode="tn", name="dw_in" + tag, tn=1408, tk=KT)
    p_in = pair([_by_cols(dw_in[:, :IN_COLS]), _by_rows(dglu_w)], ['w_in', 's5_glu_w'])
    (dxin,) = riding(p_in, ['w_in', 's5_glu_w'], dproj, w['w_in'], mode="nt", name="dx_in" + tag, tk=1408, e=dh1, epi=residual)
    tril = jnp.tril(jnp.ones((SGU_CHUNK, SGU_CHUNK), bool))
    small = dict(
        s5_lambda_re=dlam_re, s5_lambda_im=dlam_im, s5_log_step=dlog_step, s5_b_re=db_re, s5_b_im=db_im, s5_c_re=dc_re, s5_c_im=dc_im,
        s5_d=dd.reshape(S5_G, S5_H), s5_glu_b=dglu_b[0], sgu_norm_g=dsgu_ng[0], sgu_norm_b=dsgu_nb[0],
        sgu_w=jnp.where(tril, dsgu_w, 0.0), sgu_b=dsgu_bias.reshape(SGU_CHUNK, SGU_HEADS, SGU_HD).sum(-1).T, pool_w=dpool_w,
        pool_scale=dpool_scale[0], dn_conv_w=dconv_w, dn_a_log=dalog[0, :DN_HEADS], dn_dt_bias=ddtb[0, :DN_HEADS], dn_norm_g=ddn_ng[0],
        ln1_g=dln1g[0], ln1_b=dln1b[0], ln2_g=dln2g[0], ln2_b=dln2b[0])
    return dxin, reduced, small


def _pack(arrs):
    rows = []
    for a in arrs:
        n = math.prod(a.shape)
        rows.append(jnp.pad(a.reshape(-1), (0, -n % 128)).reshape(-1, 128))
    out = jnp.concatenate(rows, axis=0)
    return jnp.pad(out, ((0, -out.shape[0] % 8), (0, 0)))


def _sum_all(stacked):
    return _sum_kernel([(stacked, d) for d in range(stacked.shape[0])], name="small_sum")


def _unpack(packed, like):
    out, row = [], 0
    for a in like:
        n = math.prod(a.shape)
        rows = -(-n // 128)
        out.append(packed[row:row + rows].reshape(-1)[:n].reshape(a.shape))
        row += rows
    return out


def kernel(x, w_in, s5_lambda_re, s5_lambda_im, s5_log_step, s5_b_re, s5_b_im, s5_c_re, s5_c_im, s5_d, s5_glu_w, s5_glu_b, sgu_norm_g, sgu_norm_b, sgu_w, sgu_b, pool_w, pool_scale, dn_conv_w, dn_a_log, dn_dt_bias, dn_norm_g, w_out, ln1_g, ln1_b, w_up, w_down, ln2_g, ln2_b, loss_target, m_w_in, m_s5_lambda_re, m_s5_lambda_im, m_s5_log_step, m_s5_b_re, m_s5_b_im, m_s5_c_re, m_s5_c_im, m_s5_d, m_s5_glu_w, m_s5_glu_b, m_sgu_norm_g, m_sgu_norm_b, m_sgu_w, m_sgu_b, m_pool_w, m_pool_scale, m_dn_conv_w, m_dn_a_log, m_dn_dt_bias, m_dn_norm_g, m_w_out, m_ln1_g, m_ln1_b, m_w_up, m_w_down, m_ln2_g, m_ln2_b, v_w_in, v_s5_lambda_re, v_s5_lambda_im, v_s5_log_step, v_s5_b_re, v_s5_b_im, v_s5_c_re, v_s5_c_im, v_s5_d, v_s5_glu_w, v_s5_glu_b, v_sgu_norm_g, v_sgu_norm_b, v_sgu_w, v_sgu_b, v_pool_w, v_pool_scale, v_dn_conv_w, v_dn_a_log, v_dn_dt_bias, v_dn_norm_g, v_w_out, v_ln1_g, v_ln1_b, v_w_up, v_w_down, v_ln2_g, v_ln2_b):
    given = dict(locals())
    xs, ys = lax.axis_index("x"), lax.axis_index("y")
    chip = 2 * xs + ys
    t = given['x'].shape[1]
    x0 = given['x'].reshape(t, D_MODEL)
    target = given['loss_target'].reshape(t, D_MODEL)

    assert DEPTH == 2
    place = jnp.stack([lax.axis_index("c"), chip]).astype(jnp.int32)
    conv_local = given['dn_conv_w']
    conv_all = _allgather_all(_pack([conv_local]))
    n_conv = math.prod(conv_local.shape)
    conv_full = jnp.concatenate([conv_all[2 * j].reshape(-1)[:n_conv].reshape(conv_local.shape) for j in range(4)], axis=-1)

    ws = [dict(), dict()]

    def whole(n, blocks):
        if n == 'w_in':
            return jnp.pad(jnp.transpose(blocks, (1, 0, 2)).reshape(D_MODEL, IN_COLS), ((0, 0), (0, IN_PAD - IN_COLS)))
        if n == 'w_up':
            return blocks
        return blocks.reshape(-1, blocks.shape[-1])

    def gather(items):
        own = [given[n][i].astype(MXU_DTYPE) for n, i in items]

        def done(bufs):
            for (n, i), buf, mine in zip(items, bufs, own):
                ws[i][n] = whole(n, lax.dynamic_update_slice(buf, mine[None], (chip, 0, 0)))
        return _carry_allgather(own), done

    first, first_done = gather([('w_in', 0), ('s5_glu_w', 0), ('w_out', 0)])
    first_done(_run_carry(first, "allgather_first"))
    carries = [dict(proj=gather([('w_up', 0)]), mix_out=gather([('w_down', 0)]),
                    mlp_up=gather([('w_in', 1), ('s5_glu_w', 1), ('w_out', 1)]), mlp_down=gather([('w_up', 1)])),
               dict(proj=gather([('w_down', 1)]))]

    def layer_params(i):
        p = {n: given[n][i] for n in SMALL}
        p['dn_conv_w'] = conv_full[i]
        return p

    ps = [layer_params(i) for i in range(DEPTH)]
    cs = [_layer_consts(p) for p in ps]

    xcur, xcur16, saved = x0, x0.astype(MXU_DTYPE), []
    for i in range(DEPTH):
        xcur, xcur16, s = _layer_fwd(xcur, xcur16, ws[i], cs[i], i, carries[i])
        saved.append(s)
    dx, colsum = _loss_head(xcur, target)
    loss = lax.psum(0.5 * jnp.sum(colsum) / D_MODEL, ("x", "y", "c"))

    reduced, smalls = [None] * DEPTH, [None] * DEPTH
    for i in reversed(range(DEPTH)):
        dx, reduced[i], smalls[i] = _layer_bwd(dx, saved[i], ws[i], cs[i], ps[i], i, place)
    grad_x = dx.reshape(1, t, D_MODEL)

    small_full = [jnp.stack([smalls[i][n] for i in range(DEPTH)]) for n in SMALL]
    grads = dict(zip(SMALL, _unpack(_sum_all(_allgather_all(_pack(small_full))), small_full)))
    grads['dn_conv_w'] = lax.dynamic_slice_in_dim(grads['dn_conv_w'], chip * conv_local.shape[-1], conv_local.shape[-1], axis=2)

    delta, new_m, new_v = {}, {}, {}
    for n in BIG:
        grads[n], delta[n], new_m[n], new_v[n] = _adamw_layers(given[n], reduced[0][n], reduced[1][n], given['m_' + n], given['v_' + n],
                                                               name="adamw_" + n)
    like = [given[n] for n in SMALL]
    d, nm, nv = _adamw(_pack(like), _pack([grads[n] for n in SMALL]), _pack([given['m_' + n] for n in SMALL]),
                       _pack([given['v_' + n] for n in SMALL]), name="adamw_small")
    for out, packed in ((delta, d), (new_m, nm), (new_v, nv)):
        out.update(zip(SMALL, _unpack(packed, like)))
    return (loss, grad_x, *[grads[n] for n in WEIGHTS], *[delta[n] for n in WEIGHTS], *[new_m[n] for n in WEIGHTS],
            *[new_v[n] for n in WEIGHTS])
```

```python
import functools
import math

import jax
import jax.numpy as jnp
from jax import lax
from jax.experimental import pallas as pl
from jax.experimental.pallas import tpu as pltpu

F32 = jnp.float32
MXU_DTYPE = jnp.bfloat16
HI = lax.Precision.HIGHEST

D_MODEL = 2048
DEPTH = 2
GW = 512
S5_H = 16
S5_G = GW // S5_H
S5_P = 64
S5_N = S5_G * S5_P
SGU_CHUNK = 128
SGU_HEADS = 8
SGU_HD = GW // SGU_HEADS
POOL_WINDOWS = (2, 4, 8, 16)
POOL_GD = 128
DN_HD = 128
DN_HEADS = 4
DN_CONV = 4
DN_CHUNK = 64
D_FF = 4 * D_MODEL
IN_COLS = 4104
IN_PAD = 4224
LN_EPS = 1e-5
RMS_EPS = 1e-6
L2_EPS = 1e-6
ALPHA = (2 * DEPTH) ** 0.25
ADAM_LR, ADAM_B1, ADAM_B2, ADAM_EPS, ADAM_WD, ADAM_STEP = 0.001, 0.9, 0.999, 1e-08, 0.01, 10

VMEM_LIMIT = 56 * 1024 * 1024
MESH = pl.DeviceIdType.MESH


def _params(sem=None, vmem=VMEM_LIMIT):
    return pltpu.CompilerParams(dimension_semantics=sem, vmem_limit_bytes=vmem)


def _full(shape):
    nd = len(shape)
    return pl.BlockSpec(shape, lambda *_: (0,) * nd)


def _split(a):
    hi = a.astype(MXU_DTYPE)
    return hi, (a - hi.astype(F32)).astype(MXU_DTYPE)


def _dot(a, b, dims=(((1,), (0,)), ((), ())), exact=False):
    if exact and MXU_DTYPE == F32:
        return lax.dot_general(a, b, dims, precision=HI, preferred_element_type=F32)
    if exact:
        (ah, al), (bh, bl) = _split(a), _split(b)
        return (lax.dot_general(ah, bh, dims, preferred_element_type=F32) + lax.dot_general(al, bh, dims, preferred_element_type=F32)
                + lax.dot_general(ah, bl, dims, preferred_element_type=F32))
    return lax.dot_general(a.astype(MXU_DTYPE), b.astype(MXU_DTYPE), dims, preferred_element_type=F32)


NN = (((1,), (0,)), ((), ()))
NT = (((1,), (1,)), ((), ()))
TN = (((0,), (0,)), ((), ()))


def _gelu(x):
    c = math.sqrt(2.0 / math.pi)
    return 0.5 * x * (1.0 + jnp.tanh(c * (x + 0.044715 * x * x * x)))


def _gelu_grad(x):
    c = math.sqrt(2.0 / math.pi)
    t = jnp.tanh(c * (x + 0.044715 * x * x * x))
    return 0.5 * (1.0 + t) + 0.5 * x * (1.0 - t * t) * c * (1.0 + 3.0 * 0.044715 * x * x)


def _sigmoid(x):
    return 1.0 / (1.0 + jnp.exp(-x))


def _relu2(x):
    r = jnp.maximum(x, 0.0)
    return r * r


class _Carry:
    def __init__(self, ins, out_shapes, nsem, start, finish):
        self.ins, self.out_shapes, self.nsem, self.start, self.finish = list(ins), list(out_shapes), nsem, start, finish

    def sems(self):
        return [pltpu.SemaphoreType.DMA((self.nsem,)), pltpu.SemaphoreType.DMA((self.nsem,))]


def _run_carry(carry, name):
    n_in, n_out = len(carry.ins), len(carry.out_shapes)

    def body(*refs):
        parts = refs[:n_in], refs[n_in:n_in + n_out], refs[-2], refs[-1]
        carry.start(*parts)
        carry.finish(*parts)

    any_spec = pl.BlockSpec(memory_space=pl.ANY)
    return pl.pallas_call(body, name=name, in_specs=[any_spec] * n_in, out_specs=[any_spec] * n_out, out_shape=carry.out_shapes,
                          scratch_shapes=carry.sems())(*carry.ins)


def _matmul(a, b, *, mode, name, e=None, epi=None, out_dtypes=(F32,), tm=1024, tn=1024, tk=512, carry=None, b_blocked=False,
            out_blocked=False):
    if mode == "nn":
        (m, k), n = a.shape, (4 * b.shape[2] if b_blocked else b.shape[1])
    elif mode == "nt":
        m, n, k = a.shape[0], b.shape[-2], a.shape[1]
    else:
        (k, m), n = a.shape, b.shape[1]
    tm, tn, tk = min(tm, m), min(tn, n), min(tk, k)
    if b_blocked or out_blocked:
        tn, tk = min(tn, n // 4), (min(tk, k // 4) if mode == "nt" and b_blocked else tk)
    assert m % tm == 0 and n % tn == 0 and k % tk == 0, (name, m, n, k, tm, tn, tk)
    nk, nout = k // tk, len(out_dtypes)
    dims = {"nn": NN, "nt": NT, "tn": TN}[mode]
    a_spec = pl.BlockSpec((tk, tm), lambda i, j, l: (l, i)) if mode == "tn" else pl.BlockSpec((tm, tk), lambda i, j, l: (i, l))
    nb, kb = max(n // 4 // tn, 1), max(k // 4 // tk, 1)

    def split(idx, per):
        return lax.div(idx, jnp.int32(per)), lax.rem(idx, jnp.int32(per))

    if b_blocked and mode == "nn":
        b_spec = pl.BlockSpec((None, tk, tn), lambda i, j, l: (split(j, nb)[0], l, split(j, nb)[1]))
    elif b_blocked:
        b_spec = pl.BlockSpec((None, tn, tk), lambda i, j, l: (split(l, kb)[0], j, split(l, kb)[1]))
    else:
        b_spec = pl.BlockSpec((tn, tk), lambda i, j, l: (j, l)) if mode == "nt" else pl.BlockSpec((tk, tn), lambda i, j, l: (l, j))
    if out_blocked:
        o_spec = pl.BlockSpec((None, tm, tn), lambda i, j, l: (split(j, nb)[0], i, split(j, nb)[1]))
    else:
        o_spec = pl.BlockSpec((tm, tn), lambda i, j, l: (i, j))
    assert not (out_blocked and e is not None)
    o_shape = (4, m, n // 4) if out_blocked else (m, n)

    n_in = 2 + (e is not None)
    c_in, c_out = (len(carry.ins), len(carry.out_shapes)) if carry is not None else (0, 0)
    gm, gn = m // tm, n // tn

    def body(*refs):
        a_ref, b_ref = refs[:2]
        e_ref = refs[2] if e is not None else None
        o_refs = refs[n_in + c_in:n_in + c_in + nout]
        acc = refs[n_in + c_in + nout + c_out]
        l = pl.program_id(2)
        if carry is not None:
            parts = refs[n_in:n_in + c_in], refs[n_in + c_in + nout:n_in + c_in + nout + c_out], refs[-2], refs[-1]
            step = (pl.program_id(0) * gn + pl.program_id(1)) * nk + l

            @pl.when(step == 0)
            def _():
                carry.start(*parts)

        d = _dot(a_ref[...], b_ref[...], dims)

        def finish(r):
            outs = (r,) if epi is None else epi(r, None if e_ref is None else e_ref[...])
            for o_ref, o, dt in zip(o_refs, outs, out_dtypes, strict=True):
                o_ref[...] = o.astype(dt)

        if nk == 1:
            finish(d)
        else:
            @pl.when(l == 0)
            def _():
                acc[...] = d

            @pl.when((l > 0) & (l < nk - 1))
            def _():
                acc[...] += d

            @pl.when(l == nk - 1)
            def _():
                finish(acc[...] + d)

        if carry is not None:
            @pl.when(step == gm * gn * nk - 1)
            def _():
                carry.finish(*parts)

    ins, specs = [a, b], [a_spec, b_spec]
    if e is not None:
        ins.append(e)
        specs.append(o_spec)
    out_shape = [jax.ShapeDtypeStruct(o_shape, dt) for dt in out_dtypes]
    out_specs, scratch = [o_spec] * nout, [pltpu.VMEM((tm, tn), F32)]
    if carry is not None:
        any_spec = pl.BlockSpec(memory_space=pl.ANY)
        ins, specs = ins + carry.ins, specs + [any_spec] * c_in
        out_shape, out_specs, scratch = out_shape + carry.out_shapes, out_specs + [any_spec] * c_out, scratch + carry.sems()
    sem = ("parallel", "parallel", "arbitrary") if carry is None else ("arbitrary",) * 3
    res = pl.pallas_call(body, name=name, grid=(gm, gn, nk), in_specs=specs, out_specs=out_specs, out_shape=out_shape,
                         scratch_shapes=scratch, compiler_params=_params(sem))(*ins)
    return tuple(res) if carry is None else (tuple(res[:nout]), list(res[nout:]))


def _ln_fwd(h, g, b, *, name, tr=256):
    t, d = h.shape

    def body(h_ref, g_ref, b_ref, o_ref, o16_ref):
        x = h_ref[...]
        mu = jnp.mean(x, axis=-1, keepdims=True)
        xc = x - mu
        var = jnp.mean(xc * xc, axis=-1, keepdims=True)
        y = xc * lax.rsqrt(var + LN_EPS) * g_ref[...] + b_ref[...]
        o_ref[...] = y
        o16_ref[...] = y.astype(MXU_DTYPE)

    row = pl.BlockSpec((tr, d), lambda i: (i, 0))
    return pl.pallas_call(body, name=name, grid=(t // tr,), in_specs=[row, _full((1, d)), _full((1, d))], out_specs=[row, row],
                          out_shape=[jax.ShapeDtypeStruct((t, d), F32), jax.ShapeDtypeStruct((t, d), MXU_DTYPE)],
                          compiler_params=_params(("parallel",)))(h, g, b)


def _ln_bwd(dy, h, g, *, name, tr=256):
    t, d = h.shape

    def body(dy_ref, h_ref, g_ref, dh_ref, dh16_ref, dg_ref, db_ref):
        @pl.when(pl.program_id(0) == 0)
        def _():
            dg_ref[...] = jnp.zeros_like(dg_ref)
            db_ref[...] = jnp.zeros_like(db_ref)

        x, dyv = h_ref[...], dy_ref[...]
        mu = jnp.mean(x, axis=-1, keepdims=True)
        xc = x - mu
        rstd = lax.rsqrt(jnp.mean(xc * xc, axis=-1, keepdims=True) + LN_EPS)
        xh = xc * rstd
        w = dyv * g_ref[...]
        dh = rstd * (w - jnp.mean(w, axis=-1, keepdims=True) - xh * jnp.mean(w * xh, axis=-1, keepdims=True))
        dh_ref[...] = dh
        dh16_ref[...] = dh.astype(MXU_DTYPE)
        dg_ref[...] += jnp.sum(dyv * xh, axis=0, keepdims=True)
        db_ref[...] += jnp.sum(dyv, axis=0, keepdims=True)

    row = pl.BlockSpec((tr, d), lambda i: (i, 0))
    vec = _full((1, d))
    return pl.pallas_call(
        body, name=name, grid=(t // tr,), in_specs=[row, row, vec], out_specs=[row, row, vec, vec],
        out_shape=[jax.ShapeDtypeStruct((t, d), F32), jax.ShapeDtypeStruct((t, d), MXU_DTYPE), jax.ShapeDtypeStruct((1, d), F32),
                   jax.ShapeDtypeStruct((1, d), F32)],
        compiler_params=_params(("arbitrary",)))(dy, h, g)


def _loss_head(y, target, *, tr=256):
    t, d = y.shape

    def body(y_ref, t_ref, dy_ref, s_ref):
        @pl.when(pl.program_id(0) == 0)
        def _():
            s_ref[...] = jnp.zeros_like(s_ref)

        err = y_ref[...] - t_ref[...]
        dy_ref[...] = err * (1.0 / d)
        s_ref[...] += jnp.sum(err * err, axis=0, keepdims=True)

    row = pl.BlockSpec((tr, d), lambda i: (i, 0))
    return pl.pallas_call(
        body, name="loss_head", grid=(t // tr,), in_specs=[row, row], out_specs=[row, _full((1, d))],
        out_shape=[jax.ShapeDtypeStruct((t, d), F32), jax.ShapeDtypeStruct((1, d), F32)],
        compiler_params=_params(("arbitrary",)))(y, target)


def _cols(tb, width, cb):
    return pl.BlockSpec((tb, width), lambda i: (i, cb))


def _cols_rev(tb, width, cb, nb):
    return pl.BlockSpec((tb, width), lambda i: (nb - 1 - i, cb))


POOL_HALO = 16


def _pool_fwd(proj, w, scale, *, cb, tb=512):
    t = proj.shape[0]
    tb = min(tb, t)

    def body(p_ref, w_ref, s_ref, o_ref, pooled_ref, ext):
        i = pl.program_id(0)

        @pl.when(i == 0)
        def _():
            ext[pl.ds(0, POOL_HALO), :] = jnp.zeros((POOL_HALO, GW), F32)

        p = p_ref[...]
        ext[pl.ds(POOL_HALO, tb), :] = p
        pos = (i * tb + lax.broadcasted_iota(jnp.int32, (tb, 1), 0) + 1).astype(F32)
        for gi, win in enumerate(POOL_WINDOWS):
            c0 = gi * POOL_GD
            s = p[:, c0:c0 + POOL_GD]
            for k in range(1, win):
                s = s + ext[pl.ds(POOL_HALO - k, tb), pl.ds(c0, POOL_GD)]
            pooled = s / jnp.minimum(pos, float(win)) - p[:, c0:c0 + POOL_GD]
            pooled_ref[:, pl.ds(c0, POOL_GD)] = pooled
            o_ref[:, pl.ds(c0, POOL_GD)] = (_dot(pooled, w_ref[gi]) * s_ref[:, pl.ds(c0, POOL_GD)]).astype(o_ref.dtype)
        ext[pl.ds(0, POOL_HALO), :] = p[tb - POOL_HALO:, :]

    row = pl.BlockSpec((tb, GW), lambda i: (i, 0))
    return pl.pallas_call(
        body, name="pool_fwd", grid=(t // tb,),
        in_specs=[_cols(tb, GW, cb), _full((4, POOL_GD, POOL_GD)), _full((1, GW))], out_specs=[row, row],
        out_shape=[jax.ShapeDtypeStruct((t, GW), MXU_DTYPE), jax.ShapeDtypeStruct((t, GW), F32)],
        scratch_shapes=[pltpu.VMEM((tb + POOL_HALO, GW), F32)],
        compiler_params=_params(("arbitrary",)))(proj, w, scale)


def _pool_bwd(dmixed, pooled, w, scale, *, cb, tb=512):
    t = pooled.shape[0]
    tb = min(tb, t)
    nb = t // tb

    def body(dy_ref, pooled_ref, w_ref, s_ref, dp_ref, dw_ref, ds_ref, ext):
        i = pl.program_id(0)

        @pl.when(i == 0)
        def _():
            ext[pl.ds(tb, POOL_HALO), :] = jnp.zeros((POOL_HALO, GW), F32)
            dw_ref[...] = jnp.zeros_like(dw_ref)
            ds_ref[...] = jnp.zeros_like(ds_ref)

        dy = dy_ref[...]
        pos = ((nb - 1 - i) * tb + lax.broadcasted_iota(jnp.int32, (tb, 1), 0) + 1).astype(F32)
        dpool_all = []
        for gi, win in enumerate(POOL_WINDOWS):
            c0 = gi * POOL_GD
            pg = pooled_ref[:, pl.ds(c0, POOL_GD)]
            dyg = dy[:, c0:c0 + POOL_GD]
            ds_ref[:, pl.ds(c0, POOL_GD)] += jnp.sum(dyg * _dot(pg, w_ref[gi]), axis=0, keepdims=True)
            dmp = dyg * s_ref[:, pl.ds(c0, POOL_GD)]
            dw_ref[gi] += _dot(pg, dmp, TN)
            dpool = _dot(dmp, w_ref[gi], NT)
            dpool_all.append(dpool)
            ext[pl.ds(0, tb), pl.ds(c0, POOL_GD)] = dpool / jnp.minimum(pos, float(win))
        for gi, win in enumerate(POOL_WINDOWS):
            c0 = gi * POOL_GD
            s = ext[pl.ds(0, tb), pl.ds(c0, POOL_GD)]
            for k in range(1, win):
                s = s + ext[pl.ds(k, tb), pl.ds(c0, POOL_GD)]
            dp_ref[:, pl.ds(c0, POOL_GD)] = (s - dpool_all[gi]).astype(dp_ref.dtype)
        ext[pl.ds(tb, POOL_HALO), :] = ext[pl.ds(0, POOL_HALO), :]

    row = pl.BlockSpec((tb, GW), lambda i: (nb - 1 - i, 0))
    return pl.pallas_call(
        body, name="pool_bwd", grid=(nb,),
        in_specs=[_cols_rev(tb, GW, cb, nb), row, _full((4, POOL_GD, POOL_GD)), _full((1, GW))],
        out_specs=[row, _full((4, POOL_GD, POOL_GD)), _full((1, GW))],
        out_shape=[jax.ShapeDtypeStruct((t, GW), MXU_DTYPE), jax.ShapeDtypeStruct((4, POOL_GD, POOL_GD), F32),
                   jax.ShapeDtypeStruct((1, GW), F32)],
        scratch_shapes=[pltpu.VMEM((tb + POOL_HALO, GW), F32)], compiler_params=_params(("arbitrary",)))(dmixed, pooled, w, scale)


def _sgu_core(zu, zv, ng, nb, w_ref, bias):
    tb = zu.shape[0]
    u = _gelu(zu)
    v0 = _gelu(zv)
    mu = jnp.mean(v0, axis=-1, keepdims=True)
    vc = v0 - mu
    rstd = lax.rsqrt(jnp.mean(vc * vc, axis=-1, keepdims=True) + LN_EPS)
    xh = vc * rstd
    vn = xh * ng + nb
    low = lax.broadcasted_iota(jnp.int32, (SGU_CHUNK, 2 * SGU_HD), 1) < SGU_HD
    rows = []
    for n in range(tb // SGU_CHUNK):
        pairs = []
        for j in range(SGU_HEADS // 2):
            vp = vn[n * SGU_CHUNK:(n + 1) * SGU_CHUNK, j * 128:(j + 1) * 128]
            pairs.append(jnp.where(low, _dot(w_ref[2 * j], vp), _dot(w_ref[2 * j + 1], vp)))
        rows.append(jnp.concatenate(pairs, axis=1) + bias)
    mixed = jnp.concatenate(rows, axis=0)
    return u, xh, rstd, vn, mixed


def _sgu_fwd(proj, ng, nb, w, bias, *, cbu, cbv, tb=512):
    t = proj.shape[0]
    tb = min(tb, t)

    def body(zu_ref, zv_ref, ng_ref, nb_ref, w_ref, bias_ref, o_ref):
        u, _, _, _, mixed = _sgu_core(zu_ref[...], zv_ref[...], ng_ref[...], nb_ref[...], w_ref, bias_ref[...])
        o_ref[...] = (u * mixed).astype(o_ref.dtype)

    vec = _full((1, GW))
    return pl.pallas_call(
        body, name="sgu_fwd", grid=(t // tb,),
        in_specs=[_cols(tb, GW, cbu), _cols(tb, GW, cbv), vec, vec, _full((8, 128, 128)), _full((128, GW))],
        out_specs=pl.BlockSpec((tb, GW), lambda i: (i, 0)), out_shape=jax.ShapeDtypeStruct((t, GW), MXU_DTYPE),
        compiler_params=_params(("parallel",)))(proj, proj, ng, nb, w, bias)


def _sgu_bwd(dmixed, proj, ng, nb, w, wt, bias, *, cb, cbu, cbv, tb=512):
    t = proj.shape[0]
    tb = min(tb, t)

    def body(dy_ref, zu_ref, zv_ref, ng_ref, nb_ref, w_ref, wt_ref, bias_ref, dzu_ref, dzv_ref, dw_ref, dbias_ref, dng_ref, dnb_ref):
        @pl.when(pl.program_id(0) == 0)
        def _():
            dw_ref[...] = jnp.zeros_like(dw_ref)
            dbias_ref[...] = jnp.zeros_like(dbias_ref)
            dng_ref[...] = jnp.zeros_like(dng_ref)
            dnb_ref[...] = jnp.zeros_like(dnb_ref)

        zu, zv, dy = zu_ref[...], zv_ref[...], dy_ref[...]
        u, xh, rstd, vn, mixed = _sgu_core(zu, zv, ng_ref[...], nb_ref[...], w_ref, bias_ref[...])
        dzu_ref[...] = (dy * mixed * _gelu_grad(zu)).astype(dzu_ref.dtype)
        dmix = dy * u
        low = lax.broadcasted_iota(jnp.int32, (SGU_CHUNK, 2 * SGU_HD), 1) < SGU_HD
        dbias = jnp.zeros((SGU_CHUNK, GW), F32)
        rows = []
        for n in range(tb // SGU_CHUNK):
            dm = dmix[n * SGU_CHUNK:(n + 1) * SGU_CHUNK, :]
            dbias = dbias + dm
            pairs = []
            for j in range(SGU_HEADS // 2):
                dmp = dm[:, j * 128:(j + 1) * 128]
                vp = vn[n * SGU_CHUNK:(n + 1) * SGU_CHUNK, j * 128:(j + 1) * 128]
                dw_ref[2 * j] += _dot(jnp.where(low, dmp, 0.0), vp, NT)
                dw_ref[2 * j + 1] += _dot(jnp.where(low, 0.0, dmp), vp, NT)
                pairs.append(jnp.where(low, _dot(wt_ref[2 * j], dmp), _dot(wt_ref[2 * j + 1], dmp)))
            rows.append(jnp.concatenate(pairs, axis=1))
        dbias_ref[...] += dbias
        dvn = jnp.concatenate(rows, axis=0)
        dng_ref[...] += jnp.sum(dvn * xh, axis=0, keepdims=True)
        dnb_ref[...] += jnp.sum(dvn, axis=0, keepdims=True)
        wv = dvn * ng_ref[...]
        dv0 = rstd * (wv - jnp.mean(wv, axis=-1, keepdims=True) - xh * jnp.mean(wv * xh, axis=-1, keepdims=True))
        dzv_ref[...] = (dv0 * _gelu_grad(zv)).astype(dzv_ref.dtype)

    vec = _full((1, GW))
    row = pl.BlockSpec((tb, GW), lambda i: (i, 0))
    mat = _full((8, 128, 128))
    return pl.pallas_call(
        body, name="sgu_bwd", grid=(t // tb,),
        in_specs=[_cols(tb, GW, cb), _cols(tb, GW, cbu), _cols(tb, GW, cbv), vec, vec, mat, mat, _full((128, GW))],
        out_specs=[row, row, mat, _full((128, GW)), vec, vec],
        out_shape=[jax.ShapeDtypeStruct((t, GW), MXU_DTYPE)] * 2 + [jax.ShapeDtypeStruct((8, 128, 128), F32),
                   jax.ShapeDtypeStruct((128, GW), F32), jax.ShapeDtypeStruct((1, GW), F32), jax.ShapeDtypeStruct((1, GW), F32)],
        compiler_params=_params(("arbitrary",)))(dmixed, proj, proj, ng, nb, w, wt, bias)


S5_KB = 4
SUB = 8


def _s5_discretize(lam_re, lam_im, log_step, b_re, b_im):
    step = jnp.exp(log_step)[:, None]
    mag = jnp.exp(lam_re * step)
    lr, li = mag * jnp.cos(lam_im * step), mag * jnp.sin(lam_im * step)
    den = lam_re * lam_re + lam_im * lam_im
    fr = ((lr - 1.0) * lam_re + li * lam_im) / den
    fi = (li * lam_re - (lr - 1.0) * lam_im) / den
    return lr, li, fr[:, :, None] * b_re - fi[:, :, None] * b_im, fr[:, :, None] * b_im + fi[:, :, None] * b_re


def _cpow(lr, li, n):
    rr, ri = lr, li
    for _ in range(n - 1):
        rr, ri = rr * lr - ri * li, rr * li + ri * lr
    return rr, ri


def _s5_scan_consts(lr, li, reverse):
    lr, li = lr.reshape(1, S5_N), (-li if reverse else li).reshape(1, S5_N)
    row = jnp.arange(SUB)[:, None]
    out = []
    for s in (1, 2, 4):
        pr, pi = _cpow(lr, li, s)
        keep = (row < SUB - s) if reverse else (row >= s)
        out += [jnp.where(keep, pr, 0.0), jnp.where(keep, pi, 0.0)]
    cr, ci = [], []
    for i in range(SUB):
        pr, pi = _cpow(lr, li, SUB - i if reverse else i + 1)
        cr.append(pr)
        ci.append(pi)
    out += [jnp.concatenate(cr, axis=0), jnp.concatenate(ci, axis=0)]
    return jnp.stack(out)


def _s5_blockdiag_in(b):
    bt = jnp.swapaxes(b, 1, 2).reshape(S5_KB, 8, S5_H, S5_P)
    eye = jnp.eye(8, dtype=b.dtype)
    return jnp.einsum("kghp,gj->kghjp", bt, eye).reshape(S5_KB, 128, 512)


def _s5_blockdiag_in_extract(bb):
    x = bb.reshape(S5_KB, 8, S5_H, 8, S5_P)
    d = jnp.einsum("kghgp->kghp", x).reshape(S5_G, S5_H, S5_P)
    return jnp.swapaxes(d, 1, 2)


def _s5_blockdiag_out(c):
    ct = jnp.swapaxes(c, 1, 2).reshape(S5_KB, 8, S5_P, S5_H)
    eye = jnp.eye(8, dtype=c.dtype)
    return jnp.einsum("kgph,gj->kgpjh", ct, eye).reshape(S5_KB, 512, 128)


def _s5_blockdiag_out_extract(cc):
    x = cc.reshape(S5_KB, 8, S5_P, 8, S5_H)
    d = jnp.einsum("kgpgh->kgph", x).reshape(S5_G, S5_P, S5_H)
    return jnp.swapaxes(d, 1, 2)


def _s5_tile_scan(a, b, c_ref, carry, reverse):
    for si, s in enumerate((1, 2, 4)):
        sh = SUB - s if reverse else s
        ar, br = pltpu.roll(a, sh, 0), pltpu.roll(b, sh, 0)
        mr, mi = c_ref[2 * si], c_ref[2 * si + 1]
        a, b = a + mr * ar - mi * br, b + mr * br + mi * ar
    pr, pi = c_ref[6], c_ref[7]
    cr, ci = carry
    return a + pr * cr - pi * ci, b + pr * ci + pi * cr


def _s5_readout(xre_ref, xim_ref, ccre_ref, ccim_ref):
    return jnp.concatenate(
        [_dot(xre_ref[:, pl.ds(512 * k, 512)], ccre_ref[k]) - _dot(xim_ref[:, pl.ds(512 * k, 512)], ccim_ref[k])
         for k in range(S5_KB)], axis=1)


def _s5_fwd(proj, bbre, bbim, ccre, ccim, dvec, consts, glu_w, glu_b, *, cb, tb=256):
    t = proj.shape[0]
    tb = min(tb, t)
    nt = tb // SUB

    def body(u_ref, bbre_ref, bbim_ref, ccre_ref, ccim_ref, d_ref, c_ref, w_ref, b_ref, o_ref, xre_ref, xim_ref, car):
        @pl.when(pl.program_id(0) == 0)
        def _():
            car[...] = jnp.zeros_like(car)

        u = u_ref[...]
        for k in range(S5_KB):
            uk = u[:, 128 * k:128 * (k + 1)]
            xre_ref[:, pl.ds(512 * k, 512)] = _dot(uk, bbre_ref[k])
            xim_ref[:, pl.ds(512 * k, 512)] = _dot(uk, bbim_ref[k])

        def tile(r, carry):
            sl = pl.ds(pl.multiple_of(r * SUB, SUB), SUB)
            a, b = _s5_tile_scan(xre_ref[sl, :], xim_ref[sl, :], c_ref, carry, False)
            xre_ref[sl, :] = a
            xim_ref[sl, :] = b
            return a[SUB - 1:SUB, :], b[SUB - 1:SUB, :]

        cr, ci = lax.fori_loop(0, nt, tile, (car[0:1, :], car[1:2, :]))
        car[0:1, :] = cr
        car[1:2, :] = ci
        ys = _s5_readout(xre_ref, xim_ref, ccre_ref, ccim_ref) + d_ref[...] * u
        yg = _gelu(ys)
        o_ref[...] = (yg * _sigmoid(_dot(yg, w_ref[...]) + b_ref[...])).astype(o_ref.dtype)

    row = pl.BlockSpec((tb, GW), lambda i: (i, 0))
    xrow = pl.BlockSpec((tb, S5_N), lambda i: (i, 0))
    vec = _full((1, GW))
    return pl.pallas_call(
        body, name="s5_fwd", grid=(t // tb,),
        in_specs=[_cols(tb, GW, cb), _full((4, 128, 512)), _full((4, 128, 512)), _full((4, 512, 128)), _full((4, 512, 128)),
                  vec, _full((8, SUB, S5_N)), _full((GW, GW)), vec],
        out_specs=[row, xrow, xrow],
        out_shape=[jax.ShapeDtypeStruct((t, GW), MXU_DTYPE), jax.ShapeDtypeStruct((t, S5_N), F32), jax.ShapeDtypeStruct((t, S5_N), F32)],
        scratch_shapes=[pltpu.VMEM((SUB, S5_N), F32)], compiler_params=_params(("arbitrary",)))(
            proj, bbre, bbim, ccre, ccim, dvec, consts, glu_w, glu_b)


def _s5_bwd(dmixed, proj, xre, xim, bbre, bbim, ccre, ccim, dvec, consts, glu_w, glu_b, *, cb_dy, cb, tb=256):
    t = proj.shape[0]
    tb = min(tb, t)
    nb = t // tb
    nt = tb // SUB

    def body(dy_ref, u_ref, xre_ref, xim_ref, bbre_ref, bbim_ref, ccre_ref, ccim_ref, d_ref, c_ref, w_ref, b_ref,
             du_ref, dw_ref, db_ref, dd_ref, dccre_ref, dccim_ref, dbbre_ref, dbbim_ref, sre_ref, sim_ref, are, aim, car):
        @pl.when(pl.program_id(0) == 0)
        def _():
            car[...] = jnp.zeros_like(car)
            for r in (dw_ref, db_ref, dd_ref, dccre_ref, dccim_ref, dbbre_ref, dbbim_ref, sre_ref, sim_ref):
                r[...] = jnp.zeros_like(r)

        u, dy = u_ref[...], dy_ref[...]
        ys = _s5_readout(xre_ref, xim_ref, ccre_ref, ccim_ref) + d_ref[...] * u
        yg = _gelu(ys)
        sg = _sigmoid(_dot(yg, w_ref[...]) + b_ref[...])
        dz = dy * yg * sg * (1.0 - sg)
        dyg = dy * sg + _dot(dz, w_ref[...], NT)
        dw_ref[...] += _dot(yg, dz, TN)
        db_ref[...] += jnp.sum(dz, axis=0, keepdims=True)
        dys = dyg * _gelu_grad(ys)
        dd_ref[...] += jnp.sum(dys * u, axis=0, keepdims=True)
        for k in range(S5_KB):
            dk = dys[:, 128 * k:128 * (k + 1)]
            lanes = pl.ds(512 * k, 512)
            are[:, lanes] = _dot(dk, ccre_ref[k], NT)
            aim[:, lanes] = -_dot(dk, ccim_ref[k], NT)
            dccre_ref[k] += _dot(xre_ref[:, lanes], dk, TN)
            dccim_ref[k] -= _dot(xim_ref[:, lanes], dk, TN)

        def tile(j, carry):
            sl = pl.ds(pl.multiple_of((nt - 1 - j) * SUB, SUB), SUB)
            gr, gi = are[sl, :], aim[sl, :]
            a, b = _s5_tile_scan(gr, gi, c_ref, carry, True)
            are[sl, :] = a
            aim[sl, :] = b
            er, ei = a - gr, b - gi
            xr, xi = xre_ref[sl, :], xim_ref[sl, :]
            sre_ref[...] += xr * er + xi * ei
            sim_ref[...] += xr * ei - xi * er
            return a[0:1, :], b[0:1, :]

        cr, ci = lax.fori_loop(0, nt, tile, (car[0:1, :], car[1:2, :]))
        car[0:1, :] = cr
        car[1:2, :] = ci
        dus = []
        for k in range(S5_KB):
            uk = u[:, 128 * k:128 * (k + 1)]
            lanes = pl.ds(512 * k, 512)
            dbbre_ref[k] += _dot(uk, are[:, lanes], TN)
            dbbim_ref[k] += _dot(uk, aim[:, lanes], TN)
            dus.append(_dot(are[:, lanes], bbre_ref[k], NT) + _dot(aim[:, lanes], bbim_ref[k], NT))
        du_ref[...] = (d_ref[...] * dys + jnp.concatenate(dus, axis=1)).astype(du_ref.dtype)

    row = pl.BlockSpec((tb, GW), lambda i: (nb - 1 - i, 0))
    xrow = pl.BlockSpec((tb, S5_N), lambda i: (nb - 1 - i, 0))
    vec = _full((1, GW))
    bbs, ccs = _full((4, 128, 512)), _full((4, 512, 128))
    sds = jax.ShapeDtypeStruct
    return pl.pallas_call(
        body, name="s5_bwd", grid=(nb,),
        in_specs=[_cols_rev(tb, GW, cb_dy, nb), _cols_rev(tb, GW, cb, nb), xrow, xrow, bbs, bbs, ccs, ccs, vec,
                  _full((8, SUB, S5_N)), _full((GW, GW)), vec],
        out_specs=[row, _full((GW, GW)), vec, vec, ccs, ccs, bbs, bbs, _full((SUB, S5_N)), _full((SUB, S5_N))],
        out_shape=[sds((t, GW), MXU_DTYPE), sds((GW, GW), F32), sds((1, GW), F32), sds((1, GW), F32), sds((4, 512, 128), F32),
                   sds((4, 512, 128), F32), sds((4, 128, 512), F32), sds((4, 128, 512), F32), sds((SUB, S5_N), F32),
                   sds((SUB, S5_N), F32)],
        scratch_shapes=[pltpu.VMEM((tb, S5_N), F32), pltpu.VMEM((tb, S5_N), F32), pltpu.VMEM((SUB, S5_N), F32)],
        compiler_params=_params(("arbitrary",)))(dmixed, proj, xre, xim, bbre, bbim, ccre, ccim, dvec, consts, glu_w, glu_b)


def _s5_prepare(lam_re, lam_im, log_step, b_re, b_im, c_re, c_im):
    lr, li, bbr, bbi = _s5_discretize(lam_re, lam_im, log_step, b_re, b_im)
    return dict(bbre=_s5_blockdiag_in(bbr), bbim=_s5_blockdiag_in(bbi), ccre=_s5_blockdiag_out(c_re), ccim=_s5_blockdiag_out(c_im),
                cf=_s5_scan_consts(lr, li, False), cr=_s5_scan_consts(lr, li, True))


def _s5_param_grads(lam_re, lam_im, log_step, b_re, b_im, dbbre, dbbim, dccre, dccim, sre, sim):
    (lr, li, _, _), vjp = jax.vjp(_s5_discretize, lam_re, lam_im, log_step, b_re, b_im)
    sr, si = jnp.sum(sre, axis=0).reshape(S5_G, S5_P), jnp.sum(sim, axis=0).reshape(S5_G, S5_P)
    den = lr * lr + li * li
    glr, gli = (sr * lr - si * li) / den, (si * lr + sr * li) / den
    dlam_re, dlam_im, dlog_step, db_re, db_im = vjp((glr, gli, _s5_blockdiag_in_extract(dbbre), _s5_blockdiag_in_extract(dbbim)))
    return dlam_re, dlam_im, dlog_step, db_re, db_im, _s5_blockdiag_out_extract(dccre), _s5_blockdiag_out_extract(dccim)


HALO = 8
AB_CB = 4096 // 128
Q_SCALE = DN_HD ** -0.5


def _halo_prev(tb, width, cb):
    return pl.BlockSpec((HALO, width), lambda i: (jnp.maximum(i * (tb // HALO) - 1, 0), cb))


def _halo_next(tb, width, cb, nrows):
    last = nrows // HALO - 1
    return pl.BlockSpec((HALO, width), lambda i: (jnp.minimum((i + 1) * (tb // HALO), last), cb))


def _silu_parts(c):
    sg = _sigmoid(c)
    return c * sg, sg * (1.0 + c * (1.0 - sg))


def _softplus(x):
    return jnp.maximum(x, 0.0) + jnp.log(1.0 + jnp.exp(-jnp.abs(x)))


def _dn_conv(x_ref, halo_ref, w_ref, part, ext, first):
    tb = x_ref.shape[0]
    ext[pl.ds(0, HALO), :] = jnp.where(first, 0.0, halo_ref[...])
    ext[pl.ds(HALO, tb), :] = x_ref[...]
    c = None
    for j in range(DN_CONV):
        term = w_ref[pl.ds(j, 1), pl.ds(512 * part, 512)] * ext[pl.ds(HALO - (DN_CONV - 1) + j, tb), :]
        c = term if c is None else c + term
    return c


def _dn_gb(ab, alog, dtb):
    lane = lax.broadcasted_iota(jnp.int32, ab.shape, 1)
    pre = ab + dtb
    g = -jnp.exp(alog) * _softplus(pre)
    beta = _sigmoid(ab)
    return jnp.where(lane < DN_HEADS, g, jnp.where(lane < 2 * DN_HEADS, beta, 0.0)), pre, beta


def _dn_prep_fwd(proj, conv_w, alog, dtb, *, cbq, tb=512):
    t = proj.shape[0]
    tb = min(tb, t)

    def body(xq, xk, xv, hq, hk, hv, ab_ref, w_ref, alog_ref, dtb_ref, qn_ref, kn_ref, vs_ref, gb_ref, ext):
        first = pl.program_id(0) == 0
        for part, (x_ref, h_ref, o_ref) in enumerate(((xq, hq, qn_ref), (xk, hk, kn_ref), (xv, hv, vs_ref))):
            s, _ = _silu_parts(_dn_conv(x_ref, h_ref, w_ref, part, ext, first))
            if part < 2:
                scale = Q_SCALE if part == 0 else 1.0
                for h in range(DN_HEADS):
                    sh = s[:, DN_HD * h:DN_HD * (h + 1)]
                    rn = lax.rsqrt(jnp.sum(sh * sh, axis=-1, keepdims=True) + L2_EPS)
                    o_ref[:, pl.ds(DN_HD * h, DN_HD)] = sh * (rn * scale)
            else:
                o_ref[...] = s
        gb_ref[...] = _dn_gb(ab_ref[...], alog_ref[...], dtb_ref[...])[0]

    row = pl.BlockSpec((tb, GW), lambda i: (i, 0))
    small = pl.BlockSpec((tb, 128), lambda i: (i, 0))
    v128 = _full((1, 128))
    sds = jax.ShapeDtypeStruct
    return pl.pallas_call(
        body, name="dn_prep_fwd", grid=(t // tb,),
        in_specs=[_cols(tb, GW, cbq), _cols(tb, GW, cbq + 1), _cols(tb, GW, cbq + 2),
                  _halo_prev(tb, GW, cbq), _halo_prev(tb, GW, cbq + 1), _halo_prev(tb, GW, cbq + 2),
                  _cols(tb, 128, AB_CB), _full((DN_CONV, 3 * GW)), v128, v128],
        out_specs=[row, row, row, small],
        out_shape=[sds((t, GW), F32)] * 3 + [sds((t, 128), F32)],
        scratch_shapes=[pltpu.VMEM((tb + HALO, GW), F32)], compiler_params=_params(("parallel",)))(
            proj, proj, proj, proj, proj, proj, proj, conv_w, alog, dtb)


def _dn_prep_bwd_a(proj, conv_w, alog, dtb, dqn, dkn, dvs, dgb, *, cbq, tb=512):
    t = proj.shape[0]
    tb = min(tb, t)

    def body(xq, xk, xv, hq, hk, hv, ab_ref, w_ref, alog_ref, dtb_ref, dqn_ref, dkn_ref, dvs_ref, dgb_ref,
             dcq_ref, dck_ref, dcv_ref, dab_ref, dalog_ref, ddtb_ref, ext):
        first = pl.program_id(0) == 0

        @pl.when(first)
        def _():
            dalog_ref[...] = jnp.zeros_like(dalog_ref)
            ddtb_ref[...] = jnp.zeros_like(ddtb_ref)

        for part, (x_ref, h_ref, d_ref, o_ref) in enumerate(((xq, hq, dqn_ref, dcq_ref), (xk, hk, dkn_ref, dck_ref), (xv, hv, dvs_ref, dcv_ref))):
            s, ds_dc = _silu_parts(_dn_conv(x_ref, h_ref, w_ref, part, ext, first))
            d = d_ref[...]
            if part < 2:
                scale = Q_SCALE if part == 0 else 1.0
                for h in range(DN_HEADS):
                    lanes = slice(DN_HD * h, DN_HD * (h + 1))
                    sh, dh = s[:, lanes], d[:, lanes]
                    rn = lax.rsqrt(jnp.sum(sh * sh, axis=-1, keepdims=True) + L2_EPS)
                    dsh = scale * (rn * dh - sh * (rn * rn * rn) * jnp.sum(dh * sh, axis=-1, keepdims=True))
                    o_ref[:, pl.ds(DN_HD * h, DN_HD)] = dsh * ds_dc[:, lanes]
            else:
                o_ref[...] = d * ds_dc
        ab, dgb_v = ab_ref[...], dgb_ref[...]
        gb, pre, beta = _dn_gb(ab, alog_ref[...], dtb_ref[...])
        lane = lax.broadcasted_iota(jnp.int32, ab.shape, 1)
        is_g = lane < DN_HEADS
        da = jnp.where(is_g, dgb_v * (-jnp.exp(alog_ref[...])) * _sigmoid(pre), 0.0)
        db = jnp.where((lane >= DN_HEADS) & (lane < 2 * DN_HEADS), dgb_v * beta * (1.0 - beta), 0.0)
        dab_ref[...] = (da + db).astype(dab_ref.dtype)
        ddtb_ref[...] += jnp.sum(da, axis=0, keepdims=True)
        dalog_ref[...] += jnp.sum(jnp.where(is_g, dgb_v * gb, 0.0), axis=0, keepdims=True)

    row = pl.BlockSpec((tb, GW), lambda i: (i, 0))
    small = pl.BlockSpec((tb, 128), lambda i: (i, 0))
    v128 = _full((1, 128))
    sds = jax.ShapeDtypeStruct
    return pl.pallas_call(
        body, name="dn_prep_bwd_a", grid=(t // tb,),
        in_specs=[_cols(tb, GW, cbq), _cols(tb, GW, cbq + 1), _cols(tb, GW, cbq + 2),
                  _halo_prev(tb, GW, cbq), _halo_prev(tb, GW, cbq + 1), _halo_prev(tb, GW, cbq + 2),
                  _cols(tb, 128, AB_CB), _full((DN_CONV, 3 * GW)), v128, v128, row, row, row, small],
        out_specs=[row, row, row, small, v128, v128],
        out_shape=[sds((t, GW), F32)] * 3 + [sds((t, 128), MXU_DTYPE), sds((1, 128), F32), sds((1, 128), F32)],
        scratch_shapes=[pltpu.VMEM((tb + HALO, GW), F32)], compiler_params=_params(("arbitrary",)))(
            proj, proj, proj, proj, proj, proj, proj, conv_w, alog, dtb, dqn, dkn, dvs, dgb)


def _dn_prep_bwd_b(proj, conv_w, dcq, dck, dcv, *, cbq, tb=512):
    t = proj.shape[0]
    tb = min(tb, t)
    nb = t // tb

    def body(xq, xk, xv, hq, hk, hv, dq_in, dk_in, dv_in, nq, nk, nv, w_ref, dq_ref, dk_ref, dv_ref, dw_ref, ext):
        i = pl.program_id(0)
        first, last = i == 0, i == nb - 1

        @pl.when(first)
        def _():
            dw_ref[...] = jnp.zeros_like(dw_ref)

        for part, (x_ref, h_ref, d_ref, n_ref, o_ref) in enumerate(
                ((xq, hq, dq_in, nq, dq_ref), (xk, hk, dk_in, nk, dk_ref), (xv, hv, dv_in, nv, dv_ref))):
            lanes = pl.ds(512 * part, 512)
            d = d_ref[...]
            ext[pl.ds(0, HALO), :] = jnp.where(first, 0.0, h_ref[...])
            ext[pl.ds(HALO, tb), :] = x_ref[...]
            for j in range(DN_CONV):
                xs = ext[pl.ds(HALO - (DN_CONV - 1) + j, tb), :]
                dw_ref[pl.ds(j, 1), lanes] += jnp.sum(d * xs, axis=0, keepdims=True)
            ext[pl.ds(0, tb), :] = d
            ext[pl.ds(tb, HALO), :] = jnp.where(last, 0.0, n_ref[...])
            acc = None
            for j in range(DN_CONV):
                term = w_ref[pl.ds(j, 1), lanes] * ext[pl.ds(DN_CONV - 1 - j, tb), :]
                acc = term if acc is None else acc + term
            o_ref[...] = acc.astype(o_ref.dtype)

    row = pl.BlockSpec((tb, GW), lambda i: (i, 0))
    nxt = _halo_next(tb, GW, 0, t)
    sds = jax.ShapeDtypeStruct
    return pl.pallas_call(
        body, name="dn_prep_bwd_b", grid=(nb,),
        in_specs=[_cols(tb, GW, cbq), _cols(tb, GW, cbq + 1), _cols(tb, GW, cbq + 2),
                  _halo_prev(tb, GW, cbq), _halo_prev(tb, GW, cbq + 1), _halo_prev(tb, GW, cbq + 2),
                  row, row, row, nxt, nxt, nxt, _full((DN_CONV, 3 * GW))],
        out_specs=[row, row, row, _full((DN_CONV, 3 * GW))],
        out_shape=[sds((t, GW), MXU_DTYPE)] * 3 + [sds((DN_CONV, 3 * GW), F32)],
        scratch_shapes=[pltpu.VMEM((tb + HALO, GW), F32)], compiler_params=_params(("arbitrary",)))(
            proj, proj, proj, proj, proj, proj, dcq, dck, dcv, dcq, dck, dcv, conv_w)


CAT = DN_HEADS * DN_CHUNK
DN_LOCAL_CHUNKS = 4


def _iota_div(shape, axis, width):
    return jnp.right_shift(lax.broadcasted_iota(jnp.int32, shape, axis), width.bit_length() - 1)


def _dn_masks():
    r = lax.broadcasted_iota(jnp.int32, (DN_CHUNK, CAT), 0)
    c = jnp.bitwise_and(lax.broadcasted_iota(jnp.int32, (DN_CHUNK, CAT), 1), DN_CHUNK - 1)
    wide = _iota_div((CAT, GW), 0, DN_CHUNK) == _iota_div((CAT, GW), 1, DN_HD)
    square = _iota_div((CAT, CAT), 0, DN_CHUNK) == _iota_div((CAT, CAT), 1, DN_CHUNK)
    return dict(eye=r == c, tril=r >= c, strict=r > c, triu=r <= c, wide=wide, square=square)


def _stack4(x):
    return jnp.concatenate([x, x, x, x], axis=0)


def _diag_blocks(x, mask):
    return jnp.where(mask, _stack4(x), 0.0)


def _fold_blocks(x, mask):
    x = jnp.where(mask, x, 0.0)
    return x[0:64] + x[64:128] + x[128:192] + x[192:256]


def _expand(cols, base, width):
    head = _iota_div((cols.shape[0], DN_HEADS * width), 1, width)
    out = jnp.zeros((cols.shape[0], DN_HEADS * width), F32)
    for h in range(DN_HEADS):
        out = jnp.where(head == h, cols[:, base + h:base + h + 1], out)
    return out


def _head_sums(x, width):
    if width == DN_HD:
        return [jnp.sum(x[:, DN_HD * h:DN_HD * (h + 1)], axis=1, keepdims=True) for h in range(DN_HEADS)]
    head = _iota_div(x.shape, 1, width)
    return [jnp.sum(jnp.where(head == h, x, 0.0), axis=1, keepdims=True) for h in range(DN_HEADS)]


def _cumsum_rows(x):
    row = lax.broadcasted_iota(jnp.int32, x.shape, 0)
    for s in (1, 2, 4, 8, 16, 32):
        x = x + jnp.where(row >= s, pltpu.roll(x, s, 0), 0.0)
    return x


def _tri_inv(ns, square):
    shape = ns[0].shape
    col = jnp.bitwise_and(lax.broadcasted_iota(jnp.int32, shape, 1), DN_CHUNK - 1)
    eye = jnp.where(lax.broadcasted_iota(jnp.int32, shape, 0) == col, 1.0, 0.0)
    xs = [eye - n for n in ns]
    ps = [_dot(n, _diag_blocks(n, square), exact=True) for n in ns]
    for _ in range(4):
        pds = [_diag_blocks(p, square) for p in ps]
        xs = [x + _dot(x, pd, exact=True) for x, pd in zip(xs, pds)]
        ps = [_dot(p, pd, exact=True) for p, pd in zip(ps, pds)]
    return [x + _dot(x, _diag_blocks(p, square), exact=True) for x, p in zip(xs, ps)]


def _dn_local_math(qs, ks, vs, gbvs, m, tms=None):
    out = []
    for k, v, gbv in zip(ks, vs, gbvs):
        gc = _cumsum_rows(gbv)
        gc_cat, gc_wide = _expand(gc, 0, DN_CHUNK), _expand(gc, 0, DN_HD)
        gc_row = jnp.sum(jnp.where(m["eye"], gc_cat, 0.0), axis=0, keepdims=True)
        decay = jnp.where(m["tril"], jnp.exp(jnp.minimum(gc_cat - gc_row, 0.0)), 0.0)
        eg = jnp.exp(gc_wide)
        gl = _expand(gc[DN_CHUNK - 1:DN_CHUNK, :], 0, DN_HD)
        beta = _expand(gbv, DN_HEADS, DN_HD)
        kb = k * beta
        out.append(dict(decay=decay, eg=eg, eg_last=jnp.exp(gl), etail=jnp.exp(gl - gc_wide), beta=beta, kb=kb, vb=v * beta,
                        kbg=kb * eg, k_rows=_diag_blocks(k, m["wide"])))
    for d, q, k in zip(out, qs, ks):
        d.update(qg=q * d["eg"], ktail=k * d["etail"])
    for d in out:
        d["kk"] = _dot(d["kb"], d["k_rows"], NT)
    for d, q in zip(out, qs):
        d["qk"] = _dot(q, d["k_rows"], NT)
    if tms is None:
        tms = _tri_inv([jnp.where(m["strict"], d["kk"] * d["decay"], 0.0) for d in out], m["square"])
    for d, tm in zip(out, tms):
        d["tm"] = tm
    return out


def _dn_local_fwd(qn, kn, vs, gb):
    t = qn.shape[0]
    rows = DN_CHUNK * DN_LOCAL_CHUNKS

    def body(q_ref, k_ref, v_ref, gb_ref, u_ref, wm_ref, qg_ref, kt_ref, qkd_ref, tm_ref):
        masks = _dn_masks()
        chunks = [pl.ds(DN_CHUNK * n, DN_CHUNK) for n in range(DN_LOCAL_CHUNKS)]
        ms = _dn_local_math([q_ref[rs, :] for rs in chunks], [k_ref[rs, :] for rs in chunks], [v_ref[rs, :] for rs in chunks],
                            [gb_ref[rs, :] for rs in chunks], masks)
        us = [_dot(m["tm"], _diag_blocks(m["vb"], masks["wide"])) for m in ms]
        wms = [_dot(m["tm"], _diag_blocks(m["kbg"], masks["wide"])) for m in ms]
        for rs, m, u, wm in zip(chunks, ms, us, wms):
            u_ref[rs, :] = u
            wm_ref[rs, :] = wm.astype(wm_ref.dtype)
            qg_ref[rs, :] = m["qg"].astype(qg_ref.dtype)
            kt_ref[rs, :] = m["ktail"].astype(kt_ref.dtype)
            qkd = (m["qk"] * m["decay"]).astype(qkd_ref.dtype)
            for h in range(DN_HEADS):
                qkd_ref[rs, pl.ds(DN_HD * h, DN_CHUNK)] = qkd[:, DN_CHUNK * h:DN_CHUNK * (h + 1)]
            tm_ref[rs, :] = m["tm"]

    row = pl.BlockSpec((rows, GW), lambda i: (i, 0))
    sds = jax.ShapeDtypeStruct
    return pl.pallas_call(
        body, name="dn_local_fwd", grid=(t // rows,), in_specs=[row, row, row, pl.BlockSpec((rows, 128), lambda i: (i, 0))],
        out_specs=[row] * 5 + [pl.BlockSpec((rows, CAT), lambda i: (i, 0))],
        out_shape=[sds((t, GW), F32)] + [sds((t, GW), MXU_DTYPE)] * 4 + [sds((t, CAT), F32)],
        compiler_params=_params(("parallel",)))(qn, kn, vs, gb)


def _dn_eg_last(gbv, h):
    return jnp.exp(jnp.sum(gbv[:, h:h + 1], axis=0, keepdims=True))


def _dn_seq_fwd(u, wm, qg, ktail, qkd, gb, proj, norm_g, *, cb_gate):
    t = u.shape[0]
    nc = t // DN_CHUNK

    def body(u_ref, wm_ref, qg_ref, kt_ref, qkd_ref, gb_ref, gate_ref, ng_ref, o_ref, raw_ref, vn_ref, st_ref, s_ref):
        @pl.when(pl.program_id(0) == 0)
        def _():
            s_ref[...] = jnp.zeros_like(s_ref)

        gbv = gb_ref[...]
        heads = range(DN_HEADS)
        lanes = [pl.ds(DN_HD * h, DN_HD) for h in heads]
        ss = [s_ref[h] for h in heads]
        for h in heads:
            st_ref[0, h] = ss[h]
        ws = [_dot(wm_ref[:, lanes[h]], ss[h]) for h in heads]
        qs = [_dot(qg_ref[:, lanes[h]], ss[h]) for h in heads]
        v_news = [u_ref[:, lanes[h]] - ws[h] for h in heads]
        ks = [_dot(kt_ref[:, lanes[h]], v_news[h], TN) for h in heads]
        os_ = [qs[h] + _dot(qkd_ref[:, pl.ds(DN_HD * h, DN_CHUNK)], v_news[h]) for h in heads]
        for h in heads:
            s_ref[h] = ss[h] * _dn_eg_last(gbv, h) + ks[h]
            vn_ref[:, lanes[h]] = v_news[h].astype(vn_ref.dtype)
            o = os_[h]
            raw_ref[:, lanes[h]] = o
            r = lax.rsqrt(jnp.mean(o * o, axis=-1, keepdims=True) + RMS_EPS)
            gt = gate_ref[:, lanes[h]]
            o_ref[:, lanes[h]] = (o * r * ng_ref[...] * (gt * _sigmoid(gt))).astype(o_ref.dtype)

    row = pl.BlockSpec((DN_CHUNK, GW), lambda i: (i, 0))
    sds = jax.ShapeDtypeStruct
    return pl.pallas_call(
        body, name="dn_seq_fwd", grid=(nc,),
        in_specs=[row] * 5 + [pl.BlockSpec((DN_CHUNK, 128), lambda i: (i, 0)), _cols(DN_CHUNK, GW, cb_gate), _full((1, DN_HD))],
        out_specs=[row, row, row, pl.BlockSpec((1, DN_HEADS, DN_HD, DN_HD), lambda i: (i, 0, 0, 0))],
        out_shape=[sds((t, GW), MXU_DTYPE), sds((t, GW), F32), sds((t, GW), MXU_DTYPE), sds((nc, DN_HEADS, DN_HD, DN_HD), F32)],
        scratch_shapes=[pltpu.VMEM((DN_HEADS, DN_HD, DN_HD), F32)], compiler_params=_params(("arbitrary",)))(
            u, wm, qg, ktail, qkd, gb, proj, norm_g)


def _dn_seq_bwd(dmixed, raw, proj, norm_g, gb, wm, qg, ktail, qkd, v_new, states, *, cb_dy, cb_gate):
    t = raw.shape[0]
    nc = t // DN_CHUNK

    def body(dy_ref, raw_ref, gate_ref, ng_ref, gb_ref, wm_ref, qg_ref, kt_ref, qkd_ref, vn_ref, st_ref,
             dgate_ref, dng_ref, do_ref, dvn_ref, dwm_ref, dqg_ref, dkt_ref, degl_ref, ds_ref):
        @pl.when(pl.program_id(0) == 0)
        def _():
            ds_ref[...] = jnp.zeros_like(ds_ref)
            dng_ref[...] = jnp.zeros_like(dng_ref)

        gbv = gb_ref[...]
        lane = lax.broadcasted_iota(jnp.int32, (1, 128), 1)
        heads = range(DN_HEADS)
        lanes = [pl.ds(DN_HD * h, DN_HD) for h in heads]
        ss, dss, dos = [st_ref[0, h] for h in heads], [ds_ref[h] for h in heads], []
        for h in heads:
            o, gt, dy, ng = raw_ref[:, lanes[h]], gate_ref[:, lanes[h]], dy_ref[:, lanes[h]], ng_ref[...]
            r = lax.rsqrt(jnp.mean(o * o, axis=-1, keepdims=True) + RMS_EPS)
            sil, dsil = _silu_parts(gt)
            d_on = dy * sil
            dgate_ref[:, lanes[h]] = (dy * (o * r * ng) * dsil).astype(dgate_ref.dtype)
            dng_ref[...] += jnp.sum(d_on * o * r, axis=0, keepdims=True)
            w = d_on * ng
            dos.append(r * w - o * (r * r * r) * jnp.mean(w * o, axis=-1, keepdims=True))
        from_next = [_dot(kt_ref[:, lanes[h]], dss[h]) for h in heads]
        d_vnews = [_dot(qkd_ref[:, pl.ds(DN_HD * h, DN_CHUNK)], dos[h], TN) + from_next[h] for h in heads]
        q_terms = [_dot(qg_ref[:, lanes[h]], dos[h], TN) for h in heads]
        w_terms = [_dot(wm_ref[:, lanes[h]], d_vnews[h], TN) for h in heads]
        degl = jnp.zeros((1, 128), F32)
        for h in heads:
            ds_ref[h] = q_terms[h] + _dn_eg_last(gbv, h) * dss[h] - w_terms[h]
            do_ref[:, lanes[h]] = dos[h].astype(do_ref.dtype)
            dvn_ref[:, lanes[h]] = d_vnews[h].astype(dvn_ref.dtype)
            d_eglast = jnp.sum(jnp.sum(ss[h] * dss[h], axis=1, keepdims=True), axis=0, keepdims=True)
            degl = degl + jnp.where(lane == h, d_eglast, 0.0)
        degl_ref[0] = degl
        for h in heads:
            dwm_ref[:, lanes[h]] = (-_dot(d_vnews[h], ss[h], NT)).astype(dwm_ref.dtype)
        for h in heads:
            dqg_ref[:, lanes[h]] = _dot(dos[h], ss[h], NT)
        for h in heads:
            dkt_ref[:, lanes[h]] = _dot(vn_ref[:, lanes[h]], dss[h], NT)

    row = pl.BlockSpec((DN_CHUNK, GW), lambda i: (nc - 1 - i, 0))
    small = pl.BlockSpec((DN_CHUNK, 128), lambda i: (nc - 1 - i, 0))
    sds = jax.ShapeDtypeStruct
    return pl.pallas_call(
        body, name="dn_seq_bwd", grid=(nc,),
        in_specs=[_cols_rev(DN_CHUNK, GW, cb_dy, nc), row, _cols_rev(DN_CHUNK, GW, cb_gate, nc), _full((1, DN_HD)), small,
                  row, row, row, row, row, pl.BlockSpec((1, DN_HEADS, DN_HD, DN_HD), lambda i: (nc - 1 - i, 0, 0, 0))],
        out_specs=[row, _full((1, DN_HD)), row, row, row, row, row, pl.BlockSpec((1, 1, 128), lambda i: (nc - 1 - i, 0, 0))],
        out_shape=[sds((t, GW), MXU_DTYPE), sds((1, DN_HD), F32), sds((t, GW), MXU_DTYPE), sds((t, GW), MXU_DTYPE),
                   sds((t, GW), MXU_DTYPE), sds((t, GW), F32), sds((t, GW), F32), sds((nc, 1, 128), F32)],
        scratch_shapes=[pltpu.VMEM((DN_HEADS, DN_HD, DN_HD), F32)], compiler_params=_params(("arbitrary",)))(
            dmixed, raw, proj, norm_g, gb, wm, qg, ktail, qkd, v_new, states)


def _dn_local_bwd(qn, kn, vs, gb, tm, v_new, do, d_vnew, d_wm, d_qg, d_ktail, d_eglast):
    t = qn.shape[0]
    rows = DN_CHUNK * DN_LOCAL_CHUNKS

    def body(q_ref, k_ref, v_ref, gb_ref, tm_ref, vn_ref, do_ref, dvn_ref, dwm_ref, dqg_ref, dkt_ref, degl_ref,
             dq_ref, dk_ref, dv_ref, dgb_ref):
        masks = _dn_masks()
        wide, square = masks["wide"], masks["square"]
        lane = lax.broadcasted_iota(jnp.int32, (DN_CHUNK, 128), 1)
        last_row = lax.broadcasted_iota(jnp.int32, (DN_CHUNK, 1), 0) == DN_CHUNK - 1
        chunks = [pl.ds(DN_CHUNK * n, DN_CHUNK) for n in range(DN_LOCAL_CHUNKS)]
        ms = _dn_local_math([q_ref[rs, :] for rs in chunks], [k_ref[rs, :] for rs in chunks], [v_ref[rs, :] for rs in chunks],
                            [gb_ref[rs, :] for rs in chunks], masks, tms=[tm_ref[rs, :] for rs in chunks])

        def work(n, rs, m):
            q, k, v, tm = q_ref[rs, :], k_ref[rs, :], v_ref[rs, :], m["tm"]
            decay, eg, k_rows = m["decay"], m["eg"], m["k_rows"]
            d_vnew, d_wm, d_qg, d_ktail = dvn_ref[rs, :], dwm_ref[rs, :], dqg_ref[rs, :], dkt_ref[rs, :]
            deglv = degl_ref[n]
            dq = d_qg * eg
            dk = d_ktail * m["etail"]
            tails = _head_sums(d_ktail * m["ktail"], DN_HD)
            dgcs = _head_sums(d_qg * m["qg"], DN_HD)
            d_qkd = jnp.where(masks["tril"], _dot(do_ref[rs, :], _diag_blocks(vn_ref[rs, :], wide), NT), 0.0)
            d_tm = _dot(d_vnew, _diag_blocks(m["vb"], wide), NT) + _dot(d_wm, _diag_blocks(m["kbg"], wide), NT)
            d_vb = _fold_blocks(_dot(tm, d_vnew, TN), wide)
            d_kbg = _fold_blocks(_dot(tm, d_wm, TN), wide)
            yield
            dqk_dec = d_qkd * decay
            dq = dq + _dot(dqk_dec, k_rows)
            dk = dk + _fold_blocks(_dot(dqk_dec, q, TN), wide)
            ddecay = d_qkd * m["qk"]
            d_kb = d_kbg * eg
            kbgs = _head_sums(d_kbg * m["kbg"], DN_HD)
            x = _fold_blocks(_dot(tm, d_tm, TN, exact=True), square)
            yield
            d_n = jnp.where(masks["strict"], -_dot(x, _diag_blocks(tm, square), NT, exact=True), 0.0)
            yield
            d_kk = d_n * decay
            d_kb = d_kb + _dot(d_kk, k_rows)
            dk = dk + _fold_blocks(_dot(d_kk, m["kb"], TN), wide)
            yield
            ddecay = ddecay + d_n * m["kk"]
            dk = dk + d_kb * m["beta"]
            dbetas = [a + b for a, b in zip(_head_sums(d_kb * k, DN_HD), _head_sums(d_vb * v, DN_HD))]
            dv_ref[rs, :] = d_vb * m["beta"]
            dq_ref[rs, :] = dq
            dk_ref[rs, :] = dk
            dd = ddecay * decay
            row_sums = _head_sums(dd, DN_CHUNK)
            dgc_cols = jnp.zeros((DN_CHUNK, 128), F32)
            for h in range(DN_HEADS):
                dgl = jnp.sum(tails[h], axis=0, keepdims=True) + deglv[:, h:h + 1] * m["eg_last"][:, DN_HD * h:DN_HD * h + 1]
                dgc_cols = jnp.where(lane == h, dgcs[h] - tails[h] + kbgs[h] + row_sums[h] + jnp.where(last_row, dgl, 0.0), dgc_cols)
            dgc_row = (jnp.sum(jnp.where(masks["eye"], _expand(dgc_cols, 0, DN_CHUNK), 0.0), axis=0, keepdims=True)
                       - jnp.sum(dd, axis=0, keepdims=True))
            dgs = _head_sums(jnp.where(masks["triu"], dgc_row, 0.0), DN_CHUNK)
            dgb = jnp.zeros((DN_CHUNK, 128), F32)
            for h in range(DN_HEADS):
                dgb = jnp.where(lane == h, dgs[h], jnp.where(lane == DN_HEADS + h, dbetas[h], dgb))
            dgb_ref[rs, :] = dgb

        running = [work(n, rs, m) for n, (rs, m) in enumerate(zip(chunks, ms))]
        while running:
            running = [g for g in running if next(g, "done") != "done"]

    row = pl.BlockSpec((rows, GW), lambda i: (i, 0))
    small = pl.BlockSpec((rows, 128), lambda i: (i, 0))
    sds = jax.ShapeDtypeStruct
    return pl.pallas_call(
        body, name="dn_local_bwd", grid=(t // rows,),
        in_specs=[row, row, row, small, pl.BlockSpec((rows, CAT), lambda i: (i, 0))] + [row] * 6
        + [pl.BlockSpec((DN_LOCAL_CHUNKS, 1, 128), lambda i: (i, 0, 0))],
        out_specs=[row, row, row, small], out_shape=[sds((t, GW), F32)] * 3 + [sds((t, 128), F32)],
        compiler_params=_params(("parallel",)))(qn, kn, vs, gb, tm, v_new, do, d_vnew, d_wm, d_qg, d_ktail, d_eglast)


ANY = pl.BlockSpec(memory_space=pl.ANY)
PAIR_SPLIT = 4


def _place():
    x, y, c = lax.axis_index("x"), lax.axis_index("y"), lax.axis_index("c")
    chips = [(1 - x, y), (x, 1 - y), (1 - x, 1 - y)]
    return x, y, c, chips


def _remote(src, dst, send_sem, recv_sem, to):
    return pltpu.make_async_remote_copy(src_ref=src, dst_ref=dst, send_sem=send_sem, recv_sem=recv_sem, device_id=to,
                                        device_id_type=MESH)


def _carry_allgather(arrs):
    n = len(arrs)

    def sends(ins, outs, send_sems, recv_sems):
        x, y, c, chips = _place()
        me = 2 * x + y
        out = []
        for a in range(n):
            half = ins[a].shape[0] // 2
            mine = pl.ds(c * half, half)
            out += [_remote(ins[a].at[mine], outs[a].at[me, mine], send_sems.at[6 * a + k], recv_sems.at[6 * a + k], (*chip, c))
                    for k, chip in enumerate(chips)]
        return out

    def start(*parts):
        for s in sends(*parts):
            s.start()

    def finish(ins, outs, send_sems, recv_sems):
        x, y, c, chips = _place()
        sibling = (x, y, 1 - c)
        fwds = []
        for a in range(n):
            half = ins[a].shape[0] // 2
            mine = pl.ds(c * half, half)
            for k, (cx, cy) in enumerate(chips):
                got = outs[a].at[2 * cx + cy, mine]
                _remote(got, got, send_sems.at[6 * a + k], recv_sems.at[6 * a + k], (cx, cy, c)).wait_recv()
                f = _remote(got, got, send_sems.at[6 * a + 3 + k], recv_sems.at[6 * a + 3 + k], sibling)
                f.start()
                fwds.append(f)
        for a in range(n):
            half = ins[a].shape[0] // 2
            other = pl.ds((1 - c) * half, half)
            for k, (cx, cy) in enumerate(chips):
                got = outs[a].at[2 * cx + cy, other]
                _remote(got, got, send_sems.at[6 * a + 3 + k], recv_sems.at[6 * a + 3 + k], sibling).wait_recv()
        for s in sends(ins, outs, send_sems, recv_sems) + fwds:
            s.wait_send()

    return _Carry(arrs, [jax.ShapeDtypeStruct((4,) + a.shape, a.dtype) for a in arrs], 6 * n, start, finish)


def _carry_pair_exchange(gbs):
    n = len(gbs)

    def copies(ins, outs, send_sems, recv_sems):
        x, y, c, _ = _place()
        out = []
        for a in range(n):
            half = ins[a].shape[1] // 2
            piece = half // PAIR_SPLIT
            out += [_remote(ins[a].at[:, pl.ds((1 - c) * half + r * piece, piece)], outs[a].at[:, pl.ds(r * piece, piece)],
                            send_sems.at[PAIR_SPLIT * a + r], recv_sems.at[PAIR_SPLIT * a + r], (x, y, 1 - c))
                    for r in range(PAIR_SPLIT)]
        return out

    def start(*parts):
        for s in copies(*parts):
            s.start()

    def finish(*parts):
        for s in copies(*parts):
            s.wait()

    return _Carry(gbs, [jax.ShapeDtypeStruct((4, g.shape[1] // 2, g.shape[2]), g.dtype) for g in gbs], PAIR_SPLIT * n, start, finish)


def _carry_chip_exchange(ps):
    n = len(ps)

    def copies(ins, outs, send_sems, recv_sems):
        x, y, c, chips = _place()
        return [_remote(ins[a].at[2 * cx + cy], outs[a].at[k], send_sems.at[3 * a + k], recv_sems.at[3 * a + k], (cx, cy, c))
                for a in range(n) for k, (cx, cy) in enumerate(chips)]

    def start(*parts):
        for s in copies(*parts):
            s.start()

    def finish(*parts):
        for s in copies(*parts):
            s.wait()

    return _Carry(ps, [jax.ShapeDtypeStruct((3,) + p.shape[1:], p.dtype) for p in ps], 3 * n, start, finish)


def _pair_join(bufs, *, name):
    n = len(bufs)

    def body(*refs):
        outs = refs[n:2 * n]
        send_sems, recv_sems = refs[2 * n:]
        x, y, c, _ = _place()
        work = []
        for a in range(n):
            s = _remote(outs[a].at[c], outs[a].at[c], send_sems.at[a], recv_sems.at[a], (x, y, 1 - c))
            s.start()
            work.append(s)
        for s in work:
            s.wait()

    return pl.pallas_call(
        body, name=name, in_specs=[ANY] * n, out_specs=[ANY] * n,
        out_shape=[jax.ShapeDtypeStruct(b.shape, b.dtype) for b in bufs], input_output_aliases={a: a for a in range(n)},
        scratch_shapes=[pltpu.SemaphoreType.DMA((n,)), pltpu.SemaphoreType.DMA((n,))])(*bufs)


def _pair_sum(gb, got, place, *, name, block_bytes=1 << 20):
    _, r, cols = gb.shape
    half = r // 2
    tr = _row_tile(half, cols, block_bytes)

    def body(place_ref, g_ref, got_ref, o_ref):
        o_ref[...] = g_ref[...] + got_ref[...]

    blk = pl.BlockSpec((None, tr, cols), lambda j, i, p: (j, i, 0))
    grid_spec = pltpu.PrefetchScalarGridSpec(
        num_scalar_prefetch=1, grid=(4, half // tr),
        in_specs=[pl.BlockSpec((None, None, tr, cols), lambda j, i, p: (j, p[0], i, 0)), blk], out_specs=blk)
    return pl.pallas_call(body, name=name, grid_spec=grid_spec, out_shape=jax.ShapeDtypeStruct((4, half, cols), F32),
                          compiler_params=_params(("parallel", "parallel")))(place, gb.reshape(4, 2, half, cols), got)


def _chip_sum(p, got, place, *, name, block_bytes=1 << 20):
    _, h, cols = p.shape
    tr = _row_tile(h, cols, block_bytes)

    def body(place_ref, p_ref, g0, g1, g2, o_ref):
        o_ref[...] = p_ref[...] + g0[...] + g1[...] + g2[...]

    def got_spec(k):
        return pl.BlockSpec((None, tr, cols), functools.partial(lambda i, pr, k: (k, i, 0), k=k))

    grid_spec = pltpu.PrefetchScalarGridSpec(
        num_scalar_prefetch=1, grid=(h // tr,),
        in_specs=[pl.BlockSpec((None, tr, cols), lambda i, pr: (pr[1], i, 0)), got_spec(0), got_spec(1), got_spec(2)],
        out_specs=pl.BlockSpec((None, tr, cols), lambda i, pr: (pr[0], i, 0)))
    return pl.pallas_call(body, name=name, grid_spec=grid_spec, out_shape=jax.ShapeDtypeStruct((2, h, cols), F32),
                          compiler_params=_params(("parallel",)))(place, p, got, got, got)


def _allgather_all(v):
    def body(v_ref, out_ref, send_sems, recv_sems):
        x, y, c, chips = _place()
        me, sibling = (x, y, c), (x, y, 1 - c)

        def rows(px, py, pc):
            return out_ref.at[4 * px + 2 * py + pc]

        def copy(k, block, to, src=None):
            return _remote(rows(*block) if src is None else src, rows(*block), send_sems.at[k], recv_sems.at[k], to)

        first = [copy(0, me, sibling, src=v_ref)] + [copy(1 + j, me, (*chip, c), src=v_ref) for j, chip in enumerate(chips)]
        for cp in first:
            cp.start()
        passed = [copy(4 + j, (*chip, c), sibling) for j, chip in enumerate(chips)]
        for j, chip in enumerate(chips):
            copy(1 + j, (*chip, c), me).wait_recv()
            passed[j].start()
        copy(0, sibling, me).wait_recv()
        for j, chip in enumerate(chips):
            copy(4 + j, (*chip, 1 - c), me).wait_recv()
        for cp in first + passed:
            cp.wait_send()

    return pl.pallas_call(
        body, name="allgather_small", in_specs=[ANY], out_specs=ANY, out_shape=jax.ShapeDtypeStruct((8,) + v.shape, v.dtype),
        scratch_shapes=[pltpu.SemaphoreType.DMA((7,)), pltpu.SemaphoreType.DMA((7,))])(v)


def _row_tile(rows, cols, limit_bytes):
    for d in range(1, rows + 1):
        if rows % d == 0 and (rows // d) % 8 == 0 and (rows // d) * cols * 4 <= limit_bytes:
            return rows // d
    return rows


def _sum_kernel(parts, *, name, block_bytes=1 << 20):
    n = len(parts)
    rows, cols = parts[0][0].shape[1:] if isinstance(parts[0], tuple) else parts[0].shape
    tr = _row_tile(rows, cols, block_bytes)
    ins, specs = [], []
    for part in parts:
        if isinstance(part, tuple):
            ins.append(part[0])
            specs.append(pl.BlockSpec((None, tr, cols), functools.partial(lambda i, s: (s, i, 0), s=part[1])))
        else:
            ins.append(part)
            specs.append(pl.BlockSpec((tr, cols), lambda i: (i, 0)))

    def body(*refs):
        acc = refs[0][...]
        for r in refs[1:n]:
            acc = acc + r[...]
        refs[n][...] = acc

    return pl.pallas_call(body, name=name, grid=(rows // tr,), in_specs=specs, out_specs=pl.BlockSpec((tr, cols), lambda i: (i, 0)),
                          out_shape=jax.ShapeDtypeStruct((rows, cols), F32), compiler_params=_params(("parallel",)))(*ins)


def _adamw_math(w, gv, m, v):
    nm = ADAM_B1 * m + (1.0 - ADAM_B1) * gv
    nv = ADAM_B2 * v + (1.0 - ADAM_B2) * (gv * gv)
    m_hat = nm / (1.0 - ADAM_B1 ** ADAM_STEP)
    v_hat = nv / (1.0 - ADAM_B2 ** ADAM_STEP)
    return -ADAM_LR * (m_hat / (jnp.sqrt(v_hat) + ADAM_EPS) + ADAM_WD * w), nm, nv


def _adamw_layers(w, g0, g1, m, v, *, name, block_bytes=1 << 20):
    _, rows, cols = w.shape
    tr = _row_tile(rows, cols, block_bytes)

    def body(w_ref, g0_ref, g1_ref, m_ref, v_ref, g_ref, d_ref, nm_ref, nv_ref):
        gv = jnp.where(pl.program_id(0) == 0, g0_ref[...], g1_ref[...])
        g_ref[...] = gv
        d_ref[...], nm_ref[...], nv_ref[...] = _adamw_math(w_ref[...], gv, m_ref[...], v_ref[...])

    both = pl.BlockSpec((None, tr, cols), lambda l, i: (l, i, 0))
    specs = [both, pl.BlockSpec((tr, cols), lambda l, i: (i * (1 - l), 0)), pl.BlockSpec((tr, cols), lambda l, i: (i * l, 0)), both, both]
    return pl.pallas_call(body, name=name, grid=(2, rows // tr), in_specs=specs, out_specs=[both] * 4,
                          out_shape=[jax.ShapeDtypeStruct(w.shape, F32)] * 4, compiler_params=_params(("arbitrary", "arbitrary")))(
                              w, g0, g1, m, v)


def _adamw(w, g, m, v, *, name, block_bytes=1 << 20):
    rows, cols = w.shape
    tr = _row_tile(rows, cols, block_bytes)

    def body(w_ref, g_ref, m_ref, v_ref, d_ref, nm_ref, nv_ref):
        d_ref[...], nm_ref[...], nv_ref[...] = _adamw_math(w_ref[...], g_ref[...], m_ref[...], v_ref[...])

    spec = pl.BlockSpec((tr, cols), lambda i: (i, 0))
    return pl.pallas_call(body, name=name, grid=(rows // tr,), in_specs=[spec] * 4, out_specs=[spec] * 3,
                          out_shape=[jax.ShapeDtypeStruct((rows, cols), F32)] * 3, compiler_params=_params(("parallel",)))(w, g, m, v)


WEIGHTS = ['w_in', 's5_lambda_re', 's5_lambda_im', 's5_log_step', 's5_b_re', 's5_b_im', 's5_c_re', 's5_c_im', 's5_d', 's5_glu_w',
           's5_glu_b', 'sgu_norm_g', 'sgu_norm_b', 'sgu_w', 'sgu_b', 'pool_w', 'pool_scale', 'dn_conv_w', 'dn_a_log', 'dn_dt_bias',
           'dn_norm_g', 'w_out', 'ln1_g', 'ln1_b', 'w_up', 'w_down', 'ln2_g', 'ln2_b']
BIG = ['w_in', 's5_glu_w', 'w_out', 'w_up', 'w_down']
SMALL = [n for n in WEIGHTS if n not in BIG]
CB_S5, CB_SGU_U, CB_SGU_V, CB_POOL, CB_DN_Q, CB_DN_GATE = 0, 1, 2, 3, 4, 7
KT = 2048


def _pad_lanes(v, width=128):
    return jnp.zeros((1, width), F32).at[0, :v.shape[0]].set(v)


def _layer_consts(p):
    c = _s5_prepare(p['s5_lambda_re'], p['s5_lambda_im'], p['s5_log_step'], p['s5_b_re'], p['s5_b_im'], p['s5_c_re'], p['s5_c_im'])
    tril = jnp.tril(jnp.ones((SGU_CHUNK, SGU_CHUNK), bool))
    wm = jnp.where(tril, p['sgu_w'], 0.0)
    c.update(s5_d=p['s5_d'].reshape(1, GW), glu_b=p['s5_glu_b'].reshape(1, GW), sgu_ng=p['sgu_norm_g'].reshape(1, GW),
             sgu_nb=p['sgu_norm_b'].reshape(1, GW), sgu_w=wm, sgu_wt=jnp.swapaxes(wm, 1, 2),
             sgu_bias=jnp.repeat(p['sgu_b'].T, SGU_HD, axis=1), pool_w=p['pool_w'], pool_scale=p['pool_scale'].reshape(1, GW),
             conv_w=p['dn_conv_w'], alog=_pad_lanes(p['dn_a_log']), dtb=_pad_lanes(p['dn_dt_bias']), dn_ng=p['dn_norm_g'].reshape(1, DN_HD),
             ln1_g=p['ln1_g'].reshape(1, D_MODEL), ln1_b=p['ln1_b'].reshape(1, D_MODEL), ln2_g=p['ln2_g'].reshape(1, D_MODEL),
             ln2_b=p['ln2_b'].reshape(1, D_MODEL))
    return c


def _layer_fwd(xin, xin16, w, c, i, carries):
    tag = str(i)
    residual = lambda r, e: (r + ALPHA * e,)

    def mm(a, b_name, *, name, **kw):
        if name not in carries:
            return _matmul(a, w[b_name], name=name + tag, **kw)
        carry, done = carries[name]
        outs, extra = _matmul(a, w[b_name], name=name + tag, carry=carry, **kw)
        done(extra)
        return outs

    (proj,) = mm(xin16, 'w_in', mode="nn", name="proj", tn=1408, tk=KT)
    s5, xre, xim = _s5_fwd(proj, c['bbre'], c['bbim'], c['ccre'], c['ccim'], c['s5_d'], c['cf'], w['s5_glu_w'], c['glu_b'], cb=CB_S5)
    sgu = _sgu_fwd(proj, c['sgu_ng'], c['sgu_nb'], c['sgu_w'], c['sgu_bias'], cbu=CB_SGU_U, cbv=CB_SGU_V)
    pool, pooled = _pool_fwd(proj, c['pool_w'], c['pool_scale'], cb=CB_POOL)
    qn, kn, vs, gb = _dn_prep_fwd(proj, c['conv_w'], c['alog'], c['dtb'], cbq=CB_DN_Q)
    u, wm, qg, ktail, qkd, tm = _dn_local_fwd(qn, kn, vs, gb)
    dn, raw, v_new, states = _dn_seq_fwd(u, wm, qg, ktail, qkd, gb, proj, c['dn_ng'], cb_gate=CB_DN_GATE)
    mixed = jnp.concatenate([s5, sgu, pool, dn], axis=1)
    (h1,) = mm(mixed, 'w_out', mode="nn", name="mix_out", e=xin, epi=residual, tk=KT)
    x1, x1_16 = _ln_fwd(h1, c['ln1_g'], c['ln1_b'], name="ln1_" + tag)
    (hidden,) = mm(x1_16, 'w_up', mode="nn", name="mlp_up", epi=lambda r, e: (_relu2(r),), out_dtypes=(MXU_DTYPE,), tm=2048, tk=KT,
                   b_blocked=True)
    (h2,) = mm(hidden, 'w_down', mode="nn", name="mlp_down", e=x1, epi=residual, tk=KT)
    x2, x2_16 = _ln_fwd(h2, c['ln2_g'], c['ln2_b'], name="ln2_" + tag)
    saved = dict(xin16=xin16, proj=proj, xre=xre, xim=xim, pooled=pooled, qn=qn, kn=kn, vs=vs, gb=gb, raw=raw, states=states,
                 wm=wm, qg=qg, ktail=ktail, qkd=qkd, tm=tm, v_new=v_new, mixed=mixed, h1=h1, x1_16=x1_16, hidden=hidden, h2=h2)
    return x2, x2_16, saved


def _by_rows(g):
    return g.reshape(4, g.shape[0] // 4, g.shape[1])


def _by_cols(g):
    return jnp.transpose(g.reshape(g.shape[0], 4, g.shape[1] // 4), (1, 0, 2))


def _layer_bwd(dx2, s, w, c, p, i, place):
    tag = str(i)
    residual = lambda r, e: (r + ALPHA * e,)
    reduced = {}

    def pair_sums(blocks, got, names):
        return [_pair_sum(g, r, place, name="pair_sum_" + nm + tag) for g, r, nm in zip(blocks, got, names)]

    def pair(blocks, names):
        return pair_sums(blocks, _run_carry(_carry_pair_exchange(blocks), "grad_pair_exchange_" + names[0] + tag), names)

    def riding(ps, names, a, b, **kw):
        outs, got = _matmul(a, b, carry=_carry_chip_exchange(ps), **kw)
        bufs = _pair_join([_chip_sum(q, t, place, name="chip_sum_" + nm + tag) for q, t, nm in zip(ps, got, names)],
                          name="grad_pair_join_" + names[0] + tag)
        for nm, buf in zip(names, bufs):
            reduced[nm] = buf.reshape(-1, buf.shape[-1])
        return outs

    dh2, dh2_16, dln2g, dln2b = _ln_bwd(dx2, s['h2'], c['ln2_g'], name="ln2_bwd" + tag)
    (dw_down,) = _matmul(s['hidden'], dh2_16, mode="tn", name="dw_down" + tag, tk=KT)
    (da,), got = _matmul(dh2_16, w['w_down'], mode="nt", name="d_hidden" + tag, e=s['hidden'],
                         epi=lambda r, e: (r * (2.0 * jnp.sqrt(e.astype(F32))),), out_dtypes=(MXU_DTYPE,), tm=2048, tk=KT,
                         carry=_carry_pair_exchange([_by_rows(dw_down)]))
    p_down = pair_sums([_by_rows(dw_down)], got, ['w_down'])
    (dw_up,) = riding(p_down, ['w_down'], s['x1_16'], da, mode="tn", name="dw_up" + tag, tk=KT, out_blocked=True)
    p_up = pair([dw_up], ['w_up'])
    (dx1,) = riding(p_up, ['w_up'], da, w['w_up'], mode="nt", name="dx_mlp" + tag, e=dh2, epi=residual, tk=KT, b_blocked=True)
    dh1, dh1_16, dln1g, dln1b = _ln_bwd(dx1, s['h1'], c['ln1_g'], name="ln1_bwd" + tag)
    (dw_out,) = _matmul(s['mixed'], dh1_16, mode="tn", name="dw_out" + tag, tk=KT)
    p_out = pair([_by_rows(dw_out)], ['w_out'])
    (dmixed,) = riding(p_out, ['w_out'], dh1_16, w['w_out'], mode="nt", name="d_mixed" + tag, tk=KT)
    proj = s['proj']
    (du, dglu_w, dglu_b, dd, dccre, dccim, dbbre, dbbim, sre, sim) = _s5_bwd(
        dmixed, proj, s['xre'], s['xim'], c['bbre'], c['bbim'], c['ccre'], c['ccim'], c['s5_d'], c['cr'], w['s5_glu_w'], c['glu_b'],
        cb_dy=0, cb=CB_S5)
    dlam_re, dlam_im, dlog_step, db_re, db_im, dc_re, dc_im = _s5_param_grads(
        p['s5_lambda_re'], p['s5_lambda_im'], p['s5_log_step'], p['s5_b_re'], p['s5_b_im'], dbbre, dbbim, dccre, dccim, sre, sim)
    dzu, dzv, dsgu_w, dsgu_bias, dsgu_ng, dsgu_nb = _sgu_bwd(dmixed, proj, c['sgu_ng'], c['sgu_nb'], c['sgu_w'], c['sgu_wt'], c['sgu_bias'],
                                                            cb=1, cbu=CB_SGU_U, cbv=CB_SGU_V)
    dp, dpool_w, dpool_scale = _pool_bwd(dmixed, s['pooled'], c['pool_w'], c['pool_scale'], cb=2)
    dgate, ddn_ng, do, d_vnew, d_wm, d_qg, d_ktail, d_eglast = _dn_seq_bwd(
        dmixed, s['raw'], proj, c['dn_ng'], s['gb'], s['wm'], s['qg'], s['ktail'], s['qkd'], s['v_new'], s['states'],
        cb_dy=3, cb_gate=CB_DN_GATE)
    dqn, dkn, dvs, dgb = _dn_local_bwd(s['qn'], s['kn'], s['vs'], s['gb'], s['tm'], s['v_new'], do, d_vnew, d_wm, d_qg, d_ktail, d_eglast)
    dcq, dck, dcv, dab, dalog, ddtb = _dn_prep_bwd_a(proj, c['conv_w'], c['alog'], c['dtb'], dqn, dkn, dvs, dgb, cbq=CB_DN_Q)
    dq, dk, dv, dconv_w = _dn_prep_bwd_b(proj, c['conv_w'], dcq, dck, dcv, cbq=CB_DN_Q)
    dproj = jnp.concatenate([du, dzu, dzv, dp, dq, dk, dv, dgate, dab], axis=1)
    (dw_in,) = _matmul(s['xin16'], dproj, mode="tn", name="dw_in" + tag, tn=1408, tk=KT)
    p_in = pair([_by_cols(dw_in[:, :IN_COLS]), _by_rows(dglu_w)], ['w_in', 's5_glu_w'])
    (dxin,) = riding(p_in, ['w_in', 's5_glu_w'], dproj, w['w_in'], mode="nt", name="dx_in" + tag, tk=1408, e=dh1, epi=residual)
    tril = jnp.tril(jnp.ones((SGU_CHUNK, SGU_CHUNK), bool))
    small = dict(
        s5_lambda_re=dlam_re, s5_lambda_im=dlam_im, s5_log_step=dlog_step, s5_b_re=db_re, s5_b_im=db_im, s5_c_re=dc_re, s5_c_im=dc_im,
        s5_d=dd.reshape(S5_G, S5_H), s5_glu_b=dglu_b[0], sgu_norm_g=dsgu_ng[0], sgu_norm_b=dsgu_nb[0],
        sgu_w=jnp.where(tril, dsgu_w, 0.0), sgu_b=dsgu_bias.reshape(SGU_CHUNK, SGU_HEADS, SGU_HD).sum(-1).T, pool_w=dpool_w,
        pool_scale=dpool_scale[0], dn_conv_w=dconv_w, dn_a_log=dalog[0, :DN_HEADS], dn_dt_bias=ddtb[0, :DN_HEADS], dn_norm_g=ddn_ng[0],
        ln1_g=dln1g[0], ln1_b=dln1b[0], ln2_g=dln2g[0], ln2_b=dln2b[0])
    return dxin, reduced, small


def _pack(arrs):
    rows = []
    for a in arrs:
        n = math.prod(a.shape)
        rows.append(jnp.pad(a.reshape(-1), (0, -n % 128)).reshape(-1, 128))
    out = jnp.concatenate(rows, axis=0)
    return jnp.pad(out, ((0, -out.shape[0] % 8), (0, 0)))


def _sum_all(stacked, mine, dev):
    n, rows, cols = stacked.shape
    tr = _row_tile(rows, cols, 1 << 20)

    def body(dev_ref, mine_ref, *refs):
        acc = None
        for d in range(n):
            blk = jnp.where(dev_ref[0] == d, mine_ref[...], refs[d][...])
            acc = blk if acc is None else acc + blk
        refs[n][...] = acc

    def gathered(d):
        return pl.BlockSpec((None, tr, cols), lambda i, p: (jnp.where(p[0] == d, (d + 1) % n, d), i, 0))

    flat = pl.BlockSpec((tr, cols), lambda i, p: (i, 0))
    grid_spec = pltpu.PrefetchScalarGridSpec(num_scalar_prefetch=1, grid=(rows // tr,),
                                             in_specs=[flat] + [gathered(d) for d in range(n)], out_specs=flat)
    return pl.pallas_call(body, name="small_sum", grid_spec=grid_spec, out_shape=jax.ShapeDtypeStruct((rows, cols), F32),
                          compiler_params=_params(("parallel",)))(dev, mine, *([stacked] * n))


def _unpack(packed, like):
    out, row = [], 0
    for a in like:
        n = math.prod(a.shape)
        rows = -(-n // 128)
        out.append(packed[row:row + rows].reshape(-1)[:n].reshape(a.shape))
        row += rows
    return out


def kernel(x, w_in, s5_lambda_re, s5_lambda_im, s5_log_step, s5_b_re, s5_b_im, s5_c_re, s5_c_im, s5_d, s5_glu_w, s5_glu_b, sgu_norm_g, sgu_norm_b, sgu_w, sgu_b, pool_w, pool_scale, dn_conv_w, dn_a_log, dn_dt_bias, dn_norm_g, w_out, ln1_g, ln1_b, w_up, w_down, ln2_g, ln2_b, loss_target, m_w_in, m_s5_lambda_re, m_s5_lambda_im, m_s5_log_step, m_s5_b_re, m_s5_b_im, m_s5_c_re, m_s5_c_im, m_s5_d, m_s5_glu_w, m_s5_glu_b, m_sgu_norm_g, m_sgu_norm_b, m_sgu_w, m_sgu_b, m_pool_w, m_pool_scale, m_dn_conv_w, m_dn_a_log, m_dn_dt_bias, m_dn_norm_g, m_w_out, m_ln1_g, m_ln1_b, m_w_up, m_w_down, m_ln2_g, m_ln2_b, v_w_in, v_s5_lambda_re, v_s5_lambda_im, v_s5_log_step, v_s5_b_re, v_s5_b_im, v_s5_c_re, v_s5_c_im, v_s5_d, v_s5_glu_w, v_s5_glu_b, v_sgu_norm_g, v_sgu_norm_b, v_sgu_w, v_sgu_b, v_pool_w, v_pool_scale, v_dn_conv_w, v_dn_a_log, v_dn_dt_bias, v_dn_norm_g, v_w_out, v_ln1_g, v_ln1_b, v_w_up, v_w_down, v_ln2_g, v_ln2_b):
    given = dict(locals())
    xs, ys = lax.axis_index("x"), lax.axis_index("y")
    chip = 2 * xs + ys
    t = given['x'].shape[1]
    x0 = given['x'].reshape(t, D_MODEL)
    target = given['loss_target'].reshape(t, D_MODEL)

    assert DEPTH == 2
    place = jnp.stack([lax.axis_index("c"), chip]).astype(jnp.int32)
    conv_local = given['dn_conv_w']
    conv_all = _allgather_all(_pack([conv_local]))
    n_conv = math.prod(conv_local.shape)
    conv_full = jnp.concatenate([jnp.where(chip == j, conv_local, conv_all[2 * j].reshape(-1)[:n_conv].reshape(conv_local.shape))
                                 for j in range(4)], axis=-1)

    ws = [dict(), dict()]

    def whole(n, blocks):
        if n == 'w_in':
            return jnp.pad(jnp.transpose(blocks, (1, 0, 2)).reshape(D_MODEL, IN_COLS), ((0, 0), (0, IN_PAD - IN_COLS)))
        if n == 'w_up':
            return blocks
        return blocks.reshape(-1, blocks.shape[-1])

    def gather(items):
        own = [given[n][i].astype(MXU_DTYPE) for n, i in items]

        def done(bufs):
            for (n, i), buf, mine in zip(items, bufs, own):
                ws[i][n] = whole(n, lax.dynamic_update_slice(buf, mine[None], (chip, 0, 0)))
        return _carry_allgather(own), done

    first, first_done = gather([('w_in', 0), ('s5_glu_w', 0), ('w_out', 0)])
    first_done(_run_carry(first, "allgather_first"))
    carries = [dict(proj=gather([('w_up', 0)]), mix_out=gather([('w_down', 0)]),
                    mlp_up=gather([('w_in', 1), ('s5_glu_w', 1), ('w_out', 1)]), mlp_down=gather([('w_up', 1)])),
               dict(proj=gather([('w_down', 1)]))]

    def layer_params(i):
        p = {n: given[n][i] for n in SMALL}
        p['dn_conv_w'] = conv_full[i]
        return p

    ps = [layer_params(i) for i in range(DEPTH)]
    cs = [_layer_consts(p) for p in ps]

    xcur, xcur16, saved = x0, x0.astype(MXU_DTYPE), []
    for i in range(DEPTH):
        xcur, xcur16, s = _layer_fwd(xcur, xcur16, ws[i], cs[i], i, carries[i])
        saved.append(s)
    dx, colsum = _loss_head(xcur, target)
    loss = lax.psum(0.5 * jnp.sum(colsum) / D_MODEL, ("x", "y", "c"))

    reduced, smalls = [None] * DEPTH, [None] * DEPTH
    for i in reversed(range(DEPTH)):
        dx, reduced[i], smalls[i] = _layer_bwd(dx, saved[i], ws[i], cs[i], ps[i], i, place)
    grad_x = dx.reshape(1, t, D_MODEL)

    small_full = [jnp.stack([smalls[i][n] for i in range(DEPTH)]) for n in SMALL]
    packed = _pack(small_full)
    dev = (2 * chip + lax.axis_index("c")).astype(jnp.int32).reshape(1)
    grads = dict(zip(SMALL, _unpack(_sum_all(_allgather_all(packed), packed, dev), small_full)))
    grads['dn_conv_w'] = lax.dynamic_slice_in_dim(grads['dn_conv_w'], chip * conv_local.shape[-1], conv_local.shape[-1], axis=2)

    delta, new_m, new_v = {}, {}, {}
    for n in BIG:
        grads[n], delta[n], new_m[n], new_v[n] = _adamw_layers(given[n], reduced[0][n], reduced[1][n], given['m_' + n], given['v_' + n],
                                                               name="adamw_" + n)
    like = [given[n] for n in SMALL]
    d, nm, nv = _adamw(_pack(like), _pack([grads[n] for n in SMALL]), _pack([given['m_' + n] for n in SMALL]),
                       _pack([given['v_' + n] for n in SMALL]), name="adamw_small")
    for out, packed in ((delta, d), (new_m, nm), (new_v, nv)):
        out.update(zip(SMALL, _unpack(packed, like)))
    return (loss, grad_x, *[grads[n] for n in WEIGHTS], *[delta[n] for n in WEIGHTS], *[new_m[n] for n in WEIGHTS],
            *[new_v[n] for n in WEIGHTS])
```

```python
import functools
import math

import jax
import jax.numpy as jnp
from jax import lax
from jax.experimental import pallas as pl
from jax.experimental.pallas import tpu as pltpu

F32 = jnp.float32
MXU_DTYPE = jnp.bfloat16
HI = lax.Precision.HIGHEST

D_MODEL = 2048
DEPTH = 2
GW = 512
S5_H = 16
S5_G = GW // S5_H
S5_P = 64
S5_N = S5_G * S5_P
SGU_CHUNK = 128
SGU_HEADS = 8
SGU_HD = GW // SGU_HEADS
POOL_WINDOWS = (2, 4, 8, 16)
POOL_GD = 128
DN_HD = 128
DN_HEADS = 4
DN_CONV = 4
DN_CHUNK = 64
D_FF = 4 * D_MODEL
IN_COLS = 4104
IN_PAD = 4224
LN_EPS = 1e-5
RMS_EPS = 1e-6
L2_EPS = 1e-6
ALPHA = (2 * DEPTH) ** 0.25
ADAM_LR, ADAM_B1, ADAM_B2, ADAM_EPS, ADAM_WD, ADAM_STEP = 0.001, 0.9, 0.999, 1e-08, 0.01, 10

VMEM_LIMIT = 56 * 1024 * 1024
MESH = pl.DeviceIdType.MESH


def _params(sem=None, vmem=VMEM_LIMIT):
    return pltpu.CompilerParams(dimension_semantics=sem, vmem_limit_bytes=vmem)


def _full(shape):
    nd = len(shape)
    return pl.BlockSpec(shape, lambda *_: (0,) * nd)


def _split(a):
    hi = a.astype(MXU_DTYPE)
    return hi, (a - hi.astype(F32)).astype(MXU_DTYPE)


def _dot(a, b, dims=(((1,), (0,)), ((), ())), exact=False):
    if exact and MXU_DTYPE == F32:
        return lax.dot_general(a, b, dims, precision=HI, preferred_element_type=F32)
    if exact:
        (ah, al), (bh, bl) = _split(a), _split(b)
        return (lax.dot_general(ah, bh, dims, preferred_element_type=F32) + lax.dot_general(al, bh, dims, preferred_element_type=F32)
                + lax.dot_general(ah, bl, dims, preferred_element_type=F32))
    return lax.dot_general(a.astype(MXU_DTYPE), b.astype(MXU_DTYPE), dims, preferred_element_type=F32)


NN = (((1,), (0,)), ((), ()))
NT = (((1,), (1,)), ((), ()))
TN = (((0,), (0,)), ((), ()))


def _gelu(x):
    c = math.sqrt(2.0 / math.pi)
    return 0.5 * x * (1.0 + jnp.tanh(c * (x + 0.044715 * x * x * x)))


def _gelu_grad(x):
    c = math.sqrt(2.0 / math.pi)
    t = jnp.tanh(c * (x + 0.044715 * x * x * x))
    return 0.5 * (1.0 + t) + 0.5 * x * (1.0 - t * t) * c * (1.0 + 3.0 * 0.044715 * x * x)


def _sigmoid(x):
    return 1.0 / (1.0 + jnp.exp(-x))


def _relu2(x):
    r = jnp.maximum(x, 0.0)
    return r * r


class _Carry:
    def __init__(self, ins, out_shapes, nsem, start, finish):
        self.ins, self.out_shapes, self.nsem, self.start, self.finish = list(ins), list(out_shapes), nsem, start, finish

    def sems(self):
        return [pltpu.SemaphoreType.DMA((self.nsem,)), pltpu.SemaphoreType.DMA((self.nsem,))]


def _run_carry(carry, name):
    n_in, n_out = len(carry.ins), len(carry.out_shapes)

    def body(*refs):
        parts = refs[:n_in], refs[n_in:n_in + n_out], refs[-2], refs[-1]
        carry.start(*parts)
        carry.finish(*parts)

    any_spec = pl.BlockSpec(memory_space=pl.ANY)
    return pl.pallas_call(body, name=name, in_specs=[any_spec] * n_in, out_specs=[any_spec] * n_out, out_shape=carry.out_shapes,
                          scratch_shapes=carry.sems())(*carry.ins)


def _matmul(a, b, *, mode, name, e=None, epi=None, out_dtypes=(F32,), tm=1024, tn=1024, tk=512, carry=None, b_blocked=False,
            out_blocked=False):
    if mode == "nn":
        (m, k), n = a.shape, (4 * b.shape[2] if b_blocked else b.shape[1])
    elif mode == "nt":
        m, n, k = a.shape[0], b.shape[-2], a.shape[1]
    else:
        (k, m), n = a.shape, b.shape[1]
    tm, tn, tk = min(tm, m), min(tn, n), min(tk, k)
    if b_blocked or out_blocked:
        tn, tk = min(tn, n // 4), (min(tk, k // 4) if mode == "nt" and b_blocked else tk)
    assert m % tm == 0 and n % tn == 0 and k % tk == 0, (name, m, n, k, tm, tn, tk)
    nk, nout = k // tk, len(out_dtypes)
    dims = {"nn": NN, "nt": NT, "tn": TN}[mode]
    a_spec = pl.BlockSpec((tk, tm), lambda i, j, l: (l, i)) if mode == "tn" else pl.BlockSpec((tm, tk), lambda i, j, l: (i, l))
    nb, kb = max(n // 4 // tn, 1), max(k // 4 // tk, 1)

    def split(idx, per):
        return lax.div(idx, jnp.int32(per)), lax.rem(idx, jnp.int32(per))

    if b_blocked and mode == "nn":
        b_spec = pl.BlockSpec((None, tk, tn), lambda i, j, l: (split(j, nb)[0], l, split(j, nb)[1]))
    elif b_blocked:
        b_spec = pl.BlockSpec((None, tn, tk), lambda i, j, l: (split(l, kb)[0], j, split(l, kb)[1]))
    else:
        b_spec = pl.BlockSpec((tn, tk), lambda i, j, l: (j, l)) if mode == "nt" else pl.BlockSpec((tk, tn), lambda i, j, l: (l, j))
    if out_blocked:
        o_spec = pl.BlockSpec((None, tm, tn), lambda i, j, l: (split(j, nb)[0], i, split(j, nb)[1]))
    else:
        o_spec = pl.BlockSpec((tm, tn), lambda i, j, l: (i, j))
    assert not (out_blocked and e is not None)
    o_shape = (4, m, n // 4) if out_blocked else (m, n)

    n_in = 2 + (e is not None)
    c_in, c_out = (len(carry.ins), len(carry.out_shapes)) if carry is not None else (0, 0)
    gm, gn = m // tm, n // tn

    def body(*refs):
        a_ref, b_ref = refs[:2]
        e_ref = refs[2] if e is not None else None
        o_refs = refs[n_in + c_in:n_in + c_in + nout]
        acc = refs[n_in + c_in + nout + c_out]
        l = pl.program_id(2)
        if carry is not None:
            parts = refs[n_in:n_in + c_in], refs[n_in + c_in + nout:n_in + c_in + nout + c_out], refs[-2], refs[-1]
            step = (pl.program_id(0) * gn + pl.program_id(1)) * nk + l

            @pl.when(step == 0)
            def _():
                carry.start(*parts)

        d = _dot(a_ref[...], b_ref[...], dims)

        def finish(r):
            outs = (r,) if epi is None else epi(r, None if e_ref is None else e_ref[...])
            for o_ref, o, dt in zip(o_refs, outs, out_dtypes, strict=True):
                o_ref[...] = o.astype(dt)

        if nk == 1:
            finish(d)
        else:
            @pl.when(l == 0)
            def _():
                acc[...] = d

            @pl.when((l > 0) & (l < nk - 1))
            def _():
                acc[...] += d

            @pl.when(l == nk - 1)
            def _():
                finish(acc[...] + d)

        if carry is not None:
            @pl.when(step == gm * gn * nk - 1)
            def _():
                carry.finish(*parts)

    ins, specs = [a, b], [a_spec, b_spec]
    if e is not None:
        ins.append(e)
        specs.append(o_spec)
    out_shape = [jax.ShapeDtypeStruct(o_shape, dt) for dt in out_dtypes]
    out_specs, scratch = [o_spec] * nout, [pltpu.VMEM((tm, tn), F32)]
    if carry is not None:
        any_spec = pl.BlockSpec(memory_space=pl.ANY)
        ins, specs = ins + carry.ins, specs + [any_spec] * c_in
        out_shape, out_specs, scratch = out_shape + carry.out_shapes, out_specs + [any_spec] * c_out, scratch + carry.sems()
    sem = ("parallel", "parallel", "arbitrary") if carry is None else ("arbitrary",) * 3
    res = pl.pallas_call(body, name=name, grid=(gm, gn, nk), in_specs=specs, out_specs=out_specs, out_shape=out_shape,
                         scratch_shapes=scratch, compiler_params=_params(sem))(*ins)
    return tuple(res) if carry is None else (tuple(res[:nout]), list(res[nout:]))


def _ln_fwd(h, g, b, *, name, tr=256):
    t, d = h.shape

    def body(h_ref, g_ref, b_ref, o_ref, o16_ref):
        x = h_ref[...]
        mu = jnp.mean(x, axis=-1, keepdims=True)
        xc = x - mu
        var = jnp.mean(xc * xc, axis=-1, keepdims=True)
        y = xc * lax.rsqrt(var + LN_EPS) * g_ref[...] + b_ref[...]
        o_ref[...] = y
        o16_ref[...] = y.astype(MXU_DTYPE)

    row = pl.BlockSpec((tr, d), lambda i: (i, 0))
    return pl.pallas_call(body, name=name, grid=(t // tr,), in_specs=[row, _full((1, d)), _full((1, d))], out_specs=[row, row],
                          out_shape=[jax.ShapeDtypeStruct((t, d), F32), jax.ShapeDtypeStruct((t, d), MXU_DTYPE)],
                          compiler_params=_params(("parallel",)))(h, g, b)


def _ln_bwd(dy, h, g, *, name, tr=256):
    t, d = h.shape

    def body(dy_ref, h_ref, g_ref, dh_ref, dh16_ref, dg_ref, db_ref):
        @pl.when(pl.program_id(0) == 0)
        def _():
            dg_ref[...] = jnp.zeros_like(dg_ref)
            db_ref[...] = jnp.zeros_like(db_ref)

        x, dyv = h_ref[...], dy_ref[...]
        mu = jnp.mean(x, axis=-1, keepdims=True)
        xc = x - mu
        rstd = lax.rsqrt(jnp.mean(xc * xc, axis=-1, keepdims=True) + LN_EPS)
        xh = xc * rstd
        w = dyv * g_ref[...]
        dh = rstd * (w - jnp.mean(w, axis=-1, keepdims=True) - xh * jnp.mean(w * xh, axis=-1, keepdims=True))
        dh_ref[...] = dh
        dh16_ref[...] = dh.astype(MXU_DTYPE)
        dg_ref[...] += jnp.sum(dyv * xh, axis=0, keepdims=True)
        db_ref[...] += jnp.sum(dyv, axis=0, keepdims=True)

    row = pl.BlockSpec((tr, d), lambda i: (i, 0))
    vec = _full((1, d))
    return pl.pallas_call(
        body, name=name, grid=(t // tr,), in_specs=[row, row, vec], out_specs=[row, row, vec, vec],
        out_shape=[jax.ShapeDtypeStruct((t, d), F32), jax.ShapeDtypeStruct((t, d), MXU_DTYPE), jax.ShapeDtypeStruct((1, d), F32),
                   jax.ShapeDtypeStruct((1, d), F32)],
        compiler_params=_params(("arbitrary",)))(dy, h, g)


def _loss_head(y, target, *, tr=256):
    t, d = y.shape

    def body(y_ref, t_ref, dy_ref, s_ref):
        @pl.when(pl.program_id(0) == 0)
        def _():
            s_ref[...] = jnp.zeros_like(s_ref)

        err = y_ref[...] - t_ref[...]
        dy_ref[...] = err * (1.0 / d)
        s_ref[...] += jnp.sum(err * err, axis=0, keepdims=True)

    row = pl.BlockSpec((tr, d), lambda i: (i, 0))
    return pl.pallas_call(
        body, name="loss_head", grid=(t // tr,), in_specs=[row, row], out_specs=[row, _full((1, d))],
        out_shape=[jax.ShapeDtypeStruct((t, d), F32), jax.ShapeDtypeStruct((1, d), F32)],
        compiler_params=_params(("arbitrary",)))(y, target)


def _cols(tb, width, cb):
    return pl.BlockSpec((tb, width), lambda i: (i, cb))


def _cols_rev(tb, width, cb, nb):
    return pl.BlockSpec((tb, width), lambda i: (nb - 1 - i, cb))


POOL_HALO = 16


def _pool_fwd(proj, w, scale, *, cb, tb=512):
    t = proj.shape[0]
    tb = min(tb, t)

    def body(p_ref, w_ref, s_ref, o_ref, pooled_ref, ext):
        i = pl.program_id(0)

        @pl.when(i == 0)
        def _():
            ext[pl.ds(0, POOL_HALO), :] = jnp.zeros((POOL_HALO, GW), F32)

        p = p_ref[...]
        ext[pl.ds(POOL_HALO, tb), :] = p
        pos = (i * tb + lax.broadcasted_iota(jnp.int32, (tb, 1), 0) + 1).astype(F32)
        for gi, win in enumerate(POOL_WINDOWS):
            c0 = gi * POOL_GD
            s = p[:, c0:c0 + POOL_GD]
            for k in range(1, win):
                s = s + ext[pl.ds(POOL_HALO - k, tb), pl.ds(c0, POOL_GD)]
            pooled = s / jnp.minimum(pos, float(win)) - p[:, c0:c0 + POOL_GD]
            pooled_ref[:, pl.ds(c0, POOL_GD)] = pooled
            o_ref[:, pl.ds(c0, POOL_GD)] = (_dot(pooled, w_ref[gi]) * s_ref[:, pl.ds(c0, POOL_GD)]).astype(o_ref.dtype)
        ext[pl.ds(0, POOL_HALO), :] = p[tb - POOL_HALO:, :]

    row = pl.BlockSpec((tb, GW), lambda i: (i, 0))
    return pl.pallas_call(
        body, name="pool_fwd", grid=(t // tb,),
        in_specs=[_cols(tb, GW, cb), _full((4, POOL_GD, POOL_GD)), _full((1, GW))], out_specs=[row, row],
        out_shape=[jax.ShapeDtypeStruct((t, GW), MXU_DTYPE), jax.ShapeDtypeStruct((t, GW), F32)],
        scratch_shapes=[pltpu.VMEM((tb + POOL_HALO, GW), F32)],
        compiler_params=_params(("arbitrary",)))(proj, w, scale)


def _pool_bwd(dmixed, pooled, w, scale, *, cb, tb=512):
    t = pooled.shape[0]
    tb = min(tb, t)
    nb = t // tb

    def body(dy_ref, pooled_ref, w_ref, s_ref, dp_ref, dw_ref, ds_ref, ext):
        i = pl.program_id(0)

        @pl.when(i == 0)
        def _():
            ext[pl.ds(tb, POOL_HALO), :] = jnp.zeros((POOL_HALO, GW), F32)
            dw_ref[...] = jnp.zeros_like(dw_ref)
            ds_ref[...] = jnp.zeros_like(ds_ref)

        dy = dy_ref[...]
        pos = ((nb - 1 - i) * tb + lax.broadcasted_iota(jnp.int32, (tb, 1), 0) + 1).astype(F32)
        dpool_all = []
        for gi, win in enumerate(POOL_WINDOWS):
            c0 = gi * POOL_GD
            pg = pooled_ref[:, pl.ds(c0, POOL_GD)]
            dyg = dy[:, c0:c0 + POOL_GD]
            ds_ref[:, pl.ds(c0, POOL_GD)] += jnp.sum(dyg * _dot(pg, w_ref[gi]), axis=0, keepdims=True)
            dmp = dyg * s_ref[:, pl.ds(c0, POOL_GD)]
            dw_ref[gi] += _dot(pg, dmp, TN)
            dpool = _dot(dmp, w_ref[gi], NT)
            dpool_all.append(dpool)
            ext[pl.ds(0, tb), pl.ds(c0, POOL_GD)] = dpool / jnp.minimum(pos, float(win))
        for gi, win in enumerate(POOL_WINDOWS):
            c0 = gi * POOL_GD
            s = ext[pl.ds(0, tb), pl.ds(c0, POOL_GD)]
            for k in range(1, win):
                s = s + ext[pl.ds(k, tb), pl.ds(c0, POOL_GD)]
            dp_ref[:, pl.ds(c0, POOL_GD)] = (s - dpool_all[gi]).astype(dp_ref.dtype)
        ext[pl.ds(tb, POOL_HALO), :] = ext[pl.ds(0, POOL_HALO), :]

    row = pl.BlockSpec((tb, GW), lambda i: (nb - 1 - i, 0))
    return pl.pallas_call(
        body, name="pool_bwd", grid=(nb,),
        in_specs=[_cols_rev(tb, GW, cb, nb), row, _full((4, POOL_GD, POOL_GD)), _full((1, GW))],
        out_specs=[row, _full((4, POOL_GD, POOL_GD)), _full((1, GW))],
        out_shape=[jax.ShapeDtypeStruct((t, GW), MXU_DTYPE), jax.ShapeDtypeStruct((4, POOL_GD, POOL_GD), F32),
                   jax.ShapeDtypeStruct((1, GW), F32)],
        scratch_shapes=[pltpu.VMEM((tb + POOL_HALO, GW), F32)], compiler_params=_params(("arbitrary",)))(dmixed, pooled, w, scale)


def _sgu_core(zu, zv, ng, nb, w_ref, bias):
    tb = zu.shape[0]
    u = _gelu(zu)
    v0 = _gelu(zv)
    mu = jnp.mean(v0, axis=-1, keepdims=True)
    vc = v0 - mu
    rstd = lax.rsqrt(jnp.mean(vc * vc, axis=-1, keepdims=True) + LN_EPS)
    xh = vc * rstd
    vn = xh * ng + nb
    low = lax.broadcasted_iota(jnp.int32, (SGU_CHUNK, 2 * SGU_HD), 1) < SGU_HD
    rows = []
    for n in range(tb // SGU_CHUNK):
        pairs = []
        for j in range(SGU_HEADS // 2):
            vp = vn[n * SGU_CHUNK:(n + 1) * SGU_CHUNK, j * 128:(j + 1) * 128]
            pairs.append(jnp.where(low, _dot(w_ref[2 * j], vp), _dot(w_ref[2 * j + 1], vp)))
        rows.append(jnp.concatenate(pairs, axis=1) + bias)
    mixed = jnp.concatenate(rows, axis=0)
    return u, xh, rstd, vn, mixed


def _sgu_fwd(proj, ng, nb, w, bias, *, cbu, cbv, tb=512):
    t = proj.shape[0]
    tb = min(tb, t)

    def body(zu_ref, zv_ref, ng_ref, nb_ref, w_ref, bias_ref, o_ref):
        u, _, _, _, mixed = _sgu_core(zu_ref[...], zv_ref[...], ng_ref[...], nb_ref[...], w_ref, bias_ref[...])
        o_ref[...] = (u * mixed).astype(o_ref.dtype)

    vec = _full((1, GW))
    return pl.pallas_call(
        body, name="sgu_fwd", grid=(t // tb,),
        in_specs=[_cols(tb, GW, cbu), _cols(tb, GW, cbv), vec, vec, _full((8, 128, 128)), _full((128, GW))],
        out_specs=pl.BlockSpec((tb, GW), lambda i: (i, 0)), out_shape=jax.ShapeDtypeStruct((t, GW), MXU_DTYPE),
        compiler_params=_params(("parallel",)))(proj, proj, ng, nb, w, bias)


def _sgu_bwd(dmixed, proj, ng, nb, w, wt, bias, *, cb, cbu, cbv, tb=512):
    t = proj.shape[0]
    tb = min(tb, t)

    def body(dy_ref, zu_ref, zv_ref, ng_ref, nb_ref, w_ref, wt_ref, bias_ref, dzu_ref, dzv_ref, dw_ref, dbias_ref, dng_ref, dnb_ref):
        @pl.when(pl.program_id(0) == 0)
        def _():
            dw_ref[...] = jnp.zeros_like(dw_ref)
            dbias_ref[...] = jnp.zeros_like(dbias_ref)
            dng_ref[...] = jnp.zeros_like(dng_ref)
            dnb_ref[...] = jnp.zeros_like(dnb_ref)

        zu, zv, dy = zu_ref[...], zv_ref[...], dy_ref[...]
        u, xh, rstd, vn, mixed = _sgu_core(zu, zv, ng_ref[...], nb_ref[...], w_ref, bias_ref[...])
        dzu_ref[...] = (dy * mixed * _gelu_grad(zu)).astype(dzu_ref.dtype)
        dmix = dy * u
        low = lax.broadcasted_iota(jnp.int32, (SGU_CHUNK, 2 * SGU_HD), 1) < SGU_HD
        dbias = jnp.zeros((SGU_CHUNK, GW), F32)
        rows = []
        for n in range(tb // SGU_CHUNK):
            dm = dmix[n * SGU_CHUNK:(n + 1) * SGU_CHUNK, :]
            dbias = dbias + dm
            pairs = []
            for j in range(SGU_HEADS // 2):
                dmp = dm[:, j * 128:(j + 1) * 128]
                vp = vn[n * SGU_CHUNK:(n + 1) * SGU_CHUNK, j * 128:(j + 1) * 128]
                dw_ref[2 * j] += _dot(jnp.where(low, dmp, 0.0), vp, NT)
                dw_ref[2 * j + 1] += _dot(jnp.where(low, 0.0, dmp), vp, NT)
                pairs.append(jnp.where(low, _dot(wt_ref[2 * j], dmp), _dot(wt_ref[2 * j + 1], dmp)))
            rows.append(jnp.concatenate(pairs, axis=1))
        dbias_ref[...] += dbias
        dvn = jnp.concatenate(rows, axis=0)
        dng_ref[...] += jnp.sum(dvn * xh, axis=0, keepdims=True)
        dnb_ref[...] += jnp.sum(dvn, axis=0, keepdims=True)
        wv = dvn * ng_ref[...]
        dv0 = rstd * (wv - jnp.mean(wv, axis=-1, keepdims=True) - xh * jnp.mean(wv * xh, axis=-1, keepdims=True))
        dzv_ref[...] = (dv0 * _gelu_grad(zv)).astype(dzv_ref.dtype)

    vec = _full((1, GW))
    row = pl.BlockSpec((tb, GW), lambda i: (i, 0))
    mat = _full((8, 128, 128))
    return pl.pallas_call(
        body, name="sgu_bwd", grid=(t // tb,),
        in_specs=[_cols(tb, GW, cb), _cols(tb, GW, cbu), _cols(tb, GW, cbv), vec, vec, mat, mat, _full((128, GW))],
        out_specs=[row, row, mat, _full((128, GW)), vec, vec],
        out_shape=[jax.ShapeDtypeStruct((t, GW), MXU_DTYPE)] * 2 + [jax.ShapeDtypeStruct((8, 128, 128), F32),
                   jax.ShapeDtypeStruct((128, GW), F32), jax.ShapeDtypeStruct((1, GW), F32), jax.ShapeDtypeStruct((1, GW), F32)],
        compiler_params=_params(("arbitrary",)))(dmixed, proj, proj, ng, nb, w, wt, bias)


S5_KB = 4
SUB = 8


def _s5_discretize(lam_re, lam_im, log_step, b_re, b_im):
    step = jnp.exp(log_step)[:, None]
    mag = jnp.exp(lam_re * step)
    lr, li = mag * jnp.cos(lam_im * step), mag * jnp.sin(lam_im * step)
    den = lam_re * lam_re + lam_im * lam_im
    fr = ((lr - 1.0) * lam_re + li * lam_im) / den
    fi = (li * lam_re - (lr - 1.0) * lam_im) / den
    return lr, li, fr[:, :, None] * b_re - fi[:, :, None] * b_im, fr[:, :, None] * b_im + fi[:, :, None] * b_re


def _cpow(lr, li, n):
    rr, ri = lr, li
    for _ in range(n - 1):
        rr, ri = rr * lr - ri * li, rr * li + ri * lr
    return rr, ri


def _s5_scan_consts(lr, li, reverse):
    lr, li = lr.reshape(1, S5_N), (-li if reverse else li).reshape(1, S5_N)
    row = jnp.arange(SUB)[:, None]
    out = []
    for s in (1, 2, 4):
        pr, pi = _cpow(lr, li, s)
        keep = (row < SUB - s) if reverse else (row >= s)
        out += [jnp.where(keep, pr, 0.0), jnp.where(keep, pi, 0.0)]
    cr, ci = [], []
    for i in range(SUB):
        pr, pi = _cpow(lr, li, SUB - i if reverse else i + 1)
        cr.append(pr)
        ci.append(pi)
    out += [jnp.concatenate(cr, axis=0), jnp.concatenate(ci, axis=0)]
    return jnp.stack(out)


def _s5_blockdiag_in(b):
    bt = jnp.swapaxes(b, 1, 2).reshape(S5_KB, 8, S5_H, S5_P)
    eye = jnp.eye(8, dtype=b.dtype)
    return jnp.einsum("kghp,gj->kghjp", bt, eye).reshape(S5_KB, 128, 512)


def _s5_blockdiag_in_extract(bb):
    x = bb.reshape(S5_KB, 8, S5_H, 8, S5_P)
    d = jnp.einsum("kghgp->kghp", x).reshape(S5_G, S5_H, S5_P)
    return jnp.swapaxes(d, 1, 2)


def _s5_blockdiag_out(c):
    ct = jnp.swapaxes(c, 1, 2).reshape(S5_KB, 8, S5_P, S5_H)
    eye = jnp.eye(8, dtype=c.dtype)
    return jnp.einsum("kgph,gj->kgpjh", ct, eye).reshape(S5_KB, 512, 128)


def _s5_blockdiag_out_extract(cc):
    x = cc.reshape(S5_KB, 8, S5_P, 8, S5_H)
    d = jnp.einsum("kgpgh->kgph", x).reshape(S5_G, S5_P, S5_H)
    return jnp.swapaxes(d, 1, 2)


def _s5_tile_scan(a, b, c_ref, carry, reverse):
    for si, s in enumerate((1, 2, 4)):
        sh = SUB - s if reverse else s
        ar, br = pltpu.roll(a, sh, 0), pltpu.roll(b, sh, 0)
        mr, mi = c_ref[2 * si], c_ref[2 * si + 1]
        a, b = a + mr * ar - mi * br, b + mr * br + mi * ar
    pr, pi = c_ref[6], c_ref[7]
    cr, ci = carry
    return a + pr * cr - pi * ci, b + pr * ci + pi * cr


def _s5_readout(xre_ref, xim_ref, ccre_ref, ccim_ref):
    return jnp.concatenate(
        [_dot(xre_ref[:, pl.ds(512 * k, 512)], ccre_ref[k]) - _dot(xim_ref[:, pl.ds(512 * k, 512)], ccim_ref[k])
         for k in range(S5_KB)], axis=1)


def _s5_fwd(proj, bbre, bbim, ccre, ccim, dvec, consts, glu_w, glu_b, *, cb, tb=256):
    t = proj.shape[0]
    tb = min(tb, t)
    nt = tb // SUB

    def body(u_ref, bbre_ref, bbim_ref, ccre_ref, ccim_ref, d_ref, c_ref, w_ref, b_ref, o_ref, xre_ref, xim_ref, car):
        @pl.when(pl.program_id(0) == 0)
        def _():
            car[...] = jnp.zeros_like(car)

        u = u_ref[...]
        for k in range(S5_KB):
            uk = u[:, 128 * k:128 * (k + 1)]
            xre_ref[:, pl.ds(512 * k, 512)] = _dot(uk, bbre_ref[k])
            xim_ref[:, pl.ds(512 * k, 512)] = _dot(uk, bbim_ref[k])

        def tile(r, carry):
            sl = pl.ds(pl.multiple_of(r * SUB, SUB), SUB)
            a, b = _s5_tile_scan(xre_ref[sl, :], xim_ref[sl, :], c_ref, carry, False)
            xre_ref[sl, :] = a
            xim_ref[sl, :] = b
            return a[SUB - 1:SUB, :], b[SUB - 1:SUB, :]

        cr, ci = lax.fori_loop(0, nt, tile, (car[0:1, :], car[1:2, :]))
        car[0:1, :] = cr
        car[1:2, :] = ci
        ys = _s5_readout(xre_ref, xim_ref, ccre_ref, ccim_ref) + d_ref[...] * u
        yg = _gelu(ys)
        o_ref[...] = (yg * _sigmoid(_dot(yg, w_ref[...]) + b_ref[...])).astype(o_ref.dtype)

    row = pl.BlockSpec((tb, GW), lambda i: (i, 0))
    xrow = pl.BlockSpec((tb, S5_N), lambda i: (i, 0))
    vec = _full((1, GW))
    return pl.pallas_call(
        body, name="s5_fwd", grid=(t // tb,),
        in_specs=[_cols(tb, GW, cb), _full((4, 128, 512)), _full((4, 128, 512)), _full((4, 512, 128)), _full((4, 512, 128)),
                  vec, _full((8, SUB, S5_N)), _full((GW, GW)), vec],
        out_specs=[row, xrow, xrow],
        out_shape=[jax.ShapeDtypeStruct((t, GW), MXU_DTYPE), jax.ShapeDtypeStruct((t, S5_N), F32), jax.ShapeDtypeStruct((t, S5_N), F32)],
        scratch_shapes=[pltpu.VMEM((SUB, S5_N), F32)], compiler_params=_params(("arbitrary",)))(
            proj, bbre, bbim, ccre, ccim, dvec, consts, glu_w, glu_b)


def _s5_bwd(dmixed, proj, xre, xim, bbre, bbim, ccre, ccim, dvec, consts, glu_w, glu_b, *, cb_dy, cb, tb=256):
    t = proj.shape[0]
    tb = min(tb, t)
    nb = t // tb
    nt = tb // SUB

    def body(dy_ref, u_ref, xre_ref, xim_ref, bbre_ref, bbim_ref, ccre_ref, ccim_ref, d_ref, c_ref, w_ref, b_ref,
             du_ref, dw_ref, db_ref, dd_ref, dccre_ref, dccim_ref, dbbre_ref, dbbim_ref, sre_ref, sim_ref, are, aim, car):
        @pl.when(pl.program_id(0) == 0)
        def _():
            car[...] = jnp.zeros_like(car)
            for r in (dw_ref, db_ref, dd_ref, dccre_ref, dccim_ref, dbbre_ref, dbbim_ref, sre_ref, sim_ref):
                r[...] = jnp.zeros_like(r)

        u, dy = u_ref[...], dy_ref[...]
        ys = _s5_readout(xre_ref, xim_ref, ccre_ref, ccim_ref) + d_ref[...] * u
        yg = _gelu(ys)
        sg = _sigmoid(_dot(yg, w_ref[...]) + b_ref[...])
        dz = dy * yg * sg * (1.0 - sg)
        dyg = dy * sg + _dot(dz, w_ref[...], NT)
        dw_ref[...] += _dot(yg, dz, TN)
        db_ref[...] += jnp.sum(dz, axis=0, keepdims=True)
        dys = dyg * _gelu_grad(ys)
        dd_ref[...] += jnp.sum(dys * u, axis=0, keepdims=True)
        for k in range(S5_KB):
            dk = dys[:, 128 * k:128 * (k + 1)]
            lanes = pl.ds(512 * k, 512)
            are[:, lanes] = _dot(dk, ccre_ref[k], NT)
            aim[:, lanes] = -_dot(dk, ccim_ref[k], NT)
            dccre_ref[k] += _dot(xre_ref[:, lanes], dk, TN)
            dccim_ref[k] -= _dot(xim_ref[:, lanes], dk, TN)

        def tile(j, carry):
            sl = pl.ds(pl.multiple_of((nt - 1 - j) * SUB, SUB), SUB)
            gr, gi = are[sl, :], aim[sl, :]
            a, b = _s5_tile_scan(gr, gi, c_ref, carry, True)
            are[sl, :] = a
            aim[sl, :] = b
            er, ei = a - gr, b - gi
            xr, xi = xre_ref[sl, :], xim_ref[sl, :]
            sre_ref[...] += xr * er + xi * ei
            sim_ref[...] += xr * ei - xi * er
            return a[0:1, :], b[0:1, :]

        cr, ci = lax.fori_loop(0, nt, tile, (car[0:1, :], car[1:2, :]))
        car[0:1, :] = cr
        car[1:2, :] = ci
        dus = []
        for k in range(S5_KB):
            uk = u[:, 128 * k:128 * (k + 1)]
            lanes = pl.ds(512 * k, 512)
            dbbre_ref[k] += _dot(uk, are[:, lanes], TN)
            dbbim_ref[k] += _dot(uk, aim[:, lanes], TN)
            dus.append(_dot(are[:, lanes], bbre_ref[k], NT) + _dot(aim[:, lanes], bbim_ref[k], NT))
        du_ref[...] = (d_ref[...] * dys + jnp.concatenate(dus, axis=1)).astype(du_ref.dtype)

    row = pl.BlockSpec((tb, GW), lambda i: (nb - 1 - i, 0))
    xrow = pl.BlockSpec((tb, S5_N), lambda i: (nb - 1 - i, 0))
    vec = _full((1, GW))
    bbs, ccs = _full((4, 128, 512)), _full((4, 512, 128))
    sds = jax.ShapeDtypeStruct
    return pl.pallas_call(
        body, name="s5_bwd", grid=(nb,),
        in_specs=[_cols_rev(tb, GW, cb_dy, nb), _cols_rev(tb, GW, cb, nb), xrow, xrow, bbs, bbs, ccs, ccs, vec,
                  _full((8, SUB, S5_N)), _full((GW, GW)), vec],
        out_specs=[row, _full((GW, GW)), vec, vec, ccs, ccs, bbs, bbs, _full((SUB, S5_N)), _full((SUB, S5_N))],
        out_shape=[sds((t, GW), MXU_DTYPE), sds((GW, GW), F32), sds((1, GW), F32), sds((1, GW), F32), sds((4, 512, 128), F32),
                   sds((4, 512, 128), F32), sds((4, 128, 512), F32), sds((4, 128, 512), F32), sds((SUB, S5_N), F32),
                   sds((SUB, S5_N), F32)],
        scratch_shapes=[pltpu.VMEM((tb, S5_N), F32), pltpu.VMEM((tb, S5_N), F32), pltpu.VMEM((SUB, S5_N), F32)],
        compiler_params=_params(("arbitrary",)))(dmixed, proj, xre, xim, bbre, bbim, ccre, ccim, dvec, consts, glu_w, glu_b)


def _s5_prepare(lam_re, lam_im, log_step, b_re, b_im, c_re, c_im):
    lr, li, bbr, bbi = _s5_discretize(lam_re, lam_im, log_step, b_re, b_im)
    return dict(bbre=_s5_blockdiag_in(bbr), bbim=_s5_blockdiag_in(bbi), ccre=_s5_blockdiag_out(c_re), ccim=_s5_blockdiag_out(c_im),
                cf=_s5_scan_consts(lr, li, False), cr=_s5_scan_consts(lr, li, True))


def _s5_param_grads(lam_re, lam_im, log_step, b_re, b_im, dbbre, dbbim, dccre, dccim, sre, sim):
    (lr, li, _, _), vjp = jax.vjp(_s5_discretize, lam_re, lam_im, log_step, b_re, b_im)
    sr, si = jnp.sum(sre, axis=0).reshape(S5_G, S5_P), jnp.sum(sim, axis=0).reshape(S5_G, S5_P)
    den = lr * lr + li * li
    glr, gli = (sr * lr - si * li) / den, (si * lr + sr * li) / den
    dlam_re, dlam_im, dlog_step, db_re, db_im = vjp((glr, gli, _s5_blockdiag_in_extract(dbbre), _s5_blockdiag_in_extract(dbbim)))
    return dlam_re, dlam_im, dlog_step, db_re, db_im, _s5_blockdiag_out_extract(dccre), _s5_blockdiag_out_extract(dccim)


HALO = 8
AB_CB = 4096 // 128
Q_SCALE = DN_HD ** -0.5


def _halo_prev(tb, width, cb):
    return pl.BlockSpec((HALO, width), lambda i: (jnp.maximum(i * (tb // HALO) - 1, 0), cb))


def _halo_next(tb, width, cb, nrows):
    last = nrows // HALO - 1
    return pl.BlockSpec((HALO, width), lambda i: (jnp.minimum((i + 1) * (tb // HALO), last), cb))


def _silu_parts(c):
    sg = _sigmoid(c)
    return c * sg, sg * (1.0 + c * (1.0 - sg))


def _softplus(x):
    return jnp.maximum(x, 0.0) + jnp.log(1.0 + jnp.exp(-jnp.abs(x)))


def _dn_conv(x_ref, halo_ref, w_ref, part, ext, first):
    tb = x_ref.shape[0]
    ext[pl.ds(0, HALO), :] = jnp.where(first, 0.0, halo_ref[...])
    ext[pl.ds(HALO, tb), :] = x_ref[...]
    c = None
    for j in range(DN_CONV):
        term = w_ref[pl.ds(j, 1), pl.ds(512 * part, 512)] * ext[pl.ds(HALO - (DN_CONV - 1) + j, tb), :]
        c = term if c is None else c + term
    return c


def _dn_gb(ab, alog, dtb):
    lane = lax.broadcasted_iota(jnp.int32, ab.shape, 1)
    pre = ab + dtb
    g = -jnp.exp(alog) * _softplus(pre)
    beta = _sigmoid(ab)
    return jnp.where(lane < DN_HEADS, g, jnp.where(lane < 2 * DN_HEADS, beta, 0.0)), pre, beta


def _dn_prep_fwd(proj, conv_w, alog, dtb, *, cbq, tb=512):
    t = proj.shape[0]
    tb = min(tb, t)

    def body(xq, xk, xv, hq, hk, hv, ab_ref, w_ref, alog_ref, dtb_ref, qn_ref, kn_ref, vs_ref, gb_ref, ext):
        first = pl.program_id(0) == 0
        for part, (x_ref, h_ref, o_ref) in enumerate(((xq, hq, qn_ref), (xk, hk, kn_ref), (xv, hv, vs_ref))):
            s, _ = _silu_parts(_dn_conv(x_ref, h_ref, w_ref, part, ext, first))
            if part < 2:
                scale = Q_SCALE if part == 0 else 1.0
                for h in range(DN_HEADS):
                    sh = s[:, DN_HD * h:DN_HD * (h + 1)]
                    rn = lax.rsqrt(jnp.sum(sh * sh, axis=-1, keepdims=True) + L2_EPS)
                    o_ref[:, pl.ds(DN_HD * h, DN_HD)] = sh * (rn * scale)
            else:
                o_ref[...] = s
        gb_ref[...] = _dn_gb(ab_ref[...], alog_ref[...], dtb_ref[...])[0]

    row = pl.BlockSpec((tb, GW), lambda i: (i, 0))
    small = pl.BlockSpec((tb, 128), lambda i: (i, 0))
    v128 = _full((1, 128))
    sds = jax.ShapeDtypeStruct
    return pl.pallas_call(
        body, name="dn_prep_fwd", grid=(t // tb,),
        in_specs=[_cols(tb, GW, cbq), _cols(tb, GW, cbq + 1), _cols(tb, GW, cbq + 2),
                  _halo_prev(tb, GW, cbq), _halo_prev(tb, GW, cbq + 1), _halo_prev(tb, GW, cbq + 2),
                  _cols(tb, 128, AB_CB), _full((DN_CONV, 3 * GW)), v128, v128],
        out_specs=[row, row, row, small],
        out_shape=[sds((t, GW), F32)] * 3 + [sds((t, 128), F32)],
        scratch_shapes=[pltpu.VMEM((tb + HALO, GW), F32)], compiler_params=_params(("parallel",)))(
            proj, proj, proj, proj, proj, proj, proj, conv_w, alog, dtb)


def _dn_prep_bwd_a(proj, conv_w, alog, dtb, dqn, dkn, dvs, dgb, *, cbq, tb=512):
    t = proj.shape[0]
    tb = min(tb, t)

    def body(xq, xk, xv, hq, hk, hv, ab_ref, w_ref, alog_ref, dtb_ref, dqn_ref, dkn_ref, dvs_ref, dgb_ref,
             dcq_ref, dck_ref, dcv_ref, dab_ref, dalog_ref, ddtb_ref, ext):
        first = pl.program_id(0) == 0

        @pl.when(first)
        def _():
            dalog_ref[...] = jnp.zeros_like(dalog_ref)
            ddtb_ref[...] = jnp.zeros_like(ddtb_ref)

        for part, (x_ref, h_ref, d_ref, o_ref) in enumerate(((xq, hq, dqn_ref, dcq_ref), (xk, hk, dkn_ref, dck_ref), (xv, hv, dvs_ref, dcv_ref))):
            s, ds_dc = _silu_parts(_dn_conv(x_ref, h_ref, w_ref, part, ext, first))
            d = d_ref[...]
            if part < 2:
                scale = Q_SCALE if part == 0 else 1.0
                for h in range(DN_HEADS):
                    lanes = slice(DN_HD * h, DN_HD * (h + 1))
                    sh, dh = s[:, lanes], d[:, lanes]
                    rn = lax.rsqrt(jnp.sum(sh * sh, axis=-1, keepdims=True) + L2_EPS)
                    dsh = scale * (rn * dh - sh * (rn * rn * rn) * jnp.sum(dh * sh, axis=-1, keepdims=True))
                    o_ref[:, pl.ds(DN_HD * h, DN_HD)] = dsh * ds_dc[:, lanes]
            else:
                o_ref[...] = d * ds_dc
        ab, dgb_v = ab_ref[...], dgb_ref[...]
        gb, pre, beta = _dn_gb(ab, alog_ref[...], dtb_ref[...])
        lane = lax.broadcasted_iota(jnp.int32, ab.shape, 1)
        is_g = lane < DN_HEADS
        da = jnp.where(is_g, dgb_v * (-jnp.exp(alog_ref[...])) * _sigmoid(pre), 0.0)
        db = jnp.where((lane >= DN_HEADS) & (lane < 2 * DN_HEADS), dgb_v * beta * (1.0 - beta), 0.0)
        dab_ref[...] = (da + db).astype(dab_ref.dtype)
        ddtb_ref[...] += jnp.sum(da, axis=0, keepdims=True)
        dalog_ref[...] += jnp.sum(jnp.where(is_g, dgb_v * gb, 0.0), axis=0, keepdims=True)

    row = pl.BlockSpec((tb, GW), lambda i: (i, 0))
    small = pl.BlockSpec((tb, 128), lambda i: (i, 0))
    v128 = _full((1, 128))
    sds = jax.ShapeDtypeStruct
    return pl.pallas_call(
        body, name="dn_prep_bwd_a", grid=(t // tb,),
        in_specs=[_cols(tb, GW, cbq), _cols(tb, GW, cbq + 1), _cols(tb, GW, cbq + 2),
                  _halo_prev(tb, GW, cbq), _halo_prev(tb, GW, cbq + 1), _halo_prev(tb, GW, cbq + 2),
                  _cols(tb, 128, AB_CB), _full((DN_CONV, 3 * GW)), v128, v128, row, row, row, small],
        out_specs=[row, row, row, small, v128, v128],
        out_shape=[sds((t, GW), F32)] * 3 + [sds((t, 128), MXU_DTYPE), sds((1, 128), F32), sds((1, 128), F32)],
        scratch_shapes=[pltpu.VMEM((tb + HALO, GW), F32)], compiler_params=_params(("arbitrary",)))(
            proj, proj, proj, proj, proj, proj, proj, conv_w, alog, dtb, dqn, dkn, dvs, dgb)


def _dn_prep_bwd_b(proj, conv_w, dcq, dck, dcv, *, cbq, tb=512):
    t = proj.shape[0]
    tb = min(tb, t)
    nb = t // tb

    def body(xq, xk, xv, hq, hk, hv, dq_in, dk_in, dv_in, nq, nk, nv, w_ref, dq_ref, dk_ref, dv_ref, dw_ref, ext):
        i = pl.program_id(0)
        first, last = i == 0, i == nb - 1

        @pl.when(first)
        def _():
            dw_ref[...] = jnp.zeros_like(dw_ref)

        for part, (x_ref, h_ref, d_ref, n_ref, o_ref) in enumerate(
                ((xq, hq, dq_in, nq, dq_ref), (xk, hk, dk_in, nk, dk_ref), (xv, hv, dv_in, nv, dv_ref))):
            lanes = pl.ds(512 * part, 512)
            d = d_ref[...]
            ext[pl.ds(0, HALO), :] = jnp.where(first, 0.0, h_ref[...])
            ext[pl.ds(HALO, tb), :] = x_ref[...]
            for j in range(DN_CONV):
                xs = ext[pl.ds(HALO - (DN_CONV - 1) + j, tb), :]
                dw_ref[pl.ds(j, 1), lanes] += jnp.sum(d * xs, axis=0, keepdims=True)
            ext[pl.ds(0, tb), :] = d
            ext[pl.ds(tb, HALO), :] = jnp.where(last, 0.0, n_ref[...])
            acc = None
            for j in range(DN_CONV):
                term = w_ref[pl.ds(j, 1), lanes] * ext[pl.ds(DN_CONV - 1 - j, tb), :]
                acc = term if acc is None else acc + term
            o_ref[...] = acc.astype(o_ref.dtype)

    row = pl.BlockSpec((tb, GW), lambda i: (i, 0))
    nxt = _halo_next(tb, GW, 0, t)
    sds = jax.ShapeDtypeStruct
    return pl.pallas_call(
        body, name="dn_prep_bwd_b", grid=(nb,),
        in_specs=[_cols(tb, GW, cbq), _cols(tb, GW, cbq + 1), _cols(tb, GW, cbq + 2),
                  _halo_prev(tb, GW, cbq), _halo_prev(tb, GW, cbq + 1), _halo_prev(tb, GW, cbq + 2),
                  row, row, row, nxt, nxt, nxt, _full((DN_CONV, 3 * GW))],
        out_specs=[row, row, row, _full((DN_CONV, 3 * GW))],
        out_shape=[sds((t, GW), MXU_DTYPE)] * 3 + [sds((DN_CONV, 3 * GW), F32)],
        scratch_shapes=[pltpu.VMEM((tb + HALO, GW), F32)], compiler_params=_params(("arbitrary",)))(
            proj, proj, proj, proj, proj, proj, dcq, dck, dcv, dcq, dck, dcv, conv_w)


CAT = DN_HEADS * DN_CHUNK
DN_LOCAL_CHUNKS = 4


def _iota_div(shape, axis, width):
    return jnp.right_shift(lax.broadcasted_iota(jnp.int32, shape, axis), width.bit_length() - 1)


def _dn_masks():
    r = lax.broadcasted_iota(jnp.int32, (DN_CHUNK, CAT), 0)
    c = jnp.bitwise_and(lax.broadcasted_iota(jnp.int32, (DN_CHUNK, CAT), 1), DN_CHUNK - 1)
    wide = _iota_div((CAT, GW), 0, DN_CHUNK) == _iota_div((CAT, GW), 1, DN_HD)
    square = _iota_div((CAT, CAT), 0, DN_CHUNK) == _iota_div((CAT, CAT), 1, DN_CHUNK)
    return dict(eye=r == c, tril=r >= c, strict=r > c, triu=r <= c, wide=wide, square=square)


def _stack4(x):
    return jnp.concatenate([x, x, x, x], axis=0)


def _diag_blocks(x, mask):
    return jnp.where(mask, _stack4(x), 0.0)


def _fold_blocks(x, mask):
    x = jnp.where(mask, x, 0.0)
    return x[0:64] + x[64:128] + x[128:192] + x[192:256]


def _expand(cols, base, width):
    head = _iota_div((cols.shape[0], DN_HEADS * width), 1, width)
    out = jnp.zeros((cols.shape[0], DN_HEADS * width), F32)
    for h in range(DN_HEADS):
        out = jnp.where(head == h, cols[:, base + h:base + h + 1], out)
    return out


def _head_sums(x, width):
    if width == DN_HD:
        return [jnp.sum(x[:, DN_HD * h:DN_HD * (h + 1)], axis=1, keepdims=True) for h in range(DN_HEADS)]
    head = _iota_div(x.shape, 1, width)
    return [jnp.sum(jnp.where(head == h, x, 0.0), axis=1, keepdims=True) for h in range(DN_HEADS)]


def _cumsum_rows(x):
    row = lax.broadcasted_iota(jnp.int32, x.shape, 0)
    for s in (1, 2, 4, 8, 16, 32):
        x = x + jnp.where(row >= s, pltpu.roll(x, s, 0), 0.0)
    return x


def _tri_inv(ns, square):
    shape = ns[0].shape
    col = jnp.bitwise_and(lax.broadcasted_iota(jnp.int32, shape, 1), DN_CHUNK - 1)
    eye = jnp.where(lax.broadcasted_iota(jnp.int32, shape, 0) == col, 1.0, 0.0)
    xs = [eye - n for n in ns]
    ps = [_dot(n, _diag_blocks(n, square), exact=True) for n in ns]
    for _ in range(4):
        pds = [_diag_blocks(p, square) for p in ps]
        xs = [x + _dot(x, pd, exact=True) for x, pd in zip(xs, pds)]
        ps = [_dot(p, pd, exact=True) for p, pd in zip(ps, pds)]
    return [x + _dot(x, _diag_blocks(p, square), exact=True) for x, p in zip(xs, ps)]


def _dn_local_math(qs, ks, vs, gbvs, m, tms=None):
    out = []
    for k, v, gbv in zip(ks, vs, gbvs):
        gc = _cumsum_rows(gbv)
        gc_cat, gc_wide = _expand(gc, 0, DN_CHUNK), _expand(gc, 0, DN_HD)
        gc_row = jnp.sum(jnp.where(m["eye"], gc_cat, 0.0), axis=0, keepdims=True)
        decay = jnp.where(m["tril"], jnp.exp(jnp.minimum(gc_cat - gc_row, 0.0)), 0.0)
        eg = jnp.exp(gc_wide)
        gl = _expand(gc[DN_CHUNK - 1:DN_CHUNK, :], 0, DN_HD)
        beta = _expand(gbv, DN_HEADS, DN_HD)
        kb = k * beta
        out.append(dict(decay=decay, eg=eg, eg_last=jnp.exp(gl), etail=jnp.exp(gl - gc_wide), beta=beta, kb=kb, vb=v * beta,
                        kbg=kb * eg, k_rows=_diag_blocks(k, m["wide"])))
    for d, q, k in zip(out, qs, ks):
        d.update(qg=q * d["eg"], ktail=k * d["etail"])
    for d in out:
        d["kk"] = _dot(d["kb"], d["k_rows"], NT)
    for d, q in zip(out, qs):
        d["qk"] = _dot(q, d["k_rows"], NT)
    if tms is None:
        tms = _tri_inv([jnp.where(m["strict"], d["kk"] * d["decay"], 0.0) for d in out], m["square"])
    for d, tm in zip(out, tms):
        d["tm"] = tm
    return out


def _dn_local_fwd(qn, kn, vs, gb):
    t = qn.shape[0]
    rows = DN_CHUNK * DN_LOCAL_CHUNKS

    def body(q_ref, k_ref, v_ref, gb_ref, u_ref, wm_ref, qg_ref, kt_ref, qkd_ref, tm_ref):
        masks = _dn_masks()
        chunks = [pl.ds(DN_CHUNK * n, DN_CHUNK) for n in range(DN_LOCAL_CHUNKS)]
        ms = _dn_local_math([q_ref[rs, :] for rs in chunks], [k_ref[rs, :] for rs in chunks], [v_ref[rs, :] for rs in chunks],
                            [gb_ref[rs, :] for rs in chunks], masks)
        us = [_dot(m["tm"], _diag_blocks(m["vb"], masks["wide"])) for m in ms]
        wms = [_dot(m["tm"], _diag_blocks(m["kbg"], masks["wide"])) for m in ms]
        for rs, m, u, wm in zip(chunks, ms, us, wms):
            u_ref[rs, :] = u
            wm_ref[rs, :] = wm.astype(wm_ref.dtype)
            qg_ref[rs, :] = m["qg"].astype(qg_ref.dtype)
            kt_ref[rs, :] = m["ktail"].astype(kt_ref.dtype)
            qkd = (m["qk"] * m["decay"]).astype(qkd_ref.dtype)
            for h in range(DN_HEADS):
                qkd_ref[rs, pl.ds(DN_HD * h, DN_CHUNK)] = qkd[:, DN_CHUNK * h:DN_CHUNK * (h + 1)]
            tm_ref[rs, :] = m["tm"]

    row = pl.BlockSpec((rows, GW), lambda i: (i, 0))
    sds = jax.ShapeDtypeStruct
    return pl.pallas_call(
        body, name="dn_local_fwd", grid=(t // rows,), in_specs=[row, row, row, pl.BlockSpec((rows, 128), lambda i: (i, 0))],
        out_specs=[row] * 5 + [pl.BlockSpec((rows, CAT), lambda i: (i, 0))],
        out_shape=[sds((t, GW), F32)] + [sds((t, GW), MXU_DTYPE)] * 4 + [sds((t, CAT), F32)],
        compiler_params=_params(("parallel",)))(qn, kn, vs, gb)


def _dn_eg_last(gbv, h):
    return jnp.exp(jnp.sum(gbv[:, h:h + 1], axis=0, keepdims=True))


def _dn_seq_fwd(u, wm, qg, ktail, qkd, gb, proj, norm_g, *, cb_gate):
    t = u.shape[0]
    nc = t // DN_CHUNK

    def body(u_ref, wm_ref, qg_ref, kt_ref, qkd_ref, gb_ref, gate_ref, ng_ref, o_ref, raw_ref, vn_ref, st_ref, s_ref):
        @pl.when(pl.program_id(0) == 0)
        def _():
            s_ref[...] = jnp.zeros_like(s_ref)

        gbv = gb_ref[...]
        heads = range(DN_HEADS)
        lanes = [pl.ds(DN_HD * h, DN_HD) for h in heads]
        ss = [s_ref[h] for h in heads]
        for h in heads:
            st_ref[0, h] = ss[h]
        ws = [_dot(wm_ref[:, lanes[h]], ss[h]) for h in heads]
        qs = [_dot(qg_ref[:, lanes[h]], ss[h]) for h in heads]
        v_news = [u_ref[:, lanes[h]] - ws[h] for h in heads]
        ks = [_dot(kt_ref[:, lanes[h]], v_news[h], TN) for h in heads]
        os_ = [qs[h] + _dot(qkd_ref[:, pl.ds(DN_HD * h, DN_CHUNK)], v_news[h]) for h in heads]
        for h in heads:
            s_ref[h] = ss[h] * _dn_eg_last(gbv, h) + ks[h]
            vn_ref[:, lanes[h]] = v_news[h].astype(vn_ref.dtype)
            o = os_[h]
            raw_ref[:, lanes[h]] = o
            r = lax.rsqrt(jnp.mean(o * o, axis=-1, keepdims=True) + RMS_EPS)
            gt = gate_ref[:, lanes[h]]
            o_ref[:, lanes[h]] = (o * r * ng_ref[...] * (gt * _sigmoid(gt))).astype(o_ref.dtype)

    row = pl.BlockSpec((DN_CHUNK, GW), lambda i: (i, 0))
    sds = jax.ShapeDtypeStruct
    return pl.pallas_call(
        body, name="dn_seq_fwd", grid=(nc,),
        in_specs=[row] * 5 + [pl.BlockSpec((DN_CHUNK, 128), lambda i: (i, 0)), _cols(DN_CHUNK, GW, cb_gate), _full((1, DN_HD))],
        out_specs=[row, row, row, pl.BlockSpec((1, DN_HEADS, DN_HD, DN_HD), lambda i: (i, 0, 0, 0))],
        out_shape=[sds((t, GW), MXU_DTYPE), sds((t, GW), F32), sds((t, GW), MXU_DTYPE), sds((nc, DN_HEADS, DN_HD, DN_HD), F32)],
        scratch_shapes=[pltpu.VMEM((DN_HEADS, DN_HD, DN_HD), F32)], compiler_params=_params(("arbitrary",)))(
            u, wm, qg, ktail, qkd, gb, proj, norm_g)


def _dn_seq_bwd(dmixed, raw, proj, norm_g, gb, wm, qg, ktail, qkd, v_new, states, *, cb_dy, cb_gate):
    t = raw.shape[0]
    nc = t // DN_CHUNK

    def body(dy_ref, raw_ref, gate_ref, ng_ref, gb_ref, wm_ref, qg_ref, kt_ref, qkd_ref, vn_ref, st_ref,
             dgate_ref, dng_ref, do_ref, dvn_ref, dwm_ref, dqg_ref, dkt_ref, degl_ref, ds_ref):
        @pl.when(pl.program_id(0) == 0)
        def _():
            ds_ref[...] = jnp.zeros_like(ds_ref)
            dng_ref[...] = jnp.zeros_like(dng_ref)

        gbv = gb_ref[...]
        lane = lax.broadcasted_iota(jnp.int32, (1, 128), 1)
        heads = range(DN_HEADS)
        lanes = [pl.ds(DN_HD * h, DN_HD) for h in heads]
        ss, dss, dos = [st_ref[0, h] for h in heads], [ds_ref[h] for h in heads], []
        for h in heads:
            o, gt, dy, ng = raw_ref[:, lanes[h]], gate_ref[:, lanes[h]], dy_ref[:, lanes[h]], ng_ref[...]
            r = lax.rsqrt(jnp.mean(o * o, axis=-1, keepdims=True) + RMS_EPS)
            sil, dsil = _silu_parts(gt)
            d_on = dy * sil
            dgate_ref[:, lanes[h]] = (dy * (o * r * ng) * dsil).astype(dgate_ref.dtype)
            dng_ref[...] += jnp.sum(d_on * o * r, axis=0, keepdims=True)
            w = d_on * ng
            dos.append(r * w - o * (r * r * r) * jnp.mean(w * o, axis=-1, keepdims=True))
        from_next = [_dot(kt_ref[:, lanes[h]], dss[h]) for h in heads]
        d_vnews = [_dot(qkd_ref[:, pl.ds(DN_HD * h, DN_CHUNK)], dos[h], TN) + from_next[h] for h in heads]
        q_terms = [_dot(qg_ref[:, lanes[h]], dos[h], TN) for h in heads]
        w_terms = [_dot(wm_ref[:, lanes[h]], d_vnews[h], TN) for h in heads]
        degl = jnp.zeros((1, 128), F32)
        for h in heads:
            ds_ref[h] = q_terms[h] + _dn_eg_last(gbv, h) * dss[h] - w_terms[h]
            do_ref[:, lanes[h]] = dos[h].astype(do_ref.dtype)
            dvn_ref[:, lanes[h]] = d_vnews[h].astype(dvn_ref.dtype)
            d_eglast = jnp.sum(jnp.sum(ss[h] * dss[h], axis=1, keepdims=True), axis=0, keepdims=True)
            degl = degl + jnp.where(lane == h, d_eglast, 0.0)
        degl_ref[0] = degl
        for h in heads:
            dwm_ref[:, lanes[h]] = (-_dot(d_vnews[h], ss[h], NT)).astype(dwm_ref.dtype)
        for h in heads:
            dqg_ref[:, lanes[h]] = _dot(dos[h], ss[h], NT)
        for h in heads:
            dkt_ref[:, lanes[h]] = _dot(vn_ref[:, lanes[h]], dss[h], NT)

    row = pl.BlockSpec((DN_CHUNK, GW), lambda i: (nc - 1 - i, 0))
    small = pl.BlockSpec((DN_CHUNK, 128), lambda i: (nc - 1 - i, 0))
    sds = jax.ShapeDtypeStruct
    return pl.pallas_call(
        body, name="dn_seq_bwd", grid=(nc,),
        in_specs=[_cols_rev(DN_CHUNK, GW, cb_dy, nc), row, _cols_rev(DN_CHUNK, GW, cb_gate, nc), _full((1, DN_HD)), small,
                  row, row, row, row, row, pl.BlockSpec((1, DN_HEADS, DN_HD, DN_HD), lambda i: (nc - 1 - i, 0, 0, 0))],
        out_specs=[row, _full((1, DN_HD)), row, row, row, row, row, pl.BlockSpec((1, 1, 128), lambda i: (nc - 1 - i, 0, 0))],
        out_shape=[sds((t, GW), MXU_DTYPE), sds((1, DN_HD), F32), sds((t, GW), MXU_DTYPE), sds((t, GW), MXU_DTYPE),
                   sds((t, GW), MXU_DTYPE), sds((t, GW), F32), sds((t, GW), F32), sds((nc, 1, 128), F32)],
        scratch_shapes=[pltpu.VMEM((DN_HEADS, DN_HD, DN_HD), F32)], compiler_params=_params(("arbitrary",)))(
            dmixed, raw, proj, norm_g, gb, wm, qg, ktail, qkd, v_new, states)


def _dn_local_bwd(qn, kn, vs, gb, tm, v_new, do, d_vnew, d_wm, d_qg, d_ktail, d_eglast):
    t = qn.shape[0]
    rows = DN_CHUNK * DN_LOCAL_CHUNKS

    def body(q_ref, k_ref, v_ref, gb_ref, tm_ref, vn_ref, do_ref, dvn_ref, dwm_ref, dqg_ref, dkt_ref, degl_ref,
             dq_ref, dk_ref, dv_ref, dgb_ref):
        masks = _dn_masks()
        wide, square = masks["wide"], masks["square"]
        lane = lax.broadcasted_iota(jnp.int32, (DN_CHUNK, 128), 1)
        last_row = lax.broadcasted_iota(jnp.int32, (DN_CHUNK, 1), 0) == DN_CHUNK - 1
        chunks = [pl.ds(DN_CHUNK * n, DN_CHUNK) for n in range(DN_LOCAL_CHUNKS)]
        ms = _dn_local_math([q_ref[rs, :] for rs in chunks], [k_ref[rs, :] for rs in chunks], [v_ref[rs, :] for rs in chunks],
                            [gb_ref[rs, :] for rs in chunks], masks, tms=[tm_ref[rs, :] for rs in chunks])

        def work(n, rs, m):
            q, k, v, tm = q_ref[rs, :], k_ref[rs, :], v_ref[rs, :], m["tm"]
            decay, eg, k_rows = m["decay"], m["eg"], m["k_rows"]
            d_vnew, d_wm, d_qg, d_ktail = dvn_ref[rs, :], dwm_ref[rs, :], dqg_ref[rs, :], dkt_ref[rs, :]
            deglv = degl_ref[n]
            dq = d_qg * eg
            dk = d_ktail * m["etail"]
            tails = _head_sums(d_ktail * m["ktail"], DN_HD)
            dgcs = _head_sums(d_qg * m["qg"], DN_HD)
            d_qkd = jnp.where(masks["tril"], _dot(do_ref[rs, :], _diag_blocks(vn_ref[rs, :], wide), NT), 0.0)
            d_tm = _dot(d_vnew, _diag_blocks(m["vb"], wide), NT) + _dot(d_wm, _diag_blocks(m["kbg"], wide), NT)
            d_vb = _fold_blocks(_dot(tm, d_vnew, TN), wide)
            d_kbg = _fold_blocks(_dot(tm, d_wm, TN), wide)
            yield
            dqk_dec = d_qkd * decay
            dq = dq + _dot(dqk_dec, k_rows)
            dk = dk + _fold_blocks(_dot(dqk_dec, q, TN), wide)
            ddecay = d_qkd * m["qk"]
            d_kb = d_kbg * eg
            kbgs = _head_sums(d_kbg * m["kbg"], DN_HD)
            x = _fold_blocks(_dot(tm, d_tm, TN, exact=True), square)
            yield
            d_n = jnp.where(masks["strict"], -_dot(x, _diag_blocks(tm, square), NT, exact=True), 0.0)
            yield
            d_kk = d_n * decay
            d_kb = d_kb + _dot(d_kk, k_rows)
            dk = dk + _fold_blocks(_dot(d_kk, m["kb"], TN), wide)
            yield
            ddecay = ddecay + d_n * m["kk"]
            dk = dk + d_kb * m["beta"]
            dbetas = [a + b for a, b in zip(_head_sums(d_kb * k, DN_HD), _head_sums(d_vb * v, DN_HD))]
            dv_ref[rs, :] = d_vb * m["beta"]
            dq_ref[rs, :] = dq
            dk_ref[rs, :] = dk
            dd = ddecay * decay
            row_sums = _head_sums(dd, DN_CHUNK)
            dgc_cols = jnp.zeros((DN_CHUNK, 128), F32)
            for h in range(DN_HEADS):
                dgl = jnp.sum(tails[h], axis=0, keepdims=True) + deglv[:, h:h + 1] * m["eg_last"][:, DN_HD * h:DN_HD * h + 1]
                dgc_cols = jnp.where(lane == h, dgcs[h] - tails[h] + kbgs[h] + row_sums[h] + jnp.where(last_row, dgl, 0.0), dgc_cols)
            dgc_row = (jnp.sum(jnp.where(masks["eye"], _expand(dgc_cols, 0, DN_CHUNK), 0.0), axis=0, keepdims=True)
                       - jnp.sum(dd, axis=0, keepdims=True))
            dgs = _head_sums(jnp.where(masks["triu"], dgc_row, 0.0), DN_CHUNK)
            dgb = jnp.zeros((DN_CHUNK, 128), F32)
            for h in range(DN_HEADS):
                dgb = jnp.where(lane == h, dgs[h], jnp.where(lane == DN_HEADS + h, dbetas[h], dgb))
            dgb_ref[rs, :] = dgb

        running = [work(n, rs, m) for n, (rs, m) in enumerate(zip(chunks, ms))]
        while running:
            running = [g for g in running if next(g, "done") != "done"]

    row = pl.BlockSpec((rows, GW), lambda i: (i, 0))
    small = pl.BlockSpec((rows, 128), lambda i: (i, 0))
    sds = jax.ShapeDtypeStruct
    return pl.pallas_call(
        body, name="dn_local_bwd", grid=(t // rows,),
        in_specs=[row, row, row, small, pl.BlockSpec((rows, CAT), lambda i: (i, 0))] + [row] * 6
        + [pl.BlockSpec((DN_LOCAL_CHUNKS, 1, 128), lambda i: (i, 0, 0))],
        out_specs=[row, row, row, small], out_shape=[sds((t, GW), F32)] * 3 + [sds((t, 128), F32)],
        compiler_params=_params(("parallel",)))(qn, kn, vs, gb, tm, v_new, do, d_vnew, d_wm, d_qg, d_ktail, d_eglast)


ANY = pl.BlockSpec(memory_space=pl.ANY)
PAIR_SPLIT = 4


def _place():
    x, y, c = lax.axis_index("x"), lax.axis_index("y"), lax.axis_index("c")
    chips = [(1 - x, y), (x, 1 - y), (1 - x, 1 - y)]
    return x, y, c, chips


def _remote(src, dst, send_sem, recv_sem, to):
    return pltpu.make_async_remote_copy(src_ref=src, dst_ref=dst, send_sem=send_sem, recv_sem=recv_sem, device_id=to,
                                        device_id_type=MESH)


def _carry_allgather(arrs):
    n = len(arrs)

    def sends(ins, outs, send_sems, recv_sems):
        x, y, c, chips = _place()
        me = 2 * x + y
        out = []
        for a in range(n):
            half = ins[a].shape[0] // 2
            mine = pl.ds(c * half, half)
            out += [_remote(ins[a].at[mine], outs[a].at[me, mine], send_sems.at[6 * a + k], recv_sems.at[6 * a + k], (*chip, c))
                    for k, chip in enumerate(chips)]
        return out

    def start(*parts):
        for s in sends(*parts):
            s.start()

    def finish(ins, outs, send_sems, recv_sems):
        x, y, c, chips = _place()
        sibling = (x, y, 1 - c)
        fwds = []
        for a in range(n):
            half = ins[a].shape[0] // 2
            mine = pl.ds(c * half, half)
            for k, (cx, cy) in enumerate(chips):
                got = outs[a].at[2 * cx + cy, mine]
                _remote(got, got, send_sems.at[6 * a + k], recv_sems.at[6 * a + k], (cx, cy, c)).wait_recv()
                f = _remote(got, got, send_sems.at[6 * a + 3 + k], recv_sems.at[6 * a + 3 + k], sibling)
                f.start()
                fwds.append(f)
        for a in range(n):
            half = ins[a].shape[0] // 2
            other = pl.ds((1 - c) * half, half)
            for k, (cx, cy) in enumerate(chips):
                got = outs[a].at[2 * cx + cy, other]
                _remote(got, got, send_sems.at[6 * a + 3 + k], recv_sems.at[6 * a + 3 + k], sibling).wait_recv()
        for s in sends(ins, outs, send_sems, recv_sems) + fwds:
            s.wait_send()

    return _Carry(arrs, [jax.ShapeDtypeStruct((4,) + a.shape, a.dtype) for a in arrs], 6 * n, start, finish)


def _carry_pair_exchange(gbs):
    n = len(gbs)

    def copies(ins, outs, send_sems, recv_sems):
        x, y, c, _ = _place()
        out = []
        for a in range(n):
            half = ins[a].shape[1] // 2
            piece = half // PAIR_SPLIT
            out += [_remote(ins[a].at[:, pl.ds((1 - c) * half + r * piece, piece)], outs[a].at[:, pl.ds(r * piece, piece)],
                            send_sems.at[PAIR_SPLIT * a + r], recv_sems.at[PAIR_SPLIT * a + r], (x, y, 1 - c))
                    for r in range(PAIR_SPLIT)]
        return out

    def start(*parts):
        for s in copies(*parts):
            s.start()

    def finish(*parts):
        for s in copies(*parts):
            s.wait()

    return _Carry(gbs, [jax.ShapeDtypeStruct((4, g.shape[1] // 2, g.shape[2]), g.dtype) for g in gbs], PAIR_SPLIT * n, start, finish)


def _carry_chip_exchange(ps):
    n = len(ps)

    def copies(ins, outs, send_sems, recv_sems):
        x, y, c, chips = _place()
        return [_remote(ins[a].at[2 * cx + cy], outs[a].at[k], send_sems.at[3 * a + k], recv_sems.at[3 * a + k], (cx, cy, c))
                for a in range(n) for k, (cx, cy) in enumerate(chips)]

    def start(*parts):
        for s in copies(*parts):
            s.start()

    def finish(*parts):
        for s in copies(*parts):
            s.wait()

    return _Carry(ps, [jax.ShapeDtypeStruct((3,) + p.shape[1:], p.dtype) for p in ps], 3 * n, start, finish)


def _pair_join(bufs, *, name):
    n = len(bufs)

    def body(*refs):
        outs = refs[n:2 * n]
        send_sems, recv_sems = refs[2 * n:]
        x, y, c, _ = _place()
        work = []
        for a in range(n):
            s = _remote(outs[a].at[c], outs[a].at[c], send_sems.at[a], recv_sems.at[a], (x, y, 1 - c))
            s.start()
            work.append(s)
        for s in work:
            s.wait()

    return pl.pallas_call(
        body, name=name, in_specs=[ANY] * n, out_specs=[ANY] * n,
        out_shape=[jax.ShapeDtypeStruct(b.shape, b.dtype) for b in bufs], input_output_aliases={a: a for a in range(n)},
        scratch_shapes=[pltpu.SemaphoreType.DMA((n,)), pltpu.SemaphoreType.DMA((n,))])(*bufs)


def _pair_sum(gb, got, place, *, name, block_bytes=1 << 20):
    _, r, cols = gb.shape
    half = r // 2
    tr = _row_tile(half, cols, block_bytes)

    def body(place_ref, g_ref, got_ref, o_ref):
        o_ref[...] = (g_ref[...] + got_ref[...]).astype(o_ref.dtype)

    blk = pl.BlockSpec((None, tr, cols), lambda j, i, p: (j, i, 0))
    grid_spec = pltpu.PrefetchScalarGridSpec(
        num_scalar_prefetch=1, grid=(4, half // tr),
        in_specs=[pl.BlockSpec((None, None, tr, cols), lambda j, i, p: (j, p[0], i, 0)), blk], out_specs=blk)
    return pl.pallas_call(body, name=name, grid_spec=grid_spec, out_shape=jax.ShapeDtypeStruct((4, half, cols), MXU_DTYPE),
                          compiler_params=_params(("parallel", "parallel")))(place, gb.reshape(4, 2, half, cols), got)


def _chip_sum(gb, got_pair, got, place, *, name, block_bytes=1 << 20):
    _, r, cols = gb.shape
    h = r // 2
    tr = _row_tile(h, cols, block_bytes)

    def body(place_ref, g_ref, gp_ref, g0, g1, g2, o_ref):
        o_ref[...] = (g_ref[...] + gp_ref[...]) + g0[...].astype(F32) + g1[...].astype(F32) + g2[...].astype(F32)

    def got_spec(k):
        return pl.BlockSpec((None, tr, cols), functools.partial(lambda i, pr, k: (k, i, 0), k=k))

    grid_spec = pltpu.PrefetchScalarGridSpec(
        num_scalar_prefetch=1, grid=(h // tr,),
        in_specs=[pl.BlockSpec((None, None, tr, cols), lambda i, pr: (pr[1], pr[0], i, 0)),
                  pl.BlockSpec((None, tr, cols), lambda i, pr: (pr[1], i, 0)), got_spec(0), got_spec(1), got_spec(2)],
        out_specs=pl.BlockSpec((None, tr, cols), lambda i, pr: (pr[0], i, 0)))
    return pl.pallas_call(body, name=name, grid_spec=grid_spec, out_shape=jax.ShapeDtypeStruct((2, h, cols), F32),
                          compiler_params=_params(("parallel",)))(place, gb.reshape(4, 2, h, cols), got_pair, got, got, got)


def _allgather_all(v):
    def body(v_ref, out_ref, send_sems, recv_sems):
        x, y, c, chips = _place()
        me, sibling = (x, y, c), (x, y, 1 - c)

        def rows(px, py, pc):
            return out_ref.at[4 * px + 2 * py + pc]

        def copy(k, block, to, src=None):
            return _remote(rows(*block) if src is None else src, rows(*block), send_sems.at[k], recv_sems.at[k], to)

        first = [copy(0, me, sibling, src=v_ref)] + [copy(1 + j, me, (*chip, c), src=v_ref) for j, chip in enumerate(chips)]
        for cp in first:
            cp.start()
        passed = [copy(4 + j, (*chip, c), sibling) for j, chip in enumerate(chips)]
        for j, chip in enumerate(chips):
            copy(1 + j, (*chip, c), me).wait_recv()
            passed[j].start()
        copy(0, sibling, me).wait_recv()
        for j, chip in enumerate(chips):
            copy(4 + j, (*chip, 1 - c), me).wait_recv()
        for cp in first + passed:
            cp.wait_send()

    return pl.pallas_call(
        body, name="allgather_small", in_specs=[ANY], out_specs=ANY, out_shape=jax.ShapeDtypeStruct((8,) + v.shape, v.dtype),
        scratch_shapes=[pltpu.SemaphoreType.DMA((7,)), pltpu.SemaphoreType.DMA((7,))])(v)


def _row_tile(rows, cols, limit_bytes):
    for d in range(1, rows + 1):
        if rows % d == 0 and (rows // d) % 8 == 0 and (rows // d) * cols * 4 <= limit_bytes:
            return rows // d
    return rows


def _adamw_math(w, gv, m, v):
    nm = ADAM_B1 * m + (1.0 - ADAM_B1) * gv
    nv = ADAM_B2 * v + (1.0 - ADAM_B2) * (gv * gv)
    m_hat = nm / (1.0 - ADAM_B1 ** ADAM_STEP)
    v_hat = nv / (1.0 - ADAM_B2 ** ADAM_STEP)
    return -ADAM_LR * (m_hat / (jnp.sqrt(v_hat) + ADAM_EPS) + ADAM_WD * w), nm, nv


def _adamw_layers(w, g0, g1, m, v, *, name, block_bytes=1 << 20):
    _, rows, cols = w.shape
    tr = _row_tile(rows, cols, block_bytes)

    def body(w_ref, g0_ref, g1_ref, m_ref, v_ref, g_ref, d_ref, nm_ref, nv_ref):
        gv = jnp.where(pl.program_id(0) == 0, g0_ref[...], g1_ref[...])
        g_ref[...] = gv
        d_ref[...], nm_ref[...], nv_ref[...] = _adamw_math(w_ref[...], gv, m_ref[...], v_ref[...])

    both = pl.BlockSpec((None, tr, cols), lambda l, i: (l, i, 0))
    specs = [both, pl.BlockSpec((tr, cols), lambda l, i: (i * (1 - l), 0)), pl.BlockSpec((tr, cols), lambda l, i: (i * l, 0)), both, both]
    return pl.pallas_call(body, name=name, grid=(2, rows // tr), in_specs=specs, out_specs=[both] * 4,
                          out_shape=[jax.ShapeDtypeStruct(w.shape, F32)] * 4, compiler_params=_params(("arbitrary", "arbitrary")))(
                              w, g0, g1, m, v)


def _adamw(w, g, m, v, *, name, block_bytes=1 << 20):
    rows, cols = w.shape
    tr = _row_tile(rows, cols, block_bytes)

    def body(w_ref, g_ref, m_ref, v_ref, d_ref, nm_ref, nv_ref):
        d_ref[...], nm_ref[...], nv_ref[...] = _adamw_math(w_ref[...], g_ref[...], m_ref[...], v_ref[...])

    spec = pl.BlockSpec((tr, cols), lambda i: (i, 0))
    return pl.pallas_call(body, name=name, grid=(rows // tr,), in_specs=[spec] * 4, out_specs=[spec] * 3,
                          out_shape=[jax.ShapeDtypeStruct((rows, cols), F32)] * 3, compiler_params=_params(("parallel",)))(w, g, m, v)


WEIGHTS = ['w_in', 's5_lambda_re', 's5_lambda_im', 's5_log_step', 's5_b_re', 's5_b_im', 's5_c_re', 's5_c_im', 's5_d', 's5_glu_w',
           's5_glu_b', 'sgu_norm_g', 'sgu_norm_b', 'sgu_w', 'sgu_b', 'pool_w', 'pool_scale', 'dn_conv_w', 'dn_a_log', 'dn_dt_bias',
           'dn_norm_g', 'w_out', 'ln1_g', 'ln1_b', 'w_up', 'w_down', 'ln2_g', 'ln2_b']
BIG = ['w_in', 's5_glu_w', 'w_out', 'w_up', 'w_down']
SMALL = [n for n in WEIGHTS if n not in BIG]
CB_S5, CB_SGU_U, CB_SGU_V, CB_POOL, CB_DN_Q, CB_DN_GATE = 0, 1, 2, 3, 4, 7
KT = 2048


def _pad_lanes(v, width=128):
    return jnp.zeros((1, width), F32).at[0, :v.shape[0]].set(v)


def _layer_consts(p):
    c = _s5_prepare(p['s5_lambda_re'], p['s5_lambda_im'], p['s5_log_step'], p['s5_b_re'], p['s5_b_im'], p['s5_c_re'], p['s5_c_im'])
    tril = jnp.tril(jnp.ones((SGU_CHUNK, SGU_CHUNK), bool))
    wm = jnp.where(tril, p['sgu_w'], 0.0)
    c.update(s5_d=p['s5_d'].reshape(1, GW), glu_b=p['s5_glu_b'].reshape(1, GW), sgu_ng=p['sgu_norm_g'].reshape(1, GW),
             sgu_nb=p['sgu_norm_b'].reshape(1, GW), sgu_w=wm, sgu_wt=jnp.swapaxes(wm, 1, 2),
             sgu_bias=jnp.repeat(p['sgu_b'].T, SGU_HD, axis=1), pool_w=p['pool_w'], pool_scale=p['pool_scale'].reshape(1, GW),
             conv_w=p['dn_conv_w'], alog=_pad_lanes(p['dn_a_log']), dtb=_pad_lanes(p['dn_dt_bias']), dn_ng=p['dn_norm_g'].reshape(1, DN_HD),
             ln1_g=p['ln1_g'].reshape(1, D_MODEL), ln1_b=p['ln1_b'].reshape(1, D_MODEL), ln2_g=p['ln2_g'].reshape(1, D_MODEL),
             ln2_b=p['ln2_b'].reshape(1, D_MODEL))
    return c


def _layer_fwd(xin, xin16, w, c, i, carries):
    tag = str(i)
    residual = lambda r, e: (r + ALPHA * e,)

    def mm(a, b_name, *, name, **kw):
        if name not in carries:
            return _matmul(a, w[b_name], name=name + tag, **kw)
        carry, done = carries[name]
        outs, extra = _matmul(a, w[b_name], name=name + tag, carry=carry, **kw)
        done(extra)
        return outs

    (proj,) = mm(xin16, 'w_in', mode="nn", name="proj", tn=1408, tk=KT)
    s5, xre, xim = _s5_fwd(proj, c['bbre'], c['bbim'], c['ccre'], c['ccim'], c['s5_d'], c['cf'], w['s5_glu_w'], c['glu_b'], cb=CB_S5)
    sgu = _sgu_fwd(proj, c['sgu_ng'], c['sgu_nb'], c['sgu_w'], c['sgu_bias'], cbu=CB_SGU_U, cbv=CB_SGU_V)
    pool, pooled = _pool_fwd(proj, c['pool_w'], c['pool_scale'], cb=CB_POOL)
    qn, kn, vs, gb = _dn_prep_fwd(proj, c['conv_w'], c['alog'], c['dtb'], cbq=CB_DN_Q)
    u, wm, qg, ktail, qkd, tm = _dn_local_fwd(qn, kn, vs, gb)
    dn, raw, v_new, states = _dn_seq_fwd(u, wm, qg, ktail, qkd, gb, proj, c['dn_ng'], cb_gate=CB_DN_GATE)
    mixed = jnp.concatenate([s5, sgu, pool, dn], axis=1)
    (h1,) = mm(mixed, 'w_out', mode="nn", name="mix_out", e=xin, epi=residual, tk=KT)
    x1, x1_16 = _ln_fwd(h1, c['ln1_g'], c['ln1_b'], name="ln1_" + tag)
    (hidden,) = mm(x1_16, 'w_up', mode="nn", name="mlp_up", epi=lambda r, e: (_relu2(r),), out_dtypes=(MXU_DTYPE,), tm=2048, tk=KT,
                   b_blocked=True)
    (h2,) = mm(hidden, 'w_down', mode="nn", name="mlp_down", e=x1, epi=residual, tk=KT)
    x2, x2_16 = _ln_fwd(h2, c['ln2_g'], c['ln2_b'], name="ln2_" + tag)
    saved = dict(xin16=xin16, proj=proj, xre=xre, xim=xim, pooled=pooled, qn=qn, kn=kn, vs=vs, gb=gb, raw=raw, states=states,
                 wm=wm, qg=qg, ktail=ktail, qkd=qkd, tm=tm, v_new=v_new, mixed=mixed, h1=h1, x1_16=x1_16, hidden=hidden, h2=h2)
    return x2, x2_16, saved


def _by_rows(g):
    return g.reshape(4, g.shape[0] // 4, g.shape[1])


def _by_cols(g):
    return jnp.transpose(g.reshape(g.shape[0], 4, g.shape[1] // 4), (1, 0, 2))


def _layer_bwd(dx2, s, w, c, p, i, place):
    tag = str(i)
    residual = lambda r, e: (r + ALPHA * e,)
    reduced = {}

    def pair_sums(blocks, got, names):
        return [(g, r, _pair_sum(g, r, place, name="pair_sum_" + nm + tag)) for g, r, nm in zip(blocks, got, names)]

    def pair(blocks, names):
        return pair_sums(blocks, _run_carry(_carry_pair_exchange(blocks), "grad_pair_exchange_" + names[0] + tag), names)

    def riding(ps, names, a, b, **kw):
        outs, got = _matmul(a, b, carry=_carry_chip_exchange([p16 for _, _, p16 in ps]), **kw)
        bufs = _pair_join([_chip_sum(g, r, t, place, name="chip_sum_" + nm + tag) for (g, r, _), t, nm in zip(ps, got, names)],
                          name="grad_pair_join_" + names[0] + tag)
        for nm, buf in zip(names, bufs):
            reduced[nm] = buf.reshape(-1, buf.shape[-1])
        return outs

    dh2, dh2_16, dln2g, dln2b = _ln_bwd(dx2, s['h2'], c['ln2_g'], name="ln2_bwd" + tag)
    (dw_down,) = _matmul(s['hidden'], dh2_16, mode="tn", name="dw_down" + tag, tk=KT)
    (da,), got = _matmul(dh2_16, w['w_down'], mode="nt", name="d_hidden" + tag, e=s['hidden'],
                         epi=lambda r, e: (r * (2.0 * jnp.sqrt(e.astype(F32))),), out_dtypes=(MXU_DTYPE,), tm=2048, tk=KT,
                         carry=_carry_pair_exchange([_by_rows(dw_down)]))
    p_down = pair_sums([_by_rows(dw_down)], got, ['w_down'])
    (dw_up,) = riding(p_down, ['w_down'], s['x1_16'], da, mode="tn", name="dw_up" + tag, tk=KT, out_blocked=True)
    p_up = pair([dw_up], ['w_up'])
    (dx1,) = riding(p_up, ['w_up'], da, w['w_up'], mode="nt", name="dx_mlp" + tag, e=dh2, epi=residual, tk=KT, b_blocked=True)
    dh1, dh1_16, dln1g, dln1b = _ln_bwd(dx1, s['h1'], c['ln1_g'], name="ln1_bwd" + tag)
    (dw_out,) = _matmul(s['mixed'], dh1_16, mode="tn", name="dw_out" + tag, tk=KT)
    p_out = pair([_by_rows(dw_out)], ['w_out'])
    (dmixed,) = riding(p_out, ['w_out'], dh1_16, w['w_out'], mode="nt", name="d_mixed" + tag, tk=KT)
    proj = s['proj']
    (du, dglu_w, dglu_b, dd, dccre, dccim, dbbre, dbbim, sre, sim) = _s5_bwd(
        dmixed, proj, s['xre'], s['xim'], c['bbre'], c['bbim'], c['ccre'], c['ccim'], c['s5_d'], c['cr'], w['s5_glu_w'], c['glu_b'],
        cb_dy=0, cb=CB_S5)
    dlam_re, dlam_im, dlog_step, db_re, db_im, dc_re, dc_im = _s5_param_grads(
        p['s5_lambda_re'], p['s5_lambda_im'], p['s5_log_step'], p['s5_b_re'], p['s5_b_im'], dbbre, dbbim, dccre, dccim, sre, sim)
    dzu, dzv, dsgu_w, dsgu_bias, dsgu_ng, dsgu_nb = _sgu_bwd(dmixed, proj, c['sgu_ng'], c['sgu_nb'], c['sgu_w'], c['sgu_wt'], c['sgu_bias'],
                                                            cb=1, cbu=CB_SGU_U, cbv=CB_SGU_V)
    dp, dpool_w, dpool_scale = _pool_bwd(dmixed, s['pooled'], c['pool_w'], c['pool_scale'], cb=2)
    dgate, ddn_ng, do, d_vnew, d_wm, d_qg, d_ktail, d_eglast = _dn_seq_bwd(
        dmixed, s['raw'], proj, c['dn_ng'], s['gb'], s['wm'], s['qg'], s['ktail'], s['qkd'], s['v_new'], s['states'],
        cb_dy=3, cb_gate=CB_DN_GATE)
    dqn, dkn, dvs, dgb = _dn_local_bwd(s['qn'], s['kn'], s['vs'], s['gb'], s['tm'], s['v_new'], do, d_vnew, d_wm, d_qg, d_ktail, d_eglast)
    dcq, dck, dcv, dab, dalog, ddtb = _dn_prep_bwd_a(proj, c['conv_w'], c['alog'], c['dtb'], dqn, dkn, dvs, dgb, cbq=CB_DN_Q)
    dq, dk, dv, dconv_w = _dn_prep_bwd_b(proj, c['conv_w'], dcq, dck, dcv, cbq=CB_DN_Q)
    dproj = jnp.concatenate([du, dzu, dzv, dp, dq, dk, dv, dgate, dab], axis=1)
    (dw_in,) = _matmul(s['xin16'], dproj, mode="tn", name="dw_in" + tag, tn=1408, tk=KT)
    p_in = pair([_by_cols(dw_in[:, :IN_COLS]), _by_rows(dglu_w)], ['w_in', 's5_glu_w'])
    (dxin,) = riding(p_in, ['w_in', 's5_glu_w'], dproj, w['w_in'], mode="nt", name="dx_in" + tag, tk=1408, e=dh1, epi=residual)
    tril = jnp.tril(jnp.ones((SGU_CHUNK, SGU_CHUNK), bool))
    small = dict(
        s5_lambda_re=dlam_re, s5_lambda_im=dlam_im, s5_log_step=dlog_step, s5_b_re=db_re, s5_b_im=db_im, s5_c_re=dc_re, s5_c_im=dc_im,
        s5_d=dd.reshape(S5_G, S5_H), s5_glu_b=dglu_b[0], sgu_norm_g=dsgu_ng[0], sgu_norm_b=dsgu_nb[0],
        sgu_w=jnp.where(tril, dsgu_w, 0.0), sgu_b=dsgu_bias.reshape(SGU_CHUNK, SGU_HEADS, SGU_HD).sum(-1).T, pool_w=dpool_w,
        pool_scale=dpool_scale[0], dn_conv_w=dconv_w, dn_a_log=dalog[0, :DN_HEADS], dn_dt_bias=ddtb[0, :DN_HEADS], dn_norm_g=ddn_ng[0],
        ln1_g=dln1g[0], ln1_b=dln1b[0], ln2_g=dln2g[0], ln2_b=dln2b[0])
    return dxin, reduced, small


def _pack(arrs):
    rows = []
    for a in arrs:
        n = math.prod(a.shape)
        rows.append(jnp.pad(a.reshape(-1), (0, -n % 128)).reshape(-1, 128))
    out = jnp.concatenate(rows, axis=0)
    return jnp.pad(out, ((0, -out.shape[0] % 8), (0, 0)))


def _sum_all(stacked, mine, dev):
    n, rows, cols = stacked.shape
    tr = _row_tile(rows, cols, 1 << 20)

    def body(dev_ref, mine_ref, *refs):
        acc = None
        for d in range(n):
            blk = jnp.where(dev_ref[0] == d, mine_ref[...], refs[d][...])
            acc = blk if acc is None else acc + blk
        refs[n][...] = acc

    def gathered(d):
        return pl.BlockSpec((None, tr, cols), lambda i, p: (jnp.where(p[0] == d, (d + 1) % n, d), i, 0))

    flat = pl.BlockSpec((tr, cols), lambda i, p: (i, 0))
    grid_spec = pltpu.PrefetchScalarGridSpec(num_scalar_prefetch=1, grid=(rows // tr,),
                                             in_specs=[flat] + [gathered(d) for d in range(n)], out_specs=flat)
    return pl.pallas_call(body, name="small_sum", grid_spec=grid_spec, out_shape=jax.ShapeDtypeStruct((rows, cols), F32),
                          compiler_params=_params(("parallel",)))(dev, mine, *([stacked] * n))


def _unpack(packed, like):
    out, row = [], 0
    for a in like:
        n = math.prod(a.shape)
        rows = -(-n // 128)
        out.append(packed[row:row + rows].reshape(-1)[:n].reshape(a.shape))
        row += rows
    return out


def kernel(x, w_in, s5_lambda_re, s5_lambda_im, s5_log_step, s5_b_re, s5_b_im, s5_c_re, s5_c_im, s5_d, s5_glu_w, s5_glu_b, sgu_norm_g, sgu_norm_b, sgu_w, sgu_b, pool_w, pool_scale, dn_conv_w, dn_a_log, dn_dt_bias, dn_norm_g, w_out, ln1_g, ln1_b, w_up, w_down, ln2_g, ln2_b, loss_target, m_w_in, m_s5_lambda_re, m_s5_lambda_im, m_s5_log_step, m_s5_b_re, m_s5_b_im, m_s5_c_re, m_s5_c_im, m_s5_d, m_s5_glu_w, m_s5_glu_b, m_sgu_norm_g, m_sgu_norm_b, m_sgu_w, m_sgu_b, m_pool_w, m_pool_scale, m_dn_conv_w, m_dn_a_log, m_dn_dt_bias, m_dn_norm_g, m_w_out, m_ln1_g, m_ln1_b, m_w_up, m_w_down, m_ln2_g, m_ln2_b, v_w_in, v_s5_lambda_re, v_s5_lambda_im, v_s5_log_step, v_s5_b_re, v_s5_b_im, v_s5_c_re, v_s5_c_im, v_s5_d, v_s5_glu_w, v_s5_glu_b, v_sgu_norm_g, v_sgu_norm_b, v_sgu_w, v_sgu_b, v_pool_w, v_pool_scale, v_dn_conv_w, v_dn_a_log, v_dn_dt_bias, v_dn_norm_g, v_w_out, v_ln1_g, v_ln1_b, v_w_up, v_w_down, v_ln2_g, v_ln2_b):
    given = dict(locals())
    xs, ys = lax.axis_index("x"), lax.axis_index("y")
    chip = 2 * xs + ys
    t = given['x'].shape[1]
    x0 = given['x'].reshape(t, D_MODEL)
    target = given['loss_target'].reshape(t, D_MODEL)

    assert DEPTH == 2
    place = jnp.stack([lax.axis_index("c"), chip]).astype(jnp.int32)
    conv_local = given['dn_conv_w']
    conv_all = _allgather_all(_pack([conv_local]))
    n_conv = math.prod(conv_local.shape)
    conv_full = jnp.concatenate([jnp.where(chip == j, conv_local, conv_all[2 * j].reshape(-1)[:n_conv].reshape(conv_local.shape))
                                 for j in range(4)], axis=-1)

    ws = [dict(), dict()]

    def whole(n, blocks):
        if n == 'w_in':
            return jnp.pad(jnp.transpose(blocks, (1, 0, 2)).reshape(D_MODEL, IN_COLS), ((0, 0), (0, IN_PAD - IN_COLS)))
        if n == 'w_up':
            return blocks
        return blocks.reshape(-1, blocks.shape[-1])

    def gather(items):
        own = [given[n][i].astype(MXU_DTYPE) for n, i in items]

        def done(bufs):
            for (n, i), buf, mine in zip(items, bufs, own):
                ws[i][n] = whole(n, lax.dynamic_update_slice(buf, mine[None], (chip, 0, 0)))
        return _carry_allgather(own), done

    first, first_done = gather([('w_in', 0), ('s5_glu_w', 0), ('w_out', 0)])
    first_done(_run_carry(first, "allgather_first"))
    carries = [dict(proj=gather([('w_up', 0)]), mix_out=gather([('w_down', 0)]),
                    mlp_up=gather([('w_in', 1), ('s5_glu_w', 1), ('w_out', 1)]), mlp_down=gather([('w_up', 1)])),
               dict(proj=gather([('w_down', 1)]))]

    def layer_params(i):
        p = {n: given[n][i] for n in SMALL}
        p['dn_conv_w'] = conv_full[i]
        return p

    ps = [layer_params(i) for i in range(DEPTH)]
    cs = [_layer_consts(p) for p in ps]

    xcur, xcur16, saved = x0, x0.astype(MXU_DTYPE), []
    for i in range(DEPTH):
        xcur, xcur16, s = _layer_fwd(xcur, xcur16, ws[i], cs[i], i, carries[i])
        saved.append(s)
    dx, colsum = _loss_head(xcur, target)
    loss = lax.psum(0.5 * jnp.sum(colsum) / D_MODEL, ("x", "y", "c"))

    reduced, smalls = [None] * DEPTH, [None] * DEPTH
    for i in reversed(range(DEPTH)):
        dx, reduced[i], smalls[i] = _layer_bwd(dx, saved[i], ws[i], cs[i], ps[i], i, place)
    grad_x = dx.reshape(1, t, D_MODEL)

    small_full = [jnp.stack([smalls[i][n] for i in range(DEPTH)]) for n in SMALL]
    packed = _pack(small_full)
    dev = (2 * chip + lax.axis_index("c")).astype(jnp.int32).reshape(1)
    grads = dict(zip(SMALL, _unpack(_sum_all(_allgather_all(packed), packed, dev), small_full)))
    grads['dn_conv_w'] = lax.dynamic_slice_in_dim(grads['dn_conv_w'], chip * conv_local.shape[-1], conv_local.shape[-1], axis=2)

    delta, new_m, new_v = {}, {}, {}
    for n in BIG:
        grads[n], delta[n], new_m[n], new_v[n] = _adamw_layers(given[n], reduced[0][n], reduced[1][n], given['m_' + n], given['v_' + n],
                                                               name="adamw_" + n)
    like = [given[n] for n in SMALL]
    d, nm, nv = _adamw(_pack(like), _pack([grads[n] for n in SMALL]), _pack([given['m_' + n] for n in SMALL]),
                       _pack([given['v_' + n] for n in SMALL]), name="adamw_small")
    for out, packed in ((delta, d), (new_m, nm), (new_v, nv)):
        out.update(zip(SMALL, _unpack(packed, like)))
    return (loss, grad_x, *[grads[n] for n in WEIGHTS], *[delta[n] for n in WEIGHTS], *[new_m[n] for n in WEIGHTS],
            *[new_v[n] for n in WEIGHTS])
```

```python
import functools
import math

import jax
import jax.numpy as jnp
from jax import lax
from jax.experimental import pallas as pl
from jax.experimental.pallas import tpu as pltpu

F32 = jnp.float32
MXU_DTYPE = jnp.bfloat16
HI = lax.Precision.HIGHEST

D_MODEL = 2048
DEPTH = 2
GW = 512
S5_H = 16
S5_G = GW // S5_H
S5_P = 64
S5_N = S5_G * S5_P
SGU_CHUNK = 128
SGU_HEADS = 8
SGU_HD = GW // SGU_HEADS
POOL_WINDOWS = (2, 4, 8, 16)
POOL_GD = 128
DN_HD = 128
DN_HEADS = 4
DN_CONV = 4
DN_CHUNK = 64
D_FF = 4 * D_MODEL
IN_COLS = 4104
IN_PAD = 4224
LN_EPS = 1e-5
RMS_EPS = 1e-6
L2_EPS = 1e-6
ALPHA = (2 * DEPTH) ** 0.25
ADAM_LR, ADAM_B1, ADAM_B2, ADAM_EPS, ADAM_WD, ADAM_STEP = 0.001, 0.9, 0.999, 1e-08, 0.01, 10

VMEM_LIMIT = 56 * 1024 * 1024
MESH = pl.DeviceIdType.MESH


def _params(sem=None, vmem=VMEM_LIMIT):
    return pltpu.CompilerParams(dimension_semantics=sem, vmem_limit_bytes=vmem)


def _full(shape):
    nd = len(shape)
    return pl.BlockSpec(shape, lambda *_: (0,) * nd)


def _split(a):
    hi = a.astype(MXU_DTYPE)
    return hi, (a - hi.astype(F32)).astype(MXU_DTYPE)


def _dot(a, b, dims=(((1,), (0,)), ((), ())), exact=False):
    if exact and MXU_DTYPE == F32:
        return lax.dot_general(a, b, dims, precision=HI, preferred_element_type=F32)
    if exact:
        (ah, al), (bh, bl) = _split(a), _split(b)
        return (lax.dot_general(ah, bh, dims, preferred_element_type=F32) + lax.dot_general(al, bh, dims, preferred_element_type=F32)
                + lax.dot_general(ah, bl, dims, preferred_element_type=F32))
    return lax.dot_general(a.astype(MXU_DTYPE), b.astype(MXU_DTYPE), dims, preferred_element_type=F32)


NN = (((1,), (0,)), ((), ()))
NT = (((1,), (1,)), ((), ()))
TN = (((0,), (0,)), ((), ()))


def _gelu(x):
    c = math.sqrt(2.0 / math.pi)
    return 0.5 * x * (1.0 + jnp.tanh(c * (x + 0.044715 * x * x * x)))


def _gelu_grad(x):
    c = math.sqrt(2.0 / math.pi)
    t = jnp.tanh(c * (x + 0.044715 * x * x * x))
    return 0.5 * (1.0 + t) + 0.5 * x * (1.0 - t * t) * c * (1.0 + 3.0 * 0.044715 * x * x)


def _sigmoid(x):
    return 1.0 / (1.0 + jnp.exp(-x))


def _relu2(x):
    r = jnp.maximum(x, 0.0)
    return r * r


class _Carry:
    def __init__(self, ins, out_shapes, nsem, start, finish):
        self.ins, self.out_shapes, self.nsem, self.start, self.finish = list(ins), list(out_shapes), nsem, start, finish

    def sems(self):
        return [pltpu.SemaphoreType.DMA((self.nsem,)), pltpu.SemaphoreType.DMA((self.nsem,))]


def _run_carry(carry, name):
    n_in, n_out = len(carry.ins), len(carry.out_shapes)

    def body(*refs):
        parts = refs[:n_in], refs[n_in:n_in + n_out], refs[-2], refs[-1]
        carry.start(*parts)
        carry.finish(*parts)

    any_spec = pl.BlockSpec(memory_space=pl.ANY)
    return pl.pallas_call(body, name=name, in_specs=[any_spec] * n_in, out_specs=[any_spec] * n_out, out_shape=carry.out_shapes,
                          scratch_shapes=carry.sems())(*carry.ins)


def _matmul(a, b, *, mode, name, e=None, epi=None, out_dtypes=(F32,), tm=1024, tn=1024, tk=512, carry=None, b_blocked=False,
            out_blocked=False):
    if mode == "nn":
        (m, k), n = a.shape, (4 * b.shape[2] if b_blocked else b.shape[1])
    elif mode == "nt":
        m, n, k = a.shape[0], b.shape[-2], a.shape[1]
    else:
        (k, m), n = a.shape, b.shape[1]
    tm, tn, tk = min(tm, m), min(tn, n), min(tk, k)
    if b_blocked or out_blocked:
        tn, tk = min(tn, n // 4), (min(tk, k // 4) if mode == "nt" and b_blocked else tk)
    assert m % tm == 0 and n % tn == 0 and k % tk == 0, (name, m, n, k, tm, tn, tk)
    nk, nout = k // tk, len(out_dtypes)
    dims = {"nn": NN, "nt": NT, "tn": TN}[mode]
    a_spec = pl.BlockSpec((tk, tm), lambda i, j, l: (l, i)) if mode == "tn" else pl.BlockSpec((tm, tk), lambda i, j, l: (i, l))
    nb, kb = max(n // 4 // tn, 1), max(k // 4 // tk, 1)

    def split(idx, per):
        return lax.div(idx, jnp.int32(per)), lax.rem(idx, jnp.int32(per))

    if b_blocked and mode == "nn":
        b_spec = pl.BlockSpec((None, tk, tn), lambda i, j, l: (split(j, nb)[0], l, split(j, nb)[1]))
    elif b_blocked:
        b_spec = pl.BlockSpec((None, tn, tk), lambda i, j, l: (split(l, kb)[0], j, split(l, kb)[1]))
    else:
        b_spec = pl.BlockSpec((tn, tk), lambda i, j, l: (j, l)) if mode == "nt" else pl.BlockSpec((tk, tn), lambda i, j, l: (l, j))
    if out_blocked:
        o_spec = pl.BlockSpec((None, tm, tn), lambda i, j, l: (split(j, nb)[0], i, split(j, nb)[1]))
    else:
        o_spec = pl.BlockSpec((tm, tn), lambda i, j, l: (i, j))
    assert not (out_blocked and e is not None)
    o_shape = (4, m, n // 4) if out_blocked else (m, n)

    n_in = 2 + (e is not None)
    c_in, c_out = (len(carry.ins), len(carry.out_shapes)) if carry is not None else (0, 0)
    gm, gn = m // tm, n // tn

    def body(*refs):
        a_ref, b_ref = refs[:2]
        e_ref = refs[2] if e is not None else None
        o_refs = refs[n_in + c_in:n_in + c_in + nout]
        acc = refs[n_in + c_in + nout + c_out]
        l = pl.program_id(2)
        if carry is not None:
            parts = refs[n_in:n_in + c_in], refs[n_in + c_in + nout:n_in + c_in + nout + c_out], refs[-2], refs[-1]
            step = (pl.program_id(0) * gn + pl.program_id(1)) * nk + l

            @pl.when(step == 0)
            def _():
                carry.start(*parts)

        d = _dot(a_ref[...], b_ref[...], dims)

        def finish(r):
            outs = (r,) if epi is None else epi(r, None if e_ref is None else e_ref[...])
            for o_ref, o, dt in zip(o_refs, outs, out_dtypes, strict=True):
                o_ref[...] = o.astype(dt)

        if nk == 1:
            finish(d)
        else:
            @pl.when(l == 0)
            def _():
                acc[...] = d

            @pl.when((l > 0) & (l < nk - 1))
            def _():
                acc[...] += d

            @pl.when(l == nk - 1)
            def _():
                finish(acc[...] + d)

        if carry is not None:
            @pl.when(step == gm * gn * nk - 1)
            def _():
                carry.finish(*parts)

    ins, specs = [a, b], [a_spec, b_spec]
    if e is not None:
        ins.append(e)
        specs.append(o_spec)
    out_shape = [jax.ShapeDtypeStruct(o_shape, dt) for dt in out_dtypes]
    out_specs, scratch = [o_spec] * nout, [pltpu.VMEM((tm, tn), F32)]
    if carry is not None:
        any_spec = pl.BlockSpec(memory_space=pl.ANY)
        ins, specs = ins + carry.ins, specs + [any_spec] * c_in
        out_shape, out_specs, scratch = out_shape + carry.out_shapes, out_specs + [any_spec] * c_out, scratch + carry.sems()
    sem = ("parallel", "parallel", "arbitrary") if carry is None else ("arbitrary",) * 3
    res = pl.pallas_call(body, name=name, grid=(gm, gn, nk), in_specs=specs, out_specs=out_specs, out_shape=out_shape,
                         scratch_shapes=scratch, compiler_params=_params(sem))(*ins)
    return tuple(res) if carry is None else (tuple(res[:nout]), list(res[nout:]))


def _ln_fwd(h, g, b, *, name, tr=256):
    t, d = h.shape

    def body(h_ref, g_ref, b_ref, o_ref, o16_ref):
        x = h_ref[...]
        mu = jnp.mean(x, axis=-1, keepdims=True)
        xc = x - mu
        var = jnp.mean(xc * xc, axis=-1, keepdims=True)
        y = xc * lax.rsqrt(var + LN_EPS) * g_ref[...] + b_ref[...]
        o_ref[...] = y
        o16_ref[...] = y.astype(MXU_DTYPE)

    row = pl.BlockSpec((tr, d), lambda i: (i, 0))
    return pl.pallas_call(body, name=name, grid=(t // tr,), in_specs=[row, _full((1, d)), _full((1, d))], out_specs=[row, row],
                          out_shape=[jax.ShapeDtypeStruct((t, d), F32), jax.ShapeDtypeStruct((t, d), MXU_DTYPE)],
                          compiler_params=_params(("parallel",)))(h, g, b)


def _ln_bwd(dy, h, g, *, name, tr=256):
    t, d = h.shape

    def body(dy_ref, h_ref, g_ref, dh_ref, dh16_ref, dg_ref, db_ref):
        @pl.when(pl.program_id(0) == 0)
        def _():
            dg_ref[...] = jnp.zeros_like(dg_ref)
            db_ref[...] = jnp.zeros_like(db_ref)

        x, dyv = h_ref[...], dy_ref[...]
        mu = jnp.mean(x, axis=-1, keepdims=True)
        xc = x - mu
        rstd = lax.rsqrt(jnp.mean(xc * xc, axis=-1, keepdims=True) + LN_EPS)
        xh = xc * rstd
        w = dyv * g_ref[...]
        dh = rstd * (w - jnp.mean(w, axis=-1, keepdims=True) - xh * jnp.mean(w * xh, axis=-1, keepdims=True))
        dh_ref[...] = dh
        dh16_ref[...] = dh.astype(MXU_DTYPE)
        dg_ref[...] += jnp.sum(dyv * xh, axis=0, keepdims=True)
        db_ref[...] += jnp.sum(dyv, axis=0, keepdims=True)

    row = pl.BlockSpec((tr, d), lambda i: (i, 0))
    vec = _full((1, d))
    return pl.pallas_call(
        body, name=name, grid=(t // tr,), in_specs=[row, row, vec], out_specs=[row, row, vec, vec],
        out_shape=[jax.ShapeDtypeStruct((t, d), F32), jax.ShapeDtypeStruct((t, d), MXU_DTYPE), jax.ShapeDtypeStruct((1, d), F32),
                   jax.ShapeDtypeStruct((1, d), F32)],
        compiler_params=_params(("arbitrary",)))(dy, h, g)


def _loss_head(y, target, *, tr=256):
    t, d = y.shape

    def body(y_ref, t_ref, dy_ref, s_ref):
        @pl.when(pl.program_id(0) == 0)
        def _():
            s_ref[...] = jnp.zeros_like(s_ref)

        err = y_ref[...] - t_ref[...]
        dy_ref[...] = err * (1.0 / d)
        s_ref[...] += jnp.sum(err * err, axis=0, keepdims=True)

    row = pl.BlockSpec((tr, d), lambda i: (i, 0))
    return pl.pallas_call(
        body, name="loss_head", grid=(t // tr,), in_specs=[row, row], out_specs=[row, _full((1, d))],
        out_shape=[jax.ShapeDtypeStruct((t, d), F32), jax.ShapeDtypeStruct((1, d), F32)],
        compiler_params=_params(("arbitrary",)))(y, target)


def _cols(tb, width, cb):
    return pl.BlockSpec((tb, width), lambda i: (i, cb))


def _cols_rev(tb, width, cb, nb):
    return pl.BlockSpec((tb, width), lambda i: (nb - 1 - i, cb))


POOL_HALO = 16


def _pool_fwd(proj, w, scale, *, cb, tb=512):
    t = proj.shape[0]
    tb = min(tb, t)

    def body(p_ref, w_ref, s_ref, o_ref, pooled_ref, ext):
        i = pl.program_id(0)

        @pl.when(i == 0)
        def _():
            ext[pl.ds(0, POOL_HALO), :] = jnp.zeros((POOL_HALO, GW), F32)

        p = p_ref[...]
        ext[pl.ds(POOL_HALO, tb), :] = p
        pos = (i * tb + lax.broadcasted_iota(jnp.int32, (tb, 1), 0) + 1).astype(F32)
        for gi, win in enumerate(POOL_WINDOWS):
            c0 = gi * POOL_GD
            s = p[:, c0:c0 + POOL_GD]
            for k in range(1, win):
                s = s + ext[pl.ds(POOL_HALO - k, tb), pl.ds(c0, POOL_GD)]
            pooled = s / jnp.minimum(pos, float(win)) - p[:, c0:c0 + POOL_GD]
            pooled_ref[:, pl.ds(c0, POOL_GD)] = pooled
            o_ref[:, pl.ds(c0, POOL_GD)] = (_dot(pooled, w_ref[gi]) * s_ref[:, pl.ds(c0, POOL_GD)]).astype(o_ref.dtype)
        ext[pl.ds(0, POOL_HALO), :] = p[tb - POOL_HALO:, :]

    row = pl.BlockSpec((tb, GW), lambda i: (i, 0))
    return pl.pallas_call(
        body, name="pool_fwd", grid=(t // tb,),
        in_specs=[_cols(tb, GW, cb), _full((4, POOL_GD, POOL_GD)), _full((1, GW))], out_specs=[row, row],
        out_shape=[jax.ShapeDtypeStruct((t, GW), MXU_DTYPE), jax.ShapeDtypeStruct((t, GW), F32)],
        scratch_shapes=[pltpu.VMEM((tb + POOL_HALO, GW), F32)],
        compiler_params=_params(("arbitrary",)))(proj, w, scale)


def _pool_bwd(dmixed, pooled, w, scale, *, cb, tb=512):
    t = pooled.shape[0]
    tb = min(tb, t)
    nb = t // tb

    def body(dy_ref, pooled_ref, w_ref, s_ref, dp_ref, dw_ref, ds_ref, ext):
        i = pl.program_id(0)

        @pl.when(i == 0)
        def _():
            ext[pl.ds(tb, POOL_HALO), :] = jnp.zeros((POOL_HALO, GW), F32)
            dw_ref[...] = jnp.zeros_like(dw_ref)
            ds_ref[...] = jnp.zeros_like(ds_ref)

        dy = dy_ref[...]
        pos = ((nb - 1 - i) * tb + lax.broadcasted_iota(jnp.int32, (tb, 1), 0) + 1).astype(F32)
        dpool_all = []
        for gi, win in enumerate(POOL_WINDOWS):
            c0 = gi * POOL_GD
            pg = pooled_ref[:, pl.ds(c0, POOL_GD)]
            dyg = dy[:, c0:c0 + POOL_GD]
            ds_ref[:, pl.ds(c0, POOL_GD)] += jnp.sum(dyg * _dot(pg, w_ref[gi]), axis=0, keepdims=True)
            dmp = dyg * s_ref[:, pl.ds(c0, POOL_GD)]
            dw_ref[gi] += _dot(pg, dmp, TN)
            dpool = _dot(dmp, w_ref[gi], NT)
            dpool_all.append(dpool)
            ext[pl.ds(0, tb), pl.ds(c0, POOL_GD)] = dpool / jnp.minimum(pos, float(win))
        for gi, win in enumerate(POOL_WINDOWS):
            c0 = gi * POOL_GD
            s = ext[pl.ds(0, tb), pl.ds(c0, POOL_GD)]
            for k in range(1, win):
                s = s + ext[pl.ds(k, tb), pl.ds(c0, POOL_GD)]
            dp_ref[:, pl.ds(c0, POOL_GD)] = (s - dpool_all[gi]).astype(dp_ref.dtype)
        ext[pl.ds(tb, POOL_HALO), :] = ext[pl.ds(0, POOL_HALO), :]

    row = pl.BlockSpec((tb, GW), lambda i: (nb - 1 - i, 0))
    return pl.pallas_call(
        body, name="pool_bwd", grid=(nb,),
        in_specs=[_cols_rev(tb, GW, cb, nb), row, _full((4, POOL_GD, POOL_GD)), _full((1, GW))],
        out_specs=[row, _full((4, POOL_GD, POOL_GD)), _full((1, GW))],
        out_shape=[jax.ShapeDtypeStruct((t, GW), MXU_DTYPE), jax.ShapeDtypeStruct((4, POOL_GD, POOL_GD), F32),
                   jax.ShapeDtypeStruct((1, GW), F32)],
        scratch_shapes=[pltpu.VMEM((tb + POOL_HALO, GW), F32)], compiler_params=_params(("arbitrary",)))(dmixed, pooled, w, scale)


def _sgu_core(zu, zv, ng, nb, w_ref, bias):
    tb = zu.shape[0]
    u = _gelu(zu)
    v0 = _gelu(zv)
    mu = jnp.mean(v0, axis=-1, keepdims=True)
    vc = v0 - mu
    rstd = lax.rsqrt(jnp.mean(vc * vc, axis=-1, keepdims=True) + LN_EPS)
    xh = vc * rstd
    vn = xh * ng + nb
    low = lax.broadcasted_iota(jnp.int32, (SGU_CHUNK, 2 * SGU_HD), 1) < SGU_HD
    rows = []
    for n in range(tb // SGU_CHUNK):
        pairs = []
        for j in range(SGU_HEADS // 2):
            vp = vn[n * SGU_CHUNK:(n + 1) * SGU_CHUNK, j * 128:(j + 1) * 128]
            pairs.append(jnp.where(low, _dot(w_ref[2 * j], vp), _dot(w_ref[2 * j + 1], vp)))
        rows.append(jnp.concatenate(pairs, axis=1) + bias)
    mixed = jnp.concatenate(rows, axis=0)
    return u, xh, rstd, vn, mixed


def _sgu_fwd(proj, ng, nb, w, bias, *, cbu, cbv, tb=512):
    t = proj.shape[0]
    tb = min(tb, t)

    def body(zu_ref, zv_ref, ng_ref, nb_ref, w_ref, bias_ref, o_ref):
        u, _, _, _, mixed = _sgu_core(zu_ref[...], zv_ref[...], ng_ref[...], nb_ref[...], w_ref, bias_ref[...])
        o_ref[...] = (u * mixed).astype(o_ref.dtype)

    vec = _full((1, GW))
    return pl.pallas_call(
        body, name="sgu_fwd", grid=(t // tb,),
        in_specs=[_cols(tb, GW, cbu), _cols(tb, GW, cbv), vec, vec, _full((8, 128, 128)), _full((128, GW))],
        out_specs=pl.BlockSpec((tb, GW), lambda i: (i, 0)), out_shape=jax.ShapeDtypeStruct((t, GW), MXU_DTYPE),
        compiler_params=_params(("parallel",)))(proj, proj, ng, nb, w, bias)


def _sgu_bwd(dmixed, proj, ng, nb, w, wt, bias, *, cb, cbu, cbv, tb=512):
    t = proj.shape[0]
    tb = min(tb, t)

    def body(dy_ref, zu_ref, zv_ref, ng_ref, nb_ref, w_ref, wt_ref, bias_ref, dzu_ref, dzv_ref, dw_ref, dbias_ref, dng_ref, dnb_ref):
        @pl.when(pl.program_id(0) == 0)
        def _():
            dw_ref[...] = jnp.zeros_like(dw_ref)
            dbias_ref[...] = jnp.zeros_like(dbias_ref)
            dng_ref[...] = jnp.zeros_like(dng_ref)
            dnb_ref[...] = jnp.zeros_like(dnb_ref)

        zu, zv, dy = zu_ref[...], zv_ref[...], dy_ref[...]
        u, xh, rstd, vn, mixed = _sgu_core(zu, zv, ng_ref[...], nb_ref[...], w_ref, bias_ref[...])
        dzu_ref[...] = (dy * mixed * _gelu_grad(zu)).astype(dzu_ref.dtype)
        dmix = dy * u
        low = lax.broadcasted_iota(jnp.int32, (SGU_CHUNK, 2 * SGU_HD), 1) < SGU_HD
        dbias = jnp.zeros((SGU_CHUNK, GW), F32)
        rows = []
        for n in range(tb // SGU_CHUNK):
            dm = dmix[n * SGU_CHUNK:(n + 1) * SGU_CHUNK, :]
            dbias = dbias + dm
            pairs = []
            for j in range(SGU_HEADS // 2):
                dmp = dm[:, j * 128:(j + 1) * 128]
                vp = vn[n * SGU_CHUNK:(n + 1) * SGU_CHUNK, j * 128:(j + 1) * 128]
                dw_ref[2 * j] += _dot(jnp.where(low, dmp, 0.0), vp, NT)
                dw_ref[2 * j + 1] += _dot(jnp.where(low, 0.0, dmp), vp, NT)
                pairs.append(jnp.where(low, _dot(wt_ref[2 * j], dmp), _dot(wt_ref[2 * j + 1], dmp)))
            rows.append(jnp.concatenate(pairs, axis=1))
        dbias_ref[...] += dbias
        dvn = jnp.concatenate(rows, axis=0)
        dng_ref[...] += jnp.sum(dvn * xh, axis=0, keepdims=True)
        dnb_ref[...] += jnp.sum(dvn, axis=0, keepdims=True)
        wv = dvn * ng_ref[...]
        dv0 = rstd * (wv - jnp.mean(wv, axis=-1, keepdims=True) - xh * jnp.mean(wv * xh, axis=-1, keepdims=True))
        dzv_ref[...] = (dv0 * _gelu_grad(zv)).astype(dzv_ref.dtype)

    vec = _full((1, GW))
    row = pl.BlockSpec((tb, GW), lambda i: (i, 0))
    mat = _full((8, 128, 128))
    return pl.pallas_call(
        body, name="sgu_bwd", grid=(t // tb,),
        in_specs=[_cols(tb, GW, cb), _cols(tb, GW, cbu), _cols(tb, GW, cbv), vec, vec, mat, mat, _full((128, GW))],
        out_specs=[row, row, mat, _full((128, GW)), vec, vec],
        out_shape=[jax.ShapeDtypeStruct((t, GW), MXU_DTYPE)] * 2 + [jax.ShapeDtypeStruct((8, 128, 128), F32),
                   jax.ShapeDtypeStruct((128, GW), F32), jax.ShapeDtypeStruct((1, GW), F32), jax.ShapeDtypeStruct((1, GW), F32)],
        compiler_params=_params(("arbitrary",)))(dmixed, proj, proj, ng, nb, w, wt, bias)


S5_KB = 4
SUB = 8


def _s5_discretize(lam_re, lam_im, log_step, b_re, b_im):
    step = jnp.exp(log_step)[:, None]
    mag = jnp.exp(lam_re * step)
    lr, li = mag * jnp.cos(lam_im * step), mag * jnp.sin(lam_im * step)
    den = lam_re * lam_re + lam_im * lam_im
    fr = ((lr - 1.0) * lam_re + li * lam_im) / den
    fi = (li * lam_re - (lr - 1.0) * lam_im) / den
    return lr, li, fr[:, :, None] * b_re - fi[:, :, None] * b_im, fr[:, :, None] * b_im + fi[:, :, None] * b_re


def _cpow(lr, li, n):
    rr, ri = lr, li
    for _ in range(n - 1):
        rr, ri = rr * lr - ri * li, rr * li + ri * lr
    return rr, ri


def _s5_scan_consts(lr, li, reverse):
    lr, li = lr.reshape(1, S5_N), (-li if reverse else li).reshape(1, S5_N)
    row = jnp.arange(SUB)[:, None]
    out = []
    for s in (1, 2, 4):
        pr, pi = _cpow(lr, li, s)
        keep = (row < SUB - s) if reverse else (row >= s)
        out += [jnp.where(keep, pr, 0.0), jnp.where(keep, pi, 0.0)]
    cr, ci = [], []
    for i in range(SUB):
        pr, pi = _cpow(lr, li, SUB - i if reverse else i + 1)
        cr.append(pr)
        ci.append(pi)
    out += [jnp.concatenate(cr, axis=0), jnp.concatenate(ci, axis=0)]
    return jnp.stack(out)


def _s5_blockdiag_in(b):
    bt = jnp.swapaxes(b, 1, 2).reshape(S5_KB, 8, S5_H, S5_P)
    eye = jnp.eye(8, dtype=b.dtype)
    return jnp.einsum("kghp,gj->kghjp", bt, eye).reshape(S5_KB, 128, 512)


def _s5_blockdiag_in_extract(bb):
    x = bb.reshape(S5_KB, 8, S5_H, 8, S5_P)
    d = jnp.einsum("kghgp->kghp", x).reshape(S5_G, S5_H, S5_P)
    return jnp.swapaxes(d, 1, 2)


def _s5_blockdiag_out(c):
    ct = jnp.swapaxes(c, 1, 2).reshape(S5_KB, 8, S5_P, S5_H)
    eye = jnp.eye(8, dtype=c.dtype)
    return jnp.einsum("kgph,gj->kgpjh", ct, eye).reshape(S5_KB, 512, 128)


def _s5_blockdiag_out_extract(cc):
    x = cc.reshape(S5_KB, 8, S5_P, 8, S5_H)
    d = jnp.einsum("kgpgh->kgph", x).reshape(S5_G, S5_P, S5_H)
    return jnp.swapaxes(d, 1, 2)


def _s5_tile_scan(a, b, c_ref, carry, reverse):
    for si, s in enumerate((1, 2, 4)):
        sh = SUB - s if reverse else s
        ar, br = pltpu.roll(a, sh, 0), pltpu.roll(b, sh, 0)
        mr, mi = c_ref[2 * si], c_ref[2 * si + 1]
        a, b = a + mr * ar - mi * br, b + mr * br + mi * ar
    pr, pi = c_ref[6], c_ref[7]
    cr, ci = carry
    return a + pr * cr - pi * ci, b + pr * ci + pi * cr


def _s5_readout(xre_ref, xim_ref, ccre_ref, ccim_ref):
    return jnp.concatenate(
        [_dot(xre_ref[:, pl.ds(512 * k, 512)], ccre_ref[k]) - _dot(xim_ref[:, pl.ds(512 * k, 512)], ccim_ref[k])
         for k in range(S5_KB)], axis=1)


def _s5_fwd(proj, bbre, bbim, ccre, ccim, dvec, consts, glu_w, glu_b, *, cb, tb=256):
    t = proj.shape[0]
    tb = min(tb, t)
    nt = tb // SUB

    def body(u_ref, bbre_ref, bbim_ref, ccre_ref, ccim_ref, d_ref, c_ref, w_ref, b_ref, o_ref, xre_ref, xim_ref, car):
        @pl.when(pl.program_id(0) == 0)
        def _():
            car[...] = jnp.zeros_like(car)

        u = u_ref[...]
        for k in range(S5_KB):
            uk = u[:, 128 * k:128 * (k + 1)]
            xre_ref[:, pl.ds(512 * k, 512)] = _dot(uk, bbre_ref[k])
            xim_ref[:, pl.ds(512 * k, 512)] = _dot(uk, bbim_ref[k])

        def tile(r, carry):
            sl = pl.ds(pl.multiple_of(r * SUB, SUB), SUB)
            a, b = _s5_tile_scan(xre_ref[sl, :], xim_ref[sl, :], c_ref, carry, False)
            xre_ref[sl, :] = a
            xim_ref[sl, :] = b
            return a[SUB - 1:SUB, :], b[SUB - 1:SUB, :]

        cr, ci = lax.fori_loop(0, nt, tile, (car[0:1, :], car[1:2, :]))
        car[0:1, :] = cr
        car[1:2, :] = ci
        ys = _s5_readout(xre_ref, xim_ref, ccre_ref, ccim_ref) + d_ref[...] * u
        yg = _gelu(ys)
        o_ref[...] = (yg * _sigmoid(_dot(yg, w_ref[...]) + b_ref[...])).astype(o_ref.dtype)

    row = pl.BlockSpec((tb, GW), lambda i: (i, 0))
    xrow = pl.BlockSpec((tb, S5_N), lambda i: (i, 0))
    vec = _full((1, GW))
    return pl.pallas_call(
        body, name="s5_fwd", grid=(t // tb,),
        in_specs=[_cols(tb, GW, cb), _full((4, 128, 512)), _full((4, 128, 512)), _full((4, 512, 128)), _full((4, 512, 128)),
                  vec, _full((8, SUB, S5_N)), _full((GW, GW)), vec],
        out_specs=[row, xrow, xrow],
        out_shape=[jax.ShapeDtypeStruct((t, GW), MXU_DTYPE), jax.ShapeDtypeStruct((t, S5_N), F32), jax.ShapeDtypeStruct((t, S5_N), F32)],
        scratch_shapes=[pltpu.VMEM((SUB, S5_N), F32)], compiler_params=_params(("arbitrary",)))(
            proj, bbre, bbim, ccre, ccim, dvec, consts, glu_w, glu_b)


def _s5_bwd(dmixed, proj, xre, xim, bbre, bbim, ccre, ccim, dvec, consts, glu_w, glu_b, *, cb_dy, cb, tb=256):
    t = proj.shape[0]
    tb = min(tb, t)
    nb = t // tb
    nt = tb // SUB

    def body(dy_ref, u_ref, xre_ref, xim_ref, bbre_ref, bbim_ref, ccre_ref, ccim_ref, d_ref, c_ref, w_ref, b_ref,
             du_ref, dw_ref, db_ref, dd_ref, dccre_ref, dccim_ref, dbbre_ref, dbbim_ref, sre_ref, sim_ref, are, aim, car):
        @pl.when(pl.program_id(0) == 0)
        def _():
            car[...] = jnp.zeros_like(car)
            for r in (dw_ref, db_ref, dd_ref, dccre_ref, dccim_ref, dbbre_ref, dbbim_ref, sre_ref, sim_ref):
                r[...] = jnp.zeros_like(r)

        u, dy = u_ref[...], dy_ref[...]
        ys = _s5_readout(xre_ref, xim_ref, ccre_ref, ccim_ref) + d_ref[...] * u
        yg = _gelu(ys)
        sg = _sigmoid(_dot(yg, w_ref[...]) + b_ref[...])
        dz = dy * yg * sg * (1.0 - sg)
        dyg = dy * sg + _dot(dz, w_ref[...], NT)
        dw_ref[...] += _dot(yg, dz, TN)
        db_ref[...] += jnp.sum(dz, axis=0, keepdims=True)
        dys = dyg * _gelu_grad(ys)
        dd_ref[...] += jnp.sum(dys * u, axis=0, keepdims=True)
        for k in range(S5_KB):
            dk = dys[:, 128 * k:128 * (k + 1)]
            lanes = pl.ds(512 * k, 512)
            are[:, lanes] = _dot(dk, ccre_ref[k], NT)
            aim[:, lanes] = -_dot(dk, ccim_ref[k], NT)
            dccre_ref[k] += _dot(xre_ref[:, lanes], dk, TN)
            dccim_ref[k] -= _dot(xim_ref[:, lanes], dk, TN)

        def tile(j, carry):
            sl = pl.ds(pl.multiple_of((nt - 1 - j) * SUB, SUB), SUB)
            gr, gi = are[sl, :], aim[sl, :]
            a, b = _s5_tile_scan(gr, gi, c_ref, carry, True)
            are[sl, :] = a
            aim[sl, :] = b
            er, ei = a - gr, b - gi
            xr, xi = xre_ref[sl, :], xim_ref[sl, :]
            sre_ref[...] += xr * er + xi * ei
            sim_ref[...] += xr * ei - xi * er
            return a[0:1, :], b[0:1, :]

        cr, ci = lax.fori_loop(0, nt, tile, (car[0:1, :], car[1:2, :]))
        car[0:1, :] = cr
        car[1:2, :] = ci
        dus = []
        for k in range(S5_KB):
            uk = u[:, 128 * k:128 * (k + 1)]
            lanes = pl.ds(512 * k, 512)
            dbbre_ref[k] += _dot(uk, are[:, lanes], TN)
            dbbim_ref[k] += _dot(uk, aim[:, lanes], TN)
            dus.append(_dot(are[:, lanes], bbre_ref[k], NT) + _dot(aim[:, lanes], bbim_ref[k], NT))
        du_ref[...] = (d_ref[...] * dys + jnp.concatenate(dus, axis=1)).astype(du_ref.dtype)

    row = pl.BlockSpec((tb, GW), lambda i: (nb - 1 - i, 0))
    xrow = pl.BlockSpec((tb, S5_N), lambda i: (nb - 1 - i, 0))
    vec = _full((1, GW))
    bbs, ccs = _full((4, 128, 512)), _full((4, 512, 128))
    sds = jax.ShapeDtypeStruct
    return pl.pallas_call(
        body, name="s5_bwd", grid=(nb,),
        in_specs=[_cols_rev(tb, GW, cb_dy, nb), _cols_rev(tb, GW, cb, nb), xrow, xrow, bbs, bbs, ccs, ccs, vec,
                  _full((8, SUB, S5_N)), _full((GW, GW)), vec],
        out_specs=[row, _full((GW, GW)), vec, vec, ccs, ccs, bbs, bbs, _full((SUB, S5_N)), _full((SUB, S5_N))],
        out_shape=[sds((t, GW), MXU_DTYPE), sds((GW, GW), F32), sds((1, GW), F32), sds((1, GW), F32), sds((4, 512, 128), F32),
                   sds((4, 512, 128), F32), sds((4, 128, 512), F32), sds((4, 128, 512), F32), sds((SUB, S5_N), F32),
                   sds((SUB, S5_N), F32)],
        scratch_shapes=[pltpu.VMEM((tb, S5_N), F32), pltpu.VMEM((tb, S5_N), F32), pltpu.VMEM((SUB, S5_N), F32)],
        compiler_params=_params(("arbitrary",)))(dmixed, proj, xre, xim, bbre, bbim, ccre, ccim, dvec, consts, glu_w, glu_b)


def _s5_prepare(lam_re, lam_im, log_step, b_re, b_im, c_re, c_im):
    lr, li, bbr, bbi = _s5_discretize(lam_re, lam_im, log_step, b_re, b_im)
    return dict(bbre=_s5_blockdiag_in(bbr), bbim=_s5_blockdiag_in(bbi), ccre=_s5_blockdiag_out(c_re), ccim=_s5_blockdiag_out(c_im),
                cf=_s5_scan_consts(lr, li, False), cr=_s5_scan_consts(lr, li, True))


def _s5_param_grads(lam_re, lam_im, log_step, b_re, b_im, dbbre, dbbim, dccre, dccim, sre, sim):
    (lr, li, _, _), vjp = jax.vjp(_s5_discretize, lam_re, lam_im, log_step, b_re, b_im)
    sr, si = jnp.sum(sre, axis=0).reshape(S5_G, S5_P), jnp.sum(sim, axis=0).reshape(S5_G, S5_P)
    den = lr * lr + li * li
    glr, gli = (sr * lr - si * li) / den, (si * lr + sr * li) / den
    dlam_re, dlam_im, dlog_step, db_re, db_im = vjp((glr, gli, _s5_blockdiag_in_extract(dbbre), _s5_blockdiag_in_extract(dbbim)))
    return dlam_re, dlam_im, dlog_step, db_re, db_im, _s5_blockdiag_out_extract(dccre), _s5_blockdiag_out_extract(dccim)


HALO = 8
AB_CB = 4096 // 128
Q_SCALE = DN_HD ** -0.5


def _halo_prev(tb, width, cb):
    return pl.BlockSpec((HALO, width), lambda i: (jnp.maximum(i * (tb // HALO) - 1, 0), cb))


def _halo_next(tb, width, cb, nrows):
    last = nrows // HALO - 1
    return pl.BlockSpec((HALO, width), lambda i: (jnp.minimum((i + 1) * (tb // HALO), last), cb))


def _silu_parts(c):
    sg = _sigmoid(c)
    return c * sg, sg * (1.0 + c * (1.0 - sg))


def _softplus(x):
    return jnp.maximum(x, 0.0) + jnp.log(1.0 + jnp.exp(-jnp.abs(x)))


def _dn_conv(x_ref, halo_ref, w_ref, part, ext, first):
    tb = x_ref.shape[0]
    ext[pl.ds(0, HALO), :] = jnp.where(first, 0.0, halo_ref[...])
    ext[pl.ds(HALO, tb), :] = x_ref[...]
    c = None
    for j in range(DN_CONV):
        term = w_ref[pl.ds(j, 1), pl.ds(512 * part, 512)] * ext[pl.ds(HALO - (DN_CONV - 1) + j, tb), :]
        c = term if c is None else c + term
    return c


def _dn_gb(ab, alog, dtb):
    lane = lax.broadcasted_iota(jnp.int32, ab.shape, 1)
    pre = ab + dtb
    g = -jnp.exp(alog) * _softplus(pre)
    beta = _sigmoid(ab)
    return jnp.where(lane < DN_HEADS, g, jnp.where(lane < 2 * DN_HEADS, beta, 0.0)), pre, beta


def _dn_prep_fwd(proj, conv_w, alog, dtb, *, cbq, tb=512):
    t = proj.shape[0]
    tb = min(tb, t)

    def body(xq, xk, xv, hq, hk, hv, ab_ref, w_ref, alog_ref, dtb_ref, qn_ref, kn_ref, vs_ref, gb_ref, ext):
        first = pl.program_id(0) == 0
        for part, (x_ref, h_ref, o_ref) in enumerate(((xq, hq, qn_ref), (xk, hk, kn_ref), (xv, hv, vs_ref))):
            s, _ = _silu_parts(_dn_conv(x_ref, h_ref, w_ref, part, ext, first))
            if part < 2:
                scale = Q_SCALE if part == 0 else 1.0
                for h in range(DN_HEADS):
                    sh = s[:, DN_HD * h:DN_HD * (h + 1)]
                    rn = lax.rsqrt(jnp.sum(sh * sh, axis=-1, keepdims=True) + L2_EPS)
                    o_ref[:, pl.ds(DN_HD * h, DN_HD)] = sh * (rn * scale)
            else:
                o_ref[...] = s
        gb_ref[...] = _dn_gb(ab_ref[...], alog_ref[...], dtb_ref[...])[0]

    row = pl.BlockSpec((tb, GW), lambda i: (i, 0))
    small = pl.BlockSpec((tb, 128), lambda i: (i, 0))
    v128 = _full((1, 128))
    sds = jax.ShapeDtypeStruct
    return pl.pallas_call(
        body, name="dn_prep_fwd", grid=(t // tb,),
        in_specs=[_cols(tb, GW, cbq), _cols(tb, GW, cbq + 1), _cols(tb, GW, cbq + 2),
                  _halo_prev(tb, GW, cbq), _halo_prev(tb, GW, cbq + 1), _halo_prev(tb, GW, cbq + 2),
                  _cols(tb, 128, AB_CB), _full((DN_CONV, 3 * GW)), v128, v128],
        out_specs=[row, row, row, small],
        out_shape=[sds((t, GW), F32)] * 3 + [sds((t, 128), F32)],
        scratch_shapes=[pltpu.VMEM((tb + HALO, GW), F32)], compiler_params=_params(("parallel",)))(
            proj, proj, proj, proj, proj, proj, proj, conv_w, alog, dtb)


def _dn_prep_bwd_a(proj, conv_w, alog, dtb, dqn, dkn, dvs, dgb, *, cbq, tb=512):
    t = proj.shape[0]
    tb = min(tb, t)

    def body(xq, xk, xv, hq, hk, hv, ab_ref, w_ref, alog_ref, dtb_ref, dqn_ref, dkn_ref, dvs_ref, dgb_ref,
             dcq_ref, dck_ref, dcv_ref, dab_ref, dalog_ref, ddtb_ref, ext):
        first = pl.program_id(0) == 0

        @pl.when(first)
        def _():
            dalog_ref[...] = jnp.zeros_like(dalog_ref)
            ddtb_ref[...] = jnp.zeros_like(ddtb_ref)

        for part, (x_ref, h_ref, d_ref, o_ref) in enumerate(((xq, hq, dqn_ref, dcq_ref), (xk, hk, dkn_ref, dck_ref), (xv, hv, dvs_ref, dcv_ref))):
            s, ds_dc = _silu_parts(_dn_conv(x_ref, h_ref, w_ref, part, ext, first))
            d = d_ref[...]
            if part < 2:
                scale = Q_SCALE if part == 0 else 1.0
                for h in range(DN_HEADS):
                    lanes = slice(DN_HD * h, DN_HD * (h + 1))
                    sh, dh = s[:, lanes], d[:, lanes]
                    rn = lax.rsqrt(jnp.sum(sh * sh, axis=-1, keepdims=True) + L2_EPS)
                    dsh = scale * (rn * dh - sh * (rn * rn * rn) * jnp.sum(dh * sh, axis=-1, keepdims=True))
                    o_ref[:, pl.ds(DN_HD * h, DN_HD)] = dsh * ds_dc[:, lanes]
            else:
                o_ref[...] = d * ds_dc
        ab, dgb_v = ab_ref[...], dgb_ref[...]
        gb, pre, beta = _dn_gb(ab, alog_ref[...], dtb_ref[...])
        lane = lax.broadcasted_iota(jnp.int32, ab.shape, 1)
        is_g = lane < DN_HEADS
        da = jnp.where(is_g, dgb_v * (-jnp.exp(alog_ref[...])) * _sigmoid(pre), 0.0)
        db = jnp.where((lane >= DN_HEADS) & (lane < 2 * DN_HEADS), dgb_v * beta * (1.0 - beta), 0.0)
        dab_ref[...] = (da + db).astype(dab_ref.dtype)
        ddtb_ref[...] += jnp.sum(da, axis=0, keepdims=True)
        dalog_ref[...] += jnp.sum(jnp.where(is_g, dgb_v * gb, 0.0), axis=0, keepdims=True)

    row = pl.BlockSpec((tb, GW), lambda i: (i, 0))
    small = pl.BlockSpec((tb, 128), lambda i: (i, 0))
    v128 = _full((1, 128))
    sds = jax.ShapeDtypeStruct
    return pl.pallas_call(
        body, name="dn_prep_bwd_a", grid=(t // tb,),
        in_specs=[_cols(tb, GW, cbq), _cols(tb, GW, cbq + 1), _cols(tb, GW, cbq + 2),
                  _halo_prev(tb, GW, cbq), _halo_prev(tb, GW, cbq + 1), _halo_prev(tb, GW, cbq + 2),
                  _cols(tb, 128, AB_CB), _full((DN_CONV, 3 * GW)), v128, v128, row, row, row, small],
        out_specs=[row, row, row, small, v128, v128],
        out_shape=[sds((t, GW), F32)] * 3 + [sds((t, 128), MXU_DTYPE), sds((1, 128), F32), sds((1, 128), F32)],
        scratch_shapes=[pltpu.VMEM((tb + HALO, GW), F32)], compiler_params=_params(("arbitrary",)))(
            proj, proj, proj, proj, proj, proj, proj, conv_w, alog, dtb, dqn, dkn, dvs, dgb)


def _dn_prep_bwd_b(proj, conv_w, dcq, dck, dcv, *, cbq, tb=512):
    t = proj.shape[0]
    tb = min(tb, t)
    nb = t // tb

    def body(xq, xk, xv, hq, hk, hv, dq_in, dk_in, dv_in, nq, nk, nv, w_ref, dq_ref, dk_ref, dv_ref, dw_ref, ext):
        i = pl.program_id(0)
        first, last = i == 0, i == nb - 1

        @pl.when(first)
        def _():
            dw_ref[...] = jnp.zeros_like(dw_ref)

        for part, (x_ref, h_ref, d_ref, n_ref, o_ref) in enumerate(
                ((xq, hq, dq_in, nq, dq_ref), (xk, hk, dk_in, nk, dk_ref), (xv, hv, dv_in, nv, dv_ref))):
            lanes = pl.ds(512 * part, 512)
            d = d_ref[...]
            ext[pl.ds(0, HALO), :] = jnp.where(first, 0.0, h_ref[...])
            ext[pl.ds(HALO, tb), :] = x_ref[...]
            for j in range(DN_CONV):
                xs = ext[pl.ds(HALO - (DN_CONV - 1) + j, tb), :]
                dw_ref[pl.ds(j, 1), lanes] += jnp.sum(d * xs, axis=0, keepdims=True)
            ext[pl.ds(0, tb), :] = d
            ext[pl.ds(tb, HALO), :] = jnp.where(last, 0.0, n_ref[...])
            acc = None
            for j in range(DN_CONV):
                term = w_ref[pl.ds(j, 1), lanes] * ext[pl.ds(DN_CONV - 1 - j, tb), :]
                acc = term if acc is None else acc + term
            o_ref[...] = acc.astype(o_ref.dtype)

    row = pl.BlockSpec((tb, GW), lambda i: (i, 0))
    nxt = _halo_next(tb, GW, 0, t)
    sds = jax.ShapeDtypeStruct
    return pl.pallas_call(
        body, name="dn_prep_bwd_b", grid=(nb,),
        in_specs=[_cols(tb, GW, cbq), _cols(tb, GW, cbq + 1), _cols(tb, GW, cbq + 2),
                  _halo_prev(tb, GW, cbq), _halo_prev(tb, GW, cbq + 1), _halo_prev(tb, GW, cbq + 2),
                  row, row, row, nxt, nxt, nxt, _full((DN_CONV, 3 * GW))],
        out_specs=[row, row, row, _full((DN_CONV, 3 * GW))],
        out_shape=[sds((t, GW), MXU_DTYPE)] * 3 + [sds((DN_CONV, 3 * GW), F32)],
        scratch_shapes=[pltpu.VMEM((tb + HALO, GW), F32)], compiler_params=_params(("arbitrary",)))(
            proj, proj, proj, proj, proj, proj, dcq, dck, dcv, dcq, dck, dcv, conv_w)


CAT = DN_HEADS * DN_CHUNK
DN_LOCAL_CHUNKS = 4


def _iota_div(shape, axis, width):
    return jnp.right_shift(lax.broadcasted_iota(jnp.int32, shape, axis), width.bit_length() - 1)


def _dn_masks():
    r = lax.broadcasted_iota(jnp.int32, (DN_CHUNK, CAT), 0)
    c = jnp.bitwise_and(lax.broadcasted_iota(jnp.int32, (DN_CHUNK, CAT), 1), DN_CHUNK - 1)
    wide = _iota_div((CAT, GW), 0, DN_CHUNK) == _iota_div((CAT, GW), 1, DN_HD)
    square = _iota_div((CAT, CAT), 0, DN_CHUNK) == _iota_div((CAT, CAT), 1, DN_CHUNK)
    return dict(eye=r == c, tril=r >= c, strict=r > c, triu=r <= c, wide=wide, square=square)


def _stack4(x):
    return jnp.concatenate([x, x, x, x], axis=0)


def _diag_blocks(x, mask):
    return jnp.where(mask, _stack4(x), 0.0)


def _fold_blocks(x, mask):
    x = jnp.where(mask, x, 0.0)
    return x[0:64] + x[64:128] + x[128:192] + x[192:256]


def _expand(cols, base, width):
    head = _iota_div((cols.shape[0], DN_HEADS * width), 1, width)
    out = jnp.zeros((cols.shape[0], DN_HEADS * width), F32)
    for h in range(DN_HEADS):
        out = jnp.where(head == h, cols[:, base + h:base + h + 1], out)
    return out


def _head_sums(x, width):
    if width == DN_HD:
        return [jnp.sum(x[:, DN_HD * h:DN_HD * (h + 1)], axis=1, keepdims=True) for h in range(DN_HEADS)]
    head = _iota_div(x.shape, 1, width)
    return [jnp.sum(jnp.where(head == h, x, 0.0), axis=1, keepdims=True) for h in range(DN_HEADS)]


def _cumsum_rows(x):
    row = lax.broadcasted_iota(jnp.int32, x.shape, 0)
    for s in (1, 2, 4, 8, 16, 32):
        x = x + jnp.where(row >= s, pltpu.roll(x, s, 0), 0.0)
    return x


def _tri_inv(ns, square):
    shape = ns[0].shape
    col = jnp.bitwise_and(lax.broadcasted_iota(jnp.int32, shape, 1), DN_CHUNK - 1)
    eye = jnp.where(lax.broadcasted_iota(jnp.int32, shape, 0) == col, 1.0, 0.0)
    xs = [eye - n for n in ns]
    ps = [_dot(n, _diag_blocks(n, square), exact=True) for n in ns]
    for _ in range(4):
        pds = [_diag_blocks(p, square) for p in ps]
        xs = [x + _dot(x, pd, exact=True) for x, pd in zip(xs, pds)]
        ps = [_dot(p, pd, exact=True) for p, pd in zip(ps, pds)]
    return [x + _dot(x, _diag_blocks(p, square), exact=True) for x, p in zip(xs, ps)]


def _dn_local_math(qs, ks, vs, gbvs, m, tms=None):
    out = []
    for k, v, gbv in zip(ks, vs, gbvs):
        gc = _cumsum_rows(gbv)
        gc_cat, gc_wide = _expand(gc, 0, DN_CHUNK), _expand(gc, 0, DN_HD)
        gc_row = jnp.sum(jnp.where(m["eye"], gc_cat, 0.0), axis=0, keepdims=True)
        decay = jnp.where(m["tril"], jnp.exp(jnp.minimum(gc_cat - gc_row, 0.0)), 0.0)
        eg = jnp.exp(gc_wide)
        gl = _expand(gc[DN_CHUNK - 1:DN_CHUNK, :], 0, DN_HD)
        beta = _expand(gbv, DN_HEADS, DN_HD)
        kb = k * beta
        out.append(dict(decay=decay, eg=eg, eg_last=jnp.exp(gl), etail=jnp.exp(gl - gc_wide), beta=beta, kb=kb, vb=v * beta,
                        kbg=kb * eg, k_rows=_diag_blocks(k, m["wide"])))
    for d, q, k in zip(out, qs, ks):
        d.update(qg=q * d["eg"], ktail=k * d["etail"])
    for d in out:
        d["kk"] = _dot(d["kb"], d["k_rows"], NT)
    for d, q in zip(out, qs):
        d["qk"] = _dot(q, d["k_rows"], NT)
    if tms is None:
        tms = _tri_inv([jnp.where(m["strict"], d["kk"] * d["decay"], 0.0) for d in out], m["square"])
    for d, tm in zip(out, tms):
        d["tm"] = tm
    return out


def _dn_local_fwd(qn, kn, vs, gb):
    t = qn.shape[0]
    rows = DN_CHUNK * DN_LOCAL_CHUNKS

    def body(q_ref, k_ref, v_ref, gb_ref, u_ref, wm_ref, qg_ref, kt_ref, qkd_ref, tm_ref):
        masks = _dn_masks()
        chunks = [pl.ds(DN_CHUNK * n, DN_CHUNK) for n in range(DN_LOCAL_CHUNKS)]
        ms = _dn_local_math([q_ref[rs, :] for rs in chunks], [k_ref[rs, :] for rs in chunks], [v_ref[rs, :] for rs in chunks],
                            [gb_ref[rs, :] for rs in chunks], masks)
        us = [_dot(m["tm"], _diag_blocks(m["vb"], masks["wide"])) for m in ms]
        wms = [_dot(m["tm"], _diag_blocks(m["kbg"], masks["wide"])) for m in ms]
        for rs, m, u, wm in zip(chunks, ms, us, wms):
            u_ref[rs, :] = u
            wm_ref[rs, :] = wm.astype(wm_ref.dtype)
            qg_ref[rs, :] = m["qg"].astype(qg_ref.dtype)
            kt_ref[rs, :] = m["ktail"].astype(kt_ref.dtype)
            qkd = (m["qk"] * m["decay"]).astype(qkd_ref.dtype)
            for h in range(DN_HEADS):
                qkd_ref[rs, pl.ds(DN_HD * h, DN_CHUNK)] = qkd[:, DN_CHUNK * h:DN_CHUNK * (h + 1)]
            tm_ref[rs, :] = m["tm"]

    row = pl.BlockSpec((rows, GW), lambda i: (i, 0))
    sds = jax.ShapeDtypeStruct
    return pl.pallas_call(
        body, name="dn_local_fwd", grid=(t // rows,), in_specs=[row, row, row, pl.BlockSpec((rows, 128), lambda i: (i, 0))],
        out_specs=[row] * 5 + [pl.BlockSpec((rows, CAT), lambda i: (i, 0))],
        out_shape=[sds((t, GW), F32)] + [sds((t, GW), MXU_DTYPE)] * 4 + [sds((t, CAT), F32)],
        compiler_params=_params(("parallel",)))(qn, kn, vs, gb)


def _dn_eg_last(gbv, h):
    return jnp.exp(jnp.sum(gbv[:, h:h + 1], axis=0, keepdims=True))


def _dn_seq_fwd(u, wm, qg, ktail, qkd, gb, proj, norm_g, *, cb_gate):
    t = u.shape[0]
    nc = t // DN_CHUNK

    def body(u_ref, wm_ref, qg_ref, kt_ref, qkd_ref, gb_ref, gate_ref, ng_ref, o_ref, raw_ref, vn_ref, st_ref, s_ref):
        @pl.when(pl.program_id(0) == 0)
        def _():
            s_ref[...] = jnp.zeros_like(s_ref)

        gbv = gb_ref[...]
        heads = range(DN_HEADS)
        lanes = [pl.ds(DN_HD * h, DN_HD) for h in heads]
        ss = [s_ref[h] for h in heads]
        for h in heads:
            st_ref[0, h] = ss[h]
        ws = [_dot(wm_ref[:, lanes[h]], ss[h]) for h in heads]
        qs = [_dot(qg_ref[:, lanes[h]], ss[h]) for h in heads]
        v_news = [u_ref[:, lanes[h]] - ws[h] for h in heads]
        ks = [_dot(kt_ref[:, lanes[h]], v_news[h], TN) for h in heads]
        os_ = [qs[h] + _dot(qkd_ref[:, pl.ds(DN_HD * h, DN_CHUNK)], v_news[h]) for h in heads]
        for h in heads:
            s_ref[h] = ss[h] * _dn_eg_last(gbv, h) + ks[h]
            vn_ref[:, lanes[h]] = v_news[h].astype(vn_ref.dtype)
            o = os_[h]
            raw_ref[:, lanes[h]] = o
            r = lax.rsqrt(jnp.mean(o * o, axis=-1, keepdims=True) + RMS_EPS)
            gt = gate_ref[:, lanes[h]]
            o_ref[:, lanes[h]] = (o * r * ng_ref[...] * (gt * _sigmoid(gt))).astype(o_ref.dtype)

    row = pl.BlockSpec((DN_CHUNK, GW), lambda i: (i, 0))
    sds = jax.ShapeDtypeStruct
    return pl.pallas_call(
        body, name="dn_seq_fwd", grid=(nc,),
        in_specs=[row] * 5 + [pl.BlockSpec((DN_CHUNK, 128), lambda i: (i, 0)), _cols(DN_CHUNK, GW, cb_gate), _full((1, DN_HD))],
        out_specs=[row, row, row, pl.BlockSpec((1, DN_HEADS, DN_HD, DN_HD), lambda i: (i, 0, 0, 0))],
        out_shape=[sds((t, GW), MXU_DTYPE), sds((t, GW), F32), sds((t, GW), MXU_DTYPE), sds((nc, DN_HEADS, DN_HD, DN_HD), F32)],
        scratch_shapes=[pltpu.VMEM((DN_HEADS, DN_HD, DN_HD), F32)], compiler_params=_params(("arbitrary",)))(
            u, wm, qg, ktail, qkd, gb, proj, norm_g)


def _dn_seq_bwd(dmixed, raw, proj, norm_g, gb, wm, qg, ktail, qkd, v_new, states, *, cb_dy, cb_gate):
    t = raw.shape[0]
    nc = t // DN_CHUNK

    def body(dy_ref, raw_ref, gate_ref, ng_ref, gb_ref, wm_ref, qg_ref, kt_ref, qkd_ref, vn_ref, st_ref,
             dgate_ref, dng_ref, do_ref, dvn_ref, dwm_ref, dqg_ref, dkt_ref, degl_ref, ds_ref):
        @pl.when(pl.program_id(0) == 0)
        def _():
            ds_ref[...] = jnp.zeros_like(ds_ref)
            dng_ref[...] = jnp.zeros_like(dng_ref)

        gbv = gb_ref[...]
        lane = lax.broadcasted_iota(jnp.int32, (1, 128), 1)
        heads = range(DN_HEADS)
        lanes = [pl.ds(DN_HD * h, DN_HD) for h in heads]
        ss, dss, dos = [st_ref[0, h] for h in heads], [ds_ref[h] for h in heads], []
        for h in heads:
            o, gt, dy, ng = raw_ref[:, lanes[h]], gate_ref[:, lanes[h]], dy_ref[:, lanes[h]], ng_ref[...]
            r = lax.rsqrt(jnp.mean(o * o, axis=-1, keepdims=True) + RMS_EPS)
            sil, dsil = _silu_parts(gt)
            d_on = dy * sil
            dgate_ref[:, lanes[h]] = (dy * (o * r * ng) * dsil).astype(dgate_ref.dtype)
            dng_ref[...] += jnp.sum(d_on * o * r, axis=0, keepdims=True)
            w = d_on * ng
            dos.append(r * w - o * (r * r * r) * jnp.mean(w * o, axis=-1, keepdims=True))
        from_next = [_dot(kt_ref[:, lanes[h]], dss[h]) for h in heads]
        d_vnews = [_dot(qkd_ref[:, pl.ds(DN_HD * h, DN_CHUNK)], dos[h], TN) + from_next[h] for h in heads]
        q_terms = [_dot(qg_ref[:, lanes[h]], dos[h], TN) for h in heads]
        w_terms = [_dot(wm_ref[:, lanes[h]], d_vnews[h], TN) for h in heads]
        degl = jnp.zeros((1, 128), F32)
        for h in heads:
            ds_ref[h] = q_terms[h] + _dn_eg_last(gbv, h) * dss[h] - w_terms[h]
            do_ref[:, lanes[h]] = dos[h].astype(do_ref.dtype)
            dvn_ref[:, lanes[h]] = d_vnews[h].astype(dvn_ref.dtype)
            d_eglast = jnp.sum(jnp.sum(ss[h] * dss[h], axis=1, keepdims=True), axis=0, keepdims=True)
            degl = degl + jnp.where(lane == h, d_eglast, 0.0)
        degl_ref[0] = degl
        for h in heads:
            dwm_ref[:, lanes[h]] = (-_dot(d_vnews[h], ss[h], NT)).astype(dwm_ref.dtype)
        for h in heads:
            dqg_ref[:, lanes[h]] = _dot(dos[h], ss[h], NT)
        for h in heads:
            dkt_ref[:, lanes[h]] = _dot(vn_ref[:, lanes[h]], dss[h], NT)

    row = pl.BlockSpec((DN_CHUNK, GW), lambda i: (nc - 1 - i, 0))
    small = pl.BlockSpec((DN_CHUNK, 128), lambda i: (nc - 1 - i, 0))
    sds = jax.ShapeDtypeStruct
    return pl.pallas_call(
        body, name="dn_seq_bwd", grid=(nc,),
        in_specs=[_cols_rev(DN_CHUNK, GW, cb_dy, nc), row, _cols_rev(DN_CHUNK, GW, cb_gate, nc), _full((1, DN_HD)), small,
                  row, row, row, row, row, pl.BlockSpec((1, DN_HEADS, DN_HD, DN_HD), lambda i: (nc - 1 - i, 0, 0, 0))],
        out_specs=[row, _full((1, DN_HD)), row, row, row, row, row, pl.BlockSpec((1, 1, 128), lambda i: (nc - 1 - i, 0, 0))],
        out_shape=[sds((t, GW), MXU_DTYPE), sds((1, DN_HD), F32), sds((t, GW), MXU_DTYPE), sds((t, GW), MXU_DTYPE),
                   sds((t, GW), MXU_DTYPE), sds((t, GW), F32), sds((t, GW), F32), sds((nc, 1, 128), F32)],
        scratch_shapes=[pltpu.VMEM((DN_HEADS, DN_HD, DN_HD), F32)], compiler_params=_params(("arbitrary",)))(
            dmixed, raw, proj, norm_g, gb, wm, qg, ktail, qkd, v_new, states)


def _dn_local_bwd(qn, kn, vs, gb, tm, v_new, do, d_vnew, d_wm, d_qg, d_ktail, d_eglast):
    t = qn.shape[0]
    rows = DN_CHUNK * DN_LOCAL_CHUNKS

    def body(q_ref, k_ref, v_ref, gb_ref, tm_ref, vn_ref, do_ref, dvn_ref, dwm_ref, dqg_ref, dkt_ref, degl_ref,
             dq_ref, dk_ref, dv_ref, dgb_ref):
        masks = _dn_masks()
        wide, square = masks["wide"], masks["square"]
        lane = lax.broadcasted_iota(jnp.int32, (DN_CHUNK, 128), 1)
        last_row = lax.broadcasted_iota(jnp.int32, (DN_CHUNK, 1), 0) == DN_CHUNK - 1
        chunks = [pl.ds(DN_CHUNK * n, DN_CHUNK) for n in range(DN_LOCAL_CHUNKS)]
        ms = _dn_local_math([q_ref[rs, :] for rs in chunks], [k_ref[rs, :] for rs in chunks], [v_ref[rs, :] for rs in chunks],
                            [gb_ref[rs, :] for rs in chunks], masks, tms=[tm_ref[rs, :] for rs in chunks])

        def work(n, rs, m):
            q, k, v, tm = q_ref[rs, :], k_ref[rs, :], v_ref[rs, :], m["tm"]
            decay, eg, k_rows = m["decay"], m["eg"], m["k_rows"]
            d_vnew, d_wm, d_qg, d_ktail = dvn_ref[rs, :], dwm_ref[rs, :], dqg_ref[rs, :], dkt_ref[rs, :]
            deglv = degl_ref[n]
            dq = d_qg * eg
            dk = d_ktail * m["etail"]
            tails = _head_sums(d_ktail * m["ktail"], DN_HD)
            dgcs = _head_sums(d_qg * m["qg"], DN_HD)
            d_qkd = jnp.where(masks["tril"], _dot(do_ref[rs, :], _diag_blocks(vn_ref[rs, :], wide), NT), 0.0)
            d_tm = _dot(d_vnew, _diag_blocks(m["vb"], wide), NT) + _dot(d_wm, _diag_blocks(m["kbg"], wide), NT)
            d_vb = _fold_blocks(_dot(tm, d_vnew, TN), wide)
            d_kbg = _fold_blocks(_dot(tm, d_wm, TN), wide)
            yield
            dqk_dec = d_qkd * decay
            dq = dq + _dot(dqk_dec, k_rows)
            dk = dk + _fold_blocks(_dot(dqk_dec, q, TN), wide)
            ddecay = d_qkd * m["qk"]
            d_kb = d_kbg * eg
            kbgs = _head_sums(d_kbg * m["kbg"], DN_HD)
            x = _fold_blocks(_dot(tm, d_tm, TN, exact=True), square)
            yield
            d_n = jnp.where(masks["strict"], -_dot(x, _diag_blocks(tm, square), NT, exact=True), 0.0)
            yield
            d_kk = d_n * decay
            d_kb = d_kb + _dot(d_kk, k_rows)
            dk = dk + _fold_blocks(_dot(d_kk, m["kb"], TN), wide)
            yield
            ddecay = ddecay + d_n * m["kk"]
            dk = dk + d_kb * m["beta"]
            dbetas = [a + b for a, b in zip(_head_sums(d_kb * k, DN_HD), _head_sums(d_vb * v, DN_HD))]
            dv_ref[rs, :] = d_vb * m["beta"]
            dq_ref[rs, :] = dq
            dk_ref[rs, :] = dk
            dd = ddecay * decay
            row_sums = _head_sums(dd, DN_CHUNK)
            dgc_cols = jnp.zeros((DN_CHUNK, 128), F32)
            for h in range(DN_HEADS):
                dgl = jnp.sum(tails[h], axis=0, keepdims=True) + deglv[:, h:h + 1] * m["eg_last"][:, DN_HD * h:DN_HD * h + 1]
                dgc_cols = jnp.where(lane == h, dgcs[h] - tails[h] + kbgs[h] + row_sums[h] + jnp.where(last_row, dgl, 0.0), dgc_cols)
            dgc_row = (jnp.sum(jnp.where(masks["eye"], _expand(dgc_cols, 0, DN_CHUNK), 0.0), axis=0, keepdims=True)
                       - jnp.sum(dd, axis=0, keepdims=True))
            dgs = _head_sums(jnp.where(masks["triu"], dgc_row, 0.0), DN_CHUNK)
            dgb = jnp.zeros((DN_CHUNK, 128), F32)
            for h in range(DN_HEADS):
                dgb = jnp.where(lane == h, dgs[h], jnp.where(lane == DN_HEADS + h, dbetas[h], dgb))
            dgb_ref[rs, :] = dgb

        running = [work(n, rs, m) for n, (rs, m) in enumerate(zip(chunks, ms))]
        while running:
            running = [g for g in running if next(g, "done") != "done"]

    row = pl.BlockSpec((rows, GW), lambda i: (i, 0))
    small = pl.BlockSpec((rows, 128), lambda i: (i, 0))
    sds = jax.ShapeDtypeStruct
    return pl.pallas_call(
        body, name="dn_local_bwd", grid=(t // rows,),
        in_specs=[row, row, row, small, pl.BlockSpec((rows, CAT), lambda i: (i, 0))] + [row] * 6
        + [pl.BlockSpec((DN_LOCAL_CHUNKS, 1, 128), lambda i: (i, 0, 0))],
        out_specs=[row, row, row, small], out_shape=[sds((t, GW), F32)] * 3 + [sds((t, 128), F32)],
        compiler_params=_params(("parallel",)))(qn, kn, vs, gb, tm, v_new, do, d_vnew, d_wm, d_qg, d_ktail, d_eglast)


ANY = pl.BlockSpec(memory_space=pl.ANY)
PAIR_SPLIT = 4


def _place():
    x, y, c = lax.axis_index("x"), lax.axis_index("y"), lax.axis_index("c")
    chips = [(1 - x, y), (x, 1 - y), (1 - x, 1 - y)]
    return x, y, c, chips


def _remote(src, dst, send_sem, recv_sem, to):
    return pltpu.make_async_remote_copy(src_ref=src, dst_ref=dst, send_sem=send_sem, recv_sem=recv_sem, device_id=to,
                                        device_id_type=MESH)


def _carry_allgather(arrs):
    n = len(arrs)

    def sends(ins, outs, send_sems, recv_sems):
        x, y, c, chips = _place()
        me = 2 * x + y
        out = []
        for a in range(n):
            half = ins[a].shape[0] // 2
            mine = pl.ds(c * half, half)
            out += [_remote(ins[a].at[mine], outs[a].at[me, mine], send_sems.at[6 * a + k], recv_sems.at[6 * a + k], (*chip, c))
                    for k, chip in enumerate(chips)]
        return out

    def start(*parts):
        for s in sends(*parts):
            s.start()

    def finish(ins, outs, send_sems, recv_sems):
        x, y, c, chips = _place()
        sibling = (x, y, 1 - c)
        fwds = []
        for a in range(n):
            half = ins[a].shape[0] // 2
            mine = pl.ds(c * half, half)
            for k, (cx, cy) in enumerate(chips):
                got = outs[a].at[2 * cx + cy, mine]
                _remote(got, got, send_sems.at[6 * a + k], recv_sems.at[6 * a + k], (cx, cy, c)).wait_recv()
                f = _remote(got, got, send_sems.at[6 * a + 3 + k], recv_sems.at[6 * a + 3 + k], sibling)
                f.start()
                fwds.append(f)
        for a in range(n):
            half = ins[a].shape[0] // 2
            other = pl.ds((1 - c) * half, half)
            for k, (cx, cy) in enumerate(chips):
                got = outs[a].at[2 * cx + cy, other]
                _remote(got, got, send_sems.at[6 * a + 3 + k], recv_sems.at[6 * a + 3 + k], sibling).wait_recv()
        for s in sends(ins, outs, send_sems, recv_sems) + fwds:
            s.wait_send()

    return _Carry(arrs, [jax.ShapeDtypeStruct((4,) + a.shape, a.dtype) for a in arrs], 6 * n, start, finish)


def _carry_pair_exchange(gbs):
    n = len(gbs)

    def copies(ins, outs, send_sems, recv_sems):
        x, y, c, _ = _place()
        out = []
        for a in range(n):
            half = ins[a].shape[1] // 2
            piece = half // PAIR_SPLIT
            out += [_remote(ins[a].at[:, pl.ds((1 - c) * half + r * piece, piece)], outs[a].at[:, pl.ds(r * piece, piece)],
                            send_sems.at[PAIR_SPLIT * a + r], recv_sems.at[PAIR_SPLIT * a + r], (x, y, 1 - c))
                    for r in range(PAIR_SPLIT)]
        return out

    def start(*parts):
        for s in copies(*parts):
            s.start()

    def finish(*parts):
        for s in copies(*parts):
            s.wait()

    return _Carry(gbs, [jax.ShapeDtypeStruct((4, g.shape[1] // 2, g.shape[2]), g.dtype) for g in gbs], PAIR_SPLIT * n, start, finish)


def _carry_chip_exchange(ps):
    n = len(ps)

    def copies(ins, outs, send_sems, recv_sems):
        x, y, c, chips = _place()
        return [_remote(ins[a].at[2 * cx + cy], outs[a].at[k], send_sems.at[3 * a + k], recv_sems.at[3 * a + k], (cx, cy, c))
                for a in range(n) for k, (cx, cy) in enumerate(chips)]

    def start(*parts):
        for s in copies(*parts):
            s.start()

    def finish(*parts):
        for s in copies(*parts):
            s.wait()

    return _Carry(ps, [jax.ShapeDtypeStruct((3,) + p.shape[1:], p.dtype) for p in ps], 3 * n, start, finish)


def _pair_join(bufs, *, name):
    n = len(bufs)

    def body(*refs):
        outs = refs[n:2 * n]
        send_sems, recv_sems = refs[2 * n:]
        x, y, c, _ = _place()
        work = []
        for a in range(n):
            s = _remote(outs[a].at[c], outs[a].at[c], send_sems.at[a], recv_sems.at[a], (x, y, 1 - c))
            s.start()
            work.append(s)
        for s in work:
            s.wait()

    return pl.pallas_call(
        body, name=name, in_specs=[ANY] * n, out_specs=[ANY] * n,
        out_shape=[jax.ShapeDtypeStruct(b.shape, b.dtype) for b in bufs], input_output_aliases={a: a for a in range(n)},
        scratch_shapes=[pltpu.SemaphoreType.DMA((n,)), pltpu.SemaphoreType.DMA((n,))])(*bufs)


def _pair_sum(gb, got, place, *, name, block_bytes=1 << 20):
    _, r, cols = gb.shape
    half = r // 2
    tr = _row_tile(half, cols, block_bytes)

    def body(place_ref, g_ref, got_ref, o_ref):
        o_ref[...] = (g_ref[...] + got_ref[...]).astype(o_ref.dtype)

    blk = pl.BlockSpec((None, tr, cols), lambda j, i, p: (j, i, 0))
    grid_spec = pltpu.PrefetchScalarGridSpec(
        num_scalar_prefetch=1, grid=(4, half // tr),
        in_specs=[pl.BlockSpec((None, None, tr, cols), lambda j, i, p: (j, p[0], i, 0)), blk], out_specs=blk)
    return pl.pallas_call(body, name=name, grid_spec=grid_spec, out_shape=jax.ShapeDtypeStruct((4, half, cols), MXU_DTYPE),
                          compiler_params=_params(("parallel", "parallel")))(place, gb.reshape(4, 2, half, cols), got)


def _chip_sum(gb, got_pair, got, place, *, name, block_bytes=1 << 20):
    _, r, cols = gb.shape
    h = r // 2
    tr = _row_tile(h, cols, block_bytes)

    def body(place_ref, g_ref, gp_ref, g0, g1, g2, o_ref):
        o_ref[...] = (g_ref[...] + gp_ref[...]) + g0[...].astype(F32) + g1[...].astype(F32) + g2[...].astype(F32)

    def got_spec(k):
        return pl.BlockSpec((None, tr, cols), functools.partial(lambda i, pr, k: (k, i, 0), k=k))

    grid_spec = pltpu.PrefetchScalarGridSpec(
        num_scalar_prefetch=1, grid=(h // tr,),
        in_specs=[pl.BlockSpec((None, None, tr, cols), lambda i, pr: (pr[1], pr[0], i, 0)),
                  pl.BlockSpec((None, tr, cols), lambda i, pr: (pr[1], i, 0)), got_spec(0), got_spec(1), got_spec(2)],
        out_specs=pl.BlockSpec((None, tr, cols), lambda i, pr: (pr[0], i, 0)))
    return pl.pallas_call(body, name=name, grid_spec=grid_spec, out_shape=jax.ShapeDtypeStruct((2, h, cols), F32),
                          compiler_params=_params(("parallel",)))(place, gb.reshape(4, 2, h, cols), got_pair, got, got, got)


def _carry_allgather_all(v):
    def copies(ins, outs, send_sems, recv_sems):
        (v_ref,), (out_ref,) = ins, outs
        x, y, c, chips = _place()
        me, sibling = (x, y, c), (x, y, 1 - c)

        def rows(px, py, pc):
            return out_ref.at[4 * px + 2 * py + pc]

        def copy(k, block, to, src=None):
            return _remote(rows(*block) if src is None else src, rows(*block), send_sems.at[k], recv_sems.at[k], to)

        first = [copy(0, me, sibling, src=v_ref)] + [copy(1 + j, me, (*chip, c), src=v_ref) for j, chip in enumerate(chips)]
        passed = [copy(4 + j, (*chip, c), sibling) for j, chip in enumerate(chips)]
        arrived = [copy(1 + j, (*chip, c), me) for j, chip in enumerate(chips)]
        from_sibling = [copy(0, sibling, me)] + [copy(4 + j, (*chip, 1 - c), me) for j, chip in enumerate(chips)]
        return first, passed, arrived, from_sibling

    def start(*parts):
        for cp in copies(*parts)[0]:
            cp.start()

    def finish(*parts):
        first, passed, arrived, from_sibling = copies(*parts)
        for a, p in zip(arrived, passed):
            a.wait_recv()
            p.start()
        for cp in from_sibling:
            cp.wait_recv()
        for cp in first + passed:
            cp.wait_send()

    return _Carry([v], [jax.ShapeDtypeStruct((8,) + v.shape, v.dtype)], 7, start, finish)


def _allgather_all(v, name):
    return _run_carry(_carry_allgather_all(v), name)[0]


def _row_tile(rows, cols, limit_bytes):
    for d in range(1, rows + 1):
        if rows % d == 0 and (rows // d) % 8 == 0 and (rows // d) * cols * 4 <= limit_bytes:
            return rows // d
    return rows


def _adamw_math(w, gv, m, v):
    nm = ADAM_B1 * m + (1.0 - ADAM_B1) * gv
    nv = ADAM_B2 * v + (1.0 - ADAM_B2) * (gv * gv)
    m_hat = nm / (1.0 - ADAM_B1 ** ADAM_STEP)
    v_hat = nv / (1.0 - ADAM_B2 ** ADAM_STEP)
    return -ADAM_LR * (m_hat / (jnp.sqrt(v_hat) + ADAM_EPS) + ADAM_WD * w), nm, nv


def _adamw_layers(w, g0, g1, m, v, *, name, block_bytes=1 << 20):
    _, rows, cols = w.shape
    tr = _row_tile(rows, cols, block_bytes)

    def body(w_ref, g0_ref, g1_ref, m_ref, v_ref, g_ref, d_ref, nm_ref, nv_ref):
        gv = jnp.where(pl.program_id(0) == 0, g0_ref[...], g1_ref[...])
        g_ref[...] = gv
        d_ref[...], nm_ref[...], nv_ref[...] = _adamw_math(w_ref[...], gv, m_ref[...], v_ref[...])

    both = pl.BlockSpec((None, tr, cols), lambda l, i: (l, i, 0))
    specs = [both, pl.BlockSpec((tr, cols), lambda l, i: (i * (1 - l), 0)), pl.BlockSpec((tr, cols), lambda l, i: (i * l, 0)), both, both]
    return pl.pallas_call(body, name=name, grid=(2, rows // tr), in_specs=specs, out_specs=[both] * 4,
                          out_shape=[jax.ShapeDtypeStruct(w.shape, F32)] * 4, compiler_params=_params(("arbitrary", "arbitrary")))(
                              w, g0, g1, m, v)


def _adamw(w, g, m, v, *, name, block_bytes=1 << 20):
    rows, cols = w.shape
    tr = _row_tile(rows, cols, block_bytes)

    def body(w_ref, g_ref, m_ref, v_ref, d_ref, nm_ref, nv_ref):
        d_ref[...], nm_ref[...], nv_ref[...] = _adamw_math(w_ref[...], g_ref[...], m_ref[...], v_ref[...])

    spec = pl.BlockSpec((tr, cols), lambda i: (i, 0))
    return pl.pallas_call(body, name=name, grid=(rows // tr,), in_specs=[spec] * 4, out_specs=[spec] * 3,
                          out_shape=[jax.ShapeDtypeStruct((rows, cols), F32)] * 3, compiler_params=_params(("parallel",)))(w, g, m, v)


WEIGHTS = ['w_in', 's5_lambda_re', 's5_lambda_im', 's5_log_step', 's5_b_re', 's5_b_im', 's5_c_re', 's5_c_im', 's5_d', 's5_glu_w',
           's5_glu_b', 'sgu_norm_g', 'sgu_norm_b', 'sgu_w', 'sgu_b', 'pool_w', 'pool_scale', 'dn_conv_w', 'dn_a_log', 'dn_dt_bias',
           'dn_norm_g', 'w_out', 'ln1_g', 'ln1_b', 'w_up', 'w_down', 'ln2_g', 'ln2_b']
BIG = ['w_in', 's5_glu_w', 'w_out', 'w_up', 'w_down']
SMALL = [n for n in WEIGHTS if n not in BIG]
CB_S5, CB_SGU_U, CB_SGU_V, CB_POOL, CB_DN_Q, CB_DN_GATE = 0, 1, 2, 3, 4, 7
KT = 2048


def _pad_lanes(v, width=128):
    return jnp.zeros((1, width), F32).at[0, :v.shape[0]].set(v)


def _layer_consts(p):
    c = _s5_prepare(p['s5_lambda_re'], p['s5_lambda_im'], p['s5_log_step'], p['s5_b_re'], p['s5_b_im'], p['s5_c_re'], p['s5_c_im'])
    tril = jnp.tril(jnp.ones((SGU_CHUNK, SGU_CHUNK), bool))
    wm = jnp.where(tril, p['sgu_w'], 0.0)
    c.update(s5_d=p['s5_d'].reshape(1, GW), glu_b=p['s5_glu_b'].reshape(1, GW), sgu_ng=p['sgu_norm_g'].reshape(1, GW),
             sgu_nb=p['sgu_norm_b'].reshape(1, GW), sgu_w=wm, sgu_wt=jnp.swapaxes(wm, 1, 2),
             sgu_bias=jnp.repeat(p['sgu_b'].T, SGU_HD, axis=1), pool_w=p['pool_w'], pool_scale=p['pool_scale'].reshape(1, GW),
             conv_w=p['dn_conv_w'], alog=_pad_lanes(p['dn_a_log']), dtb=_pad_lanes(p['dn_dt_bias']), dn_ng=p['dn_norm_g'].reshape(1, DN_HD),
             ln1_g=p['ln1_g'].reshape(1, D_MODEL), ln1_b=p['ln1_b'].reshape(1, D_MODEL), ln2_g=p['ln2_g'].reshape(1, D_MODEL),
             ln2_b=p['ln2_b'].reshape(1, D_MODEL))
    return c


def _layer_fwd(xin, xin16, w, c, i, carries):
    tag = str(i)
    residual = lambda r, e: (r + ALPHA * e,)

    def mm(a, b_name, *, name, **kw):
        if name not in carries:
            return _matmul(a, w[b_name], name=name + tag, **kw)
        carry, done = carries[name]
        outs, extra = _matmul(a, w[b_name], name=name + tag, carry=carry, **kw)
        done(extra)
        return outs

    (proj,) = mm(xin16, 'w_in', mode="nn", name="proj", tn=1408, tk=KT)
    s5, xre, xim = _s5_fwd(proj, c['bbre'], c['bbim'], c['ccre'], c['ccim'], c['s5_d'], c['cf'], w['s5_glu_w'], c['glu_b'], cb=CB_S5)
    sgu = _sgu_fwd(proj, c['sgu_ng'], c['sgu_nb'], c['sgu_w'], c['sgu_bias'], cbu=CB_SGU_U, cbv=CB_SGU_V)
    pool, pooled = _pool_fwd(proj, c['pool_w'], c['pool_scale'], cb=CB_POOL)
    qn, kn, vs, gb = _dn_prep_fwd(proj, c['conv_w'], c['alog'], c['dtb'], cbq=CB_DN_Q)
    u, wm, qg, ktail, qkd, tm = _dn_local_fwd(qn, kn, vs, gb)
    dn, raw, v_new, states = _dn_seq_fwd(u, wm, qg, ktail, qkd, gb, proj, c['dn_ng'], cb_gate=CB_DN_GATE)
    mixed = jnp.concatenate([s5, sgu, pool, dn], axis=1)
    (h1,) = mm(mixed, 'w_out', mode="nn", name="mix_out", e=xin, epi=residual, tk=KT)
    x1, x1_16 = _ln_fwd(h1, c['ln1_g'], c['ln1_b'], name="ln1_" + tag)
    (hidden,) = mm(x1_16, 'w_up', mode="nn", name="mlp_up", epi=lambda r, e: (_relu2(r),), out_dtypes=(MXU_DTYPE,), tm=2048, tk=KT,
                   b_blocked=True)
    (h2,) = mm(hidden, 'w_down', mode="nn", name="mlp_down", e=x1, epi=residual, tk=KT)
    x2, x2_16 = _ln_fwd(h2, c['ln2_g'], c['ln2_b'], name="ln2_" + tag)
    saved = dict(xin16=xin16, proj=proj, xre=xre, xim=xim, pooled=pooled, qn=qn, kn=kn, vs=vs, gb=gb, raw=raw, states=states,
                 wm=wm, qg=qg, ktail=ktail, qkd=qkd, tm=tm, v_new=v_new, mixed=mixed, h1=h1, x1_16=x1_16, hidden=hidden, h2=h2)
    return x2, x2_16, saved


def _by_rows(g):
    return g.reshape(4, g.shape[0] // 4, g.shape[1])


def _by_cols(g):
    return jnp.transpose(g.reshape(g.shape[0], 4, g.shape[1] // 4), (1, 0, 2))


def _layer_bwd(dx2, s, w, c, p, i, place, small_ride=None):
    tag = str(i)
    residual = lambda r, e: (r + ALPHA * e,)
    reduced = {}

    def pair_sums(blocks, got, names):
        return [(g, r, _pair_sum(g, r, place, name="pair_sum_" + nm + tag)) for g, r, nm in zip(blocks, got, names)]

    def pair(blocks, names):
        return pair_sums(blocks, _run_carry(_carry_pair_exchange(blocks), "grad_pair_exchange_" + names[0] + tag), names)

    def riding(ps, names, a, b, **kw):
        outs, got = _matmul(a, b, carry=_carry_chip_exchange([p16 for _, _, p16 in ps]), **kw)
        bufs = _pair_join([_chip_sum(g, r, t, place, name="chip_sum_" + nm + tag) for (g, r, _), t, nm in zip(ps, got, names)],
                          name="grad_pair_join_" + names[0] + tag)
        for nm, buf in zip(names, bufs):
            reduced[nm] = buf.reshape(-1, buf.shape[-1])
        return outs

    dh2, dh2_16, dln2g, dln2b = _ln_bwd(dx2, s['h2'], c['ln2_g'], name="ln2_bwd" + tag)
    (dw_down,) = _matmul(s['hidden'], dh2_16, mode="tn", name="dw_down" + tag, tk=KT)
    (da,), got = _matmul(dh2_16, w['w_down'], mode="nt", name="d_hidden" + tag, e=s['hidden'],
                         epi=lambda r, e: (r * (2.0 * jnp.sqrt(e.astype(F32))),), out_dtypes=(MXU_DTYPE,), tm=2048, tk=KT,
                         carry=_carry_pair_exchange([_by_rows(dw_down)]))
    p_down = pair_sums([_by_rows(dw_down)], got, ['w_down'])
    (dw_up,) = riding(p_down, ['w_down'], s['x1_16'], da, mode="tn", name="dw_up" + tag, tk=KT, out_blocked=True)
    p_up = pair([dw_up], ['w_up'])
    (dx1,) = riding(p_up, ['w_up'], da, w['w_up'], mode="nt", name="dx_mlp" + tag, e=dh2, epi=residual, tk=KT, b_blocked=True)
    dh1, dh1_16, dln1g, dln1b = _ln_bwd(dx1, s['h1'], c['ln1_g'], name="ln1_bwd" + tag)
    (dw_out,) = _matmul(s['mixed'], dh1_16, mode="tn", name="dw_out" + tag, tk=KT)
    p_out = pair([_by_rows(dw_out)], ['w_out'])
    (dmixed,) = riding(p_out, ['w_out'], dh1_16, w['w_out'], mode="nt", name="d_mixed" + tag, tk=KT)
    proj = s['proj']
    (du, dglu_w, dglu_b, dd, dccre, dccim, dbbre, dbbim, sre, sim) = _s5_bwd(
        dmixed, proj, s['xre'], s['xim'], c['bbre'], c['bbim'], c['ccre'], c['ccim'], c['s5_d'], c['cr'], w['s5_glu_w'], c['glu_b'],
        cb_dy=0, cb=CB_S5)
    dlam_re, dlam_im, dlog_step, db_re, db_im, dc_re, dc_im = _s5_param_grads(
        p['s5_lambda_re'], p['s5_lambda_im'], p['s5_log_step'], p['s5_b_re'], p['s5_b_im'], dbbre, dbbim, dccre, dccim, sre, sim)
    dzu, dzv, dsgu_w, dsgu_bias, dsgu_ng, dsgu_nb = _sgu_bwd(dmixed, proj, c['sgu_ng'], c['sgu_nb'], c['sgu_w'], c['sgu_wt'], c['sgu_bias'],
                                                            cb=1, cbu=CB_SGU_U, cbv=CB_SGU_V)
    dp, dpool_w, dpool_scale = _pool_bwd(dmixed, s['pooled'], c['pool_w'], c['pool_scale'], cb=2)
    dgate, ddn_ng, do, d_vnew, d_wm, d_qg, d_ktail, d_eglast = _dn_seq_bwd(
        dmixed, s['raw'], proj, c['dn_ng'], s['gb'], s['wm'], s['qg'], s['ktail'], s['qkd'], s['v_new'], s['states'],
        cb_dy=3, cb_gate=CB_DN_GATE)
    dqn, dkn, dvs, dgb = _dn_local_bwd(s['qn'], s['kn'], s['vs'], s['gb'], s['tm'], s['v_new'], do, d_vnew, d_wm, d_qg, d_ktail, d_eglast)
    dcq, dck, dcv, dab, dalog, ddtb = _dn_prep_bwd_a(proj, c['conv_w'], c['alog'], c['dtb'], dqn, dkn, dvs, dgb, cbq=CB_DN_Q)
    dq, dk, dv, dconv_w = _dn_prep_bwd_b(proj, c['conv_w'], dcq, dck, dcv, cbq=CB_DN_Q)
    dproj = jnp.concatenate([du, dzu, dzv, dp, dq, dk, dv, dgate, dab], axis=1)
    tril = jnp.tril(jnp.ones((SGU_CHUNK, SGU_CHUNK), bool))
    small = dict(
        s5_lambda_re=dlam_re, s5_lambda_im=dlam_im, s5_log_step=dlog_step, s5_b_re=db_re, s5_b_im=db_im, s5_c_re=dc_re, s5_c_im=dc_im,
        s5_d=dd.reshape(S5_G, S5_H), s5_glu_b=dglu_b[0], sgu_norm_g=dsgu_ng[0], sgu_norm_b=dsgu_nb[0],
        sgu_w=jnp.where(tril, dsgu_w, 0.0), sgu_b=dsgu_bias.reshape(SGU_CHUNK, SGU_HEADS, SGU_HD).sum(-1).T, pool_w=dpool_w,
        pool_scale=dpool_scale[0], dn_conv_w=dconv_w, dn_a_log=dalog[0, :DN_HEADS], dn_dt_bias=ddtb[0, :DN_HEADS], dn_norm_g=ddn_ng[0],
        ln1_g=dln1g[0], ln1_b=dln1b[0], ln2_g=dln2g[0], ln2_b=dln2b[0])
    if small_ride is None:
        (dw_in,) = _matmul(s['xin16'], dproj, mode="tn", name="dw_in" + tag, tn=1408, tk=KT)
    else:
        carry, done = small_ride(small)
        (dw_in,), extra = _matmul(s['xin16'], dproj, mode="tn", name="dw_in" + tag, tn=1408, tk=KT, carry=carry)
        done(extra)
    p_in = pair([_by_cols(dw_in[:, :IN_COLS]), _by_rows(dglu_w)], ['w_in', 's5_glu_w'])
    (dxin,) = riding(p_in, ['w_in', 's5_glu_w'], dproj, w['w_in'], mode="nt", name="dx_in" + tag, tk=1408, e=dh1, epi=residual)
    return dxin, reduced, small


def _pack(arrs):
    rows = []
    for a in arrs:
        n = math.prod(a.shape)
        rows.append(jnp.pad(a.reshape(-1), (0, -n % 128)).reshape(-1, 128))
    out = jnp.concatenate(rows, axis=0)
    return jnp.pad(out, ((0, -out.shape[0] % 8), (0, 0)))


def _sum_all(stacked, mine, dev):
    n, rows, cols = stacked.shape
    tr = _row_tile(rows, cols, 1 << 20)

    def body(dev_ref, mine_ref, *refs):
        acc = None
        for d in range(n):
            blk = jnp.where(dev_ref[0] == d, mine_ref[...], refs[d][...])
            acc = blk if acc is None else acc + blk
        refs[n][...] = acc

    def gathered(d):
        return pl.BlockSpec((None, tr, cols), lambda i, p: (jnp.where(p[0] == d, (d + 1) % n, d), i, 0))

    flat = pl.BlockSpec((tr, cols), lambda i, p: (i, 0))
    grid_spec = pltpu.PrefetchScalarGridSpec(num_scalar_prefetch=1, grid=(rows // tr,),
                                             in_specs=[flat] + [gathered(d) for d in range(n)], out_specs=flat)
    return pl.pallas_call(body, name="small_sum", grid_spec=grid_spec, out_shape=jax.ShapeDtypeStruct((rows, cols), F32),
                          compiler_params=_params(("parallel",)))(dev, mine, *([stacked] * n))


def _unpack(packed, like):
    out, row = [], 0
    for a in like:
        n = math.prod(a.shape)
        rows = -(-n // 128)
        out.append(packed[row:row + rows].reshape(-1)[:n].reshape(a.shape))
        row += rows
    return out


def kernel(x, w_in, s5_lambda_re, s5_lambda_im, s5_log_step, s5_b_re, s5_b_im, s5_c_re, s5_c_im, s5_d, s5_glu_w, s5_glu_b, sgu_norm_g, sgu_norm_b, sgu_w, sgu_b, pool_w, pool_scale, dn_conv_w, dn_a_log, dn_dt_bias, dn_norm_g, w_out, ln1_g, ln1_b, w_up, w_down, ln2_g, ln2_b, loss_target, m_w_in, m_s5_lambda_re, m_s5_lambda_im, m_s5_log_step, m_s5_b_re, m_s5_b_im, m_s5_c_re, m_s5_c_im, m_s5_d, m_s5_glu_w, m_s5_glu_b, m_sgu_norm_g, m_sgu_norm_b, m_sgu_w, m_sgu_b, m_pool_w, m_pool_scale, m_dn_conv_w, m_dn_a_log, m_dn_dt_bias, m_dn_norm_g, m_w_out, m_ln1_g, m_ln1_b, m_w_up, m_w_down, m_ln2_g, m_ln2_b, v_w_in, v_s5_lambda_re, v_s5_lambda_im, v_s5_log_step, v_s5_b_re, v_s5_b_im, v_s5_c_re, v_s5_c_im, v_s5_d, v_s5_glu_w, v_s5_glu_b, v_sgu_norm_g, v_sgu_norm_b, v_sgu_w, v_sgu_b, v_pool_w, v_pool_scale, v_dn_conv_w, v_dn_a_log, v_dn_dt_bias, v_dn_norm_g, v_w_out, v_ln1_g, v_ln1_b, v_w_up, v_w_down, v_ln2_g, v_ln2_b):
    given = dict(locals())
    xs, ys = lax.axis_index("x"), lax.axis_index("y")
    chip = 2 * xs + ys
    t = given['x'].shape[1]
    x0 = given['x'].reshape(t, D_MODEL)
    target = given['loss_target'].reshape(t, D_MODEL)

    assert DEPTH == 2
    place = jnp.stack([lax.axis_index("c"), chip]).astype(jnp.int32)
    conv_local = given['dn_conv_w']
    conv_all = _allgather_all(_pack([conv_local]), "allgather_conv")
    n_conv = math.prod(conv_local.shape)
    conv_full = jnp.concatenate([jnp.where(chip == j, conv_local, conv_all[2 * j].reshape(-1)[:n_conv].reshape(conv_local.shape))
                                 for j in range(4)], axis=-1)

    ws = [dict(), dict()]

    def whole(n, blocks):
        if n == 'w_in':
            return jnp.pad(jnp.transpose(blocks, (1, 0, 2)).reshape(D_MODEL, IN_COLS), ((0, 0), (0, IN_PAD - IN_COLS)))
        if n == 'w_up':
            return blocks
        return blocks.reshape(-1, blocks.shape[-1])

    def gather(items):
        own = [given[n][i].astype(MXU_DTYPE) for n, i in items]

        def done(bufs):
            for (n, i), buf, mine in zip(items, bufs, own):
                ws[i][n] = whole(n, lax.dynamic_update_slice(buf, mine[None], (chip, 0, 0)))
        return _carry_allgather(own), done

    first, first_done = gather([('w_in', 0), ('s5_glu_w', 0), ('w_out', 0)])
    first_done(_run_carry(first, "allgather_first"))
    carries = [dict(proj=gather([('w_up', 0)]), mix_out=gather([('w_down', 0)]),
                    mlp_up=gather([('w_in', 1), ('s5_glu_w', 1), ('w_out', 1)]), mlp_down=gather([('w_up', 1)])),
               dict(proj=gather([('w_down', 1)]))]

    def layer_params(i):
        p = {n: given[n][i] for n in SMALL}
        p['dn_conv_w'] = conv_full[i]
        return p

    ps = [layer_params(i) for i in range(DEPTH)]
    cs = [_layer_consts(p) for p in ps]

    xcur, xcur16, saved = x0, x0.astype(MXU_DTYPE), []
    for i in range(DEPTH):
        xcur, xcur16, s = _layer_fwd(xcur, xcur16, ws[i], cs[i], i, carries[i])
        saved.append(s)
    dx, colsum = _loss_head(xcur, target)
    loss = lax.psum(0.5 * jnp.sum(colsum) / D_MODEL, ("x", "y", "c"))

    reduced, smalls, small_comm = [None] * DEPTH, [None] * DEPTH, {}

    def small_ride(first_layer):
        full = [jnp.stack([first_layer[n], smalls[1][n]]) for n in SMALL]
        small_comm.update(full=full, packed=_pack(full))
        return _carry_allgather_all(small_comm['packed']), lambda outs: small_comm.update(gathered=outs[0])

    for i in reversed(range(DEPTH)):
        dx, reduced[i], smalls[i] = _layer_bwd(dx, saved[i], ws[i], cs[i], ps[i], i, place, small_ride if i == 0 else None)
    grad_x = dx.reshape(1, t, D_MODEL)
    dev = (2 * chip + lax.axis_index("c")).astype(jnp.int32).reshape(1)
    grads = dict(zip(SMALL, _unpack(_sum_all(small_comm['gathered'], small_comm['packed'], dev), small_comm['full'])))
    grads['dn_conv_w'] = lax.dynamic_slice_in_dim(grads['dn_conv_w'], chip * conv_local.shape[-1], conv_local.shape[-1], axis=2)

    delta, new_m, new_v = {}, {}, {}
    for n in BIG:
        grads[n], delta[n], new_m[n], new_v[n] = _adamw_layers(given[n], reduced[0][n], reduced[1][n], given['m_' + n], given['v_' + n],
                                                               name="adamw_" + n)
    like = [given[n] for n in SMALL]
    d, nm, nv = _adamw(_pack(like), _pack([grads[n] for n in SMALL]), _pack([given['m_' + n] for n in SMALL]),
                       _pack([given['v_' + n] for n in SMALL]), name="adamw_small")
    for out, packed in ((delta, d), (new_m, nm), (new_v, nv)):
        out.update(zip(SMALL, _unpack(packed, like)))
    return (loss, grad_x, *[grads[n] for n in WEIGHTS], *[delta[n] for n in WEIGHTS], *[new_m[n] for n in WEIGHTS],
            *[new_v[n] for n in WEIGHTS])
```

```python
import functools
import math

import jax
import jax.numpy as jnp
from jax import lax
from jax.experimental import pallas as pl
from jax.experimental.pallas import tpu as pltpu

F32 = jnp.float32
MXU_DTYPE = jnp.bfloat16
HI = lax.Precision.HIGHEST

D_MODEL = 2048
DEPTH = 2
GW = 512
S5_H = 16
S5_G = GW // S5_H
S5_P = 64
S5_N = S5_G * S5_P
SGU_CHUNK = 128
SGU_HEADS = 8
SGU_HD = GW // SGU_HEADS
POOL_WINDOWS = (2, 4, 8, 16)
POOL_GD = 128
DN_HD = 128
DN_HEADS = 4
DN_CONV = 4
DN_CHUNK = 64
D_FF = 4 * D_MODEL
IN_COLS = 4104
IN_PAD = 4224
LN_EPS = 1e-5
RMS_EPS = 1e-6
L2_EPS = 1e-6
ALPHA = (2 * DEPTH) ** 0.25
ADAM_LR, ADAM_B1, ADAM_B2, ADAM_EPS, ADAM_WD, ADAM_STEP = 0.001, 0.9, 0.999, 1e-08, 0.01, 10

VMEM_LIMIT = 56 * 1024 * 1024
MESH = pl.DeviceIdType.MESH


def _params(sem=None, vmem=VMEM_LIMIT):
    return pltpu.CompilerParams(dimension_semantics=sem, vmem_limit_bytes=vmem)


def _full(shape):
    nd = len(shape)
    return pl.BlockSpec(shape, lambda *_: (0,) * nd)


def _split(a):
    hi = a.astype(MXU_DTYPE)
    return hi, (a - hi.astype(F32)).astype(MXU_DTYPE)


def _dot(a, b, dims=(((1,), (0,)), ((), ())), exact=False):
    if exact and MXU_DTYPE == F32:
        return lax.dot_general(a, b, dims, precision=HI, preferred_element_type=F32)
    if exact:
        (ah, al), (bh, bl) = _split(a), _split(b)
        return (lax.dot_general(ah, bh, dims, preferred_element_type=F32) + lax.dot_general(al, bh, dims, preferred_element_type=F32)
                + lax.dot_general(ah, bl, dims, preferred_element_type=F32))
    return lax.dot_general(a.astype(MXU_DTYPE), b.astype(MXU_DTYPE), dims, preferred_element_type=F32)


NN = (((1,), (0,)), ((), ()))
NT = (((1,), (1,)), ((), ()))
TN = (((0,), (0,)), ((), ()))


def _gelu(x):
    c = math.sqrt(2.0 / math.pi)
    return 0.5 * x * (1.0 + jnp.tanh(c * (x + 0.044715 * x * x * x)))


def _gelu_grad(x):
    c = math.sqrt(2.0 / math.pi)
    t = jnp.tanh(c * (x + 0.044715 * x * x * x))
    return 0.5 * (1.0 + t) + 0.5 * x * (1.0 - t * t) * c * (1.0 + 3.0 * 0.044715 * x * x)


def _sigmoid(x):
    return 1.0 / (1.0 + jnp.exp(-x))


def _relu2(x):
    r = jnp.maximum(x, 0.0)
    return r * r


class _Carry:
    def __init__(self, ins, out_shapes, nsem, start, finish):
        self.ins, self.out_shapes, self.nsem, self.start, self.finish = list(ins), list(out_shapes), nsem, start, finish

    def sems(self):
        return [pltpu.SemaphoreType.DMA((self.nsem,)), pltpu.SemaphoreType.DMA((self.nsem,))]


def _run_carry(carry, name):
    n_in, n_out = len(carry.ins), len(carry.out_shapes)

    def body(*refs):
        parts = refs[:n_in], refs[n_in:n_in + n_out], refs[-2], refs[-1]
        carry.start(*parts)
        carry.finish(*parts)

    any_spec = pl.BlockSpec(memory_space=pl.ANY)
    return pl.pallas_call(body, name=name, in_specs=[any_spec] * n_in, out_specs=[any_spec] * n_out, out_shape=carry.out_shapes,
                          scratch_shapes=carry.sems())(*carry.ins)


def _matmul(a, b, *, mode, name, e=None, epi=None, out_dtypes=(F32,), tm=1024, tn=1024, tk=512, carry=None, b_blocked=False,
            out_blocked=False):
    if mode == "nn":
        (m, k), n = a.shape, (4 * b.shape[2] if b_blocked else b.shape[1])
    elif mode == "nt":
        m, n, k = a.shape[0], b.shape[-2], a.shape[1]
    else:
        (k, m), n = a.shape, b.shape[1]
    tm, tn, tk = min(tm, m), min(tn, n), min(tk, k)
    if b_blocked or out_blocked:
        tn, tk = min(tn, n // 4), (min(tk, k // 4) if mode == "nt" and b_blocked else tk)
    assert m % tm == 0 and n % tn == 0 and k % tk == 0, (name, m, n, k, tm, tn, tk)
    nk, nout = k // tk, len(out_dtypes)
    dims = {"nn": NN, "nt": NT, "tn": TN}[mode]
    a_spec = pl.BlockSpec((tk, tm), lambda i, j, l: (l, i)) if mode == "tn" else pl.BlockSpec((tm, tk), lambda i, j, l: (i, l))
    nb, kb = max(n // 4 // tn, 1), max(k // 4 // tk, 1)

    def split(idx, per):
        return lax.div(idx, jnp.int32(per)), lax.rem(idx, jnp.int32(per))

    if b_blocked and mode == "nn":
        b_spec = pl.BlockSpec((None, tk, tn), lambda i, j, l: (split(j, nb)[0], l, split(j, nb)[1]))
    elif b_blocked:
        b_spec = pl.BlockSpec((None, tn, tk), lambda i, j, l: (split(l, kb)[0], j, split(l, kb)[1]))
    else:
        b_spec = pl.BlockSpec((tn, tk), lambda i, j, l: (j, l)) if mode == "nt" else pl.BlockSpec((tk, tn), lambda i, j, l: (l, j))
    if out_blocked:
        o_spec = pl.BlockSpec((None, tm, tn), lambda i, j, l: (split(j, nb)[0], i, split(j, nb)[1]))
    else:
        o_spec = pl.BlockSpec((tm, tn), lambda i, j, l: (i, j))
    assert not (out_blocked and e is not None)
    o_shape = (4, m, n // 4) if out_blocked else (m, n)

    n_in = 2 + (e is not None)
    c_in, c_out = (len(carry.ins), len(carry.out_shapes)) if carry is not None else (0, 0)
    gm, gn = m // tm, n // tn

    def body(*refs):
        a_ref, b_ref = refs[:2]
        e_ref = refs[2] if e is not None else None
        o_refs = refs[n_in + c_in:n_in + c_in + nout]
        acc = refs[n_in + c_in + nout + c_out]
        l = pl.program_id(2)
        if carry is not None:
            parts = refs[n_in:n_in + c_in], refs[n_in + c_in + nout:n_in + c_in + nout + c_out], refs[-2], refs[-1]
            step = (pl.program_id(0) * gn + pl.program_id(1)) * nk + l

            @pl.when(step == 0)
            def _():
                carry.start(*parts)

        d = _dot(a_ref[...], b_ref[...], dims)

        def finish(r):
            outs = (r,) if epi is None else epi(r, None if e_ref is None else e_ref[...])
            for o_ref, o, dt in zip(o_refs, outs, out_dtypes, strict=True):
                o_ref[...] = o.astype(dt)

        if nk == 1:
            finish(d)
        else:
            @pl.when(l == 0)
            def _():
                acc[...] = d

            @pl.when((l > 0) & (l < nk - 1))
            def _():
                acc[...] += d

            @pl.when(l == nk - 1)
            def _():
                finish(acc[...] + d)

        if carry is not None:
            @pl.when(step == gm * gn * nk - 1)
            def _():
                carry.finish(*parts)

    ins, specs = [a, b], [a_spec, b_spec]
    if e is not None:
        ins.append(e)
        specs.append(o_spec)
    out_shape = [jax.ShapeDtypeStruct(o_shape, dt) for dt in out_dtypes]
    out_specs, scratch = [o_spec] * nout, [pltpu.VMEM((tm, tn), F32)]
    if carry is not None:
        any_spec = pl.BlockSpec(memory_space=pl.ANY)
        ins, specs = ins + carry.ins, specs + [any_spec] * c_in
        out_shape, out_specs, scratch = out_shape + carry.out_shapes, out_specs + [any_spec] * c_out, scratch + carry.sems()
    sem = ("parallel", "parallel", "arbitrary") if carry is None else ("arbitrary",) * 3
    res = pl.pallas_call(body, name=name, grid=(gm, gn, nk), in_specs=specs, out_specs=out_specs, out_shape=out_shape,
                         scratch_shapes=scratch, compiler_params=_params(sem))(*ins)
    return tuple(res) if carry is None else (tuple(res[:nout]), list(res[nout:]))


def _ln_fwd(h, g, b, *, name, tr=256):
    t, d = h.shape

    def body(h_ref, g_ref, b_ref, o_ref, o16_ref):
        x = h_ref[...]
        mu = jnp.mean(x, axis=-1, keepdims=True)
        xc = x - mu
        var = jnp.mean(xc * xc, axis=-1, keepdims=True)
        y = xc * lax.rsqrt(var + LN_EPS) * g_ref[...] + b_ref[...]
        o_ref[...] = y
        o16_ref[...] = y.astype(MXU_DTYPE)

    row = pl.BlockSpec((tr, d), lambda i: (i, 0))
    return pl.pallas_call(body, name=name, grid=(t // tr,), in_specs=[row, _full((1, d)), _full((1, d))], out_specs=[row, row],
                          out_shape=[jax.ShapeDtypeStruct((t, d), F32), jax.ShapeDtypeStruct((t, d), MXU_DTYPE)],
                          compiler_params=_params(("parallel",)))(h, g, b)


def _ln_bwd(dy, h, g, *, name, tr=256):
    t, d = h.shape

    def body(dy_ref, h_ref, g_ref, dh_ref, dh16_ref, dg_ref, db_ref):
        @pl.when(pl.program_id(0) == 0)
        def _():
            dg_ref[...] = jnp.zeros_like(dg_ref)
            db_ref[...] = jnp.zeros_like(db_ref)

        x, dyv = h_ref[...], dy_ref[...]
        mu = jnp.mean(x, axis=-1, keepdims=True)
        xc = x - mu
        rstd = lax.rsqrt(jnp.mean(xc * xc, axis=-1, keepdims=True) + LN_EPS)
        xh = xc * rstd
        w = dyv * g_ref[...]
        dh = rstd * (w - jnp.mean(w, axis=-1, keepdims=True) - xh * jnp.mean(w * xh, axis=-1, keepdims=True))
        dh_ref[...] = dh
        dh16_ref[...] = dh.astype(MXU_DTYPE)
        dg_ref[...] += jnp.sum(dyv * xh, axis=0, keepdims=True)
        db_ref[...] += jnp.sum(dyv, axis=0, keepdims=True)

    row = pl.BlockSpec((tr, d), lambda i: (i, 0))
    vec = _full((1, d))
    return pl.pallas_call(
        body, name=name, grid=(t // tr,), in_specs=[row, row, vec], out_specs=[row, row, vec, vec],
        out_shape=[jax.ShapeDtypeStruct((t, d), F32), jax.ShapeDtypeStruct((t, d), MXU_DTYPE), jax.ShapeDtypeStruct((1, d), F32),
                   jax.ShapeDtypeStruct((1, d), F32)],
        compiler_params=_params(("arbitrary",)))(dy, h, g)


def _loss_head(y, target, *, tr=256):
    t, d = y.shape

    def body(y_ref, t_ref, dy_ref, s_ref):
        @pl.when(pl.program_id(0) == 0)
        def _():
            s_ref[...] = jnp.zeros_like(s_ref)

        err = y_ref[...] - t_ref[...]
        dy_ref[...] = err * (1.0 / d)
        s_ref[...] += jnp.sum(err * err, axis=0, keepdims=True)

    row = pl.BlockSpec((tr, d), lambda i: (i, 0))
    return pl.pallas_call(
        body, name="loss_head", grid=(t // tr,), in_specs=[row, row], out_specs=[row, _full((1, d))],
        out_shape=[jax.ShapeDtypeStruct((t, d), F32), jax.ShapeDtypeStruct((1, d), F32)],
        compiler_params=_params(("arbitrary",)))(y, target)


def _cols(tb, width, cb):
    return pl.BlockSpec((tb, width), lambda i: (i, cb))


def _cols_rev(tb, width, cb, nb):
    return pl.BlockSpec((tb, width), lambda i: (nb - 1 - i, cb))


POOL_HALO = 16


def _pool_fwd(proj, w, scale, *, cb, tb=512):
    t = proj.shape[0]
    tb = min(tb, t)

    def body(p_ref, w_ref, s_ref, o_ref, pooled_ref, ext):
        i = pl.program_id(0)

        @pl.when(i == 0)
        def _():
            ext[pl.ds(0, POOL_HALO), :] = jnp.zeros((POOL_HALO, GW), F32)

        p = p_ref[...]
        ext[pl.ds(POOL_HALO, tb), :] = p
        pos = (i * tb + lax.broadcasted_iota(jnp.int32, (tb, 1), 0) + 1).astype(F32)
        for gi, win in enumerate(POOL_WINDOWS):
            c0 = gi * POOL_GD
            s = p[:, c0:c0 + POOL_GD]
            for k in range(1, win):
                s = s + ext[pl.ds(POOL_HALO - k, tb), pl.ds(c0, POOL_GD)]
            pooled = s / jnp.minimum(pos, float(win)) - p[:, c0:c0 + POOL_GD]
            pooled_ref[:, pl.ds(c0, POOL_GD)] = pooled
            o_ref[:, pl.ds(c0, POOL_GD)] = (_dot(pooled, w_ref[gi]) * s_ref[:, pl.ds(c0, POOL_GD)]).astype(o_ref.dtype)
        ext[pl.ds(0, POOL_HALO), :] = p[tb - POOL_HALO:, :]

    row = pl.BlockSpec((tb, GW), lambda i: (i, 0))
    return pl.pallas_call(
        body, name="pool_fwd", grid=(t // tb,),
        in_specs=[_cols(tb, GW, cb), _full((4, POOL_GD, POOL_GD)), _full((1, GW))], out_specs=[row, row],
        out_shape=[jax.ShapeDtypeStruct((t, GW), MXU_DTYPE), jax.ShapeDtypeStruct((t, GW), F32)],
        scratch_shapes=[pltpu.VMEM((tb + POOL_HALO, GW), F32)],
        compiler_params=_params(("arbitrary",)))(proj, w, scale)


def _pool_bwd(dmixed, pooled, w, scale, *, cb, tb=512):
    t = pooled.shape[0]
    tb = min(tb, t)
    nb = t // tb

    def body(dy_ref, pooled_ref, w_ref, s_ref, dp_ref, dw_ref, ds_ref, ext):
        i = pl.program_id(0)

        @pl.when(i == 0)
        def _():
            ext[pl.ds(tb, POOL_HALO), :] = jnp.zeros((POOL_HALO, GW), F32)
            dw_ref[...] = jnp.zeros_like(dw_ref)
            ds_ref[...] = jnp.zeros_like(ds_ref)

        dy = dy_ref[...]
        pos = ((nb - 1 - i) * tb + lax.broadcasted_iota(jnp.int32, (tb, 1), 0) + 1).astype(F32)
        dpool_all = []
        for gi, win in enumerate(POOL_WINDOWS):
            c0 = gi * POOL_GD
            pg = pooled_ref[:, pl.ds(c0, POOL_GD)]
            dyg = dy[:, c0:c0 + POOL_GD]
            ds_ref[:, pl.ds(c0, POOL_GD)] += jnp.sum(dyg * _dot(pg, w_ref[gi]), axis=0, keepdims=True)
            dmp = dyg * s_ref[:, pl.ds(c0, POOL_GD)]
            dw_ref[gi] += _dot(pg, dmp, TN)
            dpool = _dot(dmp, w_ref[gi], NT)
            dpool_all.append(dpool)
            ext[pl.ds(0, tb), pl.ds(c0, POOL_GD)] = dpool / jnp.minimum(pos, float(win))
        for gi, win in enumerate(POOL_WINDOWS):
            c0 = gi * POOL_GD
            s = ext[pl.ds(0, tb), pl.ds(c0, POOL_GD)]
            for k in range(1, win):
                s = s + ext[pl.ds(k, tb), pl.ds(c0, POOL_GD)]
            dp_ref[:, pl.ds(c0, POOL_GD)] = (s - dpool_all[gi]).astype(dp_ref.dtype)
        ext[pl.ds(tb, POOL_HALO), :] = ext[pl.ds(0, POOL_HALO), :]

    row = pl.BlockSpec((tb, GW), lambda i: (nb - 1 - i, 0))
    return pl.pallas_call(
        body, name="pool_bwd", grid=(nb,),
        in_specs=[_cols_rev(tb, GW, cb, nb), row, _full((4, POOL_GD, POOL_GD)), _full((1, GW))],
        out_specs=[row, _full((4, POOL_GD, POOL_GD)), _full((1, GW))],
        out_shape=[jax.ShapeDtypeStruct((t, GW), MXU_DTYPE), jax.ShapeDtypeStruct((4, POOL_GD, POOL_GD), F32),
                   jax.ShapeDtypeStruct((1, GW), F32)],
        scratch_shapes=[pltpu.VMEM((tb + POOL_HALO, GW), F32)], compiler_params=_params(("arbitrary",)))(dmixed, pooled, w, scale)


def _sgu_core(zu, zv, ng, nb, w_ref, bias):
    tb = zu.shape[0]
    u = _gelu(zu)
    v0 = _gelu(zv)
    mu = jnp.mean(v0, axis=-1, keepdims=True)
    vc = v0 - mu
    rstd = lax.rsqrt(jnp.mean(vc * vc, axis=-1, keepdims=True) + LN_EPS)
    xh = vc * rstd
    vn = xh * ng + nb
    low = lax.broadcasted_iota(jnp.int32, (SGU_CHUNK, 2 * SGU_HD), 1) < SGU_HD
    rows = []
    for n in range(tb // SGU_CHUNK):
        pairs = []
        for j in range(SGU_HEADS // 2):
            vp = vn[n * SGU_CHUNK:(n + 1) * SGU_CHUNK, j * 128:(j + 1) * 128]
            pairs.append(jnp.where(low, _dot(w_ref[2 * j], vp), _dot(w_ref[2 * j + 1], vp)))
        rows.append(jnp.concatenate(pairs, axis=1) + bias)
    mixed = jnp.concatenate(rows, axis=0)
    return u, xh, rstd, vn, mixed


def _sgu_fwd(proj, ng, nb, w, bias, *, cbu, cbv, tb=512):
    t = proj.shape[0]
    tb = min(tb, t)

    def body(zu_ref, zv_ref, ng_ref, nb_ref, w_ref, bias_ref, o_ref):
        u, _, _, _, mixed = _sgu_core(zu_ref[...], zv_ref[...], ng_ref[...], nb_ref[...], w_ref, bias_ref[...])
        o_ref[...] = (u * mixed).astype(o_ref.dtype)

    vec = _full((1, GW))
    return pl.pallas_call(
        body, name="sgu_fwd", grid=(t // tb,),
        in_specs=[_cols(tb, GW, cbu), _cols(tb, GW, cbv), vec, vec, _full((8, 128, 128)), _full((128, GW))],
        out_specs=pl.BlockSpec((tb, GW), lambda i: (i, 0)), out_shape=jax.ShapeDtypeStruct((t, GW), MXU_DTYPE),
        compiler_params=_params(("parallel",)))(proj, proj, ng, nb, w, bias)


def _sgu_bwd(dmixed, proj, ng, nb, w, wt, bias, *, cb, cbu, cbv, tb=512):
    t = proj.shape[0]
    tb = min(tb, t)

    def body(dy_ref, zu_ref, zv_ref, ng_ref, nb_ref, w_ref, wt_ref, bias_ref, dzu_ref, dzv_ref, dw_ref, dbias_ref, dng_ref, dnb_ref):
        @pl.when(pl.program_id(0) == 0)
        def _():
            dw_ref[...] = jnp.zeros_like(dw_ref)
            dbias_ref[...] = jnp.zeros_like(dbias_ref)
            dng_ref[...] = jnp.zeros_like(dng_ref)
            dnb_ref[...] = jnp.zeros_like(dnb_ref)

        zu, zv, dy = zu_ref[...], zv_ref[...], dy_ref[...]
        u, xh, rstd, vn, mixed = _sgu_core(zu, zv, ng_ref[...], nb_ref[...], w_ref, bias_ref[...])
        dzu_ref[...] = (dy * mixed * _gelu_grad(zu)).astype(dzu_ref.dtype)
        dmix = dy * u
        low = lax.broadcasted_iota(jnp.int32, (SGU_CHUNK, 2 * SGU_HD), 1) < SGU_HD
        dbias = jnp.zeros((SGU_CHUNK, GW), F32)
        rows = []
        for n in range(tb // SGU_CHUNK):
            dm = dmix[n * SGU_CHUNK:(n + 1) * SGU_CHUNK, :]
            dbias = dbias + dm
            pairs = []
            for j in range(SGU_HEADS // 2):
                dmp = dm[:, j * 128:(j + 1) * 128]
                vp = vn[n * SGU_CHUNK:(n + 1) * SGU_CHUNK, j * 128:(j + 1) * 128]
                dw_ref[2 * j] += _dot(jnp.where(low, dmp, 0.0), vp, NT)
                dw_ref[2 * j + 1] += _dot(jnp.where(low, 0.0, dmp), vp, NT)
                pairs.append(jnp.where(low, _dot(wt_ref[2 * j], dmp), _dot(wt_ref[2 * j + 1], dmp)))
            rows.append(jnp.concatenate(pairs, axis=1))
        dbias_ref[...] += dbias
        dvn = jnp.concatenate(rows, axis=0)
        dng_ref[...] += jnp.sum(dvn * xh, axis=0, keepdims=True)
        dnb_ref[...] += jnp.sum(dvn, axis=0, keepdims=True)
        wv = dvn * ng_ref[...]
        dv0 = rstd * (wv - jnp.mean(wv, axis=-1, keepdims=True) - xh * jnp.mean(wv * xh, axis=-1, keepdims=True))
        dzv_ref[...] = (dv0 * _gelu_grad(zv)).astype(dzv_ref.dtype)

    vec = _full((1, GW))
    row = pl.BlockSpec((tb, GW), lambda i: (i, 0))
    mat = _full((8, 128, 128))
    return pl.pallas_call(
        body, name="sgu_bwd", grid=(t // tb,),
        in_specs=[_cols(tb, GW, cb), _cols(tb, GW, cbu), _cols(tb, GW, cbv), vec, vec, mat, mat, _full((128, GW))],
        out_specs=[row, row, mat, _full((128, GW)), vec, vec],
        out_shape=[jax.ShapeDtypeStruct((t, GW), MXU_DTYPE)] * 2 + [jax.ShapeDtypeStruct((8, 128, 128), F32),
                   jax.ShapeDtypeStruct((128, GW), F32), jax.ShapeDtypeStruct((1, GW), F32), jax.ShapeDtypeStruct((1, GW), F32)],
        compiler_params=_params(("arbitrary",)))(dmixed, proj, proj, ng, nb, w, wt, bias)


S5_KB = 4
SUB = 8


def _s5_discretize(lam_re, lam_im, log_step, b_re, b_im):
    step = jnp.exp(log_step)[:, None]
    mag = jnp.exp(lam_re * step)
    lr, li = mag * jnp.cos(lam_im * step), mag * jnp.sin(lam_im * step)
    den = lam_re * lam_re + lam_im * lam_im
    fr = ((lr - 1.0) * lam_re + li * lam_im) / den
    fi = (li * lam_re - (lr - 1.0) * lam_im) / den
    return lr, li, fr[:, :, None] * b_re - fi[:, :, None] * b_im, fr[:, :, None] * b_im + fi[:, :, None] * b_re


def _cpow(lr, li, n):
    rr, ri = lr, li
    for _ in range(n - 1):
        rr, ri = rr * lr - ri * li, rr * li + ri * lr
    return rr, ri


def _s5_scan_consts(lr, li, reverse):
    lr, li = lr.reshape(1, S5_N), (-li if reverse else li).reshape(1, S5_N)
    row = jnp.arange(SUB)[:, None]
    out = []
    for s in (1, 2, 4):
        pr, pi = _cpow(lr, li, s)
        keep = (row < SUB - s) if reverse else (row >= s)
        out += [jnp.where(keep, pr, 0.0), jnp.where(keep, pi, 0.0)]
    cr, ci = [], []
    for i in range(SUB):
        pr, pi = _cpow(lr, li, SUB - i if reverse else i + 1)
        cr.append(pr)
        ci.append(pi)
    out += [jnp.concatenate(cr, axis=0), jnp.concatenate(ci, axis=0)]
    return jnp.stack(out)


def _s5_blockdiag_in(b):
    bt = jnp.swapaxes(b, 1, 2).reshape(S5_KB, 8, S5_H, S5_P)
    eye = jnp.eye(8, dtype=b.dtype)
    return jnp.einsum("kghp,gj->kghjp", bt, eye).reshape(S5_KB, 128, 512)


def _s5_blockdiag_in_extract(bb):
    x = bb.reshape(S5_KB, 8, S5_H, 8, S5_P)
    d = jnp.einsum("kghgp->kghp", x).reshape(S5_G, S5_H, S5_P)
    return jnp.swapaxes(d, 1, 2)


def _s5_blockdiag_out(c):
    ct = jnp.swapaxes(c, 1, 2).reshape(S5_KB, 8, S5_P, S5_H)
    eye = jnp.eye(8, dtype=c.dtype)
    return jnp.einsum("kgph,gj->kgpjh", ct, eye).reshape(S5_KB, 512, 128)


def _s5_blockdiag_out_extract(cc):
    x = cc.reshape(S5_KB, 8, S5_P, 8, S5_H)
    d = jnp.einsum("kgpgh->kgph", x).reshape(S5_G, S5_P, S5_H)
    return jnp.swapaxes(d, 1, 2)


def _s5_tile_scan(a, b, c_ref, carry, reverse):
    for si, s in enumerate((1, 2, 4)):
        sh = SUB - s if reverse else s
        ar, br = pltpu.roll(a, sh, 0), pltpu.roll(b, sh, 0)
        mr, mi = c_ref[2 * si], c_ref[2 * si + 1]
        a, b = a + mr * ar - mi * br, b + mr * br + mi * ar
    pr, pi = c_ref[6], c_ref[7]
    cr, ci = carry
    return a + pr * cr - pi * ci, b + pr * ci + pi * cr


def _s5_readout(xre_ref, xim_ref, ccre_ref, ccim_ref):
    return jnp.concatenate(
        [_dot(xre_ref[:, pl.ds(512 * k, 512)], ccre_ref[k]) - _dot(xim_ref[:, pl.ds(512 * k, 512)], ccim_ref[k])
         for k in range(S5_KB)], axis=1)


def _s5_fwd(proj, bbre, bbim, ccre, ccim, dvec, consts, glu_w, glu_b, *, cb, tb=256):
    t = proj.shape[0]
    tb = min(tb, t)
    nt = tb // SUB

    def body(u_ref, bbre_ref, bbim_ref, ccre_ref, ccim_ref, d_ref, c_ref, w_ref, b_ref, o_ref, xre_ref, xim_ref, car):
        @pl.when(pl.program_id(0) == 0)
        def _():
            car[...] = jnp.zeros_like(car)

        u = u_ref[...]
        for k in range(S5_KB):
            uk = u[:, 128 * k:128 * (k + 1)]
            xre_ref[:, pl.ds(512 * k, 512)] = _dot(uk, bbre_ref[k])
            xim_ref[:, pl.ds(512 * k, 512)] = _dot(uk, bbim_ref[k])

        def tile(r, carry):
            sl = pl.ds(pl.multiple_of(r * SUB, SUB), SUB)
            a, b = _s5_tile_scan(xre_ref[sl, :], xim_ref[sl, :], c_ref, carry, False)
            xre_ref[sl, :] = a
            xim_ref[sl, :] = b
            return a[SUB - 1:SUB, :], b[SUB - 1:SUB, :]

        cr, ci = lax.fori_loop(0, nt, tile, (car[0:1, :], car[1:2, :]))
        car[0:1, :] = cr
        car[1:2, :] = ci
        ys = _s5_readout(xre_ref, xim_ref, ccre_ref, ccim_ref) + d_ref[...] * u
        yg = _gelu(ys)
        o_ref[...] = (yg * _sigmoid(_dot(yg, w_ref[...]) + b_ref[...])).astype(o_ref.dtype)

    row = pl.BlockSpec((tb, GW), lambda i: (i, 0))
    xrow = pl.BlockSpec((tb, S5_N), lambda i: (i, 0))
    vec = _full((1, GW))
    return pl.pallas_call(
        body, name="s5_fwd", grid=(t // tb,),
        in_specs=[_cols(tb, GW, cb), _full((4, 128, 512)), _full((4, 128, 512)), _full((4, 512, 128)), _full((4, 512, 128)),
                  vec, _full((8, SUB, S5_N)), _full((GW, GW)), vec],
        out_specs=[row, xrow, xrow],
        out_shape=[jax.ShapeDtypeStruct((t, GW), MXU_DTYPE), jax.ShapeDtypeStruct((t, S5_N), F32), jax.ShapeDtypeStruct((t, S5_N), F32)],
        scratch_shapes=[pltpu.VMEM((SUB, S5_N), F32)], compiler_params=_params(("arbitrary",)))(
            proj, bbre, bbim, ccre, ccim, dvec, consts, glu_w, glu_b)


def _s5_bwd(dmixed, proj, xre, xim, bbre, bbim, ccre, ccim, dvec, consts, glu_w, glu_b, *, cb_dy, cb, tb=256):
    t = proj.shape[0]
    tb = min(tb, t)
    nb = t // tb
    nt = tb // SUB

    def body(dy_ref, u_ref, xre_ref, xim_ref, bbre_ref, bbim_ref, ccre_ref, ccim_ref, d_ref, c_ref, w_ref, b_ref,
             du_ref, dw_ref, db_ref, dd_ref, dccre_ref, dccim_ref, dbbre_ref, dbbim_ref, sre_ref, sim_ref, are, aim, car):
        @pl.when(pl.program_id(0) == 0)
        def _():
            car[...] = jnp.zeros_like(car)
            for r in (dw_ref, db_ref, dd_ref, dccre_ref, dccim_ref, dbbre_ref, dbbim_ref, sre_ref, sim_ref):
                r[...] = jnp.zeros_like(r)

        u, dy = u_ref[...], dy_ref[...]
        ys = _s5_readout(xre_ref, xim_ref, ccre_ref, ccim_ref) + d_ref[...] * u
        yg = _gelu(ys)
        sg = _sigmoid(_dot(yg, w_ref[...]) + b_ref[...])
        dz = dy * yg * sg * (1.0 - sg)
        dyg = dy * sg + _dot(dz, w_ref[...], NT)
        dw_ref[...] += _dot(yg, dz, TN)
        db_ref[...] += jnp.sum(dz, axis=0, keepdims=True)
        dys = dyg * _gelu_grad(ys)
        dd_ref[...] += jnp.sum(dys * u, axis=0, keepdims=True)
        for k in range(S5_KB):
            dk = dys[:, 128 * k:128 * (k + 1)]
            lanes = pl.ds(512 * k, 512)
            are[:, lanes] = _dot(dk, ccre_ref[k], NT)
            aim[:, lanes] = -_dot(dk, ccim_ref[k], NT)
            dccre_ref[k] += _dot(xre_ref[:, lanes], dk, TN)
            dccim_ref[k] -= _dot(xim_ref[:, lanes], dk, TN)

        def tile(j, carry):
            sl = pl.ds(pl.multiple_of((nt - 1 - j) * SUB, SUB), SUB)
            gr, gi = are[sl, :], aim[sl, :]
            a, b = _s5_tile_scan(gr, gi, c_ref, carry, True)
            are[sl, :] = a
            aim[sl, :] = b
            er, ei = a - gr, b - gi
            xr, xi = xre_ref[sl, :], xim_ref[sl, :]
            sre_ref[...] += xr * er + xi * ei
            sim_ref[...] += xr * ei - xi * er
            return a[0:1, :], b[0:1, :]

        cr, ci = lax.fori_loop(0, nt, tile, (car[0:1, :], car[1:2, :]))
        car[0:1, :] = cr
        car[1:2, :] = ci
        dus = []
        for k in range(S5_KB):
            uk = u[:, 128 * k:128 * (k + 1)]
            lanes = pl.ds(512 * k, 512)
            dbbre_ref[k] += _dot(uk, are[:, lanes], TN)
            dbbim_ref[k] += _dot(uk, aim[:, lanes], TN)
            dus.append(_dot(are[:, lanes], bbre_ref[k], NT) + _dot(aim[:, lanes], bbim_ref[k], NT))
        du_ref[...] = (d_ref[...] * dys + jnp.concatenate(dus, axis=1)).astype(du_ref.dtype)

    row = pl.BlockSpec((tb, GW), lambda i: (nb - 1 - i, 0))
    xrow = pl.BlockSpec((tb, S5_N), lambda i: (nb - 1 - i, 0))
    vec = _full((1, GW))
    bbs, ccs = _full((4, 128, 512)), _full((4, 512, 128))
    sds = jax.ShapeDtypeStruct
    return pl.pallas_call(
        body, name="s5_bwd", grid=(nb,),
        in_specs=[_cols_rev(tb, GW, cb_dy, nb), _cols_rev(tb, GW, cb, nb), xrow, xrow, bbs, bbs, ccs, ccs, vec,
                  _full((8, SUB, S5_N)), _full((GW, GW)), vec],
        out_specs=[row, _full((GW, GW)), vec, vec, ccs, ccs, bbs, bbs, _full((SUB, S5_N)), _full((SUB, S5_N))],
        out_shape=[sds((t, GW), MXU_DTYPE), sds((GW, GW), F32), sds((1, GW), F32), sds((1, GW), F32), sds((4, 512, 128), F32),
                   sds((4, 512, 128), F32), sds((4, 128, 512), F32), sds((4, 128, 512), F32), sds((SUB, S5_N), F32),
                   sds((SUB, S5_N), F32)],
        scratch_shapes=[pltpu.VMEM((tb, S5_N), F32), pltpu.VMEM((tb, S5_N), F32), pltpu.VMEM((SUB, S5_N), F32)],
        compiler_params=_params(("arbitrary",)))(dmixed, proj, xre, xim, bbre, bbim, ccre, ccim, dvec, consts, glu_w, glu_b)


def _s5_prepare(lam_re, lam_im, log_step, b_re, b_im, c_re, c_im):
    lr, li, bbr, bbi = _s5_discretize(lam_re, lam_im, log_step, b_re, b_im)
    return dict(bbre=_s5_blockdiag_in(bbr), bbim=_s5_blockdiag_in(bbi), ccre=_s5_blockdiag_out(c_re), ccim=_s5_blockdiag_out(c_im),
                cf=_s5_scan_consts(lr, li, False), cr=_s5_scan_consts(lr, li, True))


def _s5_param_grads(lam_re, lam_im, log_step, b_re, b_im, dbbre, dbbim, dccre, dccim, sre, sim):
    (lr, li, _, _), vjp = jax.vjp(_s5_discretize, lam_re, lam_im, log_step, b_re, b_im)
    sr, si = jnp.sum(sre, axis=0).reshape(S5_G, S5_P), jnp.sum(sim, axis=0).reshape(S5_G, S5_P)
    den = lr * lr + li * li
    glr, gli = (sr * lr - si * li) / den, (si * lr + sr * li) / den
    dlam_re, dlam_im, dlog_step, db_re, db_im = vjp((glr, gli, _s5_blockdiag_in_extract(dbbre), _s5_blockdiag_in_extract(dbbim)))
    return dlam_re, dlam_im, dlog_step, db_re, db_im, _s5_blockdiag_out_extract(dccre), _s5_blockdiag_out_extract(dccim)


HALO = 8
AB_CB = 4096 // 128
Q_SCALE = DN_HD ** -0.5


def _halo_prev(tb, width, cb):
    return pl.BlockSpec((HALO, width), lambda i: (jnp.maximum(i * (tb // HALO) - 1, 0), cb))


def _halo_next(tb, width, cb, nrows):
    last = nrows // HALO - 1
    return pl.BlockSpec((HALO, width), lambda i: (jnp.minimum((i + 1) * (tb // HALO), last), cb))


def _silu_parts(c):
    sg = _sigmoid(c)
    return c * sg, sg * (1.0 + c * (1.0 - sg))


def _softplus(x):
    return jnp.maximum(x, 0.0) + jnp.log(1.0 + jnp.exp(-jnp.abs(x)))


def _dn_conv(x_ref, halo_ref, w_ref, part, ext, first):
    tb = x_ref.shape[0]
    ext[pl.ds(0, HALO), :] = jnp.where(first, 0.0, halo_ref[...])
    ext[pl.ds(HALO, tb), :] = x_ref[...]
    c = None
    for j in range(DN_CONV):
        term = w_ref[pl.ds(j, 1), pl.ds(512 * part, 512)] * ext[pl.ds(HALO - (DN_CONV - 1) + j, tb), :]
        c = term if c is None else c + term
    return c


def _dn_gb(ab, alog, dtb):
    lane = lax.broadcasted_iota(jnp.int32, ab.shape, 1)
    pre = ab + dtb
    g = -jnp.exp(alog) * _softplus(pre)
    beta = _sigmoid(ab)
    return jnp.where(lane < DN_HEADS, g, jnp.where(lane < 2 * DN_HEADS, beta, 0.0)), pre, beta


def _dn_prep_fwd(proj, conv_w, alog, dtb, *, cbq, tb=512):
    t = proj.shape[0]
    tb = min(tb, t)

    def body(xq, xk, xv, hq, hk, hv, ab_ref, w_ref, alog_ref, dtb_ref, qn_ref, kn_ref, vs_ref, gb_ref, ext):
        first = pl.program_id(0) == 0
        for part, (x_ref, h_ref, o_ref) in enumerate(((xq, hq, qn_ref), (xk, hk, kn_ref), (xv, hv, vs_ref))):
            s, _ = _silu_parts(_dn_conv(x_ref, h_ref, w_ref, part, ext, first))
            if part < 2:
                scale = Q_SCALE if part == 0 else 1.0
                for h in range(DN_HEADS):
                    sh = s[:, DN_HD * h:DN_HD * (h + 1)]
                    rn = lax.rsqrt(jnp.sum(sh * sh, axis=-1, keepdims=True) + L2_EPS)
                    o_ref[:, pl.ds(DN_HD * h, DN_HD)] = sh * (rn * scale)
            else:
                o_ref[...] = s
        gb_ref[...] = _dn_gb(ab_ref[...], alog_ref[...], dtb_ref[...])[0]

    row = pl.BlockSpec((tb, GW), lambda i: (i, 0))
    small = pl.BlockSpec((tb, 128), lambda i: (i, 0))
    v128 = _full((1, 128))
    sds = jax.ShapeDtypeStruct
    return pl.pallas_call(
        body, name="dn_prep_fwd", grid=(t // tb,),
        in_specs=[_cols(tb, GW, cbq), _cols(tb, GW, cbq + 1), _cols(tb, GW, cbq + 2),
                  _halo_prev(tb, GW, cbq), _halo_prev(tb, GW, cbq + 1), _halo_prev(tb, GW, cbq + 2),
                  _cols(tb, 128, AB_CB), _full((DN_CONV, 3 * GW)), v128, v128],
        out_specs=[row, row, row, small],
        out_shape=[sds((t, GW), F32)] * 3 + [sds((t, 128), F32)],
        scratch_shapes=[pltpu.VMEM((tb + HALO, GW), F32)], compiler_params=_params(("parallel",)))(
            proj, proj, proj, proj, proj, proj, proj, conv_w, alog, dtb)


def _dn_prep_bwd_a(proj, conv_w, alog, dtb, dqn, dkn, dvs, dgb, *, cbq, tb=512):
    t = proj.shape[0]
    tb = min(tb, t)

    def body(xq, xk, xv, hq, hk, hv, ab_ref, w_ref, alog_ref, dtb_ref, dqn_ref, dkn_ref, dvs_ref, dgb_ref,
             dcq_ref, dck_ref, dcv_ref, dab_ref, dalog_ref, ddtb_ref, ext):
        first = pl.program_id(0) == 0

        @pl.when(first)
        def _():
            dalog_ref[...] = jnp.zeros_like(dalog_ref)
            ddtb_ref[...] = jnp.zeros_like(ddtb_ref)

        for part, (x_ref, h_ref, d_ref, o_ref) in enumerate(((xq, hq, dqn_ref, dcq_ref), (xk, hk, dkn_ref, dck_ref), (xv, hv, dvs_ref, dcv_ref))):
            s, ds_dc = _silu_parts(_dn_conv(x_ref, h_ref, w_ref, part, ext, first))
            d = d_ref[...]
            if part < 2:
                scale = Q_SCALE if part == 0 else 1.0
                for h in range(DN_HEADS):
                    lanes = slice(DN_HD * h, DN_HD * (h + 1))
                    sh, dh = s[:, lanes], d[:, lanes]
                    rn = lax.rsqrt(jnp.sum(sh * sh, axis=-1, keepdims=True) + L2_EPS)
                    dsh = scale * (rn * dh - sh * (rn * rn * rn) * jnp.sum(dh * sh, axis=-1, keepdims=True))
                    o_ref[:, pl.ds(DN_HD * h, DN_HD)] = dsh * ds_dc[:, lanes]
            else:
                o_ref[...] = d * ds_dc
        ab, dgb_v = ab_ref[...], dgb_ref[...]
        gb, pre, beta = _dn_gb(ab, alog_ref[...], dtb_ref[...])
        lane = lax.broadcasted_iota(jnp.int32, ab.shape, 1)
        is_g = lane < DN_HEADS
        da = jnp.where(is_g, dgb_v * (-jnp.exp(alog_ref[...])) * _sigmoid(pre), 0.0)
        db = jnp.where((lane >= DN_HEADS) & (lane < 2 * DN_HEADS), dgb_v * beta * (1.0 - beta), 0.0)
        dab_ref[...] = (da + db).astype(dab_ref.dtype)
        ddtb_ref[...] += jnp.sum(da, axis=0, keepdims=True)
        dalog_ref[...] += jnp.sum(jnp.where(is_g, dgb_v * gb, 0.0), axis=0, keepdims=True)

    row = pl.BlockSpec((tb, GW), lambda i: (i, 0))
    small = pl.BlockSpec((tb, 128), lambda i: (i, 0))
    v128 = _full((1, 128))
    sds = jax.ShapeDtypeStruct
    return pl.pallas_call(
        body, name="dn_prep_bwd_a", grid=(t // tb,),
        in_specs=[_cols(tb, GW, cbq), _cols(tb, GW, cbq + 1), _cols(tb, GW, cbq + 2),
                  _halo_prev(tb, GW, cbq), _halo_prev(tb, GW, cbq + 1), _halo_prev(tb, GW, cbq + 2),
                  _cols(tb, 128, AB_CB), _full((DN_CONV, 3 * GW)), v128, v128, row, row, row, small],
        out_specs=[row, row, row, small, v128, v128],
        out_shape=[sds((t, GW), F32)] * 3 + [sds((t, 128), MXU_DTYPE), sds((1, 128), F32), sds((1, 128), F32)],
        scratch_shapes=[pltpu.VMEM((tb + HALO, GW), F32)], compiler_params=_params(("arbitrary",)))(
            proj, proj, proj, proj, proj, proj, proj, conv_w, alog, dtb, dqn, dkn, dvs, dgb)


def _dn_prep_bwd_b(proj, conv_w, dcq, dck, dcv, *, cbq, tb=512):
    t = proj.shape[0]
    tb = min(tb, t)
    nb = t // tb

    def body(xq, xk, xv, hq, hk, hv, dq_in, dk_in, dv_in, nq, nk, nv, w_ref, dq_ref, dk_ref, dv_ref, dw_ref, ext):
        i = pl.program_id(0)
        first, last = i == 0, i == nb - 1

        @pl.when(first)
        def _():
            dw_ref[...] = jnp.zeros_like(dw_ref)

        for part, (x_ref, h_ref, d_ref, n_ref, o_ref) in enumerate(
                ((xq, hq, dq_in, nq, dq_ref), (xk, hk, dk_in, nk, dk_ref), (xv, hv, dv_in, nv, dv_ref))):
            lanes = pl.ds(512 * part, 512)
            d = d_ref[...]
            ext[pl.ds(0, HALO), :] = jnp.where(first, 0.0, h_ref[...])
            ext[pl.ds(HALO, tb), :] = x_ref[...]
            for j in range(DN_CONV):
                xs = ext[pl.ds(HALO - (DN_CONV - 1) + j, tb), :]
                dw_ref[pl.ds(j, 1), lanes] += jnp.sum(d * xs, axis=0, keepdims=True)
            ext[pl.ds(0, tb), :] = d
            ext[pl.ds(tb, HALO), :] = jnp.where(last, 0.0, n_ref[...])
            acc = None
            for j in range(DN_CONV):
                term = w_ref[pl.ds(j, 1), lanes] * ext[pl.ds(DN_CONV - 1 - j, tb), :]
                acc = term if acc is None else acc + term
            o_ref[...] = acc.astype(o_ref.dtype)

    row = pl.BlockSpec((tb, GW), lambda i: (i, 0))
    nxt = _halo_next(tb, GW, 0, t)
    sds = jax.ShapeDtypeStruct
    return pl.pallas_call(
        body, name="dn_prep_bwd_b", grid=(nb,),
        in_specs=[_cols(tb, GW, cbq), _cols(tb, GW, cbq + 1), _cols(tb, GW, cbq + 2),
                  _halo_prev(tb, GW, cbq), _halo_prev(tb, GW, cbq + 1), _halo_prev(tb, GW, cbq + 2),
                  row, row, row, nxt, nxt, nxt, _full((DN_CONV, 3 * GW))],
        out_specs=[row, row, row, _full((DN_CONV, 3 * GW))],
        out_shape=[sds((t, GW), MXU_DTYPE)] * 3 + [sds((DN_CONV, 3 * GW), F32)],
        scratch_shapes=[pltpu.VMEM((tb + HALO, GW), F32)], compiler_params=_params(("arbitrary",)))(
            proj, proj, proj, proj, proj, proj, dcq, dck, dcv, dcq, dck, dcv, conv_w)


CAT = DN_HEADS * DN_CHUNK
DN_LOCAL_CHUNKS = 4


def _iota_div(shape, axis, width):
    return jnp.right_shift(lax.broadcasted_iota(jnp.int32, shape, axis), width.bit_length() - 1)


def _dn_masks():
    r = lax.broadcasted_iota(jnp.int32, (DN_CHUNK, CAT), 0)
    c = jnp.bitwise_and(lax.broadcasted_iota(jnp.int32, (DN_CHUNK, CAT), 1), DN_CHUNK - 1)
    wide = _iota_div((CAT, GW), 0, DN_CHUNK) == _iota_div((CAT, GW), 1, DN_HD)
    square = _iota_div((CAT, CAT), 0, DN_CHUNK) == _iota_div((CAT, CAT), 1, DN_CHUNK)
    return dict(eye=r == c, tril=r >= c, strict=r > c, triu=r <= c, wide=wide, square=square)


def _stack4(x):
    return jnp.concatenate([x, x, x, x], axis=0)


def _diag_blocks(x, mask):
    return jnp.where(mask, _stack4(x), 0.0)


def _fold_blocks(x, mask):
    x = jnp.where(mask, x, 0.0)
    return x[0:64] + x[64:128] + x[128:192] + x[192:256]


def _expand(cols, base, width):
    head = _iota_div((cols.shape[0], DN_HEADS * width), 1, width)
    out = jnp.zeros((cols.shape[0], DN_HEADS * width), F32)
    for h in range(DN_HEADS):
        out = jnp.where(head == h, cols[:, base + h:base + h + 1], out)
    return out


def _head_sums(x, width):
    if width == DN_HD:
        return [jnp.sum(x[:, DN_HD * h:DN_HD * (h + 1)], axis=1, keepdims=True) for h in range(DN_HEADS)]
    head = _iota_div(x.shape, 1, width)
    return [jnp.sum(jnp.where(head == h, x, 0.0), axis=1, keepdims=True) for h in range(DN_HEADS)]


def _cumsum_rows(x):
    row = lax.broadcasted_iota(jnp.int32, x.shape, 0)
    for s in (1, 2, 4, 8, 16, 32):
        x = x + jnp.where(row >= s, pltpu.roll(x, s, 0), 0.0)
    return x


def _tri_inv(ns, square):
    shape = ns[0].shape
    col = jnp.bitwise_and(lax.broadcasted_iota(jnp.int32, shape, 1), DN_CHUNK - 1)
    eye = jnp.where(lax.broadcasted_iota(jnp.int32, shape, 0) == col, 1.0, 0.0)
    xs = [eye - n for n in ns]
    ps = [_dot(n, _diag_blocks(n, square), exact=True) for n in ns]
    for _ in range(4):
        pds = [_diag_blocks(p, square) for p in ps]
        xs = [x + _dot(x, pd, exact=True) for x, pd in zip(xs, pds)]
        ps = [_dot(p, pd, exact=True) for p, pd in zip(ps, pds)]
    return [x + _dot(x, _diag_blocks(p, square), exact=True) for x, p in zip(xs, ps)]


def _dn_local_math(qs, ks, vs, gbvs, m, tms=None):
    out = []
    for k, v, gbv in zip(ks, vs, gbvs):
        gc = _cumsum_rows(gbv)
        gc_cat, gc_wide = _expand(gc, 0, DN_CHUNK), _expand(gc, 0, DN_HD)
        gc_row = jnp.sum(jnp.where(m["eye"], gc_cat, 0.0), axis=0, keepdims=True)
        decay = jnp.where(m["tril"], jnp.exp(jnp.minimum(gc_cat - gc_row, 0.0)), 0.0)
        eg = jnp.exp(gc_wide)
        gl = _expand(gc[DN_CHUNK - 1:DN_CHUNK, :], 0, DN_HD)
        beta = _expand(gbv, DN_HEADS, DN_HD)
        kb = k * beta
        out.append(dict(decay=decay, eg=eg, eg_last=jnp.exp(gl), etail=jnp.exp(gl - gc_wide), beta=beta, kb=kb, vb=v * beta,
                        kbg=kb * eg, k_rows=_diag_blocks(k, m["wide"])))
    for d, q, k in zip(out, qs, ks):
        d.update(qg=q * d["eg"], ktail=k * d["etail"])
    for d in out:
        d["kk"] = _dot(d["kb"], d["k_rows"], NT)
    for d, q in zip(out, qs):
        d["qk"] = _dot(q, d["k_rows"], NT)
    if tms is None:
        tms = _tri_inv([jnp.where(m["strict"], d["kk"] * d["decay"], 0.0) for d in out], m["square"])
    for d, tm in zip(out, tms):
        d["tm"] = tm
    return out


def _dn_local_fwd(qn, kn, vs, gb):
    t = qn.shape[0]
    rows = DN_CHUNK * DN_LOCAL_CHUNKS

    def body(q_ref, k_ref, v_ref, gb_ref, u_ref, wm_ref, qg_ref, kt_ref, qkd_ref, tm_ref):
        masks = _dn_masks()
        chunks = [pl.ds(DN_CHUNK * n, DN_CHUNK) for n in range(DN_LOCAL_CHUNKS)]
        ms = _dn_local_math([q_ref[rs, :] for rs in chunks], [k_ref[rs, :] for rs in chunks], [v_ref[rs, :] for rs in chunks],
                            [gb_ref[rs, :] for rs in chunks], masks)
        us = [_dot(m["tm"], _diag_blocks(m["vb"], masks["wide"])) for m in ms]
        wms = [_dot(m["tm"], _diag_blocks(m["kbg"], masks["wide"])) for m in ms]
        for rs, m, u, wm in zip(chunks, ms, us, wms):
            u_ref[rs, :] = u
            wm_ref[rs, :] = wm.astype(wm_ref.dtype)
            qg_ref[rs, :] = m["qg"].astype(qg_ref.dtype)
            kt_ref[rs, :] = m["ktail"].astype(kt_ref.dtype)
            qkd = (m["qk"] * m["decay"]).astype(qkd_ref.dtype)
            for h in range(DN_HEADS):
                qkd_ref[rs, pl.ds(DN_HD * h, DN_CHUNK)] = qkd[:, DN_CHUNK * h:DN_CHUNK * (h + 1)]
            tm_ref[rs, :] = m["tm"]

    row = pl.BlockSpec((rows, GW), lambda i: (i, 0))
    sds = jax.ShapeDtypeStruct
    return pl.pallas_call(
        body, name="dn_local_fwd", grid=(t // rows,), in_specs=[row, row, row, pl.BlockSpec((rows, 128), lambda i: (i, 0))],
        out_specs=[row] * 5 + [pl.BlockSpec((rows, CAT), lambda i: (i, 0))],
        out_shape=[sds((t, GW), F32)] + [sds((t, GW), MXU_DTYPE)] * 4 + [sds((t, CAT), F32)],
        compiler_params=_params(("parallel",)))(qn, kn, vs, gb)


def _dn_eg_last(gbv, h):
    return jnp.exp(jnp.sum(gbv[:, h:h + 1], axis=0, keepdims=True))


def _dn_seq_fwd(u, wm, qg, ktail, qkd, gb, proj, norm_g, *, cb_gate):
    t = u.shape[0]
    nc = t // DN_CHUNK

    def body(u_ref, wm_ref, qg_ref, kt_ref, qkd_ref, gb_ref, gate_ref, ng_ref, o_ref, raw_ref, vn_ref, st_ref, s_ref):
        @pl.when(pl.program_id(0) == 0)
        def _():
            s_ref[...] = jnp.zeros_like(s_ref)

        gbv = gb_ref[...]
        heads = range(DN_HEADS)
        lanes = [pl.ds(DN_HD * h, DN_HD) for h in heads]
        ss = [s_ref[h] for h in heads]
        for h in heads:
            st_ref[0, h] = ss[h]
        ws = [_dot(wm_ref[:, lanes[h]], ss[h]) for h in heads]
        qs = [_dot(qg_ref[:, lanes[h]], ss[h]) for h in heads]
        v_news = [u_ref[:, lanes[h]] - ws[h] for h in heads]
        ks = [_dot(kt_ref[:, lanes[h]], v_news[h], TN) for h in heads]
        os_ = [qs[h] + _dot(qkd_ref[:, pl.ds(DN_HD * h, DN_CHUNK)], v_news[h]) for h in heads]
        for h in heads:
            s_ref[h] = ss[h] * _dn_eg_last(gbv, h) + ks[h]
            vn_ref[:, lanes[h]] = v_news[h].astype(vn_ref.dtype)
            o = os_[h]
            raw_ref[:, lanes[h]] = o
            r = lax.rsqrt(jnp.mean(o * o, axis=-1, keepdims=True) + RMS_EPS)
            gt = gate_ref[:, lanes[h]]
            o_ref[:, lanes[h]] = (o * r * ng_ref[...] * (gt * _sigmoid(gt))).astype(o_ref.dtype)

    row = pl.BlockSpec((DN_CHUNK, GW), lambda i: (i, 0))
    sds = jax.ShapeDtypeStruct
    return pl.pallas_call(
        body, name="dn_seq_fwd", grid=(nc,),
        in_specs=[row] * 5 + [pl.BlockSpec((DN_CHUNK, 128), lambda i: (i, 0)), _cols(DN_CHUNK, GW, cb_gate), _full((1, DN_HD))],
        out_specs=[row, row, row, pl.BlockSpec((1, DN_HEADS, DN_HD, DN_HD), lambda i: (i, 0, 0, 0))],
        out_shape=[sds((t, GW), MXU_DTYPE), sds((t, GW), F32), sds((t, GW), MXU_DTYPE), sds((nc, DN_HEADS, DN_HD, DN_HD), F32)],
        scratch_shapes=[pltpu.VMEM((DN_HEADS, DN_HD, DN_HD), F32)], compiler_params=_params(("arbitrary",)))(
            u, wm, qg, ktail, qkd, gb, proj, norm_g)


def _dn_seq_bwd(dmixed, raw, proj, norm_g, gb, wm, qg, ktail, qkd, v_new, states, *, cb_dy, cb_gate):
    t = raw.shape[0]
    nc = t // DN_CHUNK

    def body(dy_ref, raw_ref, gate_ref, ng_ref, gb_ref, wm_ref, qg_ref, kt_ref, qkd_ref, vn_ref, st_ref,
             dgate_ref, dng_ref, do_ref, dvn_ref, dwm_ref, dqg_ref, dkt_ref, degl_ref, ds_ref):
        @pl.when(pl.program_id(0) == 0)
        def _():
            ds_ref[...] = jnp.zeros_like(ds_ref)
            dng_ref[...] = jnp.zeros_like(dng_ref)

        gbv = gb_ref[...]
        lane = lax.broadcasted_iota(jnp.int32, (1, 128), 1)
        heads = range(DN_HEADS)
        lanes = [pl.ds(DN_HD * h, DN_HD) for h in heads]
        ss, dss, dos = [st_ref[0, h] for h in heads], [ds_ref[h] for h in heads], []
        for h in heads:
            o, gt, dy, ng = raw_ref[:, lanes[h]], gate_ref[:, lanes[h]], dy_ref[:, lanes[h]], ng_ref[...]
            r = lax.rsqrt(jnp.mean(o * o, axis=-1, keepdims=True) + RMS_EPS)
            sil, dsil = _silu_parts(gt)
            d_on = dy * sil
            dgate_ref[:, lanes[h]] = (dy * (o * r * ng) * dsil).astype(dgate_ref.dtype)
            dng_ref[...] += jnp.sum(d_on * o * r, axis=0, keepdims=True)
            w = d_on * ng
            dos.append(r * w - o * (r * r * r) * jnp.mean(w * o, axis=-1, keepdims=True))
        from_next = [_dot(kt_ref[:, lanes[h]], dss[h]) for h in heads]
        d_vnews = [_dot(qkd_ref[:, pl.ds(DN_HD * h, DN_CHUNK)], dos[h], TN) + from_next[h] for h in heads]
        q_terms = [_dot(qg_ref[:, lanes[h]], dos[h], TN) for h in heads]
        w_terms = [_dot(wm_ref[:, lanes[h]], d_vnews[h], TN) for h in heads]
        degl = jnp.zeros((1, 128), F32)
        for h in heads:
            ds_ref[h] = q_terms[h] + _dn_eg_last(gbv, h) * dss[h] - w_terms[h]
            do_ref[:, lanes[h]] = dos[h].astype(do_ref.dtype)
            dvn_ref[:, lanes[h]] = d_vnews[h].astype(dvn_ref.dtype)
            d_eglast = jnp.sum(jnp.sum(ss[h] * dss[h], axis=1, keepdims=True), axis=0, keepdims=True)
            degl = degl + jnp.where(lane == h, d_eglast, 0.0)
        degl_ref[0] = degl
        for h in heads:
            dwm_ref[:, lanes[h]] = (-_dot(d_vnews[h], ss[h], NT)).astype(dwm_ref.dtype)
        for h in heads:
            dqg_ref[:, lanes[h]] = _dot(dos[h], ss[h], NT)
        for h in heads:
            dkt_ref[:, lanes[h]] = _dot(vn_ref[:, lanes[h]], dss[h], NT)

    row = pl.BlockSpec((DN_CHUNK, GW), lambda i: (nc - 1 - i, 0))
    small = pl.BlockSpec((DN_CHUNK, 128), lambda i: (nc - 1 - i, 0))
    sds = jax.ShapeDtypeStruct
    return pl.pallas_call(
        body, name="dn_seq_bwd", grid=(nc,),
        in_specs=[_cols_rev(DN_CHUNK, GW, cb_dy, nc), row, _cols_rev(DN_CHUNK, GW, cb_gate, nc), _full((1, DN_HD)), small,
                  row, row, row, row, row, pl.BlockSpec((1, DN_HEADS, DN_HD, DN_HD), lambda i: (nc - 1 - i, 0, 0, 0))],
        out_specs=[row, _full((1, DN_HD)), row, row, row, row, row, pl.BlockSpec((1, 1, 128), lambda i: (nc - 1 - i, 0, 0))],
        out_shape=[sds((t, GW), MXU_DTYPE), sds((1, DN_HD), F32), sds((t, GW), MXU_DTYPE), sds((t, GW), MXU_DTYPE),
                   sds((t, GW), MXU_DTYPE), sds((t, GW), F32), sds((t, GW), F32), sds((nc, 1, 128), F32)],
        scratch_shapes=[pltpu.VMEM((DN_HEADS, DN_HD, DN_HD), F32)], compiler_params=_params(("arbitrary",)))(
            dmixed, raw, proj, norm_g, gb, wm, qg, ktail, qkd, v_new, states)


def _dn_local_bwd(qn, kn, vs, gb, tm, v_new, do, d_vnew, d_wm, d_qg, d_ktail, d_eglast):
    t = qn.shape[0]
    rows = DN_CHUNK * DN_LOCAL_CHUNKS

    def body(q_ref, k_ref, v_ref, gb_ref, tm_ref, vn_ref, do_ref, dvn_ref, dwm_ref, dqg_ref, dkt_ref, degl_ref,
             dq_ref, dk_ref, dv_ref, dgb_ref):
        masks = _dn_masks()
        wide, square = masks["wide"], masks["square"]
        lane = lax.broadcasted_iota(jnp.int32, (DN_CHUNK, 128), 1)
        last_row = lax.broadcasted_iota(jnp.int32, (DN_CHUNK, 1), 0) == DN_CHUNK - 1
        chunks = [pl.ds(DN_CHUNK * n, DN_CHUNK) for n in range(DN_LOCAL_CHUNKS)]
        ms = _dn_local_math([q_ref[rs, :] for rs in chunks], [k_ref[rs, :] for rs in chunks], [v_ref[rs, :] for rs in chunks],
                            [gb_ref[rs, :] for rs in chunks], masks, tms=[tm_ref[rs, :] for rs in chunks])

        def work(n, rs, m):
            q, k, v, tm = q_ref[rs, :], k_ref[rs, :], v_ref[rs, :], m["tm"]
            decay, eg, k_rows = m["decay"], m["eg"], m["k_rows"]
            d_vnew, d_wm, d_qg, d_ktail = dvn_ref[rs, :], dwm_ref[rs, :], dqg_ref[rs, :], dkt_ref[rs, :]
            deglv = degl_ref[n]
            dq = d_qg * eg
            dk = d_ktail * m["etail"]
            tails = _head_sums(d_ktail * m["ktail"], DN_HD)
            dgcs = _head_sums(d_qg * m["qg"], DN_HD)
            d_qkd = jnp.where(masks["tril"], _dot(do_ref[rs, :], _diag_blocks(vn_ref[rs, :], wide), NT), 0.0)
            d_tm = _dot(d_vnew, _diag_blocks(m["vb"], wide), NT) + _dot(d_wm, _diag_blocks(m["kbg"], wide), NT)
            d_vb = _fold_blocks(_dot(tm, d_vnew, TN), wide)
            d_kbg = _fold_blocks(_dot(tm, d_wm, TN), wide)
            yield
            dqk_dec = d_qkd * decay
            dq = dq + _dot(dqk_dec, k_rows)
            dk = dk + _fold_blocks(_dot(dqk_dec, q, TN), wide)
            ddecay = d_qkd * m["qk"]
            d_kb = d_kbg * eg
            kbgs = _head_sums(d_kbg * m["kbg"], DN_HD)
            x = _fold_blocks(_dot(tm, d_tm, TN, exact=True), square)
            yield
            d_n = jnp.where(masks["strict"], -_dot(x, _diag_blocks(tm, square), NT, exact=True), 0.0)
            yield
            d_kk = d_n * decay
            d_kb = d_kb + _dot(d_kk, k_rows)
            dk = dk + _fold_blocks(_dot(d_kk, m["kb"], TN), wide)
            yield
            ddecay = ddecay + d_n * m["kk"]
            dk = dk + d_kb * m["beta"]
            dbetas = [a + b for a, b in zip(_head_sums(d_kb * k, DN_HD), _head_sums(d_vb * v, DN_HD))]
            dv_ref[rs, :] = d_vb * m["beta"]
            dq_ref[rs, :] = dq
            dk_ref[rs, :] = dk
            dd = ddecay * decay
            row_sums = _head_sums(dd, DN_CHUNK)
            dgc_cols = jnp.zeros((DN_CHUNK, 128), F32)
            for h in range(DN_HEADS):
                dgl = jnp.sum(tails[h], axis=0, keepdims=True) + deglv[:, h:h + 1] * m["eg_last"][:, DN_HD * h:DN_HD * h + 1]
                dgc_cols = jnp.where(lane == h, dgcs[h] - tails[h] + kbgs[h] + row_sums[h] + jnp.where(last_row, dgl, 0.0), dgc_cols)
            dgc_row = (jnp.sum(jnp.where(masks["eye"], _expand(dgc_cols, 0, DN_CHUNK), 0.0), axis=0, keepdims=True)
                       - jnp.sum(dd, axis=0, keepdims=True))
            dgs = _head_sums(jnp.where(masks["triu"], dgc_row, 0.0), DN_CHUNK)
            dgb = jnp.zeros((DN_CHUNK, 128), F32)
            for h in range(DN_HEADS):
                dgb = jnp.where(lane == h, dgs[h], jnp.where(lane == DN_HEADS + h, dbetas[h], dgb))
            dgb_ref[rs, :] = dgb

        running = [work(n, rs, m) for n, (rs, m) in enumerate(zip(chunks, ms))]
        while running:
            running = [g for g in running if next(g, "done") != "done"]

    row = pl.BlockSpec((rows, GW), lambda i: (i, 0))
    small = pl.BlockSpec((rows, 128), lambda i: (i, 0))
    sds = jax.ShapeDtypeStruct
    return pl.pallas_call(
        body, name="dn_local_bwd", grid=(t // rows,),
        in_specs=[row, row, row, small, pl.BlockSpec((rows, CAT), lambda i: (i, 0))] + [row] * 6
        + [pl.BlockSpec((DN_LOCAL_CHUNKS, 1, 128), lambda i: (i, 0, 0))],
        out_specs=[row, row, row, small], out_shape=[sds((t, GW), F32)] * 3 + [sds((t, 128), F32)],
        compiler_params=_params(("parallel",)))(qn, kn, vs, gb, tm, v_new, do, d_vnew, d_wm, d_qg, d_ktail, d_eglast)


ANY = pl.BlockSpec(memory_space=pl.ANY)
PAIR_SPLIT = 4


def _place():
    x, y, c = lax.axis_index("x"), lax.axis_index("y"), lax.axis_index("c")
    chips = [(1 - x, y), (x, 1 - y), (1 - x, 1 - y)]
    return x, y, c, chips


def _remote(src, dst, send_sem, recv_sem, to):
    return pltpu.make_async_remote_copy(src_ref=src, dst_ref=dst, send_sem=send_sem, recv_sem=recv_sem, device_id=to,
                                        device_id_type=MESH)


def _carry_allgather(arrs):
    n = len(arrs)

    def sends(ins, outs, send_sems, recv_sems):
        x, y, c, chips = _place()
        me = 2 * x + y
        out = []
        for a in range(n):
            half = ins[a].shape[0] // 2
            mine = pl.ds(c * half, half)
            out += [_remote(ins[a].at[mine], outs[a].at[me, mine], send_sems.at[6 * a + k], recv_sems.at[6 * a + k], (*chip, c))
                    for k, chip in enumerate(chips)]
        return out

    def start(*parts):
        for s in sends(*parts):
            s.start()

    def finish(ins, outs, send_sems, recv_sems):
        x, y, c, chips = _place()
        sibling = (x, y, 1 - c)
        fwds = []
        for a in range(n):
            half = ins[a].shape[0] // 2
            mine = pl.ds(c * half, half)
            for k, (cx, cy) in enumerate(chips):
                got = outs[a].at[2 * cx + cy, mine]
                _remote(got, got, send_sems.at[6 * a + k], recv_sems.at[6 * a + k], (cx, cy, c)).wait_recv()
                f = _remote(got, got, send_sems.at[6 * a + 3 + k], recv_sems.at[6 * a + 3 + k], sibling)
                f.start()
                fwds.append(f)
        for a in range(n):
            half = ins[a].shape[0] // 2
            other = pl.ds((1 - c) * half, half)
            for k, (cx, cy) in enumerate(chips):
                got = outs[a].at[2 * cx + cy, other]
                _remote(got, got, send_sems.at[6 * a + 3 + k], recv_sems.at[6 * a + 3 + k], sibling).wait_recv()
        for s in sends(ins, outs, send_sems, recv_sems) + fwds:
            s.wait_send()

    return _Carry(arrs, [jax.ShapeDtypeStruct((4,) + a.shape, a.dtype) for a in arrs], 6 * n, start, finish)


def _carry_pair_exchange(gbs):
    n = len(gbs)

    def copies(ins, outs, send_sems, recv_sems):
        x, y, c, _ = _place()
        out = []
        for a in range(n):
            half = ins[a].shape[1] // 2
            piece = half // PAIR_SPLIT
            out += [_remote(ins[a].at[:, pl.ds((1 - c) * half + r * piece, piece)], outs[a].at[:, pl.ds(r * piece, piece)],
                            send_sems.at[PAIR_SPLIT * a + r], recv_sems.at[PAIR_SPLIT * a + r], (x, y, 1 - c))
                    for r in range(PAIR_SPLIT)]
        return out

    def start(*parts):
        for s in copies(*parts):
            s.start()

    def finish(*parts):
        for s in copies(*parts):
            s.wait()

    return _Carry(gbs, [jax.ShapeDtypeStruct((4, g.shape[1] // 2, g.shape[2]), g.dtype) for g in gbs], PAIR_SPLIT * n, start, finish)


def _carry_chip_exchange(ps):
    n = len(ps)

    def copies(ins, outs, send_sems, recv_sems):
        x, y, c, chips = _place()
        return [_remote(ins[a].at[2 * cx + cy], outs[a].at[k], send_sems.at[3 * a + k], recv_sems.at[3 * a + k], (cx, cy, c))
                for a in range(n) for k, (cx, cy) in enumerate(chips)]

    def start(*parts):
        for s in copies(*parts):
            s.start()

    def finish(*parts):
        for s in copies(*parts):
            s.wait()

    return _Carry(ps, [jax.ShapeDtypeStruct((3,) + p.shape[1:], p.dtype) for p in ps], 3 * n, start, finish)


def _pair_join(bufs, *, name):
    n = len(bufs)

    def body(*refs):
        outs = refs[n:2 * n]
        send_sems, recv_sems = refs[2 * n:]
        x, y, c, _ = _place()
        work = []
        for a in range(n):
            s = _remote(outs[a].at[c], outs[a].at[c], send_sems.at[a], recv_sems.at[a], (x, y, 1 - c))
            s.start()
            work.append(s)
        for s in work:
            s.wait()

    return pl.pallas_call(
        body, name=name, in_specs=[ANY] * n, out_specs=[ANY] * n,
        out_shape=[jax.ShapeDtypeStruct(b.shape, b.dtype) for b in bufs], input_output_aliases={a: a for a in range(n)},
        scratch_shapes=[pltpu.SemaphoreType.DMA((n,)), pltpu.SemaphoreType.DMA((n,))])(*bufs)


def _pair_sum(gb, got, place, *, name, block_bytes=1 << 20):
    _, r, cols = gb.shape
    half = r // 2
    tr = _row_tile(half, cols, block_bytes)

    def body(place_ref, g_ref, got_ref, o_ref):
        o_ref[...] = (g_ref[...] + got_ref[...]).astype(o_ref.dtype)

    blk = pl.BlockSpec((None, tr, cols), lambda j, i, p: (j, i, 0))
    grid_spec = pltpu.PrefetchScalarGridSpec(
        num_scalar_prefetch=1, grid=(4, half // tr),
        in_specs=[pl.BlockSpec((None, None, tr, cols), lambda j, i, p: (j, p[0], i, 0)), blk], out_specs=blk)
    return pl.pallas_call(body, name=name, grid_spec=grid_spec, out_shape=jax.ShapeDtypeStruct((4, half, cols), MXU_DTYPE),
                          compiler_params=_params(("parallel", "parallel")))(place, gb.reshape(4, 2, half, cols), got)


def _chip_sum(gb, got_pair, got, place, *, name, block_bytes=1 << 20):
    _, r, cols = gb.shape
    h = r // 2
    tr = _row_tile(h, cols, block_bytes)

    def body(place_ref, g_ref, gp_ref, g0, g1, g2, o_ref):
        o_ref[...] = (g_ref[...] + gp_ref[...]) + g0[...].astype(F32) + g1[...].astype(F32) + g2[...].astype(F32)

    def got_spec(k):
        return pl.BlockSpec((None, tr, cols), functools.partial(lambda i, pr, k: (k, i, 0), k=k))

    grid_spec = pltpu.PrefetchScalarGridSpec(
        num_scalar_prefetch=1, grid=(h // tr,),
        in_specs=[pl.BlockSpec((None, None, tr, cols), lambda i, pr: (pr[1], pr[0], i, 0)),
                  pl.BlockSpec((None, tr, cols), lambda i, pr: (pr[1], i, 0)), got_spec(0), got_spec(1), got_spec(2)],
        out_specs=pl.BlockSpec((None, tr, cols), lambda i, pr: (pr[0], i, 0)))
    return pl.pallas_call(body, name=name, grid_spec=grid_spec, out_shape=jax.ShapeDtypeStruct((2, h, cols), F32),
                          compiler_params=_params(("parallel",)))(place, gb.reshape(4, 2, h, cols), got_pair, got, got, got)


def _carry_allgather_all(v):
    def copies(ins, outs, send_sems, recv_sems):
        (v_ref,), (out_ref,) = ins, outs
        x, y, c, chips = _place()
        me, sibling = (x, y, c), (x, y, 1 - c)

        def rows(px, py, pc):
            return out_ref.at[4 * px + 2 * py + pc]

        def copy(k, block, to, src=None):
            return _remote(rows(*block) if src is None else src, rows(*block), send_sems.at[k], recv_sems.at[k], to)

        first = [copy(0, me, sibling, src=v_ref)] + [copy(1 + j, me, (*chip, c), src=v_ref) for j, chip in enumerate(chips)]
        passed = [copy(4 + j, (*chip, c), sibling) for j, chip in enumerate(chips)]
        arrived = [copy(1 + j, (*chip, c), me) for j, chip in enumerate(chips)]
        from_sibling = [copy(0, sibling, me)] + [copy(4 + j, (*chip, 1 - c), me) for j, chip in enumerate(chips)]
        return first, passed, arrived, from_sibling

    def start(*parts):
        for cp in copies(*parts)[0]:
            cp.start()

    def finish(*parts):
        first, passed, arrived, from_sibling = copies(*parts)
        for a, p in zip(arrived, passed):
            a.wait_recv()
            p.start()
        for cp in from_sibling:
            cp.wait_recv()
        for cp in first + passed:
            cp.wait_send()

    return _Carry([v], [jax.ShapeDtypeStruct((8,) + v.shape, v.dtype)], 7, start, finish)


def _allgather_all(v, name):
    return _run_carry(_carry_allgather_all(v), name)[0]


def _row_tile(rows, cols, limit_bytes):
    for d in range(1, rows + 1):
        if rows % d == 0 and (rows // d) % 8 == 0 and (rows // d) * cols * 4 <= limit_bytes:
            return rows // d
    return rows


def _adamw_math(w, gv, m, v):
    nm = ADAM_B1 * m + (1.0 - ADAM_B1) * gv
    nv = ADAM_B2 * v + (1.0 - ADAM_B2) * (gv * gv)
    m_hat = nm / (1.0 - ADAM_B1 ** ADAM_STEP)
    v_hat = nv / (1.0 - ADAM_B2 ** ADAM_STEP)
    return -ADAM_LR * (m_hat / (jnp.sqrt(v_hat) + ADAM_EPS) + ADAM_WD * w), nm, nv


def _adamw_layers(w, g0, g1, m, v, *, name, block_bytes=1 << 20):
    _, rows, cols = w.shape
    tr = _row_tile(rows, cols, block_bytes)

    def body(w_ref, g0_ref, g1_ref, m_ref, v_ref, g_ref, d_ref, nm_ref, nv_ref):
        gv = jnp.where(pl.program_id(0) == 0, g0_ref[...], g1_ref[...])
        g_ref[...] = gv
        d_ref[...], nm_ref[...], nv_ref[...] = _adamw_math(w_ref[...], gv, m_ref[...], v_ref[...])

    both = pl.BlockSpec((None, tr, cols), lambda l, i: (l, i, 0))
    specs = [both, pl.BlockSpec((tr, cols), lambda l, i: (i * (1 - l), 0)), pl.BlockSpec((tr, cols), lambda l, i: (i * l, 0)), both, both]
    return pl.pallas_call(body, name=name, grid=(2, rows // tr), in_specs=specs, out_specs=[both] * 4,
                          out_shape=[jax.ShapeDtypeStruct(w.shape, F32)] * 4, compiler_params=_params(("arbitrary", "arbitrary")))(
                              w, g0, g1, m, v)


def _adamw(w, g, m, v, *, name, block_bytes=1 << 20):
    rows, cols = w.shape
    tr = _row_tile(rows, cols, block_bytes)

    def body(w_ref, g_ref, m_ref, v_ref, d_ref, nm_ref, nv_ref):
        d_ref[...], nm_ref[...], nv_ref[...] = _adamw_math(w_ref[...], g_ref[...], m_ref[...], v_ref[...])

    spec = pl.BlockSpec((tr, cols), lambda i: (i, 0))
    return pl.pallas_call(body, name=name, grid=(rows // tr,), in_specs=[spec] * 4, out_specs=[spec] * 3,
                          out_shape=[jax.ShapeDtypeStruct((rows, cols), F32)] * 3, compiler_params=_params(("parallel",)))(w, g, m, v)


WEIGHTS = ['w_in', 's5_lambda_re', 's5_lambda_im', 's5_log_step', 's5_b_re', 's5_b_im', 's5_c_re', 's5_c_im', 's5_d', 's5_glu_w',
           's5_glu_b', 'sgu_norm_g', 'sgu_norm_b', 'sgu_w', 'sgu_b', 'pool_w', 'pool_scale', 'dn_conv_w', 'dn_a_log', 'dn_dt_bias',
           'dn_norm_g', 'w_out', 'ln1_g', 'ln1_b', 'w_up', 'w_down', 'ln2_g', 'ln2_b']
BIG = ['w_in', 's5_glu_w', 'w_out', 'w_up', 'w_down']
SMALL = [n for n in WEIGHTS if n not in BIG]
CB_S5, CB_SGU_U, CB_SGU_V, CB_POOL, CB_DN_Q, CB_DN_GATE = 0, 1, 2, 3, 4, 7
KT = 2048


def _pad_lanes(v, width=128):
    return jnp.zeros((1, width), F32).at[0, :v.shape[0]].set(v)


def _layer_consts(p):
    c = _s5_prepare(p['s5_lambda_re'], p['s5_lambda_im'], p['s5_log_step'], p['s5_b_re'], p['s5_b_im'], p['s5_c_re'], p['s5_c_im'])
    tril = jnp.tril(jnp.ones((SGU_CHUNK, SGU_CHUNK), bool))
    wm = jnp.where(tril, p['sgu_w'], 0.0)
    c.update(s5_d=p['s5_d'].reshape(1, GW), glu_b=p['s5_glu_b'].reshape(1, GW), sgu_ng=p['sgu_norm_g'].reshape(1, GW),
             sgu_nb=p['sgu_norm_b'].reshape(1, GW), sgu_w=wm, sgu_wt=jnp.swapaxes(wm, 1, 2),
             sgu_bias=jnp.repeat(p['sgu_b'].T, SGU_HD, axis=1), pool_w=p['pool_w'], pool_scale=p['pool_scale'].reshape(1, GW),
             conv_w=p['dn_conv_w'], alog=_pad_lanes(p['dn_a_log']), dtb=_pad_lanes(p['dn_dt_bias']), dn_ng=p['dn_norm_g'].reshape(1, DN_HD),
             ln1_g=p['ln1_g'].reshape(1, D_MODEL), ln1_b=p['ln1_b'].reshape(1, D_MODEL), ln2_g=p['ln2_g'].reshape(1, D_MODEL),
             ln2_b=p['ln2_b'].reshape(1, D_MODEL))
    return c


def _layer_fwd(xin, xin16, w, c, i, carries):
    tag = str(i)
    residual = lambda r, e: (r + ALPHA * e,)

    def mm(a, b_name, *, name, **kw):
        if name not in carries:
            return _matmul(a, w[b_name], name=name + tag, **kw)
        carry, done = carries[name]
        outs, extra = _matmul(a, w[b_name], name=name + tag, carry=carry, **kw)
        done(extra)
        return outs

    (proj,) = mm(xin16, 'w_in', mode="nn", name="proj", tn=1408, tk=KT)
    s5, xre, xim = _s5_fwd(proj, c['bbre'], c['bbim'], c['ccre'], c['ccim'], c['s5_d'], c['cf'], w['s5_glu_w'], c['glu_b'], cb=CB_S5)
    sgu = _sgu_fwd(proj, c['sgu_ng'], c['sgu_nb'], c['sgu_w'], c['sgu_bias'], cbu=CB_SGU_U, cbv=CB_SGU_V)
    pool, pooled = _pool_fwd(proj, c['pool_w'], c['pool_scale'], cb=CB_POOL)
    qn, kn, vs, gb = _dn_prep_fwd(proj, c['conv_w'], c['alog'], c['dtb'], cbq=CB_DN_Q)
    u, wm, qg, ktail, qkd, tm = _dn_local_fwd(qn, kn, vs, gb)
    dn, raw, v_new, states = _dn_seq_fwd(u, wm, qg, ktail, qkd, gb, proj, c['dn_ng'], cb_gate=CB_DN_GATE)
    mixed = jnp.concatenate([s5, sgu, pool, dn], axis=1)
    (h1,) = mm(mixed, 'w_out', mode="nn", name="mix_out", e=xin, epi=residual, tk=KT)
    x1, x1_16 = _ln_fwd(h1, c['ln1_g'], c['ln1_b'], name="ln1_" + tag)
    (hidden,) = mm(x1_16, 'w_up', mode="nn", name="mlp_up", epi=lambda r, e: (_relu2(r),), out_dtypes=(MXU_DTYPE,), tm=2048, tk=KT,
                   b_blocked=True)
    (h2,) = mm(hidden, 'w_down', mode="nn", name="mlp_down", e=x1, epi=residual, tk=KT)
    x2, x2_16 = _ln_fwd(h2, c['ln2_g'], c['ln2_b'], name="ln2_" + tag)
    saved = dict(xin16=xin16, proj=proj, xre=xre, xim=xim, pooled=pooled, qn=qn, kn=kn, vs=vs, gb=gb, raw=raw, states=states,
                 wm=wm, qg=qg, ktail=ktail, qkd=qkd, tm=tm, v_new=v_new, mixed=mixed, h1=h1, x1_16=x1_16, hidden=hidden, h2=h2)
    return x2, x2_16, saved


def _by_rows(g):
    return g.reshape(4, g.shape[0] // 4, g.shape[1])


def _by_cols(g):
    return jnp.transpose(g.reshape(g.shape[0], 4, g.shape[1] // 4), (1, 0, 2))


def _layer_bwd(dx2, s, w, c, p, i, place, small_ride=None):
    tag = str(i)
    residual = lambda r, e: (r + ALPHA * e,)
    reduced = {}

    def pair_sums(blocks, got, names):
        return [(g, r, _pair_sum(g, r, place, name="pair_sum_" + nm + tag)) for g, r, nm in zip(blocks, got, names)]

    def pair(blocks, names):
        return pair_sums(blocks, _run_carry(_carry_pair_exchange(blocks), "grad_pair_exchange_" + names[0] + tag), names)

    def riding(ps, names, a, b, **kw):
        outs, got = _matmul(a, b, carry=_carry_chip_exchange([p16 for _, _, p16 in ps]), **kw)
        bufs = _pair_join([_chip_sum(g, r, t, place, name="chip_sum_" + nm + tag) for (g, r, _), t, nm in zip(ps, got, names)],
                          name="grad_pair_join_" + names[0] + tag)
        for nm, buf in zip(names, bufs):
            reduced[nm] = buf.reshape(-1, buf.shape[-1])
        return outs

    dh2, dh2_16, dln2g, dln2b = _ln_bwd(dx2, s['h2'], c['ln2_g'], name="ln2_bwd" + tag)
    (dw_down,) = _matmul(s['hidden'], dh2_16, mode="tn", name="dw_down" + tag, tk=KT)
    (da,), got = _matmul(dh2_16, w['w_down'], mode="nt", name="d_hidden" + tag, e=s['hidden'],
                         epi=lambda r, e: (r * (2.0 * jnp.sqrt(e.astype(F32))),), out_dtypes=(MXU_DTYPE,), tm=2048, tk=KT,
                         carry=_carry_pair_exchange([_by_rows(dw_down)]))
    p_down = pair_sums([_by_rows(dw_down)], got, ['w_down'])
    (dw_up,) = riding(p_down, ['w_down'], s['x1_16'], da, mode="tn", name="dw_up" + tag, tk=KT, out_blocked=True)
    p_up = pair([dw_up], ['w_up'])
    (dx1,) = riding(p_up, ['w_up'], da, w['w_up'], mode="nt", name="dx_mlp" + tag, e=dh2, epi=residual, tk=KT, b_blocked=True)
    dh1, dh1_16, dln1g, dln1b = _ln_bwd(dx1, s['h1'], c['ln1_g'], name="ln1_bwd" + tag)
    (dw_out,) = _matmul(s['mixed'], dh1_16, mode="tn", name="dw_out" + tag, tk=KT)
    p_out = pair([_by_rows(dw_out)], ['w_out'])
    (dmixed,) = riding(p_out, ['w_out'], dh1_16, w['w_out'], mode="nt", name="d_mixed" + tag, tk=KT)
    proj = s['proj']
    (du, dglu_w, dglu_b, dd, dccre, dccim, dbbre, dbbim, sre, sim) = _s5_bwd(
        dmixed, proj, s['xre'], s['xim'], c['bbre'], c['bbim'], c['ccre'], c['ccim'], c['s5_d'], c['cr'], w['s5_glu_w'], c['glu_b'],
        cb_dy=0, cb=CB_S5)
    dlam_re, dlam_im, dlog_step, db_re, db_im, dc_re, dc_im = _s5_param_grads(
        p['s5_lambda_re'], p['s5_lambda_im'], p['s5_log_step'], p['s5_b_re'], p['s5_b_im'], dbbre, dbbim, dccre, dccim, sre, sim)
    dzu, dzv, dsgu_w, dsgu_bias, dsgu_ng, dsgu_nb = _sgu_bwd(dmixed, proj, c['sgu_ng'], c['sgu_nb'], c['sgu_w'], c['sgu_wt'], c['sgu_bias'],
                                                            cb=1, cbu=CB_SGU_U, cbv=CB_SGU_V)
    dp, dpool_w, dpool_scale = _pool_bwd(dmixed, s['pooled'], c['pool_w'], c['pool_scale'], cb=2)
    dgate, ddn_ng, do, d_vnew, d_wm, d_qg, d_ktail, d_eglast = _dn_seq_bwd(
        dmixed, s['raw'], proj, c['dn_ng'], s['gb'], s['wm'], s['qg'], s['ktail'], s['qkd'], s['v_new'], s['states'],
        cb_dy=3, cb_gate=CB_DN_GATE)
    dqn, dkn, dvs, dgb = _dn_local_bwd(s['qn'], s['kn'], s['vs'], s['gb'], s['tm'], s['v_new'], do, d_vnew, d_wm, d_qg, d_ktail, d_eglast)
    dcq, dck, dcv, dab, dalog, ddtb = _dn_prep_bwd_a(proj, c['conv_w'], c['alog'], c['dtb'], dqn, dkn, dvs, dgb, cbq=CB_DN_Q)
    dq, dk, dv, dconv_w = _dn_prep_bwd_b(proj, c['conv_w'], dcq, dck, dcv, cbq=CB_DN_Q)
    dproj = jnp.concatenate([du, dzu, dzv, dp, dq, dk, dv, dgate, dab], axis=1)
    tril = jnp.tril(jnp.ones((SGU_CHUNK, SGU_CHUNK), bool))
    small = dict(
        s5_lambda_re=dlam_re, s5_lambda_im=dlam_im, s5_log_step=dlog_step, s5_b_re=db_re, s5_b_im=db_im, s5_c_re=dc_re, s5_c_im=dc_im,
        s5_d=dd.reshape(S5_G, S5_H), s5_glu_b=dglu_b[0], sgu_norm_g=dsgu_ng[0], sgu_norm_b=dsgu_nb[0],
        sgu_w=jnp.where(tril, dsgu_w, 0.0), sgu_b=dsgu_bias.reshape(SGU_CHUNK, SGU_HEADS, SGU_HD).sum(-1).T, pool_w=dpool_w,
        pool_scale=dpool_scale[0], dn_conv_w=dconv_w, dn_a_log=dalog[0, :DN_HEADS], dn_dt_bias=ddtb[0, :DN_HEADS], dn_norm_g=ddn_ng[0],
        ln1_g=dln1g[0], ln1_b=dln1b[0], ln2_g=dln2g[0], ln2_b=dln2b[0])
    if small_ride is None:
        (dw_in,) = _matmul(s['xin16'], dproj, mode="tn", name="dw_in" + tag, tn=1408, tk=KT)
    else:
        carry, done = small_ride(small)
        (dw_in,), extra = _matmul(s['xin16'], dproj, mode="tn", name="dw_in" + tag, tn=1408, tk=KT, carry=carry)
        done(extra)
    p_in = pair([_by_cols(dw_in[:, :IN_COLS]), _by_rows(dglu_w)], ['w_in', 's5_glu_w'])
    (dxin,) = riding(p_in, ['w_in', 's5_glu_w'], dproj, w['w_in'], mode="nt", name="dx_in" + tag, tk=1408, e=dh1, epi=residual)
    return dxin, reduced, small


def _pack(arrs):
    rows = []
    for a in arrs:
        n = math.prod(a.shape)
        rows.append(jnp.pad(a.reshape(-1), (0, -n % 128)).reshape(-1, 128))
    out = jnp.concatenate(rows, axis=0)
    return jnp.pad(out, ((0, -out.shape[0] % 8), (0, 0)))


def _sum_all(stacked, mine, dev):
    n, rows, cols = stacked.shape
    tr = _row_tile(rows, cols, 1 << 20)

    def body(dev_ref, mine_ref, *refs):
        acc = None
        for d in range(n):
            blk = jnp.where(dev_ref[0] == d, mine_ref[...], refs[d][...])
            acc = blk if acc is None else acc + blk
        refs[n][...] = acc

    def gathered(d):
        return pl.BlockSpec((None, tr, cols), lambda i, p: (jnp.where(p[0] == d, (d + 1) % n, d), i, 0))

    flat = pl.BlockSpec((tr, cols), lambda i, p: (i, 0))
    grid_spec = pltpu.PrefetchScalarGridSpec(num_scalar_prefetch=1, grid=(rows // tr,),
                                             in_specs=[flat] + [gathered(d) for d in range(n)], out_specs=flat)
    return pl.pallas_call(body, name="small_sum", grid_spec=grid_spec, out_shape=jax.ShapeDtypeStruct((rows, cols), F32),
                          compiler_params=_params(("parallel",)))(dev, mine, *([stacked] * n))


def _unpack(packed, like):
    out, row = [], 0
    for a in like:
        n = math.prod(a.shape)
        rows = -(-n // 128)
        out.append(packed[row:row + rows].reshape(-1)[:n].reshape(a.shape))
        row += rows
    return out


def kernel(x, w_in, s5_lambda_re, s5_lambda_im, s5_log_step, s5_b_re, s5_b_im, s5_c_re, s5_c_im, s5_d, s5_glu_w, s5_glu_b, sgu_norm_g, sgu_norm_b, sgu_w, sgu_b, pool_w, pool_scale, dn_conv_w, dn_a_log, dn_dt_bias, dn_norm_g, w_out, ln1_g, ln1_b, w_up, w_down, ln2_g, ln2_b, loss_target, m_w_in, m_s5_lambda_re, m_s5_lambda_im, m_s5_log_step, m_s5_b_re, m_s5_b_im, m_s5_c_re, m_s5_c_im, m_s5_d, m_s5_glu_w, m_s5_glu_b, m_sgu_norm_g, m_sgu_norm_b, m_sgu_w, m_sgu_b, m_pool_w, m_pool_scale, m_dn_conv_w, m_dn_a_log, m_dn_dt_bias, m_dn_norm_g, m_w_out, m_ln1_g, m_ln1_b, m_w_up, m_w_down, m_ln2_g, m_ln2_b, v_w_in, v_s5_lambda_re, v_s5_lambda_im, v_s5_log_step, v_s5_b_re, v_s5_b_im, v_s5_c_re, v_s5_c_im, v_s5_d, v_s5_glu_w, v_s5_glu_b, v_sgu_norm_g, v_sgu_norm_b, v_sgu_w, v_sgu_b, v_pool_w, v_pool_scale, v_dn_conv_w, v_dn_a_log, v_dn_dt_bias, v_dn_norm_g, v_w_out, v_ln1_g, v_ln1_b, v_w_up, v_w_down, v_ln2_g, v_ln2_b):
    given = dict(locals())
    xs, ys = lax.axis_index("x"), lax.axis_index("y")
    chip = 2 * xs + ys
    t = given['x'].shape[1]
    x0 = given['x'].reshape(t, D_MODEL)
    target = given['loss_target'].reshape(t, D_MODEL)

    assert DEPTH == 2
    place = jnp.stack([lax.axis_index("c"), chip]).astype(jnp.int32)
    conv_local = given['dn_conv_w']
    conv_all = _allgather_all(_pack([conv_local]), "allgather_conv")
    n_conv = math.prod(conv_local.shape)
    conv_full = jnp.concatenate([jnp.where(chip == j, conv_local, conv_all[2 * j].reshape(-1)[:n_conv].reshape(conv_local.shape))
                                 for j in range(4)], axis=-1)

    ws = [dict(), dict()]

    def whole(n, blocks):
        if n == 'w_in':
            return jnp.pad(jnp.transpose(blocks, (1, 0, 2)).reshape(D_MODEL, IN_COLS), ((0, 0), (0, IN_PAD - IN_COLS)))
        if n == 'w_up':
            return blocks
        return blocks.reshape(-1, blocks.shape[-1])

    def gather(items):
        own = [given[n][i].astype(MXU_DTYPE) for n, i in items]

        def done(bufs):
            for (n, i), buf, mine in zip(items, bufs, own):
                ws[i][n] = whole(n, lax.dynamic_update_slice(buf, mine[None], (chip, 0, 0)))
        return _carry_allgather(own), done

    first, first_done = gather([('w_in', 0), ('s5_glu_w', 0), ('w_out', 0)])
    first_done(_run_carry(first, "allgather_first"))
    carries = [dict(proj=gather([('w_up', 0)]), mix_out=gather([('w_in', 1), ('s5_glu_w', 1), ('w_out', 1)]),
                    mlp_up=gather([('w_down', 0)]), mlp_down=gather([('w_up', 1)])),
               dict(proj=gather([('w_down', 1)]))]

    def layer_params(i):
        p = {n: given[n][i] for n in SMALL}
        p['dn_conv_w'] = conv_full[i]
        return p

    ps = [layer_params(i) for i in range(DEPTH)]
    cs = [_layer_consts(p) for p in ps]

    xcur, xcur16, saved = x0, x0.astype(MXU_DTYPE), []
    for i in range(DEPTH):
        xcur, xcur16, s = _layer_fwd(xcur, xcur16, ws[i], cs[i], i, carries[i])
        saved.append(s)
    dx, colsum = _loss_head(xcur, target)
    loss = lax.psum(0.5 * jnp.sum(colsum) / D_MODEL, ("x", "y", "c"))

    reduced, smalls, small_comm = [None] * DEPTH, [None] * DEPTH, {}

    def small_ride(first_layer):
        full = [jnp.stack([first_layer[n], smalls[1][n]]) for n in SMALL]
        small_comm.update(full=full, packed=_pack(full))
        return _carry_allgather_all(small_comm['packed']), lambda outs: small_comm.update(gathered=outs[0])

    for i in reversed(range(DEPTH)):
        dx, reduced[i], smalls[i] = _layer_bwd(dx, saved[i], ws[i], cs[i], ps[i], i, place, small_ride if i == 0 else None)
    grad_x = dx.reshape(1, t, D_MODEL)
    dev = (2 * chip + lax.axis_index("c")).astype(jnp.int32).reshape(1)
    grads = dict(zip(SMALL, _unpack(_sum_all(small_comm['gathered'], small_comm['packed'], dev), small_comm['full'])))
    grads['dn_conv_w'] = lax.dynamic_slice_in_dim(grads['dn_conv_w'], chip * conv_local.shape[-1], conv_local.shape[-1], axis=2)

    delta, new_m, new_v = {}, {}, {}
    for n in BIG:
        grads[n], delta[n], new_m[n], new_v[n] = _adamw_layers(given[n], reduced[0][n], reduced[1][n], given['m_' + n], given['v_' + n],
                                                               name="adamw_" + n)
    like = [given[n] for n in SMALL]
    d, nm, nv = _adamw(_pack(like), _pack([grads[n] for n in SMALL]), _pack([given['m_' + n] for n in SMALL]),
                       _pack([given['v_' + n] for n in SMALL]), name="adamw_small")
    for out, packed in ((delta, d), (new_m, nm), (new_v, nv)):
        out.update(zip(SMALL, _unpack(packed, like)))
    return (loss, grad_x, *[grads[n] for n in WEIGHTS], *[delta[n] for n in WEIGHTS], *[new_m[n] for n in WEIGHTS],
            *[new_v[n] for n in WEIGHTS])
```
